```python
import jax, jax.numpy as jnp
from jax import lax
import numpy as np

D_MODEL = 1024
BATCH = 8
SEQ = 4096
DEPTH = 2

GRID_W = 64
CTX_LEN = 256
MIX_W = D_MODEL
RG_W = MIX_W // 2
RG_BLOCKS = 8
RG_BLOCK_W = RG_W // RG_BLOCKS
RG_C = 8.0
CONV_W = 4
CONV_LEFT = 2
RET_W = MIX_W - RG_W
RET_HEADS = 4
RET_HEAD_DIM = RET_W // RET_HEADS
RET_CHUNK = 128
ROPE_BASE = 10000.0
IN_W = 2 * RG_W + 4 * RET_W
D_FF = 2816
N_EXPERTS = 8
TOP_K = 2
D_EXPERT = 3584
MOE_BLOCK = 512
N_DENSE = (DEPTH + 1) // 2
N_MOE = DEPTH // 2
EPS = 1e-6

kernel_name = 'hybrid_rglru_retention_moe_dit'


def rms_norm(x, g):
    xf = x.astype(jnp.float32)
    y = xf * lax.rsqrt(jnp.mean(xf * xf, axis=-1, keepdims=True) + EPS)
    return (y * g.astype(jnp.float32)).astype(x.dtype)


def modulate(h, shift, scale):
    return h * (1 + scale) + shift


def short_conv(u, w, b):
    n = u.shape[1]
    up = jnp.pad(u, ((0, 0), (CONV_LEFT, CONV_W - 1 - CONV_LEFT), (0, 0)))
    return sum(up[:, k:k + n] * w[k] for k in range(CONV_W)) + b


def _lin_combine(e1, e2):
    a1, b1 = e1
    a2, b2 = e2
    return a1 * a2, a2 * b1 + b2


def rglru(u, w_a, b_a, w_x, b_x, lam, h0, reverse):
    f32 = jnp.float32
    bsz, n, _ = u.shape
    ub = u.reshape(bsz, n, RG_BLOCKS, RG_BLOCK_W)
    r = jax.nn.sigmoid(jnp.einsum('bnkc,kcd->bnkd', ub, w_a.astype(f32)).reshape(bsz, n, RG_W) + b_a.astype(f32))
    i = jax.nn.sigmoid(jnp.einsum('bnkc,kcd->bnkd', ub, w_x.astype(f32)).reshape(bsz, n, RG_W) + b_x.astype(f32))
    log_a = -RG_C * r * jax.nn.softplus(-lam.astype(f32))
    a = jnp.exp(log_a)
    bt = jnp.sqrt(-jnp.expm1(2.0 * log_a)) * (i * u)
    a_cum, h = lax.associative_scan(_lin_combine, (a, bt), axis=1, reverse=reverse)
    h = h + a_cum * h0[:, None, :]
    return h, (h[:, 0] if reverse else h[:, -1])


def retention_dir(q, k, v, log_g, s0, strict):
    bsz, nh, n, d = q.shape
    L = RET_CHUNK
    nc = n // L
    q = q.reshape(bsz, nh, nc, L, d)
    k = k.reshape(bsz, nh, nc, L, d)
    v = v.reshape(bsz, nh, nc, L, d)
    j = jnp.arange(L, dtype=jnp.float32)
    diff = j[:, None] - j[None, :]
    mask = (diff > 0) if strict else (diff >= 0)
    dec = jnp.where(mask, jnp.exp(jnp.maximum(diff, 0.0) * log_g[:, None, None]), 0.0)
    scores = jnp.einsum('bhcid,bhcjd->bhcij', q, k) * dec[None, :, None]
    inner = jnp.einsum('bhcij,bhcjd->bhcid', scores, v)
    k_dec = k * jnp.exp((L - 1 - j)[None, :] * log_g[:, None])[None, :, None, :, None]
    kv = jnp.einsum('bhcjd,bhcje->cbhde', k_dec, v)
    chunk_decay = jnp.exp(L * log_g)[None, :, None, None]

    def step(s, kv_c):
        return chunk_decay * s + kv_c, s

    s_final, s_prev = lax.scan(step, s0, kv)
    q_dec = q * jnp.exp((j + 1)[None, :] * log_g[:, None])[None, :, None, :, None]
    cross = jnp.einsum('bhcid,cbhde->bhcie', q_dec, s_prev)
    return (inner + cross).reshape(bsz, nh, n, d), s_final


def retention_bidir(q, k, v, log_g_f, log_g_b, s0_f, s0_b):
    o_f, s_f = retention_dir(q, k, v, log_g_f, s0_f, False)
    flip = lambda t: jnp.flip(t, axis=2)
    o_b, s_b = retention_dir(flip(q), flip(k), flip(v), log_g_b, s0_b, True)
    return o_f + flip(o_b), s_f, s_b


def apply_rope(t, cos, sin):
    half = t.shape[-1] // 2
    t1, t2 = t[..., :half], t[..., half:]
    return jnp.concatenate([t1 * cos - t2 * sin, t1 * sin + t2 * cos], axis=-1)


def grid_rope(rows):
    row = jnp.repeat(jnp.arange(rows, dtype=jnp.float32), GRID_W)
    col = jnp.tile(jnp.arange(GRID_W, dtype=jnp.float32), rows)
    n_freq = RET_HEAD_DIM // 4
    inv = ROPE_BASE ** (-jnp.arange(n_freq, dtype=jnp.float32) / n_freq)
    ang = jnp.concatenate([row[:, None] * inv, col[:, None] * inv], axis=-1)
    return jnp.cos(ang), jnp.sin(ang)


def token_mixer(p, rope, states, conv_w, conv_b, rg_wa, rg_ba, rg_wx, rg_bx, rg_lam, ret_decay):
    f32 = jnp.float32
    bsz, n, _ = p.shape
    p = p.astype(f32)
    u, y_gate, q, k, v, g = jnp.split(
        p, [RG_W, 2 * RG_W, 2 * RG_W + RET_W, 2 * RG_W + 2 * RET_W, 2 * RG_W + 3 * RET_W], axis=-1)
    h_f0, h_b0, s_f0, s_b0 = states
    u = short_conv(u, conv_w.astype(f32), conv_b.astype(f32))
    h_f, hf_last = rglru(u, rg_wa[0], rg_ba[0], rg_wx[0], rg_bx[0], rg_lam[0], h_f0, False)
    h_b, hb_last = rglru(u, rg_wa[1], rg_ba[1], rg_wx[1], rg_bx[1], rg_lam[1], h_b0, True)
    rg_out = jax.nn.gelu(y_gate) * (h_f + h_b)
    heads = lambda t: t.reshape(bsz, n, RET_HEADS, RET_HEAD_DIM).transpose(0, 2, 1, 3)
    q, k, v = heads(q), heads(k) * RET_HEAD_DIM ** -0.5, heads(v)
    if rope is not None:
        cos, sin = rope
        q, k = apply_rope(q, cos, sin), apply_rope(k, cos, sin)
    log_g = jax.nn.log_sigmoid(ret_decay.astype(f32))
    o, s_f, s_b = retention_bidir(q, k, v, log_g[0], log_g[1], s_f0, s_b0)
    mu = jnp.mean(o, axis=-1, keepdims=True)
    var = jnp.mean(jnp.square(o - mu), axis=-1, keepdims=True)
    o = ((o - mu) * lax.rsqrt(var + EPS)).transpose(0, 2, 1, 3).reshape(bsz, n, RET_W)
    ret_out = jax.nn.silu(g) * o
    y = jnp.concatenate([rg_out, ret_out], axis=-1)
    return y, (hf_last, hb_last, s_f, s_b)


def swiglu(h, w1, w3, w2):
    return (jax.nn.silu(h @ w1) * (h @ w3)) @ w2


def moe_swiglu(h, w_router, b_router, w1, w3, w2):
    bsz, n, d = h.shape
    t = h.reshape(-1, d)
    n_tok = t.shape[0]
    n_asg = n_tok * TOP_K
    logits = (t @ w_router).astype(jnp.float32) + b_router.astype(jnp.float32)
    top_v, top_i = lax.top_k(logits, TOP_K)
    top_w = jax.nn.softmax(top_v, axis=-1)
    flat_e = top_i.reshape(-1)
    flat_w = top_w.reshape(-1)
    flat_tok = jnp.arange(n_asg, dtype=jnp.int32) // TOP_K
    order = jnp.argsort(flat_e)
    sorted_e = flat_e[order]
    counts = jnp.zeros((N_EXPERTS,), jnp.int32).at[flat_e].add(1)
    padded = (counts + MOE_BLOCK - 1) // MOE_BLOCK * MOE_BLOCK
    pad_end = jnp.cumsum(padded)
    pad_start = pad_end - padded
    start = jnp.cumsum(counts) - counts
    rank = jnp.arange(n_asg, dtype=jnp.int32) - start[sorted_e]
    dest = pad_start[sorted_e] + rank
    n_rows = (-(-n_asg // MOE_BLOCK)) * MOE_BLOCK + N_EXPERTS * MOE_BLOCK
    n_blocks = n_rows // MOE_BLOCK
    buf_tok = jnp.zeros((n_rows,), jnp.int32).at[dest].set(flat_tok[order])
    buf_w = jnp.zeros((n_rows,), jnp.float32).at[dest].set(flat_w[order])
    block_e = jnp.minimum(
        jnp.searchsorted(pad_end, jnp.arange(n_blocks, dtype=jnp.int32) * MOE_BLOCK, side='right'),
        N_EXPERTS - 1)

    def block_ffn(args):
        tok, e = args
        return swiglu(t[tok], w1[e], w3[e], w2[e])

    y = lax.map(block_ffn, (buf_tok.reshape(n_blocks, MOE_BLOCK), block_e))
    out = jnp.zeros((n_tok, d), jnp.float32).at[buf_tok].add(
        buf_w[:, None] * y.reshape(n_rows, d).astype(jnp.float32))
    return out.reshape(bsz, n, d).astype(h.dtype)


def channel_mixer(l, h, ffn_w1, ffn_w3, ffn_w2, moe_router, moe_router_b, moe_w1, moe_w3, moe_w2):
    i = l // 2
    if l % 2 == 0:
        return swiglu(h, ffn_w1[i], ffn_w3[i], ffn_w2[i])
    return moe_swiglu(h, moe_router[i], moe_router_b[i], moe_w1[i], moe_w3[i], moe_w2[i])


def setup_inputs(seed: int = 0) -> dict:
    key = jax.random.key(seed)
    ks = jax.random.split(key, 32)
    f32 = jnp.float32
    nrm = lambda k, shape, s: jax.random.normal(k, shape, f32) * s
    D = D_MODEL
    u = jax.random.uniform(ks[14], (DEPTH, 2, RG_W), f32, 0.9, 0.999)
    a = u ** (1.0 / RG_C)
    rg_lam = jnp.log(a) - jnp.log1p(-a)
    gamma = 1.0 - 2.0 ** (-5.0 - jnp.arange(RET_HEADS, dtype=f32))
    ret_decay = (jnp.log(gamma) - jnp.log1p(-gamma))[None, None, :] + nrm(ks[15], (DEPTH, 2, RET_HEADS), 0.1)
    return {
        'x': nrm(ks[0], (BATCH, SEQ, D), 1.0),
        'c': nrm(ks[1], (BATCH, D), 1.0),
        'ctx': nrm(ks[2], (BATCH, CTX_LEN, D), 1.0),
        'c_ctx': nrm(ks[3], (D,), 1.0),
        'w_mod': nrm(ks[4], (DEPTH, D, 6 * D), D ** -0.5),
        'b_mod': nrm(ks[5], (DEPTH, 6 * D), 0.02),
        'g_mix': 1.0 + nrm(ks[6], (DEPTH, D), 0.02),
        'g_ffn': 1.0 + nrm(ks[7], (DEPTH, D), 0.02),
        'g_final': 1.0 + nrm(ks[8], (D,), 0.02),
        'w_in': nrm(ks[9], (DEPTH, D, IN_W), D ** -0.5),
        'w_out': nrm(ks[10], (DEPTH, MIX_W, D), MIX_W ** -0.5),
        'conv_w': nrm(ks[11], (DEPTH, CONV_W, RG_W), CONV_W ** -0.5),
        'conv_b': nrm(ks[12], (DEPTH, RG_W), 0.02),
        'rg_wa': nrm(ks[13], (DEPTH, 2, RG_BLOCKS, RG_BLOCK_W, RG_BLOCK_W), RG_BLOCK_W ** -0.5),
        'rg_ba': nrm(ks[16], (DEPTH, 2, RG_W), 0.02),
        'rg_wx': nrm(ks[17], (DEPTH, 2, RG_BLOCKS, RG_BLOCK_W, RG_BLOCK_W), RG_BLOCK_W ** -0.5),
        'rg_bx': nrm(ks[18], (DEPTH, 2, RG_W), 0.02),
        'rg_lam': rg_lam,
        'ret_decay': ret_decay,
        'ffn_w1': nrm(ks[19], (N_DENSE, D, D_FF), D ** -0.5),
        'ffn_w3': nrm(ks[20], (N_DENSE, D, D_FF), D ** -0.5),
        'ffn_w2': nrm(ks[21], (N_DENSE, D_FF, D), D_FF ** -0.5),
        'moe_router': nrm(ks[22], (N_MOE, D, N_EXPERTS), D ** -0.5),
        'moe_router_b': nrm(ks[23], (N_MOE, N_EXPERTS), 0.01),
        'moe_w1': nrm(ks[24], (N_MOE, N_EXPERTS, D, D_EXPERT), D ** -0.5),
        'moe_w3': nrm(ks[25], (N_MOE, N_EXPERTS, D, D_EXPERT), D ** -0.5),
        'moe_w2': nrm(ks[26], (N_MOE, N_EXPERTS, D_EXPERT, D), D_EXPERT ** -0.5),
    }


def reference(x, c, ctx, c_ctx, w_mod, b_mod, g_mix, g_ffn, g_final, w_in, w_out, conv_w, conv_b,
              rg_wa, rg_ba, rg_wx, rg_bx, rg_lam, ret_decay, ffn_w1, ffn_w3, ffn_w2,
              moe_router, moe_router_b, moe_w1, moe_w3, moe_w2):
    f32 = jnp.float32
    bsz, n_lat, _ = x.shape
    rows = n_lat // GRID_W
    rope = grid_rope(rows)
    sc_lat = jax.nn.silu(c)
    sc_ctx = jax.nn.silu(c_ctx)
    ffn_args = (ffn_w1, ffn_w3, ffn_w2, moe_router, moe_router_b, moe_w1, moe_w3, moe_w2)
    for l in range(DEPTH):
        last = l == DEPTH - 1
        mod_lat = jnp.split((sc_lat @ w_mod[l] + b_mod[l])[:, None, :], 6, axis=-1)
        mod_ctx = jnp.split((sc_ctx @ w_mod[l] + b_mod[l])[None, None, :], 6, axis=-1)
        mix_p = (conv_w[l], conv_b[l], rg_wa[l], rg_ba[l], rg_wx[l], rg_bx[l], rg_lam[l], ret_decay[l])
        zero_states = (jnp.zeros((bsz, RG_W), f32), jnp.zeros((bsz, RG_W), f32),
                       jnp.zeros((bsz, RET_HEADS, RET_HEAD_DIM, RET_HEAD_DIM), f32),
                       jnp.zeros((bsz, RET_HEADS, RET_HEAD_DIM, RET_HEAD_DIM), f32))
        pc = modulate(rms_norm(ctx, g_mix[l]), mod_ctx[0], mod_ctx[1]) @ w_in[l]
        y_ctx, ctx_states = token_mixer(pc, None, zero_states, *mix_p)
        px = modulate(rms_norm(x, g_mix[l]), mod_lat[0], mod_lat[1]) @ w_in[l]
        y_lat, _ = token_mixer(px, rope, ctx_states, *mix_p)
        x = x + mod_lat[2] * (y_lat.astype(x.dtype) @ w_out[l])
        hx = modulate(rms_norm(x, g_ffn[l]), mod_lat[3], mod_lat[4])
        x = x + mod_lat[5] * channel_mixer(l, hx, *ffn_args)
        if not last:
            ctx = ctx + mod_ctx[2] * (y_ctx.astype(ctx.dtype) @ w_out[l])
            hc = modulate(rms_norm(ctx, g_ffn[l]), mod_ctx[3], mod_ctx[4])
            ctx = ctx + mod_ctx[5] * channel_mixer(l, hc, *ffn_args)
    return rms_norm(x, g_final)
```

```python
import functools

import jax
import jax.numpy as jnp
from jax import lax
from jax.experimental import pallas as pl
from jax.experimental.pallas import tpu as pltpu

F32 = jnp.float32
BF16 = jnp.bfloat16

EPS = 1e-6
RG_C = 8.0
CONV_W = 4
CONV_LEFT = 2
RG_BLOCKS = 8
RET_HEADS = 4
RET_CHUNK = 128
ROPE_BASE = 10000.0
GRID_W = 64
N_EXPERTS = 8
N_MOD = 6

LANES = 128
SUBLANES = 8
VMEM_LIMIT = 56 * 1024 * 1024
NEG_BIG = -1e30


def _cparams(sem):
    return pltpu.CompilerParams(dimension_semantics=sem, vmem_limit_bytes=VMEM_LIMIT)


def _modulation_kernel(c_ref, w_ref, b_ref, o_ref):
    c = c_ref[...]
    s = c * jax.nn.sigmoid(c)
    o_ref[...] = jnp.dot(s, w_ref[...], precision=lax.Precision.HIGHEST,
                         preferred_element_type=F32) + b_ref[...]


def _modulation(c_all, w_mod, b_mod):
    depth, d, six_d = w_mod.shape
    rows = c_all.shape[0]
    tn = 1536
    return pl.pallas_call(
        _modulation_kernel,
        grid=(depth, six_d // tn),
        in_specs=[
            pl.BlockSpec((rows, d), lambda l, j: (0, 0)),
            pl.BlockSpec((None, d, tn), lambda l, j: (l, 0, j)),
            pl.BlockSpec((None, 1, tn), lambda l, j: (l, 0, j)),
        ],
        out_specs=pl.BlockSpec((None, rows, tn), lambda l, j: (l, 0, j)),
        out_shape=jax.ShapeDtypeStruct((depth, rows, six_d), F32),
        compiler_params=_cparams(("arbitrary", "arbitrary")),
        name="modulation",
    )(c_all, w_mod, b_mod.reshape(depth, 1, six_d))


def _norm_mod(x, g, shift, scale):
    ms = jnp.mean(x * x, axis=-1, keepdims=True)
    y = x * lax.rsqrt(ms + EPS) * g
    return y * (1.0 + scale) + shift


def _in_proj_kernel(x_ref, g_ref, shift_ref, scale_ref, w_ref, o_ref):
    h = _norm_mod(x_ref[...], g_ref[...], shift_ref[...], scale_ref[...])
    o_ref[...] = jnp.dot(h.astype(BF16), w_ref[...],
                         preferred_element_type=F32).astype(o_ref.dtype)


def _in_proj(x, g, shift, scale, w):
    bsz, n, d = x.shape
    f = w.shape[1]
    tm = min(512, n)
    vec = pl.BlockSpec((None, 1, d), lambda b, i: (b, 0, 0))
    return pl.pallas_call(
        _in_proj_kernel,
        grid=(bsz, n // tm),
        in_specs=[
            pl.BlockSpec((None, tm, d), lambda b, i: (b, i, 0)),
            pl.BlockSpec((1, d), lambda b, i: (0, 0)),
            vec, vec,
            pl.BlockSpec((d, f), lambda b, i: (0, 0)),
        ],
        out_specs=pl.BlockSpec((None, tm, f), lambda b, i: (b, i, 0)),
        out_shape=jax.ShapeDtypeStruct((bsz, n, f), BF16),
        compiler_params=_cparams(("parallel", "parallel")),
        name="in_proj",
    )(x, g, shift, scale, w)


RG_TILE = 256
RG_CG = 256
RG_HALO = SUBLANES


def _rglru_kernel(u_ref, yg_ref, cw_ref, cb_ref, wg_ref, bg_ref, lam_ref, h0_ref,
                  o_ref, hl_ref, u_scr, hf_scr):
    n = u_ref.shape[0]
    tt = min(RG_TILE, n)
    nt = n // tt
    cg = u_ref.shape[1]
    ext = tt + 2 * RG_HALO

    zeros = jnp.zeros((RG_HALO, cg), F32)
    u_scr[pl.ds(0, RG_HALO), :] = zeros
    u_scr[pl.ds(RG_HALO + n, RG_HALO), :] = zeros
    u_scr[pl.ds(RG_HALO, n), :] = u_ref[...].astype(F32)

    row = lax.broadcasted_iota(jnp.int32, (tt, 1), 0)
    cw = cw_ref[...]
    cb = cb_ref[...]

    def conv_tile(t0):
        xe = u_scr[pl.ds(t0, ext), :]
        acc = None
        for k in range(CONV_W):
            off = RG_HALO + k - CONV_LEFT
            xk = pltpu.roll(xe, ext - off, axis=0)[0:tt, :]
            term = cw[k:k + 1, :] * xk
            acc = term if acc is None else acc + term
        return acc + cb

    def gates(uc, d):
        ub = uc.astype(BF16)
        ra = jnp.dot(ub, wg_ref[d, 0], preferred_element_type=F32) + bg_ref[d, 0:1, :]
        rx = jnp.dot(ub, wg_ref[d, 1], preferred_element_type=F32) + bg_ref[d, 1:2, :]
        r = jax.nn.sigmoid(ra)
        i = jax.nn.sigmoid(rx)
        z = -lam_ref[d:d + 1, :]
        sp = jnp.maximum(z, 0.0) + jnp.log1p(jnp.exp(-jnp.abs(z)))
        a = jnp.exp(-RG_C * r * sp)
        b = jnp.sqrt(1.0 - a * a) * (i * uc)
        return a, b

    def scan_tile(a, b, reverse):
        s = 1
        while s < tt:
            if reverse:
                a_sh = pltpu.roll(a, tt - s, axis=0)
                b_sh = pltpu.roll(b, tt - s, axis=0)
                m = row < tt - s
            else:
                a_sh = pltpu.roll(a, s, axis=0)
                b_sh = pltpu.roll(b, s, axis=0)
                m = row >= s
            b = jnp.where(m, a * b_sh + b, b)
            a = jnp.where(m, a * a_sh, a)
            s *= 2
        return a, b

    def fwd_body(i, carry):
        t0 = pl.multiple_of(i * tt, tt)
        uc = conv_tile(t0)
        a, b = gates(uc, 0)
        a, b = scan_tile(a, b, False)
        h = b + a * carry
        hf_scr[pl.ds(t0, tt), :] = h
        return h[tt - 1:tt, :]

    hf_last = lax.fori_loop(0, nt, fwd_body, h0_ref[0:1, :])

    def bwd_body(i, carry):
        t0 = pl.multiple_of((nt - 1 - i) * tt, tt)
        uc = conv_tile(t0)
        a, b = gates(uc, 1)
        a, b = scan_tile(a, b, True)
        h = b + a * carry
        yg = yg_ref[pl.ds(t0, tt), :].astype(F32)
        o_ref[pl.ds(t0, tt), :] = (jax.nn.gelu(yg) * (hf_scr[pl.ds(t0, tt), :] + h)).astype(o_ref.dtype)
        return h[0:1, :]

    hb_last = lax.fori_loop(0, nt, bwd_body, h0_ref[1:2, :])
    hl_ref[0:1, :] = hf_last
    hl_ref[1:2, :] = hb_last


def _rglru(p, conv_w, conv_b, wg, bg, lam, h0):
    bsz, n, _ = p.shape
    rg_w = lam.shape[1]
    ncg = rg_w // RG_CG
    return pl.pallas_call(
        _rglru_kernel,
        grid=(bsz, ncg),
        in_specs=[
            pl.BlockSpec((None, n, RG_CG), lambda b, c: (b, 0, c)),
            pl.BlockSpec((None, n, RG_CG), lambda b, c: (b, 0, ncg + c)),
            pl.BlockSpec((CONV_W, RG_CG), lambda b, c: (0, c)),
            pl.BlockSpec((1, RG_CG), lambda b, c: (0, c)),
            pl.BlockSpec((2, 2, None, RG_CG, RG_CG), lambda b, c: (0, 0, c, 0, 0)),
            pl.BlockSpec((2, 2, RG_CG), lambda b, c: (0, 0, c)),
            pl.BlockSpec((2, RG_CG), lambda b, c: (0, c)),
            pl.BlockSpec((None, 2, RG_CG), lambda b, c: (b, 0, c)),
        ],
        out_specs=[
            pl.BlockSpec((None, n, RG_CG), lambda b, c: (b, 0, c)),
            pl.BlockSpec((None, 2, RG_CG), lambda b, c: (b, 0, c)),
        ],
        out_shape=[
            jax.ShapeDtypeStruct((bsz, n, rg_w), BF16),
            jax.ShapeDtypeStruct((bsz, 2, rg_w), F32),
        ],
        scratch_shapes=[
            pltpu.VMEM((n + 2 * RG_HALO, RG_CG), F32),
            pltpu.VMEM((n, RG_CG), F32),
        ],
        compiler_params=_cparams(("parallel", "parallel")),
        name="rglru",
    )(p, p, conv_w, conv_b.reshape(1, rg_w), wg, bg, lam, h0)


def _retention_kernel(lg_ref, q_ref, k_ref, v_ref, g_ref, cs_ref, sn_ref, s0_ref,
                      o_ref, so_ref, qr_scr, kr_scr, r_scr):
    n, hd = q_ref.shape
    L = RET_CHUNK
    nc = n // L
    head = pl.program_id(1)
    lgf = lg_ref[0, head]
    lgb = lg_ref[1, head]

    def rope(t):
        return t * cs_ref[...] + pltpu.roll(t, hd // 2, axis=1) * sn_ref[...]

    qr_scr[...] = rope(q_ref[...].astype(F32)).astype(BF16)
    kr_scr[...] = (rope(k_ref[...].astype(F32)) * (hd ** -0.5)).astype(BF16)

    jc = lax.broadcasted_iota(jnp.int32, (L, 1), 0).astype(F32)
    q_dec_f = jnp.exp((jc + 1.0) * lgf)
    q_dec_b = jnp.exp((L - jc) * lgb)
    k_dec_f = jnp.exp((L - 1.0 - jc) * lgf)
    k_dec_b = jnp.exp(jc * lgb)
    ones = jnp.ones((1, hd), F32)
    chunk_f = jnp.exp(ones * (L * lgf))
    chunk_b = jnp.exp(ones * (L * lgb))
    ii = lax.broadcasted_iota(jnp.int32, (L, L), 0)
    jj = lax.broadcasted_iota(jnp.int32, (L, L), 1)
    diff = (ii - jj).astype(F32)
    dmat = jnp.where(diff >= 0.0, jnp.exp(jnp.maximum(diff, 0.0) * lgf),
                     jnp.exp(jnp.maximum(-diff, 0.0) * lgb))

    tn_dims = (((0,), (0,)), ((), ()))
    nt_dims = (((1,), (1,)), ((), ()))

    def bwd_body(i, r_state):
        c = nc - 1 - i
        r_scr[c] = r_state
        t0 = pl.multiple_of(c * L, L)
        kd = (kr_scr[pl.ds(t0, L), :].astype(F32) * k_dec_b).astype(BF16)
        upd = lax.dot_general(kd, v_ref[pl.ds(t0, L), :], tn_dims, preferred_element_type=F32)
        return chunk_b * r_state + upd

    so_ref[1] = lax.fori_loop(0, nc, bwd_body, s0_ref[1])

    def fwd_body(c, f_state):
        t0 = pl.multiple_of(c * L, L)
        qc = qr_scr[pl.ds(t0, L), :]
        kc = kr_scr[pl.ds(t0, L), :]
        vc = v_ref[pl.ds(t0, L), :]
        s = lax.dot_general(qc, kc, nt_dims, preferred_element_type=F32)
        o = jnp.dot((s * dmat).astype(BF16), vc, preferred_element_type=F32)
        o = o + jnp.dot(qc, f_state.astype(BF16), preferred_element_type=F32) * q_dec_f
        o = o + jnp.dot(qc, r_scr[c].astype(BF16), preferred_element_type=F32) * q_dec_b
        mu = jnp.mean(o, axis=-1, keepdims=True)
        oc = o - mu
        var = jnp.mean(oc * oc, axis=-1, keepdims=True)
        gate = g_ref[pl.ds(t0, L), :].astype(F32)
        o_ref[pl.ds(t0, L), :] = (gate * jax.nn.sigmoid(gate) * (oc * lax.rsqrt(var + EPS))).astype(o_ref.dtype)
        kd = (kc.astype(F32) * k_dec_f).astype(BF16)
        upd = lax.dot_general(kd, vc, tn_dims, preferred_element_type=F32)
        return chunk_f * f_state + upd

    so_ref[0] = lax.fori_loop(0, nc, fwd_body, s0_ref[0])


def _retention(p, log_g, cs, sn, s0, col0):
    bsz, n, _ = p.shape
    nh = log_g.shape[1]
    hd = cs.shape[1]
    cb0 = col0 // hd

    def col(which):
        return pl.BlockSpec((None, n, hd), lambda b, h, lg: (b, 0, cb0 + which * nh + h))

    state = pl.BlockSpec((None, 2, None, hd, hd), lambda b, h, lg: (b, 0, h, 0, 0))
    table = pl.BlockSpec((n, hd), lambda b, h, lg: (0, 0))
    grid_spec = pltpu.PrefetchScalarGridSpec(
        num_scalar_prefetch=1,
        grid=(bsz, nh),
        in_specs=[col(0), col(1), col(2), col(3), table, table, state],
        out_specs=[pl.BlockSpec((None, n, hd), lambda b, h, lg: (b, 0, h)), state],
        scratch_shapes=[
            pltpu.VMEM((n, hd), BF16),
            pltpu.VMEM((n, hd), BF16),
            pltpu.VMEM((n // RET_CHUNK, hd, hd), F32),
        ],
    )
    return pl.pallas_call(
        _retention_kernel,
        grid_spec=grid_spec,
        out_shape=[
            jax.ShapeDtypeStruct((bsz, n, nh * hd), BF16),
            jax.ShapeDtypeStruct((bsz, 2, nh, hd, hd), F32),
        ],
        compiler_params=_cparams(("parallel", "parallel")),
        name="retention",
    )(log_g, p, p, p, p, cs, sn, s0)


def _out_proj_kernel(route, x_ref, rg_ref, ret_ref, wo_ref, gate_ref, g_ref, shift_ref, scale_ref,
                     *rest):
    if route:
        wr_ref, br_ref, xo_ref, h_ref, cw_ref = rest
    else:
        xo_ref, h_ref = rest
    rg_w = rg_ref.shape[1]
    y = jnp.dot(rg_ref[...], wo_ref[0:rg_w, :], preferred_element_type=F32)
    y = y + jnp.dot(ret_ref[...], wo_ref[rg_w:, :], preferred_element_type=F32)
    xn = x_ref[...] + gate_ref[...] * y
    xo_ref[...] = xn
    hx = _norm_mod(xn, g_ref[...], shift_ref[...], scale_ref[...])
    h_ref[...] = hx.astype(h_ref.dtype)
    if route:
        logits = jnp.dot(hx, wr_ref[...], precision=lax.Precision.HIGHEST,
                         preferred_element_type=F32) + br_ref[...]
        lane = lax.broadcasted_iota(jnp.int32, logits.shape, 1).astype(F32)
        m0 = jnp.max(logits, axis=-1, keepdims=True)
        i0 = jnp.min(jnp.where(logits == m0, lane, float(LANES)), axis=-1, keepdims=True)
        rest_l = jnp.where(lane == i0, NEG_BIG, logits)
        m1 = jnp.max(rest_l, axis=-1, keepdims=True)
        i1 = jnp.min(jnp.where(rest_l == m1, lane, float(LANES)), axis=-1, keepdims=True)
        e = jnp.exp(m1 - m0)
        w0 = 1.0 / (1.0 + e)
        w1 = e / (1.0 + e)
        cw_ref[...] = jnp.where(lane == i0, w0, 0.0) + jnp.where(lane == i1, w1, 0.0)


def _out_proj(x, rg, ret, w_out, gate, g, shift, scale, router=None):
    bsz, n, d = x.shape
    tm = min(512, n)
    route = router is not None
    vec = pl.BlockSpec((None, 1, d), lambda b, i: (b, 0, 0))
    tile = lambda w: pl.BlockSpec((None, tm, w), lambda b, i: (b, i, 0))
    in_specs = [
        tile(d), tile(rg.shape[2]), tile(ret.shape[2]),
        pl.BlockSpec(w_out.shape, lambda b, i: (0, 0)),
        vec, pl.BlockSpec((1, d), lambda b, i: (0, 0)), vec, vec,
    ]
    args = [x, rg, ret, w_out, gate, g, shift, scale]
    out_specs = [tile(d), tile(d)]
    out_shape = [jax.ShapeDtypeStruct((bsz, n, d), F32), jax.ShapeDtypeStruct((bsz, n, d), BF16)]
    if route:
        in_specs += [pl.BlockSpec((d, LANES), lambda b, i: (0, 0)),
                     pl.BlockSpec((1, LANES), lambda b, i: (0, 0))]
        args += list(router)
        out_specs.append(tile(LANES))
        out_shape.append(jax.ShapeDtypeStruct((bsz, n, LANES), F32))
    return pl.pallas_call(
        functools.partial(_out_proj_kernel, route),
        grid=(bsz, n // tm),
        in_specs=in_specs,
        out_specs=out_specs,
        out_shape=out_shape,
        compiler_params=_cparams(("parallel", "parallel")),
        name="out_proj_route" if route else "out_proj",
    )(*args)


def _swiglu_acc(h, w1_ref, w3_ref, w2_ref, row_scale=None):
    a = jnp.dot(h, w1_ref[...], preferred_element_type=F32)
    b = jnp.dot(h, w3_ref[...], preferred_element_type=F32)
    act = a * jax.nn.sigmoid(a) * b
    if row_scale is not None:
        act = act * row_scale
    return jnp.dot(act.astype(BF16), w2_ref[...], preferred_element_type=F32)


def _ffn_kernel(h_ref, x_ref, gate_ref, w1_ref, w3_ref, w2_ref, o_ref, acc_ref):
    f = pl.program_id(2)

    @pl.when(f == 0)
    def _():
        acc_ref[...] = jnp.zeros_like(acc_ref)

    acc_ref[...] += _swiglu_acc(h_ref[...], w1_ref, w3_ref, w2_ref)

    @pl.when(f == pl.num_programs(2) - 1)
    def _():
        o_ref[...] = x_ref[...] + gate_ref[...] * acc_ref[...]


def _ffn(h, x, gate, w1, w3, w2):
    bsz, n, d = x.shape
    f = w1.shape[1]
    tm = min(1024, n)
    tf = 1408 if f % 1408 == 0 else f
    tile = pl.BlockSpec((None, tm, d), lambda b, i, j: (b, i, 0))
    return pl.pallas_call(
        _ffn_kernel,
        grid=(bsz, n // tm, f // tf),
        in_specs=[
            tile, tile,
            pl.BlockSpec((None, 1, d), lambda b, i, j: (b, 0, 0)),
            pl.BlockSpec((d, tf), lambda b, i, j: (0, j)),
            pl.BlockSpec((d, tf), lambda b, i, j: (0, j)),
            pl.BlockSpec((tf, d), lambda b, i, j: (j, 0)),
        ],
        out_specs=tile,
        out_shape=jax.ShapeDtypeStruct((bsz, n, d), F32),
        scratch_shapes=[pltpu.VMEM((tm, d), F32)],
        compiler_params=_cparams(("parallel", "parallel", "arbitrary")),
        name="ffn",
    )(h, x, gate, w1, w3, w2)


def _moe_kernel(h_ref, x_ref, gate_ref, cw_ref, gfin_ref, w1_ref, w3_ref, w2_ref, o_ref, acc_ref):
    e = pl.program_id(2)
    f = pl.program_id(3)

    @pl.when((e == 0) & (f == 0))
    def _():
        acc_ref[...] = jnp.zeros_like(acc_ref)

    cw = cw_ref[...]
    lane = lax.broadcasted_iota(jnp.int32, cw.shape, 1)
    w_e = jnp.sum(jnp.where(lane == e, cw, 0.0), axis=-1, keepdims=True)
    acc_ref[...] += _swiglu_acc(h_ref[...], w1_ref, w3_ref, w2_ref, row_scale=w_e)

    @pl.when((e == pl.num_programs(2) - 1) & (f == pl.num_programs(3) - 1))
    def _():
        xn = x_ref[...] + gate_ref[...] * acc_ref[...]
        ms = jnp.mean(xn * xn, axis=-1, keepdims=True)
        o_ref[...] = xn * lax.rsqrt(ms + EPS) * gfin_ref[...]


def _moe(h, x, gate, cw, g_final, w1, w3, w2):
    bsz, n, d = x.shape
    ne, _, f = w1.shape
    tm = min(1024, n)
    tf = 512 if f % 512 == 0 else f
    tile = lambda w: pl.BlockSpec((None, tm, w), lambda b, i, e, j: (b, i, 0))
    return pl.pallas_call(
        _moe_kernel,
        grid=(bsz, n // tm, ne, f // tf),
        in_specs=[
            tile(d), tile(d),
            pl.BlockSpec((None, 1, d), lambda b, i, e, j: (b, 0, 0)),
            tile(LANES),
            pl.BlockSpec((1, d), lambda b, i, e, j: (0, 0)),
            pl.BlockSpec((None, d, tf), lambda b, i, e, j: (e, 0, j)),
            pl.BlockSpec((None, d, tf), lambda b, i, e, j: (e, 0, j)),
            pl.BlockSpec((None, tf, d), lambda b, i, e, j: (e, j, 0)),
        ],
        out_specs=tile(d),
        out_shape=jax.ShapeDtypeStruct((bsz, n, d), F32),
        scratch_shapes=[pltpu.VMEM((tm, d), F32)],
        compiler_params=_cparams(("parallel", "parallel", "arbitrary", "arbitrary")),
        name="moe",
    )(h, x, gate, cw, g_final, w1, w3, w2)


def _rope_tables(n, hd):
    rows = n // GRID_W
    row = jnp.repeat(jnp.arange(rows, dtype=F32), GRID_W)
    col = jnp.tile(jnp.arange(GRID_W, dtype=F32), rows)
    n_freq = hd // 4
    inv = ROPE_BASE ** (-jnp.arange(n_freq, dtype=F32) / n_freq)
    ang = jnp.concatenate([row[:, None] * inv, col[:, None] * inv], axis=-1)
    cos, sin = jnp.cos(ang), jnp.sin(ang)
    return jnp.concatenate([cos, cos], axis=-1), jnp.concatenate([-sin, sin], axis=-1)


def _block_diag_gates(rg_wa, rg_wx):
    w = jnp.stack([rg_wa, rg_wx], axis=1)
    nd, ng, nk, c, _ = w.shape
    per = RG_CG // c
    w = w.reshape(nd, ng, nk // per, per, c, c)
    eye = jnp.eye(per, dtype=w.dtype)
    full = jnp.einsum('dgmpij,pq->dgmpiqj', w, eye)
    return full.reshape(nd, ng, nk // per, RG_CG, RG_CG).astype(BF16)


def _mixer(p, rope, states, conv_w, conv_b, wg, bg, lam, log_g, rg_w):
    rg_out, h_last = _rglru(p, conv_w, conv_b, wg, bg, lam, states[0])
    ret_out, s_last = _retention(p, log_g, rope[0], rope[1], states[1], 2 * rg_w)
    return rg_out, ret_out, (h_last, s_last)


def kernel(x, c, ctx, c_ctx, w_mod, b_mod, g_mix, g_ffn, g_final, w_in, w_out, conv_w, conv_b,
           rg_wa, rg_ba, rg_wx, rg_bx, rg_lam, ret_decay, ffn_w1, ffn_w3, ffn_w2,
           moe_router, moe_router_b, moe_w1, moe_w3, moe_w2):
    bsz, n_lat, d = x.shape
    n_ctx = ctx.shape[1]
    depth = w_mod.shape[0]
    rg_w = rg_lam.shape[2]
    nh = ret_decay.shape[2]
    hd = (w_out.shape[1] - rg_w) // nh
    ne = moe_router.shape[2]

    rows = -(-(bsz + 1) // SUBLANES) * SUBLANES
    c_all = jnp.zeros((rows, d), F32).at[:bsz].set(c).at[bsz].set(c_ctx)
    mod = _modulation(c_all, w_mod, b_mod)
    mod = mod.reshape(depth, rows, N_MOD, d)
    mod_lat = mod[:, :bsz, None]
    mod_ctx = jnp.broadcast_to(mod[:, bsz:bsz + 1, None], mod_lat.shape)

    rope_lat = _rope_tables(n_lat, hd)
    rope_ctx = (jnp.ones((n_ctx, hd), F32), jnp.zeros((n_ctx, hd), F32))
    log_g = jax.nn.log_sigmoid(ret_decay.astype(F32))
    zero_states = (jnp.zeros((bsz, 2, rg_w), F32), jnp.zeros((bsz, 2, nh, hd, hd), F32))

    for l in range(depth):
        last = l == depth - 1
        ml = [mod_lat[l, :, :, j] for j in range(N_MOD)]
        mc = [mod_ctx[l, :, :, j] for j in range(N_MOD)]
        g_m = g_mix[l].reshape(1, d)
        g_f = g_ffn[l].reshape(1, d)
        w_in_l = w_in[l].astype(BF16)
        w_out_l = w_out[l].astype(BF16)
        wg = _block_diag_gates(rg_wa[l], rg_wx[l])
        bg = jnp.stack([rg_ba[l], rg_bx[l]], axis=1)
        mix_p = (conv_w[l], conv_b[l], wg, bg, rg_lam[l], log_g[l], rg_w)

        pc = _in_proj(ctx, g_m, mc[0], mc[1], w_in_l)
        rg_c, ret_c, ctx_states = _mixer(pc, rope_ctx, zero_states, *mix_p)
        px = _in_proj(x, g_m, ml[0], ml[1], w_in_l)
        rg_x, ret_x, _ = _mixer(px, rope_lat, ctx_states, *mix_p)

        if l % 2 == 0:
            i = l // 2
            w1, w3, w2 = ffn_w1[i].astype(BF16), ffn_w3[i].astype(BF16), ffn_w2[i].astype(BF16)
            x, hx = _out_proj(x, rg_x, ret_x, w_out_l, ml[2], g_f, ml[3], ml[4])
            x = _ffn(hx, x, ml[5], w1, w3, w2)
            if not last:
                ctx, hc = _out_proj(ctx, rg_c, ret_c, w_out_l, mc[2], g_f, mc[3], mc[4])
                ctx = _ffn(hc, ctx, mc[5], w1, w3, w2)
        else:
            i = l // 2
            wr = jnp.zeros((d, LANES), F32).at[:, :ne].set(moe_router[i])
            br = jnp.full((1, LANES), NEG_BIG, F32).at[0, :ne].set(moe_router_b[i])
            x, hx, cw = _out_proj(x, rg_x, ret_x, w_out_l, ml[2], g_f, ml[3], ml[4], router=(wr, br))
            x = _moe(hx, x, ml[5], cw, g_final.reshape(1, d),
                     moe_w1[i].astype(BF16), moe_w3[i].astype(BF16), moe_w2[i].astype(BF16))
    return x
```

```python
import functools

import jax
import jax.numpy as jnp
from jax import lax
from jax.experimental import pallas as pl
from jax.experimental.pallas import tpu as pltpu

F32 = jnp.float32
BF16 = jnp.bfloat16

EPS = 1e-6
RG_C = 8.0
CONV_W = 4
CONV_LEFT = 2
RG_BLOCKS = 8
RET_HEADS = 4
RET_CHUNK = 128
ROPE_BASE = 10000.0
GRID_W = 64
N_EXPERTS = 8
N_MOD = 6

LANES = 128
SUBLANES = 8
VMEM_LIMIT = 56 * 1024 * 1024
NEG_BIG = -1e30


def _cparams(sem):
    return pltpu.CompilerParams(dimension_semantics=sem, vmem_limit_bytes=VMEM_LIMIT)


def _modulation_kernel(c_ref, w_ref, b_ref, o_ref):
    c = c_ref[...]
    s = c * jax.nn.sigmoid(c)
    o_ref[...] = jnp.dot(s, w_ref[...], precision=lax.Precision.HIGHEST,
                         preferred_element_type=F32) + b_ref[...]


def _modulation(c_all, w_mod, b_mod):
    depth, d, six_d = w_mod.shape
    rows = c_all.shape[0]
    tn = 1536
    return pl.pallas_call(
        _modulation_kernel,
        grid=(depth, six_d // tn),
        in_specs=[
            pl.BlockSpec((rows, d), lambda l, j: (0, 0)),
            pl.BlockSpec((None, d, tn), lambda l, j: (l, 0, j)),
            pl.BlockSpec((None, 1, tn), lambda l, j: (l, 0, j)),
        ],
        out_specs=pl.BlockSpec((None, rows, tn), lambda l, j: (l, 0, j)),
        out_shape=jax.ShapeDtypeStruct((depth, rows, six_d), F32),
        compiler_params=_cparams(("arbitrary", "arbitrary")),
        name="modulation",
    )(c_all, w_mod, b_mod.reshape(depth, 1, six_d))


def _norm_mod(x, g, shift, scale):
    ms = jnp.mean(x * x, axis=-1, keepdims=True)
    y = x * lax.rsqrt(ms + EPS) * g
    return y * (1.0 + scale) + shift


def _in_proj_kernel(x_ref, g_ref, shift_ref, scale_ref, w_ref, o_ref):
    h = _norm_mod(x_ref[...], g_ref[...], shift_ref[...], scale_ref[...])
    o_ref[...] = jnp.dot(h.astype(BF16), w_ref[...],
                         preferred_element_type=F32).astype(o_ref.dtype)


def _in_proj(x, g, shift, scale, w):
    bsz, n, d = x.shape
    f = w.shape[1]
    tm = min(512, n)
    vec = pl.BlockSpec((None, 1, d), lambda b, i: (b, 0, 0))
    return pl.pallas_call(
        _in_proj_kernel,
        grid=(bsz, n // tm),
        in_specs=[
            pl.BlockSpec((None, tm, d), lambda b, i: (b, i, 0)),
            pl.BlockSpec((1, d), lambda b, i: (0, 0)),
            vec, vec,
            pl.BlockSpec((d, f), lambda b, i: (0, 0)),
        ],
        out_specs=pl.BlockSpec((None, tm, f), lambda b, i: (b, i, 0)),
        out_shape=jax.ShapeDtypeStruct((bsz, n, f), BF16),
        compiler_params=_cparams(("parallel", "parallel")),
        name="in_proj",
    )(x, g, shift, scale, w)


RG_TILE = 256
RG_CG = 256
RG_HALO = SUBLANES


def _rglru_kernel(u_ref, yg_ref, cw_ref, cb_ref, wg_ref, bg_ref, lam_ref, h0_ref,
                  o_ref, hl_ref, u_scr, hf_scr):
    n = u_ref.shape[0]
    tt = min(RG_TILE, n)
    nt = n // tt
    cg = u_ref.shape[1]
    ext = tt + 2 * RG_HALO

    zeros = jnp.zeros((RG_HALO, cg), F32)
    u_scr[pl.ds(0, RG_HALO), :] = zeros
    u_scr[pl.ds(RG_HALO + n, RG_HALO), :] = zeros
    u_scr[pl.ds(RG_HALO, n), :] = u_ref[...].astype(F32)

    row = lax.broadcasted_iota(jnp.int32, (tt, 1), 0)
    cw = cw_ref[...]
    cb = cb_ref[...]

    def conv_tile(t0):
        xe = u_scr[pl.ds(t0, ext), :]
        acc = None
        for k in range(CONV_W):
            off = RG_HALO + k - CONV_LEFT
            xk = pltpu.roll(xe, ext - off, axis=0)[0:tt, :]
            term = cw[k:k + 1, :] * xk
            acc = term if acc is None else acc + term
        return acc + cb

    def gates(uc, d):
        ub = uc.astype(BF16)
        ra = jnp.dot(ub, wg_ref[d, 0], preferred_element_type=F32) + bg_ref[d, 0:1, :]
        rx = jnp.dot(ub, wg_ref[d, 1], preferred_element_type=F32) + bg_ref[d, 1:2, :]
        r = jax.nn.sigmoid(ra)
        i = jax.nn.sigmoid(rx)
        z = -lam_ref[d:d + 1, :]
        sp = jnp.maximum(z, 0.0) + jnp.log1p(jnp.exp(-jnp.abs(z)))
        a = jnp.exp(-RG_C * r * sp)
        b = jnp.sqrt(1.0 - a * a) * (i * uc)
        return a, b

    def scan_tile(a, b, reverse):
        s = 1
        while s < tt:
            if reverse:
                a_sh = pltpu.roll(a, tt - s, axis=0)
                b_sh = pltpu.roll(b, tt - s, axis=0)
                m = row < tt - s
            else:
                a_sh = pltpu.roll(a, s, axis=0)
                b_sh = pltpu.roll(b, s, axis=0)
                m = row >= s
            b = jnp.where(m, a * b_sh + b, b)
            a = jnp.where(m, a * a_sh, a)
            s *= 2
        return a, b

    def fwd_body(i, carry):
        t0 = pl.multiple_of(i * tt, tt)
        uc = conv_tile(t0)
        a, b = gates(uc, 0)
        a, b = scan_tile(a, b, False)
        h = b + a * carry
        hf_scr[pl.ds(t0, tt), :] = h
        return h[tt - 1:tt, :]

    hf_last = lax.fori_loop(0, nt, fwd_body, h0_ref[0:1, :])

    def bwd_body(i, carry):
        t0 = pl.multiple_of((nt - 1 - i) * tt, tt)
        uc = conv_tile(t0)
        a, b = gates(uc, 1)
        a, b = scan_tile(a, b, True)
        h = b + a * carry
        yg = yg_ref[pl.ds(t0, tt), :].astype(F32)
        o_ref[pl.ds(t0, tt), :] = (jax.nn.gelu(yg) * (hf_scr[pl.ds(t0, tt), :] + h)).astype(o_ref.dtype)
        return h[0:1, :]

    hb_last = lax.fori_loop(0, nt, bwd_body, h0_ref[1:2, :])
    hl_ref[0:1, :] = hf_last
    hl_ref[1:2, :] = hb_last


def _rglru(p, conv_w, conv_b, wg, bg, lam, h0):
    bsz, n, _ = p.shape
    rg_w = lam.shape[1]
    ncg = rg_w // RG_CG
    return pl.pallas_call(
        _rglru_kernel,
        grid=(bsz, ncg),
        in_specs=[
            pl.BlockSpec((None, n, RG_CG), lambda b, c: (b, 0, c)),
            pl.BlockSpec((None, n, RG_CG), lambda b, c: (b, 0, ncg + c)),
            pl.BlockSpec((CONV_W, RG_CG), lambda b, c: (0, c)),
            pl.BlockSpec((1, RG_CG), lambda b, c: (0, c)),
            pl.BlockSpec((2, 2, None, RG_CG, RG_CG), lambda b, c: (0, 0, c, 0, 0)),
            pl.BlockSpec((2, 2, RG_CG), lambda b, c: (0, 0, c)),
            pl.BlockSpec((2, RG_CG), lambda b, c: (0, c)),
            pl.BlockSpec((None, 2, RG_CG), lambda b, c: (b, 0, c)),
        ],
        out_specs=[
            pl.BlockSpec((None, n, RG_CG), lambda b, c: (b, 0, c)),
            pl.BlockSpec((None, 2, RG_CG), lambda b, c: (b, 0, c)),
        ],
        out_shape=[
            jax.ShapeDtypeStruct((bsz, n, rg_w), BF16),
            jax.ShapeDtypeStruct((bsz, 2, rg_w), F32),
        ],
        scratch_shapes=[
            pltpu.VMEM((n + 2 * RG_HALO, RG_CG), F32),
            pltpu.VMEM((n, RG_CG), F32),
        ],
        compiler_params=_cparams(("parallel", "parallel")),
        name="rglru",
    )(p, p, conv_w, conv_b.reshape(1, rg_w), wg, bg, lam, h0)


def _retention_kernel(lg_ref, q_ref, k_ref, v_ref, g_ref, cs_ref, sn_ref, s0_ref,
                      o_ref, so_ref, qr_scr, kr_scr, r_scr):
    n, hd = q_ref.shape
    L = RET_CHUNK
    nc = n // L
    head = pl.program_id(1)
    lgf = lg_ref[0, head]
    lgb = lg_ref[1, head]

    def rope(t):
        return t * cs_ref[...] + pltpu.roll(t, hd // 2, axis=1) * sn_ref[...]

    qr_scr[...] = rope(q_ref[...].astype(F32)).astype(BF16)
    kr_scr[...] = (rope(k_ref[...].astype(F32)) * (hd ** -0.5)).astype(BF16)

    jc = lax.broadcasted_iota(jnp.int32, (L, 1), 0).astype(F32)
    q_dec_f = jnp.exp((jc + 1.0) * lgf)
    q_dec_b = jnp.exp((L - jc) * lgb)
    k_dec_f = jnp.exp((L - 1.0 - jc) * lgf)
    k_dec_b = jnp.exp(jc * lgb)
    ones = jnp.ones((1, hd), F32)
    chunk_f = jnp.exp(ones * (L * lgf))
    chunk_b = jnp.exp(ones * (L * lgb))
    ii = lax.broadcasted_iota(jnp.int32, (L, L), 0)
    jj = lax.broadcasted_iota(jnp.int32, (L, L), 1)
    diff = (ii - jj).astype(F32)
    dmat = jnp.where(diff >= 0.0, jnp.exp(jnp.maximum(diff, 0.0) * lgf),
                     jnp.exp(jnp.maximum(-diff, 0.0) * lgb))

    tn_dims = (((0,), (0,)), ((), ()))
    nt_dims = (((1,), (1,)), ((), ()))

    def bwd_body(i, r_state):
        c = nc - 1 - i
        r_scr[c] = r_state
        t0 = pl.multiple_of(c * L, L)
        kd = (kr_scr[pl.ds(t0, L), :].astype(F32) * k_dec_b).astype(BF16)
        upd = lax.dot_general(kd, v_ref[pl.ds(t0, L), :], tn_dims, preferred_element_type=F32)
        return chunk_b * r_state + upd

    so_ref[1] = lax.fori_loop(0, nc, bwd_body, s0_ref[1])

    def fwd_body(c, f_state):
        t0 = pl.multiple_of(c * L, L)
        qc = qr_scr[pl.ds(t0, L), :]
        kc = kr_scr[pl.ds(t0, L), :]
        vc = v_ref[pl.ds(t0, L), :]
        s = lax.dot_general(qc, kc, nt_dims, preferred_element_type=F32)
        o = jnp.dot((s * dmat).astype(BF16), vc, preferred_element_type=F32)
        o = o + jnp.dot(qc, f_state.astype(BF16), preferred_element_type=F32) * q_dec_f
        o = o + jnp.dot(qc, r_scr[c].astype(BF16), preferred_element_type=F32) * q_dec_b
        mu = jnp.mean(o, axis=-1, keepdims=True)
        oc = o - mu
        var = jnp.mean(oc * oc, axis=-1, keepdims=True)
        gate = g_ref[pl.ds(t0, L), :].astype(F32)
        o_ref[pl.ds(t0, L), :] = (gate * jax.nn.sigmoid(gate) * (oc * lax.rsqrt(var + EPS))).astype(o_ref.dtype)
        kd = (kc.astype(F32) * k_dec_f).astype(BF16)
        upd = lax.dot_general(kd, vc, tn_dims, preferred_element_type=F32)
        return chunk_f * f_state + upd

    so_ref[0] = lax.fori_loop(0, nc, fwd_body, s0_ref[0])


def _retention(p, log_g, cs, sn, s0, col0):
    bsz, n, _ = p.shape
    nh = log_g.shape[1]
    hd = cs.shape[1]
    cb0 = col0 // hd

    def col(which):
        return pl.BlockSpec((None, n, hd), lambda b, h, lg: (b, 0, cb0 + which * nh + h))

    state = pl.BlockSpec((None, 2, None, hd, hd), lambda b, h, lg: (b, 0, h, 0, 0))
    table = pl.BlockSpec((n, hd), lambda b, h, lg: (0, 0))
    grid_spec = pltpu.PrefetchScalarGridSpec(
        num_scalar_prefetch=1,
        grid=(bsz, nh),
        in_specs=[col(0), col(1), col(2), col(3), table, table, state],
        out_specs=[pl.BlockSpec((None, n, hd), lambda b, h, lg: (b, 0, h)), state],
        scratch_shapes=[
            pltpu.VMEM((n, hd), BF16),
            pltpu.VMEM((n, hd), BF16),
            pltpu.VMEM((n // RET_CHUNK, hd, hd), F32),
        ],
    )
    return pl.pallas_call(
        _retention_kernel,
        grid_spec=grid_spec,
        out_shape=[
            jax.ShapeDtypeStruct((bsz, n, nh * hd), BF16),
            jax.ShapeDtypeStruct((bsz, 2, nh, hd, hd), F32),
        ],
        compiler_params=_cparams(("parallel", "parallel")),
        name="retention",
    )(log_g, p, p, p, p, cs, sn, s0)


R_RANK0, R_RANK1, R_E0, R_E1, R_W0, R_W1 = range(6)


def _out_proj_kernel(route, x_ref, rg_ref, ret_ref, wo_ref, gate_ref, g_ref, shift_ref, scale_ref,
                     *rest):
    if route:
        wr2_ref, br_ref, xo_ref, h_ref, rt_ref, cnt_ref, carry_scr = rest
    else:
        xo_ref, h_ref = rest
    rg_w = rg_ref.shape[1]
    y = jnp.dot(rg_ref[...], wo_ref[0:rg_w, :], preferred_element_type=F32)
    y = y + jnp.dot(ret_ref[...], wo_ref[rg_w:, :], preferred_element_type=F32)
    xn = x_ref[...] + gate_ref[...] * y
    xo_ref[...] = xn
    hx = _norm_mod(xn, g_ref[...], shift_ref[...], scale_ref[...])
    if not route:
        h_ref[...] = hx.astype(h_ref.dtype)
        return

    tm, d = hx.shape
    for s in range(d // LANES):
        h_ref[:, s, :] = hx[:, s * LANES:(s + 1) * LANES]

    h_hi = hx.astype(BF16)
    h_lo = (hx - h_hi.astype(F32)).astype(BF16)
    part = jnp.dot(h_hi, wr2_ref[...], preferred_element_type=F32)
    logits = (part[:, :LANES] + part[:, LANES:]
              + jnp.dot(h_lo, wr2_ref[:, :LANES], preferred_element_type=F32) + br_ref[...])
    lane_i = lax.broadcasted_iota(jnp.int32, logits.shape, 1)
    lane = lane_i.astype(F32)
    m0 = jnp.max(logits, axis=-1, keepdims=True)
    i0 = jnp.min(jnp.where(logits == m0, lane, float(LANES)), axis=-1, keepdims=True)
    rest_l = jnp.where(lane == i0, NEG_BIG, logits)
    m1 = jnp.max(rest_l, axis=-1, keepdims=True)
    i1 = jnp.min(jnp.where(rest_l == m1, lane, float(LANES)), axis=-1, keepdims=True)
    e = jnp.exp(m1 - m0)
    w0 = 1.0 / (1.0 + e)
    w1 = e / (1.0 + e)

    @pl.when((pl.program_id(0) == 0) & (pl.program_id(1) == 0))
    def _():
        carry_scr[...] = jnp.zeros_like(carry_scr)

    sel0 = lane == i0
    sel1 = lane == i1
    mask = jnp.where(sel0 | sel1, 1.0, 0.0)
    ri = lax.broadcasted_iota(jnp.int32, (tm, tm), 0)
    ci = lax.broadcasted_iota(jnp.int32, (tm, tm), 1)
    tri = jnp.where(ri > ci, 1.0, 0.0).astype(BF16)
    before = jnp.dot(tri, mask.astype(BF16), preferred_element_type=F32) + carry_scr[...]
    rank0 = jnp.sum(jnp.where(sel0, before, 0.0), axis=-1, keepdims=True)
    rank1 = jnp.sum(jnp.where(sel1, before, 0.0), axis=-1, keepdims=True)
    total = carry_scr[...] + jnp.sum(mask, axis=0, keepdims=True)
    carry_scr[...] = total
    cnt_ref[...] = total
    rec = jnp.zeros_like(logits)
    for idx, val in ((R_RANK0, rank0), (R_RANK1, rank1), (R_E0, i0), (R_E1, i1),
                     (R_W0, w0), (R_W1, w1)):
        rec = jnp.where(lane_i == idx, val, rec)
    rt_ref[...] = rec


def _out_proj(x, rg, ret, w_out, gate, g, shift, scale, router=None):
    bsz, n, d = x.shape
    tm = min(512, n)
    route = router is not None
    vec = pl.BlockSpec((None, 1, d), lambda b, i: (b, 0, 0))
    tile = lambda w: pl.BlockSpec((None, tm, w), lambda b, i: (b, i, 0))
    in_specs = [
        tile(d), tile(rg.shape[2]), tile(ret.shape[2]),
        pl.BlockSpec(w_out.shape, lambda b, i: (0, 0)),
        vec, pl.BlockSpec((1, d), lambda b, i: (0, 0)), vec, vec,
    ]
    args = [x, rg, ret, w_out, gate, g, shift, scale]
    scratch = []
    if route:
        in_specs += [pl.BlockSpec((d, 2 * LANES), lambda b, i: (0, 0)),
                     pl.BlockSpec((1, LANES), lambda b, i: (0, 0))]
        args += list(router)
        out_specs = [tile(d),
                     pl.BlockSpec((None, tm, d // LANES, LANES), lambda b, i: (b, i, 0, 0)),
                     tile(LANES),
                     pl.BlockSpec((1, LANES), lambda b, i: (0, 0))]
        out_shape = [jax.ShapeDtypeStruct((bsz, n, d), F32),
                     jax.ShapeDtypeStruct((bsz, n, d // LANES, LANES), F32),
                     jax.ShapeDtypeStruct((bsz, n, LANES), F32),
                     jax.ShapeDtypeStruct((1, LANES), F32)]
        scratch = [pltpu.VMEM((1, LANES), F32)]
        sem = ("arbitrary", "arbitrary")
    else:
        out_specs = [tile(d), tile(d)]
        out_shape = [jax.ShapeDtypeStruct((bsz, n, d), F32), jax.ShapeDtypeStruct((bsz, n, d), BF16)]
        sem = ("parallel", "parallel")
    return pl.pallas_call(
        functools.partial(_out_proj_kernel, route),
        grid=(bsz, n // tm),
        in_specs=in_specs,
        out_specs=out_specs,
        out_shape=out_shape,
        scratch_shapes=scratch,
        compiler_params=_cparams(sem),
        name="out_proj_route" if route else "out_proj",
    )(*args)


def _swiglu_acc(h, w1_ref, w3_ref, w2_ref, row_scale=None):
    a = jnp.dot(h, w1_ref[...], preferred_element_type=F32)
    b = jnp.dot(h, w3_ref[...], preferred_element_type=F32)
    act = a * jax.nn.sigmoid(a) * b
    if row_scale is not None:
        act = act * row_scale
    return jnp.dot(act.astype(BF16), w2_ref[...], preferred_element_type=F32)


def _ffn_kernel(h_ref, x_ref, gate_ref, w1_ref, w3_ref, w2_ref, o_ref, acc_ref):
    f = pl.program_id(2)

    @pl.when(f == 0)
    def _():
        acc_ref[...] = jnp.zeros_like(acc_ref)

    acc_ref[...] += _swiglu_acc(h_ref[...], w1_ref, w3_ref, w2_ref)

    @pl.when(f == pl.num_programs(2) - 1)
    def _():
        o_ref[...] = x_ref[...] + gate_ref[...] * acc_ref[...]


def _ffn(h, x, gate, w1, w3, w2):
    bsz, n, d = x.shape
    f = w1.shape[1]
    tm = min(1024, n)
    tf = 1408 if f % 1408 == 0 else f
    tile = pl.BlockSpec((None, tm, d), lambda b, i, j: (b, i, 0))
    return pl.pallas_call(
        _ffn_kernel,
        grid=(bsz, n // tm, f // tf),
        in_specs=[
            tile, tile,
            pl.BlockSpec((None, 1, d), lambda b, i, j: (b, 0, 0)),
            pl.BlockSpec((d, tf), lambda b, i, j: (0, j)),
            pl.BlockSpec((d, tf), lambda b, i, j: (0, j)),
            pl.BlockSpec((tf, d), lambda b, i, j: (j, 0)),
        ],
        out_specs=tile,
        out_shape=jax.ShapeDtypeStruct((bsz, n, d), F32),
        scratch_shapes=[pltpu.VMEM((tm, d), F32)],
        compiler_params=_cparams(("parallel", "parallel", "arbitrary")),
        name="ffn",
    )(h, x, gate, w1, w3, w2)


MOE_BLK = 512
MOE_TF = 896


def _dispatch_kernel(d0_ref, d1_ref, hx_ref, init_ref, hs_ref, sem):
    del init_ref
    tm = hx_ref.shape[0]
    base = pl.program_id(0) * tm

    def issue(t, carry):
        pltpu.make_async_copy(hx_ref.at[t], hs_ref.at[d0_ref[base + t]], sem).start()
        pltpu.make_async_copy(hx_ref.at[t], hs_ref.at[d1_ref[base + t]], sem).start()
        return carry

    lax.fori_loop(0, tm, issue, 0, unroll=8)
    for _ in range(2):
        pltpu.make_async_copy(hx_ref, hs_ref.at[pl.ds(0, tm)], sem).wait()


def _dispatch(dest0, dest1, hx, n_rows):
    n_tok, s, lanes = hx.shape
    tm = 512
    grid_spec = pltpu.PrefetchScalarGridSpec(
        num_scalar_prefetch=2,
        grid=(n_tok // tm,),
        in_specs=[pl.BlockSpec((tm, s, lanes), lambda i, d0, d1: (i, 0, 0)),
                  pl.BlockSpec(memory_space=pl.ANY)],
        out_specs=pl.BlockSpec(memory_space=pl.ANY),
        scratch_shapes=[pltpu.SemaphoreType.DMA(())],
    )
    return pl.pallas_call(
        _dispatch_kernel,
        grid_spec=grid_spec,
        out_shape=jax.ShapeDtypeStruct((n_rows, s, lanes), F32),
        input_output_aliases={3: 0},
        compiler_params=_cparams(("arbitrary",)),
        name="moe_dispatch",
    )(dest0, dest1, hx, jnp.zeros((n_rows, s, lanes), F32))


def _experts_kernel(be_ref, nu_ref, hs_ref, w1_ref, w3_ref, w2_ref, y_ref, hb_scr, acc_scr):
    j = pl.program_id(0)
    f = pl.program_id(1)
    ns = hs_ref.shape[1]

    @pl.when(j < nu_ref[0])
    def _():
        @pl.when(f == 0)
        def _():
            for s in range(ns):
                hb_scr[:, s * LANES:(s + 1) * LANES] = hs_ref[:, s, :].astype(BF16)
            acc_scr[...] = jnp.zeros_like(acc_scr)

        acc_scr[...] += _swiglu_acc(hb_scr[...], w1_ref, w3_ref, w2_ref)

        @pl.when(f == pl.num_programs(1) - 1)
        def _():
            for s in range(ns):
                y_ref[:, s, :] = acc_scr[:, s * LANES:(s + 1) * LANES]

    @pl.when((j >= nu_ref[0]) & (f == 0))
    def _():
        y_ref[...] = jnp.zeros_like(y_ref)


def _experts(block_expert, n_used, hs, w1, w3, w2):
    n_rows, s, lanes = hs.shape
    d = s * lanes
    ne, _, f = w1.shape
    nf = f // MOE_TF
    n_blocks = n_rows // MOE_BLK

    def row_map(j, f_, be, nu):
        return (jnp.minimum(j, nu[0] - 1), 0, 0)

    def f_idx(j, f_, nu):
        return jnp.where(j < nu[0], f_, nf - 1)

    grid_spec = pltpu.PrefetchScalarGridSpec(
        num_scalar_prefetch=2,
        grid=(n_blocks, nf),
        in_specs=[
            pl.BlockSpec((MOE_BLK, s, lanes), row_map),
            pl.BlockSpec((None, d, MOE_TF), lambda j, f_, be, nu: (be[j], 0, f_idx(j, f_, nu))),
            pl.BlockSpec((None, d, MOE_TF), lambda j, f_, be, nu: (be[j], 0, f_idx(j, f_, nu))),
            pl.BlockSpec((None, MOE_TF, d), lambda j, f_, be, nu: (be[j], f_idx(j, f_, nu), 0)),
        ],
        out_specs=pl.BlockSpec((MOE_BLK, s, lanes), lambda j, f_, be, nu: (j, 0, 0)),
        scratch_shapes=[pltpu.VMEM((MOE_BLK, d), BF16), pltpu.VMEM((MOE_BLK, d), F32)],
    )
    return pl.pallas_call(
        _experts_kernel,
        grid_spec=grid_spec,
        out_shape=jax.ShapeDtypeStruct((n_rows, s, lanes), F32),
        compiler_params=_cparams(("arbitrary", "arbitrary")),
        name="moe_experts",
    )(block_expert, n_used, hs, w1, w3, w2)


def _combine_kernel(d0_ref, d1_ref, x_ref, gate_ref, rt_ref, gfin_ref, y_ref, o_ref, ybuf, sem):
    tm, d = x_ref.shape
    base = (pl.program_id(0) * pl.num_programs(1) + pl.program_id(1)) * tm

    def issue(t, carry):
        pltpu.make_async_copy(y_ref.at[d0_ref[base + t]], ybuf.at[0, t], sem).start()
        pltpu.make_async_copy(y_ref.at[d1_ref[base + t]], ybuf.at[1, t], sem).start()
        return carry

    lax.fori_loop(0, tm, issue, 0, unroll=8)
    for k in range(2):
        pltpu.make_async_copy(y_ref.at[pl.ds(0, tm)], ybuf.at[k], sem).wait()

    rt = rt_ref[...]
    w0 = rt[:, R_W0:R_W0 + 1]
    w1 = rt[:, R_W1:R_W1 + 1]
    ss = jnp.zeros((tm, 1), F32)
    for s in range(d // LANES):
        cols = slice(s * LANES, (s + 1) * LANES)
        moe = w0 * ybuf[0, :, s, :] + w1 * ybuf[1, :, s, :]
        xn = x_ref[:, cols] + gate_ref[:, cols] * moe
        ss = ss + jnp.sum(xn * xn, axis=-1, keepdims=True)
        o_ref[:, cols] = xn
    o_ref[...] = o_ref[...] * lax.rsqrt(ss / d + EPS) * gfin_ref[...]


def _combine(dest0, dest1, x, gate, route, g_final, y):
    bsz, n, d = x.shape
    _, s, lanes = y.shape
    tm = 512
    grid_spec = pltpu.PrefetchScalarGridSpec(
        num_scalar_prefetch=2,
        grid=(bsz, n // tm),
        in_specs=[
            pl.BlockSpec((None, tm, d), lambda b, i, d0, d1: (b, i, 0)),
            pl.BlockSpec((None, 1, d), lambda b, i, d0, d1: (b, 0, 0)),
            pl.BlockSpec((None, tm, LANES), lambda b, i, d0, d1: (b, i, 0)),
            pl.BlockSpec((1, d), lambda b, i, d0, d1: (0, 0)),
            pl.BlockSpec(memory_space=pl.ANY),
        ],
        out_specs=pl.BlockSpec((None, tm, d), lambda b, i, d0, d1: (b, i, 0)),
        scratch_shapes=[pltpu.VMEM((2, tm, s, lanes), F32), pltpu.SemaphoreType.DMA(())],
    )
    return pl.pallas_call(
        _combine_kernel,
        grid_spec=grid_spec,
        out_shape=jax.ShapeDtypeStruct((bsz, n, d), F32),
        compiler_params=_cparams(("arbitrary", "arbitrary")),
        name="moe_combine",
    )(dest0, dest1, x, gate, route, g_final, y)


def _moe(hx, x, gate, route, counts, g_final, w1, w3, w2):
    bsz, n, d = x.shape
    ne = w1.shape[0]
    n_tok = bsz * n
    n_rows = -(-(2 * n_tok) // MOE_BLK) * MOE_BLK + ne * MOE_BLK
    n_blocks = n_rows // MOE_BLK
    cnt = counts[0, :ne].astype(jnp.int32)
    padded = (cnt + MOE_BLK - 1) // MOE_BLK * MOE_BLK
    pad_end = jnp.cumsum(padded)
    pad_start = pad_end - padded
    n_used = pad_end[-1] // MOE_BLK
    blk = jnp.minimum(jnp.arange(n_blocks, dtype=jnp.int32), n_used - 1)
    block_expert = jnp.minimum(
        jnp.searchsorted(pad_end, blk * MOE_BLK, side='right'), ne - 1).astype(jnp.int32)
    rt = route.reshape(n_tok, LANES)
    e0 = rt[:, R_E0].astype(jnp.int32)
    e1 = rt[:, R_E1].astype(jnp.int32)
    dest0 = pad_start[e0] + rt[:, R_RANK0].astype(jnp.int32)
    dest1 = pad_start[e1] + rt[:, R_RANK1].astype(jnp.int32)

    hs = _dispatch(dest0, dest1, hx.reshape(n_tok, d // LANES, LANES), n_rows)
    y = _experts(block_expert, n_used.reshape(1).astype(jnp.int32), hs, w1, w3, w2)
    return _combine(dest0, dest1, x, gate, route, g_final, y)


def _rope_tables(n, hd):
    rows = n // GRID_W
    row = jnp.repeat(jnp.arange(rows, dtype=F32), GRID_W)
    col = jnp.tile(jnp.arange(GRID_W, dtype=F32), rows)
    n_freq = hd // 4
    inv = ROPE_BASE ** (-jnp.arange(n_freq, dtype=F32) / n_freq)
    ang = jnp.concatenate([row[:, None] * inv, col[:, None] * inv], axis=-1)
    cos, sin = jnp.cos(ang), jnp.sin(ang)
    return jnp.concatenate([cos, cos], axis=-1), jnp.concatenate([-sin, sin], axis=-1)


def _block_diag_gates(rg_wa, rg_wx):
    w = jnp.stack([rg_wa, rg_wx], axis=1)
    nd, ng, nk, c, _ = w.shape
    per = RG_CG // c
    w = w.reshape(nd, ng, nk // per, per, c, c)
    eye = jnp.eye(per, dtype=w.dtype)
    full = jnp.einsum('dgmpij,pq->dgmpiqj', w, eye)
    return full.reshape(nd, ng, nk // per, RG_CG, RG_CG).astype(BF16)


def _mixer(p, rope, states, conv_w, conv_b, wg, bg, lam, log_g, rg_w):
    rg_out, h_last = _rglru(p, conv_w, conv_b, wg, bg, lam, states[0])
    ret_out, s_last = _retention(p, log_g, rope[0], rope[1], states[1], 2 * rg_w)
    return rg_out, ret_out, (h_last, s_last)


def kernel(x, c, ctx, c_ctx, w_mod, b_mod, g_mix, g_ffn, g_final, w_in, w_out, conv_w, conv_b,
           rg_wa, rg_ba, rg_wx, rg_bx, rg_lam, ret_decay, ffn_w1, ffn_w3, ffn_w2,
           moe_router, moe_router_b, moe_w1, moe_w3, moe_w2):
    bsz, n_lat, d = x.shape
    n_ctx = ctx.shape[1]
    depth = w_mod.shape[0]
    rg_w = rg_lam.shape[2]
    nh = ret_decay.shape[2]
    hd = (w_out.shape[1] - rg_w) // nh
    ne = moe_router.shape[2]
    assert depth == 2, "kernel is written for the two-layer block (dense FFN, then MoE)"

    rows = -(-(bsz + 1) // SUBLANES) * SUBLANES
    c_all = jnp.zeros((rows, d), F32).at[:bsz].set(c).at[bsz].set(c_ctx)
    mod = _modulation(c_all, w_mod, b_mod)
    mod = mod.reshape(depth, rows, N_MOD, d)
    mod_lat = mod[:, :bsz, None]
    mod_ctx = jnp.broadcast_to(mod[:, bsz:bsz + 1, None], mod_lat.shape)

    rope_lat = _rope_tables(n_lat, hd)
    rope_ctx = (jnp.ones((n_ctx, hd), F32), jnp.zeros((n_ctx, hd), F32))
    log_g = jax.nn.log_sigmoid(ret_decay.astype(F32))
    zero_states = (jnp.zeros((bsz, 2, rg_w), F32), jnp.zeros((bsz, 2, nh, hd, hd), F32))

    for l in range(depth):
        last = l == depth - 1
        ml = [mod_lat[l, :, :, j] for j in range(N_MOD)]
        mc = [mod_ctx[l, :, :, j] for j in range(N_MOD)]
        g_m = g_mix[l].reshape(1, d)
        g_f = g_ffn[l].reshape(1, d)
        w_in_l = w_in[l].astype(BF16)
        w_out_l = w_out[l].astype(BF16)
        wg = _block_diag_gates(rg_wa[l], rg_wx[l])
        bg = jnp.stack([rg_ba[l], rg_bx[l]], axis=1)
        mix_p = (conv_w[l], conv_b[l], wg, bg, rg_lam[l], log_g[l], rg_w)

        pc = _in_proj(ctx, g_m, mc[0], mc[1], w_in_l)
        rg_c, ret_c, ctx_states = _mixer(pc, rope_ctx, zero_states, *mix_p)
        px = _in_proj(x, g_m, ml[0], ml[1], w_in_l)
        rg_x, ret_x, _ = _mixer(px, rope_lat, ctx_states, *mix_p)

        if l % 2 == 0:
            i = l // 2
            w1, w3, w2 = ffn_w1[i].astype(BF16), ffn_w3[i].astype(BF16), ffn_w2[i].astype(BF16)
            x, hx = _out_proj(x, rg_x, ret_x, w_out_l, ml[2], g_f, ml[3], ml[4])
            x = _ffn(hx, x, ml[5], w1, w3, w2)
            if not last:
                ctx, hc = _out_proj(ctx, rg_c, ret_c, w_out_l, mc[2], g_f, mc[3], mc[4])
                ctx = _ffn(hc, ctx, mc[5], w1, w3, w2)
        else:
            i = l // 2
            wr = jnp.zeros((d, LANES), F32).at[:, :ne].set(moe_router[i])
            wr_hi = wr.astype(BF16)
            wr_lo = (wr - wr_hi.astype(F32)).astype(BF16)
            wr2 = jnp.concatenate([wr_hi, wr_lo], axis=1)
            br = jnp.full((1, LANES), NEG_BIG, F32).at[0, :ne].set(moe_router_b[i])
            x, hx, route, counts = _out_proj(x, rg_x, ret_x, w_out_l, ml[2], g_f, ml[3], ml[4],
                                             router=(wr2, br))
            x = _moe(hx, x, ml[5], route, counts, g_final.reshape(1, d),
                     moe_w1[i].astype(BF16), moe_w3[i].astype(BF16), moe_w2[i].astype(BF16))
    return x
```

```python
import functools

import jax
import jax.numpy as jnp
from jax import lax
from jax.experimental import pallas as pl
from jax.experimental.pallas import tpu as pltpu

F32 = jnp.float32
BF16 = jnp.bfloat16

EPS = 1e-6
RG_C = 8.0
CONV_W = 4
CONV_LEFT = 2
RG_BLOCKS = 8
RET_HEADS = 4
RET_CHUNK = 128
ROPE_BASE = 10000.0
GRID_W = 64
N_EXPERTS = 8
N_MOD = 6

LANES = 128
SUBLANES = 8
VMEM_LIMIT = 56 * 1024 * 1024
NEG_BIG = -1e30
LOG2_E = 1.4426950408889634


def _cparams(sem):
    return pltpu.CompilerParams(dimension_semantics=sem, vmem_limit_bytes=VMEM_LIMIT)


def _modulation_kernel(c_ref, w_ref, b_ref, o_ref):
    c = c_ref[...]
    s = c * jax.nn.sigmoid(c)
    o_ref[...] = jnp.dot(s, w_ref[...], precision=lax.Precision.HIGHEST,
                         preferred_element_type=F32) + b_ref[...]


def _modulation(c_all, w_mod, b_mod):
    depth, d, six_d = w_mod.shape
    rows = c_all.shape[0]
    tn = 1536
    return pl.pallas_call(
        _modulation_kernel,
        grid=(depth, six_d // tn),
        in_specs=[
            pl.BlockSpec((rows, d), lambda l, j: (0, 0)),
            pl.BlockSpec((None, d, tn), lambda l, j: (l, 0, j)),
            pl.BlockSpec((None, 1, tn), lambda l, j: (l, 0, j)),
        ],
        out_specs=pl.BlockSpec((None, rows, tn), lambda l, j: (l, 0, j)),
        out_shape=jax.ShapeDtypeStruct((depth, rows, six_d), F32),
        compiler_params=_cparams(("arbitrary", "arbitrary")),
        name="modulation",
    )(c_all, w_mod, b_mod.reshape(depth, 1, six_d))


def _norm_mod(x, g, shift, scale):
    ms = jnp.mean(x * x, axis=-1, keepdims=True)
    y = x * lax.rsqrt(ms + EPS) * g
    return y * (1.0 + scale) + shift


def _in_proj_kernel(x_ref, g_ref, shift_ref, scale_ref, w_ref, o_ref):
    h = _norm_mod(x_ref[...], g_ref[...], shift_ref[...], scale_ref[...])
    o_ref[...] = jnp.dot(h.astype(BF16), w_ref[...],
                         preferred_element_type=F32).astype(o_ref.dtype)


def _in_proj(x, g, shift, scale, w):
    bsz, n, d = x.shape
    f = w.shape[1]
    tm = min(512, n)
    vec = pl.BlockSpec((None, 1, d), lambda b, i: (b, 0, 0))
    return pl.pallas_call(
        _in_proj_kernel,
        grid=(bsz, n // tm),
        in_specs=[
            pl.BlockSpec((None, tm, d), lambda b, i: (b, i, 0)),
            pl.BlockSpec((1, d), lambda b, i: (0, 0)),
            vec, vec,
            pl.BlockSpec((d, f), lambda b, i: (0, 0)),
        ],
        out_specs=pl.BlockSpec((None, tm, f), lambda b, i: (b, i, 0)),
        out_shape=jax.ShapeDtypeStruct((bsz, n, f), BF16),
        compiler_params=_cparams(("parallel", "parallel")),
        name="in_proj",
    )(x, g, shift, scale, w)


RG_TILE = 256
RG_CG = 256
RG_HALO = SUBLANES


def _rglru_kernel(u_ref, yg_ref, cw_ref, cb_ref, wg_ref, bg_ref, lam_ref, h0_ref,
                  o_ref, hl_ref, u_scr, hf_scr, nat_scr):
    n = u_ref.shape[0]
    tt = min(RG_TILE, n)
    nt = n // tt
    cg = u_ref.shape[1]

    zeros = jnp.zeros((RG_HALO, LANES), F32)
    for g in range(cg // LANES):
        u_scr[g, pl.ds(0, RG_HALO), :] = zeros
        u_scr[g, pl.ds(RG_HALO + n, RG_HALO), :] = zeros
        u_scr[g, pl.ds(RG_HALO, n), :] = u_ref[:, g * LANES:(g + 1) * LANES].astype(F32)

    seg = tt // SUBLANES
    lane_cols = [slice(g * LANES, (g + 1) * LANES) for g in range(cg // LANES)]
    cw_rows = [[jnp.broadcast_to(cw_ref[k:k + 1, c], (SUBLANES, LANES)) for c in lane_cols]
               for k in range(CONV_W)]
    cb_rows = [jnp.broadcast_to(cb_ref[:, c], (SUBLANES, LANES)) for c in lane_cols]

    def conv_tile(t0):
        blocks = []
        for j in range(seg):
            lane_groups = []
            for g in range(cg // LANES):
                acc = cb_rows[g]
                for k in range(CONV_W):
                    start = t0 + (RG_HALO + j + k - CONV_LEFT)
                    acc = acc + cw_rows[k][g] * u_scr[g, pl.ds(start, SUBLANES, stride=seg), :]
                lane_groups.append(acc)
            blocks.append(jnp.concatenate(lane_groups, axis=1))
        return jnp.concatenate(blocks, axis=0)

    def gates(uc, d):
        ub = uc.astype(BF16)
        ta = jnp.tanh(jnp.dot(ub, wg_ref[d, 0], preferred_element_type=F32) + bg_ref[d, 0:1, :])
        ti = jnp.tanh(jnp.dot(ub, wg_ref[d, 1], preferred_element_type=F32) + bg_ref[d, 1:2, :])
        z = -lam_ref[d:d + 1, :]
        sp = jnp.maximum(z, 0.0) + jnp.log1p(jnp.exp(-jnp.abs(z)))
        c2 = (-0.5 * RG_C * LOG2_E) * sp
        a = jnp.exp2(c2 + c2 * ta)
        b = jnp.exp(0.5 * jnp.log(1.0 - a * a)) * ((0.5 + 0.5 * ti) * uc)
        return a, b

    def scan_tile(a, b, carry, reverse):
        steps = range(seg - 1, -1, -1) if reverse else range(seg)
        h = acum = None
        h_loc = [None] * seg
        a_cum = [None] * seg
        for j in steps:
            rows = slice(j * SUBLANES, (j + 1) * SUBLANES)
            h = b[rows] if h is None else a[rows] * h + b[rows]
            acum = a[rows] if acum is None else a[rows] * acum
            h_loc[j], a_cum[j] = h, acum
        order = range(SUBLANES - 1, -1, -1) if reverse else range(SUBLANES)
        c = carry
        c_in = [None] * SUBLANES
        for s in order:
            c_in[s] = c
            c = h[s:s + 1, :] + acum[s:s + 1, :] * c
        c_in = jnp.concatenate(c_in, axis=0)
        out = jnp.concatenate([h_loc[j] + a_cum[j] * c_in for j in range(seg)], axis=0)
        return out, c

    def fwd_body(i, carry):
        t0 = pl.multiple_of(i * tt, tt)
        uc = conv_tile(t0)
        a, b = gates(uc, 0)
        h, carry = scan_tile(a, b, carry, False)
        hf_scr[pl.ds(t0, tt), :] = h
        return carry

    hf_last = lax.fori_loop(0, nt, fwd_body, h0_ref[0:1, :])

    def bwd_body(i, carry):
        t0 = pl.multiple_of((nt - 1 - i) * tt, tt)
        uc = conv_tile(t0)
        a, b = gates(uc, 1)
        h, carry = scan_tile(a, b, carry, True)
        hsum = hf_scr[pl.ds(t0, tt), :] + h
        for j in range(seg):
            for g in range(cg // LANES):
                nat_scr[g, pl.ds(j, SUBLANES, stride=seg), :] = (
                    hsum[j * SUBLANES:(j + 1) * SUBLANES, g * LANES:(g + 1) * LANES])
        hnat = jnp.concatenate([nat_scr[g] for g in range(cg // LANES)], axis=1)
        yg = yg_ref[pl.ds(t0, tt), :].astype(F32)
        o_ref[pl.ds(t0, tt), :] = (jax.nn.gelu(yg) * hnat).astype(o_ref.dtype)
        return carry

    hb_last = lax.fori_loop(0, nt, bwd_body, h0_ref[1:2, :])
    hl_ref[0:1, :] = hf_last
    hl_ref[1:2, :] = hb_last


def _rglru(p, conv_w, conv_b, wg, bg, lam, h0):
    bsz, n, _ = p.shape
    rg_w = lam.shape[1]
    ncg = rg_w // RG_CG
    return pl.pallas_call(
        _rglru_kernel,
        grid=(bsz, ncg),
        in_specs=[
            pl.BlockSpec((None, n, RG_CG), lambda b, c: (b, 0, c)),
            pl.BlockSpec((None, n, RG_CG), lambda b, c: (b, 0, ncg + c)),
            pl.BlockSpec((CONV_W, RG_CG), lambda b, c: (0, c)),
            pl.BlockSpec((1, RG_CG), lambda b, c: (0, c)),
            pl.BlockSpec((2, 2, None, RG_CG, RG_CG), lambda b, c: (0, 0, c, 0, 0)),
            pl.BlockSpec((2, 2, RG_CG), lambda b, c: (0, 0, c)),
            pl.BlockSpec((2, RG_CG), lambda b, c: (0, c)),
            pl.BlockSpec((None, 2, RG_CG), lambda b, c: (b, 0, c)),
        ],
        out_specs=[
            pl.BlockSpec((None, n, RG_CG), lambda b, c: (b, 0, c)),
            pl.BlockSpec((None, 2, RG_CG), lambda b, c: (b, 0, c)),
        ],
        out_shape=[
            jax.ShapeDtypeStruct((bsz, n, rg_w), BF16),
            jax.ShapeDtypeStruct((bsz, 2, rg_w), F32),
        ],
        scratch_shapes=[
            pltpu.VMEM((RG_CG // LANES, n + 2 * RG_HALO, LANES), F32),
            pltpu.VMEM((n, RG_CG), F32),
            pltpu.VMEM((RG_CG // LANES, min(RG_TILE, n), LANES), F32),
        ],
        compiler_params=_cparams(("parallel", "parallel")),
        name="rglru",
    )(p, p, conv_w, conv_b.reshape(1, rg_w), wg, bg, lam, h0)


RET_UNROLL = 4


def _retention_kernel(lg_ref, q_ref, k_ref, v_ref, g_ref, cs_ref, sn_ref, s0_ref,
                      o_ref, so_ref, qr_scr, kr_scr, st_scr):
    n, hd = q_ref.shape
    L = RET_CHUNK
    nc = n // L
    head = pl.program_id(1)
    lgf = lg_ref[0, head]
    lgb = lg_ref[1, head]

    def rope(t):
        return t * cs_ref[...] + pltpu.roll(t, hd // 2, axis=1) * sn_ref[...]

    qr_scr[...] = rope(q_ref[...].astype(F32)).astype(BF16)
    kr_scr[...] = (rope(k_ref[...].astype(F32)) * (hd ** -0.5)).astype(BF16)

    jc = lax.broadcasted_iota(jnp.int32, (L, 1), 0).astype(F32)
    q_dec_f = jnp.exp((jc + 1.0) * lgf)
    q_dec_b = jnp.exp((L - jc) * lgb)
    k_dec_f = jnp.exp((L - 1.0 - jc) * lgf)
    k_dec_b = jnp.exp(jc * lgb)
    ones = jnp.ones((1, hd), F32)
    chunk_f = jnp.exp(ones * (L * lgf))
    chunk_b = jnp.exp(ones * (L * lgb))
    ii = lax.broadcasted_iota(jnp.int32, (L, L), 0)
    jj = lax.broadcasted_iota(jnp.int32, (L, L), 1)
    diff = (ii - jj).astype(F32)
    dmat = jnp.where(diff >= 0.0, jnp.exp(jnp.maximum(diff, 0.0) * lgf),
                     jnp.exp(jnp.maximum(-diff, 0.0) * lgb))

    tn_dims = (((0,), (0,)), ((), ()))
    nt_dims = (((1,), (1,)), ((), ()))

    def state_update(c, state, k_dec, chunk_dec):
        t0 = pl.multiple_of(c * L, L)
        kd = (kr_scr[pl.ds(t0, L), :].astype(F32) * k_dec).astype(BF16)
        upd = lax.dot_general(kd, v_ref[pl.ds(t0, L), :], tn_dims, preferred_element_type=F32)
        return chunk_dec * state + upd

    def state_body(i, carry):
        f_state, r_state = carry
        cf = i
        cb = nc - 1 - i
        st_scr[cf, :, 0:hd] = f_state.astype(BF16)
        st_scr[cb, :, hd:2 * hd] = r_state.astype(BF16)
        return (state_update(cf, f_state, k_dec_f, chunk_f),
                state_update(cb, r_state, k_dec_b, chunk_b))

    f_fin, r_fin = lax.fori_loop(0, nc, state_body, (s0_ref[0], s0_ref[1]),
                                 unroll=min(RET_UNROLL, nc))
    so_ref[0] = f_fin
    so_ref[1] = r_fin

    def out_body(c, carry):
        t0 = pl.multiple_of(c * L, L)
        qc = qr_scr[pl.ds(t0, L), :]
        kc = kr_scr[pl.ds(t0, L), :]
        vc = v_ref[pl.ds(t0, L), :]
        s = lax.dot_general(qc, kc, nt_dims, preferred_element_type=F32)
        o = jnp.dot((s * dmat).astype(BF16), vc, preferred_element_type=F32)
        cross = jnp.dot(qc, st_scr[c], preferred_element_type=F32)
        o = o + cross[:, 0:hd] * q_dec_f + cross[:, hd:2 * hd] * q_dec_b
        mu = jnp.mean(o, axis=-1, keepdims=True)
        oc = o - mu
        var = jnp.mean(oc * oc, axis=-1, keepdims=True)
        gate = g_ref[pl.ds(t0, L), :].astype(F32)
        o_ref[pl.ds(t0, L), :] = (gate * jax.nn.sigmoid(gate) * (oc * lax.rsqrt(var + EPS))).astype(o_ref.dtype)
        return carry

    lax.fori_loop(0, nc, out_body, 0, unroll=min(RET_UNROLL, nc))


def _retention(p, log_g, cs, sn, s0, col0):
    bsz, n, _ = p.shape
    nh = log_g.shape[1]
    hd = cs.shape[1]
    cb0 = col0 // hd

    def col(which):
        return pl.BlockSpec((None, n, hd), lambda b, h, lg: (b, 0, cb0 + which * nh + h))

    state = pl.BlockSpec((None, 2, None, hd, hd), lambda b, h, lg: (b, 0, h, 0, 0))
    table = pl.BlockSpec((n, hd), lambda b, h, lg: (0, 0))
    grid_spec = pltpu.PrefetchScalarGridSpec(
        num_scalar_prefetch=1,
        grid=(bsz, nh),
        in_specs=[col(0), col(1), col(2), col(3), table, table, state],
        out_specs=[pl.BlockSpec((None, n, hd), lambda b, h, lg: (b, 0, h)), state],
        scratch_shapes=[
            pltpu.VMEM((n, hd), BF16),
            pltpu.VMEM((n, hd), BF16),
            pltpu.VMEM((n // RET_CHUNK, hd, 2 * hd), BF16),
        ],
    )
    return pl.pallas_call(
        _retention_kernel,
        grid_spec=grid_spec,
        out_shape=[
            jax.ShapeDtypeStruct((bsz, n, nh * hd), BF16),
            jax.ShapeDtypeStruct((bsz, 2, nh, hd, hd), F32),
        ],
        compiler_params=_cparams(("parallel", "parallel")),
        name="retention",
    )(log_g, p, p, p, p, cs, sn, s0)


R_RANK0, R_RANK1, R_E0, R_E1, R_W0, R_W1 = range(6)


def _out_proj_kernel(route, x_ref, rg_ref, ret_ref, wo_ref, gate_ref, g_ref, shift_ref, scale_ref,
                     *rest):
    if route:
        wr2_ref, br_ref, xo_ref, h_ref, rt_ref, cnt_ref, carry_scr = rest
    else:
        xo_ref, h_ref = rest
    rg_w = rg_ref.shape[1]
    y = jnp.dot(rg_ref[...], wo_ref[0:rg_w, :], preferred_element_type=F32)
    y = y + jnp.dot(ret_ref[...], wo_ref[rg_w:, :], preferred_element_type=F32)
    xn = x_ref[...] + gate_ref[...] * y
    xo_ref[...] = xn
    hx = _norm_mod(xn, g_ref[...], shift_ref[...], scale_ref[...])
    if not route:
        h_ref[...] = hx.astype(h_ref.dtype)
        return

    tm, d = hx.shape
    for s in range(d // LANES):
        h_ref[pl.ds(s, tm, stride=d // LANES), :] = hx[:, s * LANES:(s + 1) * LANES]

    h_hi = hx.astype(BF16)
    h_lo = (hx - h_hi.astype(F32)).astype(BF16)
    part = jnp.dot(h_hi, wr2_ref[...], preferred_element_type=F32)
    logits = (part[:, :LANES] + part[:, LANES:]
              + jnp.dot(h_lo, wr2_ref[:, :LANES], preferred_element_type=F32) + br_ref[...])
    lane_i = lax.broadcasted_iota(jnp.int32, logits.shape, 1)
    lane = lane_i.astype(F32)
    m0 = jnp.max(logits, axis=-1, keepdims=True)
    i0 = jnp.min(jnp.where(logits == m0, lane, float(LANES)), axis=-1, keepdims=True)
    rest_l = jnp.where(lane == i0, NEG_BIG, logits)
    m1 = jnp.max(rest_l, axis=-1, keepdims=True)
    i1 = jnp.min(jnp.where(rest_l == m1, lane, float(LANES)), axis=-1, keepdims=True)
    e = jnp.exp(m1 - m0)
    w0 = 1.0 / (1.0 + e)
    w1 = e / (1.0 + e)

    @pl.when((pl.program_id(0) == 0) & (pl.program_id(1) == 0))
    def _():
        carry_scr[...] = jnp.zeros_like(carry_scr)

    sel0 = lane == i0
    sel1 = lane == i1
    mask = jnp.where(sel0 | sel1, 1.0, 0.0)
    ri = lax.broadcasted_iota(jnp.int32, (tm, tm), 0)
    ci = lax.broadcasted_iota(jnp.int32, (tm, tm), 1)
    tri = jnp.where(ri > ci, 1.0, 0.0).astype(BF16)
    before = jnp.dot(tri, mask.astype(BF16), preferred_element_type=F32) + carry_scr[...]
    rank0 = jnp.sum(jnp.where(sel0, before, 0.0), axis=-1, keepdims=True)
    rank1 = jnp.sum(jnp.where(sel1, before, 0.0), axis=-1, keepdims=True)
    total = carry_scr[...] + jnp.sum(mask, axis=0, keepdims=True)
    carry_scr[...] = total
    cnt_ref[...] = total
    rec = jnp.zeros_like(logits)
    for idx, val in ((R_RANK0, rank0), (R_RANK1, rank1), (R_E0, i0), (R_E1, i1),
                     (R_W0, w0), (R_W1, w1)):
        rec = jnp.where(lane_i == idx, val, rec)
    rt_ref[...] = rec


def _out_proj(x, rg, ret, w_out, gate, g, shift, scale, router=None):
    bsz, n, d = x.shape
    tm = min(512, n)
    route = router is not None
    vec = pl.BlockSpec((None, 1, d), lambda b, i: (b, 0, 0))
    tile = lambda w: pl.BlockSpec((None, tm, w), lambda b, i: (b, i, 0))
    in_specs = [
        tile(d), tile(rg.shape[2]), tile(ret.shape[2]),
        pl.BlockSpec(w_out.shape, lambda b, i: (0, 0)),
        vec, pl.BlockSpec((1, d), lambda b, i: (0, 0)), vec, vec,
    ]
    args = [x, rg, ret, w_out, gate, g, shift, scale]
    scratch = []
    if route:
        in_specs += [pl.BlockSpec((d, 2 * LANES), lambda b, i: (0, 0)),
                     pl.BlockSpec((1, LANES), lambda b, i: (0, 0))]
        args += list(router)
        out_specs = [tile(d),
                     pl.BlockSpec((None, tm * (d // LANES), LANES), lambda b, i: (b, i, 0)),
                     tile(LANES),
                     pl.BlockSpec((1, LANES), lambda b, i: (0, 0))]
        out_shape = [jax.ShapeDtypeStruct((bsz, n, d), F32),
                     jax.ShapeDtypeStruct((bsz, n * (d // LANES), LANES), F32),
                     jax.ShapeDtypeStruct((bsz, n, LANES), F32),
                     jax.ShapeDtypeStruct((1, LANES), F32)]
        scratch = [pltpu.VMEM((1, LANES), F32)]
        sem = ("arbitrary", "arbitrary")
    else:
        out_specs = [tile(d), tile(d)]
        out_shape = [jax.ShapeDtypeStruct((bsz, n, d), F32), jax.ShapeDtypeStruct((bsz, n, d), BF16)]
        sem = ("parallel", "parallel")
    return pl.pallas_call(
        functools.partial(_out_proj_kernel, route),
        grid=(bsz, n // tm),
        in_specs=in_specs,
        out_specs=out_specs,
        out_shape=out_shape,
        scratch_shapes=scratch,
        compiler_params=_cparams(sem),
        name="out_proj_route" if route else "out_proj",
    )(*args)


def _swiglu_acc(h, w1_ref, w3_ref, w2_ref, row_scale=None):
    a = jnp.dot(h, w1_ref[...], preferred_element_type=F32)
    b = jnp.dot(h, w3_ref[...], preferred_element_type=F32)
    act = a * jax.nn.sigmoid(a) * b
    if row_scale is not None:
        act = act * row_scale
    return jnp.dot(act.astype(BF16), w2_ref[...], preferred_element_type=F32)


def _ffn_kernel(h_ref, x_ref, gate_ref, w1_ref, w3_ref, w2_ref, o_ref, acc_ref):
    f = pl.program_id(2)

    @pl.when(f == 0)
    def _():
        acc_ref[...] = jnp.zeros_like(acc_ref)

    acc_ref[...] += _swiglu_acc(h_ref[...], w1_ref, w3_ref, w2_ref)

    @pl.when(f == pl.num_programs(2) - 1)
    def _():
        o_ref[...] = x_ref[...] + gate_ref[...] * acc_ref[...]


def _ffn(h, x, gate, w1, w3, w2):
    bsz, n, d = x.shape
    f = w1.shape[1]
    tm = min(1024, n)
    tf = 1408 if f % 1408 == 0 else f
    tile = pl.BlockSpec((None, tm, d), lambda b, i, j: (b, i, 0))
    return pl.pallas_call(
        _ffn_kernel,
        grid=(bsz, n // tm, f // tf),
        in_specs=[
            tile, tile,
            pl.BlockSpec((None, 1, d), lambda b, i, j: (b, 0, 0)),
            pl.BlockSpec((d, tf), lambda b, i, j: (0, j)),
            pl.BlockSpec((d, tf), lambda b, i, j: (0, j)),
            pl.BlockSpec((tf, d), lambda b, i, j: (j, 0)),
        ],
        out_specs=tile,
        out_shape=jax.ShapeDtypeStruct((bsz, n, d), F32),
        scratch_shapes=[pltpu.VMEM((tm, d), F32)],
        compiler_params=_cparams(("parallel", "parallel", "arbitrary")),
        name="ffn",
    )(h, x, gate, w1, w3, w2)


MOE_BLK = 512
MOE_TF = 896


def _dispatch_kernel(d0_ref, d1_ref, hx_ref, init_ref, hs_ref, sem):
    del init_ref
    tm = hx_ref.shape[0]
    base = pl.program_id(0) * tm

    def issue(t, carry):
        pltpu.make_async_copy(hx_ref.at[t], hs_ref.at[d0_ref[base + t]], sem).start()
        pltpu.make_async_copy(hx_ref.at[t], hs_ref.at[d1_ref[base + t]], sem).start()
        return carry

    lax.fori_loop(0, tm, issue, 0, unroll=8)
    for _ in range(2):
        pltpu.make_async_copy(hx_ref, hs_ref.at[pl.ds(0, tm)], sem).wait()


def _dispatch(dest0, dest1, hx, n_rows):
    n_tok, s, lanes = hx.shape
    tm = 512
    grid_spec = pltpu.PrefetchScalarGridSpec(
        num_scalar_prefetch=2,
        grid=(n_tok // tm,),
        in_specs=[pl.BlockSpec((tm, s, lanes), lambda i, d0, d1: (i, 0, 0)),
                  pl.BlockSpec(memory_space=pl.ANY)],
        out_specs=pl.BlockSpec(memory_space=pl.ANY),
        scratch_shapes=[pltpu.SemaphoreType.DMA(())],
    )
    return pl.pallas_call(
        _dispatch_kernel,
        grid_spec=grid_spec,
        out_shape=jax.ShapeDtypeStruct((n_rows, s, lanes), F32),
        input_output_aliases={3: 0},
        compiler_params=_cparams(("arbitrary",)),
        name="moe_dispatch",
    )(dest0, dest1, hx, jnp.zeros((n_rows, s, lanes), F32))


def _experts_kernel(be_ref, nu_ref, hs_ref, w1_ref, w3_ref, w2_ref, y_ref, hb_scr, acc_scr):
    j = pl.program_id(0)
    f = pl.program_id(1)
    blk, d = acc_scr.shape
    ns = d // LANES

    @pl.when(j < nu_ref[0])
    def _():
        @pl.when(f == 0)
        def _():
            for s in range(ns):
                hb_scr[:, s * LANES:(s + 1) * LANES] = (
                    hs_ref[pl.ds(s, blk, stride=ns), :].astype(BF16))
            acc_scr[...] = jnp.zeros_like(acc_scr)

        acc_scr[...] += _swiglu_acc(hb_scr[...], w1_ref, w3_ref, w2_ref)

        @pl.when(f == pl.num_programs(1) - 1)
        def _():
            for s in range(ns):
                y_ref[pl.ds(s, blk, stride=ns), :] = acc_scr[:, s * LANES:(s + 1) * LANES]

    @pl.when((j >= nu_ref[0]) & (f == 0))
    def _():
        y_ref[...] = jnp.zeros_like(y_ref)


def _experts(block_expert, n_used, hs, w1, w3, w2):
    ne, d, f = w1.shape
    s = d // LANES
    n_rows = hs.shape[0] // s
    nf = f // MOE_TF
    n_blocks = n_rows // MOE_BLK

    def row_map(j, f_, be, nu):
        return (jnp.minimum(j, nu[0] - 1), 0)

    def f_idx(j, f_, nu):
        return jnp.where(j < nu[0], f_, nf - 1)

    grid_spec = pltpu.PrefetchScalarGridSpec(
        num_scalar_prefetch=2,
        grid=(n_blocks, nf),
        in_specs=[
            pl.BlockSpec((MOE_BLK * s, LANES), row_map),
            pl.BlockSpec((None, d, MOE_TF), lambda j, f_, be, nu: (be[j], 0, f_idx(j, f_, nu))),
            pl.BlockSpec((None, d, MOE_TF), lambda j, f_, be, nu: (be[j], 0, f_idx(j, f_, nu))),
            pl.BlockSpec((None, MOE_TF, d), lambda j, f_, be, nu: (be[j], f_idx(j, f_, nu), 0)),
        ],
        out_specs=pl.BlockSpec((MOE_BLK * s, LANES), lambda j, f_, be, nu: (j, 0)),
        scratch_shapes=[pltpu.VMEM((MOE_BLK, d), BF16), pltpu.VMEM((MOE_BLK, d), F32)],
    )
    return pl.pallas_call(
        _experts_kernel,
        grid_spec=grid_spec,
        out_shape=jax.ShapeDtypeStruct((n_rows * s, LANES), F32),
        compiler_params=_cparams(("arbitrary", "arbitrary")),
        name="moe_experts",
    )(block_expert, n_used, hs, w1, w3, w2)


def _combine_kernel(d0_ref, d1_ref, x_ref, gate_ref, rt_ref, gfin_ref, y_ref, o_ref, ybuf, sem):
    tm, d = x_ref.shape
    ns = d // LANES
    base = (pl.program_id(0) * pl.num_programs(1) + pl.program_id(1)) * tm

    def row(ref, r):
        return ref.at[pl.ds(pl.multiple_of(r * ns, ns), ns)]

    def issue(t, carry):
        pltpu.make_async_copy(row(y_ref, d0_ref[base + t]), row(ybuf.at[0], t), sem).start()
        pltpu.make_async_copy(row(y_ref, d1_ref[base + t]), row(ybuf.at[1], t), sem).start()
        return carry

    lax.fori_loop(0, tm, issue, 0, unroll=8)
    for k in range(2):
        pltpu.make_async_copy(y_ref.at[pl.ds(0, tm * ns)], ybuf.at[k], sem).wait()

    rt = rt_ref[...]
    w0 = rt[:, R_W0:R_W0 + 1]
    w1 = rt[:, R_W1:R_W1 + 1]
    ss = jnp.zeros((tm, 1), F32)
    for s in range(ns):
        cols = slice(s * LANES, (s + 1) * LANES)
        moe = (w0 * ybuf[0, pl.ds(s, tm, stride=ns), :]
               + w1 * ybuf[1, pl.ds(s, tm, stride=ns), :])
        xn = x_ref[:, cols] + gate_ref[:, cols] * moe
        ss = ss + jnp.sum(xn * xn, axis=-1, keepdims=True)
        o_ref[:, cols] = xn
    o_ref[...] = o_ref[...] * lax.rsqrt(ss / d + EPS) * gfin_ref[...]


def _combine(dest0, dest1, x, gate, route, g_final, y):
    bsz, n, d = x.shape
    s = d // LANES
    tm = 512
    grid_spec = pltpu.PrefetchScalarGridSpec(
        num_scalar_prefetch=2,
        grid=(bsz, n // tm),
        in_specs=[
            pl.BlockSpec((None, tm, d), lambda b, i, d0, d1: (b, i, 0)),
            pl.BlockSpec((None, 1, d), lambda b, i, d0, d1: (b, 0, 0)),
            pl.BlockSpec((None, tm, LANES), lambda b, i, d0, d1: (b, i, 0)),
            pl.BlockSpec((1, d), lambda b, i, d0, d1: (0, 0)),
            pl.BlockSpec(memory_space=pl.ANY),
        ],
        out_specs=pl.BlockSpec((None, tm, d), lambda b, i, d0, d1: (b, i, 0)),
        scratch_shapes=[pltpu.VMEM((2, tm * s, LANES), F32), pltpu.SemaphoreType.DMA(())],
    )
    return pl.pallas_call(
        _combine_kernel,
        grid_spec=grid_spec,
        out_shape=jax.ShapeDtypeStruct((bsz, n, d), F32),
        compiler_params=_cparams(("arbitrary", "arbitrary")),
        name="moe_combine",
    )(dest0, dest1, x, gate, route, g_final, y)


def _moe(hx, x, gate, route, counts, g_final, w1, w3, w2):
    bsz, n, d = x.shape
    ne = w1.shape[0]
    n_tok = bsz * n
    n_rows = -(-(2 * n_tok) // MOE_BLK) * MOE_BLK + ne * MOE_BLK
    n_blocks = n_rows // MOE_BLK
    cnt = counts[0, :ne].astype(jnp.int32)
    padded = (cnt + MOE_BLK - 1) // MOE_BLK * MOE_BLK
    pad_end = jnp.cumsum(padded)
    pad_start = pad_end - padded
    n_used = pad_end[-1] // MOE_BLK
    blk = jnp.minimum(jnp.arange(n_blocks, dtype=jnp.int32), n_used - 1)
    block_expert = jnp.minimum(
        jnp.sum((blk[:, None] * MOE_BLK >= pad_end[None, :]).astype(jnp.int32), axis=1), ne - 1)
    rt = route.reshape(n_tok, LANES)
    e0 = rt[:, R_E0].astype(jnp.int32)
    e1 = rt[:, R_E1].astype(jnp.int32)
    dest0 = pad_start[e0] + rt[:, R_RANK0].astype(jnp.int32)
    dest1 = pad_start[e1] + rt[:, R_RANK1].astype(jnp.int32)

    ns = d // LANES
    hs = _dispatch(dest0, dest1, hx.reshape(n_tok, ns, LANES), n_rows)
    y = _experts(block_expert, n_used.reshape(1).astype(jnp.int32),
                 hs.reshape(n_rows * ns, LANES), w1, w3, w2)
    return _combine(dest0, dest1, x, gate, route, g_final, y)


def _rope_tables(n, hd):
    rows = n // GRID_W
    row = jnp.repeat(jnp.arange(rows, dtype=F32), GRID_W)
    col = jnp.tile(jnp.arange(GRID_W, dtype=F32), rows)
    n_freq = hd // 4
    inv = ROPE_BASE ** (-jnp.arange(n_freq, dtype=F32) / n_freq)
    ang = jnp.concatenate([row[:, None] * inv, col[:, None] * inv], axis=-1)
    cos, sin = jnp.cos(ang), jnp.sin(ang)
    return jnp.concatenate([cos, cos], axis=-1), jnp.concatenate([-sin, sin], axis=-1)


def _block_diag_gates(rg_wa, rg_wx):
    w = jnp.stack([rg_wa, rg_wx], axis=1)
    nd, ng, nk, c, _ = w.shape
    per = RG_CG // c
    w = w.reshape(nd, ng, nk // per, per, c, c)
    eye = jnp.eye(per, dtype=w.dtype)
    full = jnp.einsum('dgmpij,pq->dgmpiqj', w, eye)
    return full.reshape(nd, ng, nk // per, RG_CG, RG_CG).astype(BF16)


def _mixer(p, rope, states, conv_w, conv_b, wg, bg, lam, log_g, rg_w):
    rg_out, h_last = _rglru(p, conv_w, conv_b, wg, bg, lam, states[0])
    ret_out, s_last = _retention(p, log_g, rope[0], rope[1], states[1], 2 * rg_w)
    return rg_out, ret_out, (h_last, s_last)


def kernel(x, c, ctx, c_ctx, w_mod, b_mod, g_mix, g_ffn, g_final, w_in, w_out, conv_w, conv_b,
           rg_wa, rg_ba, rg_wx, rg_bx, rg_lam, ret_decay, ffn_w1, ffn_w3, ffn_w2,
           moe_router, moe_router_b, moe_w1, moe_w3, moe_w2):
    bsz, n_lat, d = x.shape
    n_ctx = ctx.shape[1]
    depth = w_mod.shape[0]
    rg_w = rg_lam.shape[2]
    nh = ret_decay.shape[2]
    hd = (w_out.shape[1] - rg_w) // nh
    ne = moe_router.shape[2]
    assert depth == 2, "kernel is written for the two-layer block (dense FFN, then MoE)"

    rows = -(-(bsz + 1) // SUBLANES) * SUBLANES
    c_all = jnp.zeros((rows, d), F32).at[:bsz].set(c).at[bsz].set(c_ctx)
    mod = _modulation(c_all, w_mod, b_mod)
    mod = mod.reshape(depth, rows, N_MOD, d)
    mod_lat = mod[:, :bsz, None]
    mod_ctx = jnp.broadcast_to(mod[:, bsz:bsz + 1, None], mod_lat.shape)

    rope_lat = _rope_tables(n_lat, hd)
    rope_ctx = (jnp.ones((n_ctx, hd), F32), jnp.zeros((n_ctx, hd), F32))
    log_g = jax.nn.log_sigmoid(ret_decay.astype(F32))
    zero_states = (jnp.zeros((bsz, 2, rg_w), F32), jnp.zeros((bsz, 2, nh, hd, hd), F32))

    for l in range(depth):
        last = l == depth - 1
        ml = [mod_lat[l, :, :, j] for j in range(N_MOD)]
        mc = [mod_ctx[l, :, :, j] for j in range(N_MOD)]
        g_m = g_mix[l].reshape(1, d)
        g_f = g_ffn[l].reshape(1, d)
        w_in_l = w_in[l].astype(BF16)
        w_out_l = w_out[l].astype(BF16)
        wg = _block_diag_gates(0.5 * rg_wa[l], 0.5 * rg_wx[l])
        bg = 0.5 * jnp.stack([rg_ba[l], rg_bx[l]], axis=1)
        mix_p = (conv_w[l], conv_b[l], wg, bg, rg_lam[l], log_g[l], rg_w)

        pc = _in_proj(ctx, g_m, mc[0], mc[1], w_in_l)
        rg_c, ret_c, ctx_states = _mixer(pc, rope_ctx, zero_states, *mix_p)
        px = _in_proj(x, g_m, ml[0], ml[1], w_in_l)
        rg_x, ret_x, _ = _mixer(px, rope_lat, ctx_states, *mix_p)

        if l % 2 == 0:
            i = l // 2
            w1, w3, w2 = ffn_w1[i].astype(BF16), ffn_w3[i].astype(BF16), ffn_w2[i].astype(BF16)
            x, hx = _out_proj(x, rg_x, ret_x, w_out_l, ml[2], g_f, ml[3], ml[4])
            x = _ffn(hx, x, ml[5], w1, w3, w2)
            if not last:
                ctx, hc = _out_proj(ctx, rg_c, ret_c, w_out_l, mc[2], g_f, mc[3], mc[4])
                ctx = _ffn(hc, ctx, mc[5], w1, w3, w2)
        else:
            i = l // 2
            wr = jnp.zeros((d, LANES), F32).at[:, :ne].set(moe_router[i])
            wr_hi = wr.astype(BF16)
            wr_lo = (wr - wr_hi.astype(F32)).astype(BF16)
            wr2 = jnp.concatenate([wr_hi, wr_lo], axis=1)
            br = jnp.full((1, LANES), NEG_BIG, F32).at[0, :ne].set(moe_router_b[i])
            x, hx, route, counts = _out_proj(x, rg_x, ret_x, w_out_l, ml[2], g_f, ml[3], ml[4],
                                             router=(wr2, br))
            x = _moe(hx, x, ml[5], route, counts, g_final.reshape(1, d),
                     moe_w1[i].astype(BF16), moe_w3[i].astype(BF16), moe_w2[i].astype(BF16))
    return x
```

```python
import functools

import jax
import jax.numpy as jnp
from jax import lax
from jax.experimental import pallas as pl
from jax.experimental.pallas import tpu as pltpu

F32 = jnp.float32
BF16 = jnp.bfloat16

EPS = 1e-6
RG_C = 8.0
CONV_W = 4
CONV_LEFT = 2
RG_BLOCKS = 8
RET_HEADS = 4
RET_CHUNK = 128
ROPE_BASE = 10000.0
GRID_W = 64
N_EXPERTS = 8
N_MOD = 6

LANES = 128
SUBLANES = 8
VMEM_LIMIT = 56 * 1024 * 1024
NEG_BIG = -1e30
LOG2_E = 1.4426950408889634


def _cparams(sem):
    return pltpu.CompilerParams(dimension_semantics=sem, vmem_limit_bytes=VMEM_LIMIT)


def _modulation_kernel(c_ref, w_ref, b_ref, o_ref):
    c = c_ref[...]
    s = c * jax.nn.sigmoid(c)
    o_ref[...] = jnp.dot(s, w_ref[...], precision=lax.Precision.HIGHEST,
                         preferred_element_type=F32) + b_ref[...]


def _modulation(c_all, w_mod, b_mod):
    depth, d, six_d = w_mod.shape
    rows = c_all.shape[0]
    tn = 1536
    return pl.pallas_call(
        _modulation_kernel,
        grid=(depth, six_d // tn),
        in_specs=[
            pl.BlockSpec((rows, d), lambda l, j: (0, 0)),
            pl.BlockSpec((None, d, tn), lambda l, j: (l, 0, j)),
            pl.BlockSpec((None, 1, tn), lambda l, j: (l, 0, j)),
        ],
        out_specs=pl.BlockSpec((None, rows, tn), lambda l, j: (l, 0, j)),
        out_shape=jax.ShapeDtypeStruct((depth, rows, six_d), F32),
        compiler_params=_cparams(("arbitrary", "arbitrary")),
        name="modulation",
    )(c_all, w_mod, b_mod.reshape(depth, 1, six_d))


def _norm_mod(x, g, shift, scale):
    ms = jnp.mean(x * x, axis=-1, keepdims=True)
    y = x * lax.rsqrt(ms + EPS) * g
    return y * (1.0 + scale) + shift


def _in_proj_kernel(x_ref, g_ref, shift_ref, scale_ref, w_ref, o_ref):
    h = _norm_mod(x_ref[...], g_ref[...], shift_ref[...], scale_ref[...])
    o_ref[...] = jnp.dot(h.astype(BF16), w_ref[...],
                         preferred_element_type=F32).astype(o_ref.dtype)


def _in_proj(x, g, shift, scale, w):
    bsz, n, d = x.shape
    f = w.shape[1]
    tm = min(512, n)
    vec = pl.BlockSpec((None, 1, d), lambda b, i: (b, 0, 0))
    return pl.pallas_call(
        _in_proj_kernel,
        grid=(bsz, n // tm),
        in_specs=[
            pl.BlockSpec((None, tm, d), lambda b, i: (b, i, 0)),
            pl.BlockSpec((1, d), lambda b, i: (0, 0)),
            vec, vec,
            pl.BlockSpec((d, f), lambda b, i: (0, 0)),
        ],
        out_specs=pl.BlockSpec((None, tm, f), lambda b, i: (b, i, 0)),
        out_shape=jax.ShapeDtypeStruct((bsz, n, f), BF16),
        compiler_params=_cparams(("parallel", "parallel")),
        name="in_proj",
    )(x, g, shift, scale, w)


RG_TILE = 256
RG_CG = 256
RG_HALO = SUBLANES


def _rglru_kernel(u_ref, yg_ref, cw_ref, cb_ref, wg_ref, bg_ref, lam_ref, h0_ref,
                  o_ref, hl_ref, u_scr, hf_scr, nat_scr):
    n = u_ref.shape[0]
    tt = min(RG_TILE, n)
    nt = n // tt
    cg = u_ref.shape[1]

    zeros = jnp.zeros((RG_HALO, LANES), F32)
    for g in range(cg // LANES):
        u_scr[g, pl.ds(0, RG_HALO), :] = zeros
        u_scr[g, pl.ds(RG_HALO + n, RG_HALO), :] = zeros
        u_scr[g, pl.ds(RG_HALO, n), :] = u_ref[:, g * LANES:(g + 1) * LANES].astype(F32)

    seg = tt // SUBLANES
    lane_cols = [slice(g * LANES, (g + 1) * LANES) for g in range(cg // LANES)]
    cw_rows = [[jnp.broadcast_to(cw_ref[k:k + 1, c], (SUBLANES, LANES)) for c in lane_cols]
               for k in range(CONV_W)]
    cb_rows = [jnp.broadcast_to(cb_ref[:, c], (SUBLANES, LANES)) for c in lane_cols]

    def conv_tile(t0):
        blocks = []
        for j in range(seg):
            lane_groups = []
            for g in range(cg // LANES):
                acc = cb_rows[g]
                for k in range(CONV_W):
                    start = t0 + (RG_HALO + j + k - CONV_LEFT)
                    acc = acc + cw_rows[k][g] * u_scr[g, pl.ds(start, SUBLANES, stride=seg), :]
                lane_groups.append(acc)
            blocks.append(jnp.concatenate(lane_groups, axis=1))
        return jnp.concatenate(blocks, axis=0)

    def gates(uc, d):
        ub = uc.astype(BF16)
        ta = jnp.tanh(jnp.dot(ub, wg_ref[d, 0], preferred_element_type=F32) + bg_ref[d, 0:1, :])
        ti = jnp.tanh(jnp.dot(ub, wg_ref[d, 1], preferred_element_type=F32) + bg_ref[d, 1:2, :])
        z = -lam_ref[d:d + 1, :]
        sp = jnp.maximum(z, 0.0) + jnp.log1p(jnp.exp(-jnp.abs(z)))
        c2 = (-0.5 * RG_C * LOG2_E) * sp
        a = jnp.exp2(c2 + c2 * ta)
        b = jnp.exp(0.5 * jnp.log(1.0 - a * a)) * ((0.5 + 0.5 * ti) * uc)
        return a, b

    def scan_tile(a, b, carry, reverse):
        steps = range(seg - 1, -1, -1) if reverse else range(seg)
        h = acum = None
        h_loc = [None] * seg
        a_cum = [None] * seg
        for j in steps:
            rows = slice(j * SUBLANES, (j + 1) * SUBLANES)
            h = b[rows] if h is None else a[rows] * h + b[rows]
            acum = a[rows] if acum is None else a[rows] * acum
            h_loc[j], a_cum[j] = h, acum
        order = range(SUBLANES - 1, -1, -1) if reverse else range(SUBLANES)
        c = carry
        c_in = [None] * SUBLANES
        for s in order:
            c_in[s] = c
            c = h[s:s + 1, :] + acum[s:s + 1, :] * c
        c_in = jnp.concatenate(c_in, axis=0)
        out = jnp.concatenate([h_loc[j] + a_cum[j] * c_in for j in range(seg)], axis=0)
        return out, c

    def fwd_body(i, carry):
        t0 = pl.multiple_of(i * tt, tt)
        uc = conv_tile(t0)
        a, b = gates(uc, 0)
        h, carry = scan_tile(a, b, carry, False)
        hf_scr[pl.ds(t0, tt), :] = h
        return carry

    hf_last = lax.fori_loop(0, nt, fwd_body, h0_ref[0:1, :])

    def bwd_body(i, carry):
        t0 = pl.multiple_of((nt - 1 - i) * tt, tt)
        uc = conv_tile(t0)
        a, b = gates(uc, 1)
        h, carry = scan_tile(a, b, carry, True)
        hsum = hf_scr[pl.ds(t0, tt), :] + h
        for j in range(seg):
            for g in range(cg // LANES):
                nat_scr[g, pl.ds(j, SUBLANES, stride=seg), :] = (
                    hsum[j * SUBLANES:(j + 1) * SUBLANES, g * LANES:(g + 1) * LANES])
        hnat = jnp.concatenate([nat_scr[g] for g in range(cg // LANES)], axis=1)
        yg = yg_ref[pl.ds(t0, tt), :].astype(F32)
        o_ref[pl.ds(t0, tt), :] = (jax.nn.gelu(yg) * hnat).astype(o_ref.dtype)
        return carry

    hb_last = lax.fori_loop(0, nt, bwd_body, h0_ref[1:2, :])
    hl_ref[0:1, :] = hf_last
    hl_ref[1:2, :] = hb_last


def _rglru(p, conv_w, conv_b, wg, bg, lam, h0):
    bsz, n, _ = p.shape
    rg_w = lam.shape[1]
    ncg = rg_w // RG_CG
    return pl.pallas_call(
        _rglru_kernel,
        grid=(bsz, ncg),
        in_specs=[
            pl.BlockSpec((None, n, RG_CG), lambda b, c: (b, 0, c)),
            pl.BlockSpec((None, n, RG_CG), lambda b, c: (b, 0, ncg + c)),
            pl.BlockSpec((CONV_W, RG_CG), lambda b, c: (0, c)),
            pl.BlockSpec((1, RG_CG), lambda b, c: (0, c)),
            pl.BlockSpec((2, 2, None, RG_CG, RG_CG), lambda b, c: (0, 0, c, 0, 0)),
            pl.BlockSpec((2, 2, RG_CG), lambda b, c: (0, 0, c)),
            pl.BlockSpec((2, RG_CG), lambda b, c: (0, c)),
            pl.BlockSpec((None, 2, RG_CG), lambda b, c: (b, 0, c)),
        ],
        out_specs=[
            pl.BlockSpec((None, n, RG_CG), lambda b, c: (b, 0, c)),
            pl.BlockSpec((None, 2, RG_CG), lambda b, c: (b, 0, c)),
        ],
        out_shape=[
            jax.ShapeDtypeStruct((bsz, n, rg_w), BF16),
            jax.ShapeDtypeStruct((bsz, 2, rg_w), F32),
        ],
        scratch_shapes=[
            pltpu.VMEM((RG_CG // LANES, n + 2 * RG_HALO, LANES), F32),
            pltpu.VMEM((n, RG_CG), F32),
            pltpu.VMEM((RG_CG // LANES, min(RG_TILE, n), LANES), F32),
        ],
        compiler_params=_cparams(("parallel", "parallel")),
        name="rglru",
    )(p, p, conv_w, conv_b.reshape(1, rg_w), wg, bg, lam, h0)


RET_UNROLL = 4


def _retention_kernel(lg_ref, q_ref, k_ref, v_ref, g_ref, cs_ref, sn_ref, s0_ref,
                      o_ref, so_ref, qr_scr, kr_scr, st_scr):
    n, hd = q_ref.shape
    L = RET_CHUNK
    nc = n // L
    head = pl.program_id(1)
    lgf = lg_ref[0, head]
    lgb = lg_ref[1, head]

    def rope(t):
        return t * cs_ref[...] + pltpu.roll(t, hd // 2, axis=1) * sn_ref[...]

    qr_scr[...] = rope(q_ref[...].astype(F32)).astype(BF16)
    kr_scr[...] = (rope(k_ref[...].astype(F32)) * (hd ** -0.5)).astype(BF16)

    jc = lax.broadcasted_iota(jnp.int32, (L, 1), 0).astype(F32)
    q_dec_f = jnp.exp((jc + 1.0) * lgf)
    q_dec_b = jnp.exp((L - jc) * lgb)
    k_dec_f = jnp.exp((L - 1.0 - jc) * lgf)
    k_dec_b = jnp.exp(jc * lgb)
    ones = jnp.ones((1, hd), F32)
    chunk_f = jnp.exp(ones * (L * lgf))
    chunk_b = jnp.exp(ones * (L * lgb))
    ii = lax.broadcasted_iota(jnp.int32, (L, L), 0)
    jj = lax.broadcasted_iota(jnp.int32, (L, L), 1)
    diff = (ii - jj).astype(F32)
    dmat = jnp.where(diff >= 0.0, jnp.exp(jnp.maximum(diff, 0.0) * lgf),
                     jnp.exp(jnp.maximum(-diff, 0.0) * lgb))

    tn_dims = (((0,), (0,)), ((), ()))
    nt_dims = (((1,), (1,)), ((), ()))

    def state_update(c, state, k_dec, chunk_dec):
        t0 = pl.multiple_of(c * L, L)
        kd = (kr_scr[pl.ds(t0, L), :].astype(F32) * k_dec).astype(BF16)
        upd = lax.dot_general(kd, v_ref[pl.ds(t0, L), :], tn_dims, preferred_element_type=F32)
        return chunk_dec * state + upd

    def state_body(i, carry):
        f_state, r_state = carry
        cf = i
        cb = nc - 1 - i
        st_scr[cf, :, 0:hd] = f_state.astype(BF16)
        st_scr[cb, :, hd:2 * hd] = r_state.astype(BF16)
        return (state_update(cf, f_state, k_dec_f, chunk_f),
                state_update(cb, r_state, k_dec_b, chunk_b))

    f_fin, r_fin = lax.fori_loop(0, nc, state_body, (s0_ref[0], s0_ref[1]),
                                 unroll=min(RET_UNROLL, nc))
    so_ref[0] = f_fin
    so_ref[1] = r_fin

    def out_body(c, carry):
        t0 = pl.multiple_of(c * L, L)
        qc = qr_scr[pl.ds(t0, L), :]
        kc = kr_scr[pl.ds(t0, L), :]
        vc = v_ref[pl.ds(t0, L), :]
        s = lax.dot_general(qc, kc, nt_dims, preferred_element_type=F32)
        o = jnp.dot((s * dmat).astype(BF16), vc, preferred_element_type=F32)
        cross = jnp.dot(qc, st_scr[c], preferred_element_type=F32)
        o = o + cross[:, 0:hd] * q_dec_f + cross[:, hd:2 * hd] * q_dec_b
        mu = jnp.mean(o, axis=-1, keepdims=True)
        oc = o - mu
        var = jnp.mean(oc * oc, axis=-1, keepdims=True)
        gate = g_ref[pl.ds(t0, L), :].astype(F32)
        o_ref[pl.ds(t0, L), :] = (gate * jax.nn.sigmoid(gate) * (oc * lax.rsqrt(var + EPS))).astype(o_ref.dtype)
        return carry

    lax.fori_loop(0, nc, out_body, 0, unroll=min(RET_UNROLL, nc))


def _retention(p, log_g, cs, sn, s0, col0):
    bsz, n, _ = p.shape
    nh = log_g.shape[1]
    hd = cs.shape[1]
    cb0 = col0 // hd

    def col(which):
        return pl.BlockSpec((None, n, hd), lambda b, h, lg: (b, 0, cb0 + which * nh + h))

    state = pl.BlockSpec((None, 2, None, hd, hd), lambda b, h, lg: (b, 0, h, 0, 0))
    table = pl.BlockSpec((n, hd), lambda b, h, lg: (0, 0))
    grid_spec = pltpu.PrefetchScalarGridSpec(
        num_scalar_prefetch=1,
        grid=(bsz, nh),
        in_specs=[col(0), col(1), col(2), col(3), table, table, state],
        out_specs=[pl.BlockSpec((None, n, hd), lambda b, h, lg: (b, 0, h)), state],
        scratch_shapes=[
            pltpu.VMEM((n, hd), BF16),
            pltpu.VMEM((n, hd), BF16),
            pltpu.VMEM((n // RET_CHUNK, hd, 2 * hd), BF16),
        ],
    )
    return pl.pallas_call(
        _retention_kernel,
        grid_spec=grid_spec,
        out_shape=[
            jax.ShapeDtypeStruct((bsz, n, nh * hd), BF16),
            jax.ShapeDtypeStruct((bsz, 2, nh, hd, hd), F32),
        ],
        compiler_params=_cparams(("parallel", "parallel")),
        name="retention",
    )(log_g, p, p, p, p, cs, sn, s0)


R_RANK0, R_RANK1, R_E0, R_E1, R_W0, R_W1 = range(6)


def _out_proj_kernel(route, x_ref, rg_ref, ret_ref, wo_ref, gate_ref, g_ref, shift_ref, scale_ref,
                     *rest):
    if route:
        wr2_ref, br_ref, xo_ref, h_ref, rt_ref, cnt_ref, carry_scr = rest
    else:
        xo_ref, h_ref = rest
    rg_w = rg_ref.shape[1]
    y = jnp.dot(rg_ref[...], wo_ref[0:rg_w, :], preferred_element_type=F32)
    y = y + jnp.dot(ret_ref[...], wo_ref[rg_w:, :], preferred_element_type=F32)
    xn = x_ref[...] + gate_ref[...] * y
    xo_ref[...] = xn
    hx = _norm_mod(xn, g_ref[...], shift_ref[...], scale_ref[...])
    if not route:
        h_ref[...] = hx.astype(h_ref.dtype)
        return

    tm, d = hx.shape
    for s in range(d // LANES):
        h_ref[pl.ds(s, tm, stride=d // LANES), :] = hx[:, s * LANES:(s + 1) * LANES]

    h_hi = hx.astype(BF16)
    h_lo = (hx - h_hi.astype(F32)).astype(BF16)
    part = jnp.dot(h_hi, wr2_ref[...], preferred_element_type=F32)
    logits = (part[:, :LANES] + part[:, LANES:]
              + jnp.dot(h_lo, wr2_ref[:, :LANES], preferred_element_type=F32) + br_ref[...])
    lane_i = lax.broadcasted_iota(jnp.int32, logits.shape, 1)
    lane = lane_i.astype(F32)
    m0 = jnp.max(logits, axis=-1, keepdims=True)
    i0 = jnp.min(jnp.where(logits == m0, lane, float(LANES)), axis=-1, keepdims=True)
    rest_l = jnp.where(lane == i0, NEG_BIG, logits)
    m1 = jnp.max(rest_l, axis=-1, keepdims=True)
    i1 = jnp.min(jnp.where(rest_l == m1, lane, float(LANES)), axis=-1, keepdims=True)
    e = jnp.exp(m1 - m0)
    w0 = 1.0 / (1.0 + e)
    w1 = e / (1.0 + e)

    @pl.when((pl.program_id(0) == 0) & (pl.program_id(1) == 0))
    def _():
        carry_scr[...] = jnp.zeros_like(carry_scr)

    sel0 = lane == i0
    sel1 = lane == i1
    mask = jnp.where(sel0 | sel1, 1.0, 0.0)
    ri = lax.broadcasted_iota(jnp.int32, (tm, tm), 0)
    ci = lax.broadcasted_iota(jnp.int32, (tm, tm), 1)
    tri = jnp.where(ri > ci, 1.0, 0.0).astype(BF16)
    before = jnp.dot(tri, mask.astype(BF16), preferred_element_type=F32) + carry_scr[...]
    rank0 = jnp.sum(jnp.where(sel0, before, 0.0), axis=-1, keepdims=True)
    rank1 = jnp.sum(jnp.where(sel1, before, 0.0), axis=-1, keepdims=True)
    total = carry_scr[...] + jnp.sum(mask, axis=0, keepdims=True)
    carry_scr[...] = total
    cnt_ref[...] = total
    rec = jnp.zeros_like(logits)
    for idx, val in ((R_RANK0, rank0), (R_RANK1, rank1), (R_E0, i0), (R_E1, i1),
                     (R_W0, w0), (R_W1, w1)):
        rec = jnp.where(lane_i == idx, val, rec)
    rt_ref[...] = rec


def _out_proj(x, rg, ret, w_out, gate, g, shift, scale, router=None):
    bsz, n, d = x.shape
    tm = min(512, n)
    route = router is not None
    vec = pl.BlockSpec((None, 1, d), lambda b, i: (b, 0, 0))
    tile = lambda w: pl.BlockSpec((None, tm, w), lambda b, i: (b, i, 0))
    in_specs = [
        tile(d), tile(rg.shape[2]), tile(ret.shape[2]),
        pl.BlockSpec(w_out.shape, lambda b, i: (0, 0)),
        vec, pl.BlockSpec((1, d), lambda b, i: (0, 0)), vec, vec,
    ]
    args = [x, rg, ret, w_out, gate, g, shift, scale]
    scratch = []
    if route:
        in_specs += [pl.BlockSpec((d, 2 * LANES), lambda b, i: (0, 0)),
                     pl.BlockSpec((1, LANES), lambda b, i: (0, 0))]
        args += list(router)
        out_specs = [tile(d),
                     pl.BlockSpec((None, tm * (d // LANES), LANES), lambda b, i: (b, i, 0)),
                     tile(LANES),
                     pl.BlockSpec((1, LANES), lambda b, i: (0, 0))]
        out_shape = [jax.ShapeDtypeStruct((bsz, n, d), F32),
                     jax.ShapeDtypeStruct((bsz, n * (d // LANES), LANES), F32),
                     jax.ShapeDtypeStruct((bsz, n, LANES), F32),
                     jax.ShapeDtypeStruct((1, LANES), F32)]
        scratch = [pltpu.VMEM((1, LANES), F32)]
        sem = ("arbitrary", "arbitrary")
    else:
        out_specs = [tile(d), tile(d)]
        out_shape = [jax.ShapeDtypeStruct((bsz, n, d), F32), jax.ShapeDtypeStruct((bsz, n, d), BF16)]
        sem = ("parallel", "parallel")
    return pl.pallas_call(
        functools.partial(_out_proj_kernel, route),
        grid=(bsz, n // tm),
        in_specs=in_specs,
        out_specs=out_specs,
        out_shape=out_shape,
        scratch_shapes=scratch,
        compiler_params=_cparams(sem),
        name="out_proj_route" if route else "out_proj",
    )(*args)


def _swiglu_acc(h, w1_ref, w3_ref, w2_ref, row_scale=None):
    a = jnp.dot(h, w1_ref[...], preferred_element_type=F32)
    b = jnp.dot(h, w3_ref[...], preferred_element_type=F32)
    act = a * jax.nn.sigmoid(a) * b
    if row_scale is not None:
        act = act * row_scale
    return jnp.dot(act.astype(BF16), w2_ref[...], preferred_element_type=F32)


def _ffn_kernel(h_ref, x_ref, gate_ref, w1_ref, w3_ref, w2_ref, o_ref):
    o_ref[...] = x_ref[...] + gate_ref[...] * _swiglu_acc(h_ref[...], w1_ref, w3_ref, w2_ref)


def _ffn(h, x, gate, w1, w3, w2):
    bsz, n, d = x.shape
    f = w1.shape[1]
    tm = min(512, n)
    tile = pl.BlockSpec((None, tm, d), lambda b, i: (b, i, 0))
    resident = lambda shape: pl.BlockSpec(shape, lambda b, i: (0, 0), pipeline_mode=pl.Buffered(1))
    return pl.pallas_call(
        _ffn_kernel,
        grid=(bsz, n // tm),
        in_specs=[
            tile, tile,
            pl.BlockSpec((None, 1, d), lambda b, i: (b, 0, 0)),
            resident((d, f)), resident((d, f)), resident((f, d)),
        ],
        out_specs=tile,
        out_shape=jax.ShapeDtypeStruct((bsz, n, d), F32),
        compiler_params=_cparams(("parallel", "parallel")),
        name="ffn",
    )(h, x, gate, w1, w3, w2)


MOE_BLK = 512
MOE_TF = 1792


def _dispatch_kernel(d0_ref, d1_ref, zr_ref, hx_ref, hs_ref, zbuf, sem, zsem):
    tm = hx_ref.shape[0]
    base = pl.program_id(0) * tm

    @pl.when(pl.program_id(0) == 0)
    def _():
        zbuf[...] = jnp.zeros_like(zbuf)
        for z in range(zr_ref.shape[0]):
            @pl.when(zr_ref[z] >= 0)
            def _():
                pltpu.make_async_copy(zbuf, hs_ref.at[pl.ds(zr_ref[z], MOE_BLK)], zsem).start()
        for z in range(zr_ref.shape[0]):
            @pl.when(zr_ref[z] >= 0)
            def _():
                pltpu.make_async_copy(zbuf, hs_ref.at[pl.ds(zr_ref[z], MOE_BLK)], zsem).wait()

    def issue(t, carry):
        pltpu.make_async_copy(hx_ref.at[t], hs_ref.at[d0_ref[base + t]], sem).start()
        pltpu.make_async_copy(hx_ref.at[t], hs_ref.at[d1_ref[base + t]], sem).start()
        return carry

    lax.fori_loop(0, tm, issue, 0, unroll=8)
    for _ in range(2):
        pltpu.make_async_copy(hx_ref, hs_ref.at[pl.ds(0, tm)], sem).wait()


def _dispatch(dest0, dest1, zero_rows, hx, n_rows):
    n_tok, s, lanes = hx.shape
    tm = 512
    grid_spec = pltpu.PrefetchScalarGridSpec(
        num_scalar_prefetch=3,
        grid=(n_tok // tm,),
        in_specs=[pl.BlockSpec((tm, s, lanes), lambda i, d0, d1, zr: (i, 0, 0))],
        out_specs=pl.BlockSpec(memory_space=pl.ANY),
        scratch_shapes=[pltpu.VMEM((MOE_BLK, s, lanes), F32),
                        pltpu.SemaphoreType.DMA(()), pltpu.SemaphoreType.DMA(())],
    )
    return pl.pallas_call(
        _dispatch_kernel,
        grid_spec=grid_spec,
        out_shape=jax.ShapeDtypeStruct((n_rows, s, lanes), F32),
        compiler_params=_cparams(("arbitrary",)),
        name="moe_dispatch",
    )(dest0, dest1, zero_rows, hx)


def _experts_kernel(be_ref, nu_ref, hs_ref, w1_ref, w3_ref, w2_ref, y_ref, hb_scr, acc_scr):
    j = pl.program_id(0)
    f = pl.program_id(1)
    blk, d = acc_scr.shape
    ns = d // LANES

    @pl.when(j < nu_ref[0])
    def _():
        @pl.when(f == 0)
        def _():
            for s in range(ns):
                hb_scr[:, s * LANES:(s + 1) * LANES] = (
                    hs_ref[pl.ds(s, blk, stride=ns), :].astype(BF16))
            acc_scr[...] = jnp.zeros_like(acc_scr)

        acc_scr[...] += _swiglu_acc(hb_scr[...], w1_ref, w3_ref, w2_ref)

        @pl.when(f == pl.num_programs(1) - 1)
        def _():
            for s in range(ns):
                y_ref[pl.ds(s, blk, stride=ns), :] = acc_scr[:, s * LANES:(s + 1) * LANES]

    @pl.when((j >= nu_ref[0]) & (f == 0))
    def _():
        y_ref[...] = jnp.zeros_like(y_ref)


def _experts(block_expert, n_used, hs, w1, w3, w2):
    ne, d, f = w1.shape
    s = d // LANES
    n_rows = hs.shape[0] // s
    nf = f // MOE_TF
    n_blocks = n_rows // MOE_BLK

    def row_map(j, f_, be, nu):
        return (jnp.minimum(j, nu[0] - 1), 0)

    def f_idx(j, f_, nu):
        return jnp.where(j < nu[0], f_, nf - 1)

    grid_spec = pltpu.PrefetchScalarGridSpec(
        num_scalar_prefetch=2,
        grid=(n_blocks, nf),
        in_specs=[
            pl.BlockSpec((MOE_BLK * s, LANES), row_map),
            pl.BlockSpec((None, d, MOE_TF), lambda j, f_, be, nu: (be[j], 0, f_idx(j, f_, nu))),
            pl.BlockSpec((None, d, MOE_TF), lambda j, f_, be, nu: (be[j], 0, f_idx(j, f_, nu))),
            pl.BlockSpec((None, MOE_TF, d), lambda j, f_, be, nu: (be[j], f_idx(j, f_, nu), 0)),
        ],
        out_specs=pl.BlockSpec((MOE_BLK * s, LANES), lambda j, f_, be, nu: (j, 0)),
        scratch_shapes=[pltpu.VMEM((MOE_BLK, d), BF16), pltpu.VMEM((MOE_BLK, d), F32)],
    )
    return pl.pallas_call(
        _experts_kernel,
        grid_spec=grid_spec,
        out_shape=jax.ShapeDtypeStruct((n_rows * s, LANES), F32),
        compiler_params=_cparams(("arbitrary", "arbitrary")),
        name="moe_experts",
    )(block_expert, n_used, hs, w1, w3, w2)


def _combine_kernel(d0_ref, d1_ref, x_ref, gate_ref, rt_ref, gfin_ref, y_ref, o_ref, ybuf, sem):
    tm, d = x_ref.shape
    ns = d // LANES
    step = pl.program_id(0) * pl.num_programs(1) + pl.program_id(1)
    nsteps = pl.num_programs(0) * pl.num_programs(1)
    slot = step % 2

    def row(ref, r):
        return ref.at[pl.ds(pl.multiple_of(r * ns, ns), ns)]

    def start_gather(block, into):
        base = block * tm

        def issue(t, carry):
            pltpu.make_async_copy(row(y_ref, d0_ref[base + t]), row(ybuf.at[into, 0], t),
                                  sem.at[into]).start()
            pltpu.make_async_copy(row(y_ref, d1_ref[base + t]), row(ybuf.at[into, 1], t),
                                  sem.at[into]).start()
            return carry

        lax.fori_loop(0, tm, issue, 0, unroll=8)

    @pl.when(step == 0)
    def _():
        start_gather(0, 0)

    @pl.when(step + 1 < nsteps)
    def _():
        start_gather(step + 1, 1 - slot)

    for k in range(2):
        pltpu.make_async_copy(y_ref.at[pl.ds(0, tm * ns)], ybuf.at[slot, k], sem.at[slot]).wait()

    rt = rt_ref[...]
    w0 = rt[:, R_W0:R_W0 + 1]
    w1 = rt[:, R_W1:R_W1 + 1]
    ss = jnp.zeros((tm, 1), F32)
    for s in range(ns):
        cols = slice(s * LANES, (s + 1) * LANES)
        moe = (w0 * ybuf[slot, 0, pl.ds(s, tm, stride=ns), :]
               + w1 * ybuf[slot, 1, pl.ds(s, tm, stride=ns), :])
        xn = x_ref[:, cols] + gate_ref[:, cols] * moe
        ss = ss + jnp.sum(xn * xn, axis=-1, keepdims=True)
        o_ref[:, cols] = xn
    o_ref[...] = o_ref[...] * lax.rsqrt(ss / d + EPS) * gfin_ref[...]


def _combine(dest0, dest1, x, gate, route, g_final, y):
    bsz, n, d = x.shape
    s = d // LANES
    tm = 512
    grid_spec = pltpu.PrefetchScalarGridSpec(
        num_scalar_prefetch=2,
        grid=(bsz, n // tm),
        in_specs=[
            pl.BlockSpec((None, tm, d), lambda b, i, d0, d1: (b, i, 0)),
            pl.BlockSpec((None, 1, d), lambda b, i, d0, d1: (b, 0, 0)),
            pl.BlockSpec((None, tm, LANES), lambda b, i, d0, d1: (b, i, 0)),
            pl.BlockSpec((1, d), lambda b, i, d0, d1: (0, 0)),
            pl.BlockSpec(memory_space=pl.ANY),
        ],
        out_specs=pl.BlockSpec((None, tm, d), lambda b, i, d0, d1: (b, i, 0)),
        scratch_shapes=[pltpu.VMEM((2, 2, tm * s, LANES), F32), pltpu.SemaphoreType.DMA((2,))],
    )
    return pl.pallas_call(
        _combine_kernel,
        grid_spec=grid_spec,
        out_shape=jax.ShapeDtypeStruct((bsz, n, d), F32),
        compiler_params=_cparams(("arbitrary", "arbitrary")),
        name="moe_combine",
    )(dest0, dest1, x, gate, route, g_final, y)


def _moe(hx, x, gate, route, counts, g_final, w1, w3, w2):
    bsz, n, d = x.shape
    ne = w1.shape[0]
    n_tok = bsz * n
    n_rows = -(-(2 * n_tok) // MOE_BLK) * MOE_BLK + ne * MOE_BLK
    n_blocks = n_rows // MOE_BLK
    cnt = counts[0, :ne].astype(jnp.int32)
    padded = (cnt + MOE_BLK - 1) // MOE_BLK * MOE_BLK
    pad_end = jnp.cumsum(padded)
    pad_start = pad_end - padded
    n_used = pad_end[-1] // MOE_BLK
    blk = jnp.minimum(jnp.arange(n_blocks, dtype=jnp.int32), n_used - 1)
    block_expert = jnp.minimum(
        jnp.sum((blk[:, None] * MOE_BLK >= pad_end[None, :]).astype(jnp.int32), axis=1), ne - 1)
    rt = route.reshape(n_tok, LANES)
    e0 = rt[:, R_E0].astype(jnp.int32)
    e1 = rt[:, R_E1].astype(jnp.int32)
    dest0 = pad_start[e0] + rt[:, R_RANK0].astype(jnp.int32)
    dest1 = pad_start[e1] + rt[:, R_RANK1].astype(jnp.int32)

    last_blk = jnp.where(padded > 0, pad_end - MOE_BLK, -1)
    trail = n_used + jnp.arange(ne, dtype=jnp.int32)
    trail = jnp.where(trail < n_blocks, trail * MOE_BLK, -1)
    zero_rows = jnp.concatenate([last_blk, trail]).astype(jnp.int32)

    ns = d // LANES
    hs = _dispatch(dest0, dest1, zero_rows, hx.reshape(n_tok, ns, LANES), n_rows)
    y = _experts(block_expert, n_used.reshape(1).astype(jnp.int32),
                 hs.reshape(n_rows * ns, LANES), w1, w3, w2)
    return _combine(dest0, dest1, x, gate, route, g_final, y)


def _rope_tables(n, hd):
    rows = n // GRID_W
    row = jnp.repeat(jnp.arange(rows, dtype=F32), GRID_W)
    col = jnp.tile(jnp.arange(GRID_W, dtype=F32), rows)
    n_freq = hd // 4
    inv = ROPE_BASE ** (-jnp.arange(n_freq, dtype=F32) / n_freq)
    ang = jnp.concatenate([row[:, None] * inv, col[:, None] * inv], axis=-1)
    cos, sin = jnp.cos(ang), jnp.sin(ang)
    return jnp.concatenate([cos, cos], axis=-1), jnp.concatenate([-sin, sin], axis=-1)


def _block_diag_gates(rg_wa, rg_wx):
    w = jnp.stack([rg_wa, rg_wx], axis=1)
    nd, ng, nk, c, _ = w.shape
    per = RG_CG // c
    w = w.reshape(nd, ng, nk // per, per, c, c)
    eye = jnp.eye(per, dtype=w.dtype)
    full = jnp.einsum('dgmpij,pq->dgmpiqj', w, eye)
    return full.reshape(nd, ng, nk // per, RG_CG, RG_CG).astype(BF16)


def _mixer(p, rope, states, conv_w, conv_b, wg, bg, lam, log_g, rg_w):
    rg_out, h_last = _rglru(p, conv_w, conv_b, wg, bg, lam, states[0])
    ret_out, s_last = _retention(p, log_g, rope[0], rope[1], states[1], 2 * rg_w)
    return rg_out, ret_out, (h_last, s_last)


def kernel(x, c, ctx, c_ctx, w_mod, b_mod, g_mix, g_ffn, g_final, w_in, w_out, conv_w, conv_b,
           rg_wa, rg_ba, rg_wx, rg_bx, rg_lam, ret_decay, ffn_w1, ffn_w3, ffn_w2,
           moe_router, moe_router_b, moe_w1, moe_w3, moe_w2):
    bsz, n_lat, d = x.shape
    n_ctx = ctx.shape[1]
    depth = w_mod.shape[0]
    rg_w = rg_lam.shape[2]
    nh = ret_decay.shape[2]
    hd = (w_out.shape[1] - rg_w) // nh
    ne = moe_router.shape[2]
    assert depth == 2, "kernel is written for the two-layer block (dense FFN, then MoE)"

    rows = -(-(bsz + 1) // SUBLANES) * SUBLANES
    c_all = jnp.zeros((rows, d), F32).at[:bsz].set(c).at[bsz].set(c_ctx)
    mod = _modulation(c_all, w_mod, b_mod)
    mod = mod.reshape(depth, rows, N_MOD, d)
    mod_lat = mod[:, :bsz, None]
    mod_ctx = jnp.broadcast_to(mod[:, bsz:bsz + 1, None], mod_lat.shape)

    rope_lat = _rope_tables(n_lat, hd)
    rope_ctx = (jnp.ones((n_ctx, hd), F32), jnp.zeros((n_ctx, hd), F32))
    log_g = jax.nn.log_sigmoid(ret_decay.astype(F32))
    zero_states = (jnp.zeros((bsz, 2, rg_w), F32), jnp.zeros((bsz, 2, nh, hd, hd), F32))

    for l in range(depth):
        last = l == depth - 1
        ml = [mod_lat[l, :, :, j] for j in range(N_MOD)]
        mc = [mod_ctx[l, :, :, j] for j in range(N_MOD)]
        g_m = g_mix[l].reshape(1, d)
        g_f = g_ffn[l].reshape(1, d)
        w_in_l = w_in[l].astype(BF16)
        w_out_l = w_out[l].astype(BF16)
        wg = _block_diag_gates(0.5 * rg_wa[l], 0.5 * rg_wx[l])
        bg = 0.5 * jnp.stack([rg_ba[l], rg_bx[l]], axis=1)
        mix_p = (conv_w[l], conv_b[l], wg, bg, rg_lam[l], log_g[l], rg_w)

        pc = _in_proj(ctx, g_m, mc[0], mc[1], w_in_l)
        rg_c, ret_c, ctx_states = _mixer(pc, rope_ctx, zero_states, *mix_p)
        px = _in_proj(x, g_m, ml[0], ml[1], w_in_l)
        rg_x, ret_x, _ = _mixer(px, rope_lat, ctx_states, *mix_p)

        if l % 2 == 0:
            i = l // 2
            w1, w3, w2 = ffn_w1[i].astype(BF16), ffn_w3[i].astype(BF16), ffn_w2[i].astype(BF16)
            x, hx = _out_proj(x, rg_x, ret_x, w_out_l, ml[2], g_f, ml[3], ml[4])
            x = _ffn(hx, x, ml[5], w1, w3, w2)
            if not last:
                ctx, hc = _out_proj(ctx, rg_c, ret_c, w_out_l, mc[2], g_f, mc[3], mc[4])
                ctx = _ffn(hc, ctx, mc[5], w1, w3, w2)
        else:
            i = l // 2
            wr = jnp.zeros((d, LANES), F32).at[:, :ne].set(moe_router[i])
            wr_hi = wr.astype(BF16)
            wr_lo = (wr - wr_hi.astype(F32)).astype(BF16)
            wr2 = jnp.concatenate([wr_hi, wr_lo], axis=1)
            br = jnp.full((1, LANES), NEG_BIG, F32).at[0, :ne].set(moe_router_b[i])
            x, hx, route, counts = _out_proj(x, rg_x, ret_x, w_out_l, ml[2], g_f, ml[3], ml[4],
                                             router=(wr2, br))
            x = _moe(hx, x, ml[5], route, counts, g_final.reshape(1, d),
                     moe_w1[i].astype(BF16), moe_w3[i].astype(BF16), moe_w2[i].astype(BF16))
    return x
```

```python
import functools

import jax
import jax.numpy as jnp
from jax import lax
from jax.experimental import pallas as pl
from jax.experimental.pallas import tpu as pltpu

F32 = jnp.float32
BF16 = jnp.bfloat16

EPS = 1e-6
RG_C = 8.0
CONV_W = 4
CONV_LEFT = 2
RG_BLOCKS = 8
RET_HEADS = 4
RET_CHUNK = 128
ROPE_BASE = 10000.0
GRID_W = 64
N_EXPERTS = 8
N_MOD = 6

LANES = 128
SUBLANES = 8
VMEM_LIMIT = 56 * 1024 * 1024
NEG_BIG = -1e30
LOG2_E = 1.4426950408889634


def _cparams(sem):
    return pltpu.CompilerParams(dimension_semantics=sem, vmem_limit_bytes=VMEM_LIMIT)


def _modulation_kernel(c_ref, w_ref, b_ref, o_ref):
    c = c_ref[...]
    s = c * jax.nn.sigmoid(c)
    o_ref[...] = jnp.dot(s, w_ref[...], precision=lax.Precision.HIGHEST,
                         preferred_element_type=F32) + b_ref[...]


def _modulation(c_all, w_mod, b_mod):
    depth, d, six_d = w_mod.shape
    rows = c_all.shape[0]
    tn = 1536
    return pl.pallas_call(
        _modulation_kernel,
        grid=(depth, six_d // tn),
        in_specs=[
            pl.BlockSpec((rows, d), lambda l, j: (0, 0)),
            pl.BlockSpec((None, d, tn), lambda l, j: (l, 0, j)),
            pl.BlockSpec((None, 1, tn), lambda l, j: (l, 0, j)),
        ],
        out_specs=pl.BlockSpec((None, rows, tn), lambda l, j: (l, 0, j)),
        out_shape=jax.ShapeDtypeStruct((depth, rows, six_d), F32),
        compiler_params=_cparams(("arbitrary", "arbitrary")),
        name="modulation",
    )(c_all, w_mod, b_mod.reshape(depth, 1, six_d))


def _norm_mod(x, g, shift, scale):
    ms = jnp.mean(x * x, axis=-1, keepdims=True)
    y = x * lax.rsqrt(ms + EPS) * g
    return y * (1.0 + scale) + shift


def _in_proj_kernel(x_ref, g_ref, shift_ref, scale_ref, w_ref, o_ref):
    h = _norm_mod(x_ref[...], g_ref[...], shift_ref[...], scale_ref[...])
    o_ref[...] = jnp.dot(h.astype(BF16), w_ref[...],
                         preferred_element_type=F32).astype(o_ref.dtype)


def _in_proj(x, g, shift, scale, w):
    bsz, n, d = x.shape
    f = w.shape[1]
    tm = min(512, n)
    vec = pl.BlockSpec((None, 1, d), lambda b, i: (b, 0, 0))
    return pl.pallas_call(
        _in_proj_kernel,
        grid=(bsz, n // tm),
        in_specs=[
            pl.BlockSpec((None, tm, d), lambda b, i: (b, i, 0)),
            pl.BlockSpec((1, d), lambda b, i: (0, 0)),
            vec, vec,
            pl.BlockSpec((d, f), lambda b, i: (0, 0)),
        ],
        out_specs=pl.BlockSpec((None, tm, f), lambda b, i: (b, i, 0)),
        out_shape=jax.ShapeDtypeStruct((bsz, n, f), BF16),
        compiler_params=_cparams(("parallel", "parallel")),
        name="in_proj",
    )(x, g, shift, scale, w)


RG_TILE = 256
RG_CG = 256
RG_SEG = RG_TILE // SUBLANES
RG_PITCH = RG_SEG + SUBLANES


def _rglru_kernel(u_ref, yg_ref, cw_ref, cb_ref, wg_ref, bg_ref, lam_ref, h0_ref,
                  o_ref, hl_ref, u_scr, hf_scr, nat_scr):
    n = u_ref.shape[0]
    tt = RG_TILE
    nt = n // tt
    nseg = n // RG_SEG
    cg = u_ref.shape[1]
    lane_cols = [slice(g * LANES, (g + 1) * LANES) for g in range(cg // LANES)]

    zeros = jnp.zeros((RG_PITCH, LANES), F32)
    for g, cols in enumerate(lane_cols):
        u_scr[g, pl.ds(0, RG_PITCH), :] = zeros
        u_scr[g, pl.ds(nseg * RG_PITCH + RG_SEG, SUBLANES), :] = zeros[0:SUBLANES]

    def fill(s, carry):
        src = pl.multiple_of(s * RG_SEG, RG_SEG)
        dst = pl.multiple_of((s + 1) * RG_PITCH, SUBLANES)
        for g, cols in enumerate(lane_cols):
            rows = u_ref[pl.ds(src, RG_SEG), cols].astype(F32)
            u_scr[g, pl.ds(dst, RG_SEG), :] = rows
            u_scr[g, pl.ds(dst - SUBLANES, SUBLANES), :] = rows[0:SUBLANES]
        return carry

    lax.fori_loop(0, nseg, fill, 0)

    seg = RG_SEG
    cw_rows = [[jnp.broadcast_to(cw_ref[k:k + 1, c], (SUBLANES, LANES)) for c in lane_cols]
               for k in range(CONV_W)]
    cb_rows = [jnp.broadcast_to(cb_ref[:, c], (SUBLANES, LANES)) for c in lane_cols]

    def conv_tile(i):
        base = pl.multiple_of((i * SUBLANES + 1) * RG_PITCH, SUBLANES)
        blocks = []
        for j in range(seg):
            lane_groups = []
            for g in range(cg // LANES):
                acc = cb_rows[g]
                for k in range(CONV_W):
                    q = j + k - CONV_LEFT
                    start = base + q if q >= 0 else base - RG_PITCH + seg + q
                    acc = acc + cw_rows[k][g] * u_scr[g, pl.ds(start, SUBLANES, stride=RG_PITCH), :]
                lane_groups.append(acc)
            blocks.append(jnp.concatenate(lane_groups, axis=1))
        return jnp.concatenate(blocks, axis=0)

    def gates(uc, d):
        ub = uc.astype(BF16)
        ta = jnp.tanh(jnp.dot(ub, wg_ref[d, 0], preferred_element_type=F32) + bg_ref[d, 0:1, :])
        ti = jnp.tanh(jnp.dot(ub, wg_ref[d, 1], preferred_element_type=F32) + bg_ref[d, 1:2, :])
        z = -lam_ref[d:d + 1, :]
        sp = jnp.maximum(z, 0.0) + jnp.log1p(jnp.exp(-jnp.abs(z)))
        c2 = (-0.5 * RG_C * LOG2_E) * sp
        a = jnp.exp2(c2 + c2 * ta)
        b = jnp.exp2((0.5 * LOG2_E) * jnp.log(1.0 - a * a)) * ((0.5 + 0.5 * ti) * uc)
        return a, b

    def scan_tile(a, b, carry, reverse):
        steps = range(seg - 1, -1, -1) if reverse else range(seg)
        h = acum = None
        h_loc = [None] * seg
        a_cum = [None] * seg
        for j in steps:
            rows = slice(j * SUBLANES, (j + 1) * SUBLANES)
            h = b[rows] if h is None else a[rows] * h + b[rows]
            acum = a[rows] if acum is None else a[rows] * acum
            h_loc[j], a_cum[j] = h, acum
        order = range(SUBLANES - 1, -1, -1) if reverse else range(SUBLANES)
        c = carry
        c_in = [None] * SUBLANES
        for s in order:
            c_in[s] = c
            c = h[s:s + 1, :] + acum[s:s + 1, :] * c
        c_in = jnp.concatenate(c_in, axis=0)
        out = jnp.concatenate([h_loc[j] + a_cum[j] * c_in for j in range(seg)], axis=0)
        return out, c

    def fwd_body(i, carry):
        t0 = pl.multiple_of(i * tt, tt)
        uc = conv_tile(i)
        a, b = gates(uc, 0)
        h, carry = scan_tile(a, b, carry, False)
        hf_scr[pl.ds(t0, tt), :] = h
        return carry

    hf_last = lax.fori_loop(0, nt, fwd_body, h0_ref[0:1, :], unroll=min(2, nt))

    def bwd_body(i, carry):
        tile = nt - 1 - i
        t0 = pl.multiple_of(tile * tt, tt)
        uc = conv_tile(tile)
        a, b = gates(uc, 1)
        h, carry = scan_tile(a, b, carry, True)
        hsum = hf_scr[pl.ds(t0, tt), :] + h
        for j in range(seg):
            for g, cols in enumerate(lane_cols):
                nat_scr[g, pl.ds(j, SUBLANES, stride=RG_PITCH), :] = (
                    hsum[j * SUBLANES:(j + 1) * SUBLANES, cols])
        hnat = jnp.concatenate(
            [jnp.concatenate([nat_scr[g, pl.ds(s * RG_PITCH, seg), :] for s in range(SUBLANES)],
                             axis=0) for g in range(len(lane_cols))], axis=1)
        yg = yg_ref[pl.ds(t0, tt), :].astype(F32)
        o_ref[pl.ds(t0, tt), :] = (jax.nn.gelu(yg) * hnat).astype(o_ref.dtype)
        return carry

    hb_last = lax.fori_loop(0, nt, bwd_body, h0_ref[1:2, :], unroll=min(2, nt))
    hl_ref[0:1, :] = hf_last
    hl_ref[1:2, :] = hb_last


def _rglru(p, conv_w, conv_b, wg, bg, lam, h0):
    bsz, n, _ = p.shape
    rg_w = lam.shape[1]
    ncg = rg_w // RG_CG
    assert n % RG_TILE == 0 and rg_w % RG_CG == 0
    return pl.pallas_call(
        _rglru_kernel,
        grid=(bsz, ncg),
        in_specs=[
            pl.BlockSpec((None, n, RG_CG), lambda b, c: (b, 0, c)),
            pl.BlockSpec((None, n, RG_CG), lambda b, c: (b, 0, ncg + c)),
            pl.BlockSpec((CONV_W, RG_CG), lambda b, c: (0, c)),
            pl.BlockSpec((1, RG_CG), lambda b, c: (0, c)),
            pl.BlockSpec((2, 2, None, RG_CG, RG_CG), lambda b, c: (0, 0, c, 0, 0)),
            pl.BlockSpec((2, 2, RG_CG), lambda b, c: (0, 0, c)),
            pl.BlockSpec((2, RG_CG), lambda b, c: (0, c)),
            pl.BlockSpec((None, 2, RG_CG), lambda b, c: (b, 0, c)),
        ],
        out_specs=[
            pl.BlockSpec((None, n, RG_CG), lambda b, c: (b, 0, c)),
            pl.BlockSpec((None, 2, RG_CG), lambda b, c: (b, 0, c)),
        ],
        out_shape=[
            jax.ShapeDtypeStruct((bsz, n, rg_w), BF16),
            jax.ShapeDtypeStruct((bsz, 2, rg_w), F32),
        ],
        scratch_shapes=[
            pltpu.VMEM((RG_CG // LANES, (n // RG_SEG + 1) * RG_PITCH, LANES), F32),
            pltpu.VMEM((n, RG_CG), F32),
            pltpu.VMEM((RG_CG // LANES, SUBLANES * RG_PITCH, LANES), F32),
        ],
        compiler_params=_cparams(("parallel", "parallel")),
        name="rglru",
    )(p, p, conv_w, conv_b.reshape(1, rg_w), wg, bg, lam, h0)


RET_UNROLL = 8


def _retention_kernel(lg_ref, q_ref, k_ref, v_ref, g_ref, cs_ref, sn_ref, s0_ref,
                      o_ref, so_ref, qr_scr, kr_scr, st_scr):
    n, hd = q_ref.shape
    L = RET_CHUNK
    nc = n // L
    head = pl.program_id(1)
    lgf = lg_ref[0, head]
    lgb = lg_ref[1, head]

    def rope(t):
        return t * cs_ref[...] + pltpu.roll(t, hd // 2, axis=1) * sn_ref[...]

    qr_scr[...] = rope(q_ref[...].astype(F32)).astype(BF16)
    kr_scr[...] = (rope(k_ref[...].astype(F32)) * (hd ** -0.5)).astype(BF16)

    jc = lax.broadcasted_iota(jnp.int32, (L, 1), 0).astype(F32)
    q_dec_f = jnp.exp((jc + 1.0) * lgf)
    q_dec_b = jnp.exp((L - jc) * lgb)
    k_dec_f = jnp.exp((L - 1.0 - jc) * lgf)
    k_dec_b = jnp.exp(jc * lgb)
    ones = jnp.ones((1, hd), F32)
    chunk_f = jnp.exp(ones * (L * lgf))
    chunk_b = jnp.exp(ones * (L * lgb))
    ii = lax.broadcasted_iota(jnp.int32, (L, L), 0)
    jj = lax.broadcasted_iota(jnp.int32, (L, L), 1)
    diff = (ii - jj).astype(F32)
    dmat = jnp.where(diff >= 0.0, jnp.exp(jnp.maximum(diff, 0.0) * lgf),
                     jnp.exp(jnp.maximum(-diff, 0.0) * lgb))

    tn_dims = (((0,), (0,)), ((), ()))
    nt_dims = (((1,), (1,)), ((), ()))

    def state_update(c, state, k_dec, chunk_dec):
        t0 = pl.multiple_of(c * L, L)
        kd = (kr_scr[pl.ds(t0, L), :].astype(F32) * k_dec).astype(BF16)
        upd = lax.dot_general(kd, v_ref[pl.ds(t0, L), :], tn_dims, preferred_element_type=F32)
        return chunk_dec * state + upd

    def state_body(i, carry):
        f_state, r_state = carry
        cf = i
        cb = nc - 1 - i
        st_scr[cf, :, 0:hd] = f_state.astype(BF16)
        st_scr[cb, :, hd:2 * hd] = r_state.astype(BF16)
        return (state_update(cf, f_state, k_dec_f, chunk_f),
                state_update(cb, r_state, k_dec_b, chunk_b))

    f_fin, r_fin = lax.fori_loop(0, nc, state_body, (s0_ref[0], s0_ref[1]),
                                 unroll=min(RET_UNROLL, nc))
    so_ref[0] = f_fin
    so_ref[1] = r_fin

    def out_body(c, carry):
        t0 = pl.multiple_of(c * L, L)
        qc = qr_scr[pl.ds(t0, L), :]
        kc = kr_scr[pl.ds(t0, L), :]
        vc = v_ref[pl.ds(t0, L), :]
        s = lax.dot_general(qc, kc, nt_dims, preferred_element_type=F32)
        o = jnp.dot((s * dmat).astype(BF16), vc, preferred_element_type=F32)
        cross = jnp.dot(qc, st_scr[c], preferred_element_type=F32)
        o = o + cross[:, 0:hd] * q_dec_f + cross[:, hd:2 * hd] * q_dec_b
        mu = jnp.mean(o, axis=-1, keepdims=True)
        oc = o - mu
        var = jnp.mean(oc * oc, axis=-1, keepdims=True)
        gate = g_ref[pl.ds(t0, L), :].astype(F32)
        o_ref[pl.ds(t0, L), :] = (gate * jax.nn.sigmoid(gate) * (oc * lax.rsqrt(var + EPS))).astype(o_ref.dtype)
        return carry

    lax.fori_loop(0, nc, out_body, 0, unroll=min(RET_UNROLL, nc))


def _retention(p, log_g, cs, sn, s0, col0):
    bsz, n, _ = p.shape
    nh = log_g.shape[1]
    hd = cs.shape[1]
    cb0 = col0 // hd

    def col(which):
        return pl.BlockSpec((None, n, hd), lambda b, h, lg: (b, 0, cb0 + which * nh + h))

    state = pl.BlockSpec((None, 2, None, hd, hd), lambda b, h, lg: (b, 0, h, 0, 0))
    table = pl.BlockSpec((n, hd), lambda b, h, lg: (0, 0))
    grid_spec = pltpu.PrefetchScalarGridSpec(
        num_scalar_prefetch=1,
        grid=(bsz, nh),
        in_specs=[col(0), col(1), col(2), col(3), table, table, state],
        out_specs=[pl.BlockSpec((None, n, hd), lambda b, h, lg: (b, 0, h)), state],
        scratch_shapes=[
            pltpu.VMEM((n, hd), BF16),
            pltpu.VMEM((n, hd), BF16),
            pltpu.VMEM((n // RET_CHUNK, hd, 2 * hd), BF16),
        ],
    )
    return pl.pallas_call(
        _retention_kernel,
        grid_spec=grid_spec,
        out_shape=[
            jax.ShapeDtypeStruct((bsz, n, nh * hd), BF16),
            jax.ShapeDtypeStruct((bsz, 2, nh, hd, hd), F32),
        ],
        compiler_params=_cparams(("parallel", "parallel")),
        name="retention",
    )(log_g, p, p, p, p, cs, sn, s0)


R_RANK0, R_RANK1, R_E0, R_E1, R_W0, R_W1 = range(6)


def _out_proj_kernel(route, x_ref, rg_ref, ret_ref, wo_ref, gate_ref, g_ref, shift_ref, scale_ref,
                     *rest):
    if route:
        wr2_ref, br_ref, xo_ref, h_ref, rt_ref, cnt_ref, carry_scr = rest
    else:
        xo_ref, h_ref = rest
    rg_w = rg_ref.shape[1]
    y = jnp.dot(rg_ref[...], wo_ref[0:rg_w, :], preferred_element_type=F32)
    y = y + jnp.dot(ret_ref[...], wo_ref[rg_w:, :], preferred_element_type=F32)
    xn = x_ref[...] + gate_ref[...] * y
    xo_ref[...] = xn
    hx = _norm_mod(xn, g_ref[...], shift_ref[...], scale_ref[...])
    if not route:
        h_ref[...] = hx.astype(h_ref.dtype)
        return

    tm, d = hx.shape
    for s in range(d // LANES):
        h_ref[pl.ds(s, tm, stride=d // LANES), :] = hx[:, s * LANES:(s + 1) * LANES]

    h_hi = hx.astype(BF16)
    h_lo = (hx - h_hi.astype(F32)).astype(BF16)
    part = jnp.dot(h_hi, wr2_ref[...], preferred_element_type=F32)
    logits = (part[:, :LANES] + part[:, LANES:]
              + jnp.dot(h_lo, wr2_ref[:, :LANES], preferred_element_type=F32) + br_ref[...])
    lane_i = lax.broadcasted_iota(jnp.int32, logits.shape, 1)
    lane = lane_i.astype(F32)
    m0 = jnp.max(logits, axis=-1, keepdims=True)
    i0 = jnp.min(jnp.where(logits == m0, lane, float(LANES)), axis=-1, keepdims=True)
    rest_l = jnp.where(lane == i0, NEG_BIG, logits)
    m1 = jnp.max(rest_l, axis=-1, keepdims=True)
    i1 = jnp.min(jnp.where(rest_l == m1, lane, float(LANES)), axis=-1, keepdims=True)
    e = jnp.exp(m1 - m0)
    w0 = 1.0 / (1.0 + e)
    w1 = e / (1.0 + e)

    @pl.when((pl.program_id(0) == 0) & (pl.program_id(1) == 0))
    def _():
        carry_scr[...] = jnp.zeros_like(carry_scr)

    sel0 = lane == i0
    sel1 = lane == i1
    mask = jnp.where(sel0 | sel1, 1.0, 0.0)
    ri = lax.broadcasted_iota(jnp.int32, (tm, tm), 0)
    ci = lax.broadcasted_iota(jnp.int32, (tm, tm), 1)
    tri = jnp.where(ri > ci, 1.0, 0.0).astype(BF16)
    before = jnp.dot(tri, mask.astype(BF16), preferred_element_type=F32) + carry_scr[...]
    rank0 = jnp.sum(jnp.where(sel0, before, 0.0), axis=-1, keepdims=True)
    rank1 = jnp.sum(jnp.where(sel1, before, 0.0), axis=-1, keepdims=True)
    total = carry_scr[...] + jnp.sum(mask, axis=0, keepdims=True)
    carry_scr[...] = total
    cnt_ref[...] = total
    rec = jnp.zeros_like(logits)
    for idx, val in ((R_RANK0, rank0), (R_RANK1, rank1), (R_E0, i0), (R_E1, i1),
                     (R_W0, w0), (R_W1, w1)):
        rec = jnp.where(lane_i == idx, val, rec)
    rt_ref[...] = rec


def _out_proj(x, rg, ret, w_out, gate, g, shift, scale, router=None):
    bsz, n, d = x.shape
    tm = min(512, n)
    route = router is not None
    vec = pl.BlockSpec((None, 1, d), lambda b, i: (b, 0, 0))
    tile = lambda w: pl.BlockSpec((None, tm, w), lambda b, i: (b, i, 0))
    in_specs = [
        tile(d), tile(rg.shape[2]), tile(ret.shape[2]),
        pl.BlockSpec(w_out.shape, lambda b, i: (0, 0)),
        vec, pl.BlockSpec((1, d), lambda b, i: (0, 0)), vec, vec,
    ]
    args = [x, rg, ret, w_out, gate, g, shift, scale]
    scratch = []
    if route:
        in_specs += [pl.BlockSpec((d, 2 * LANES), lambda b, i: (0, 0)),
                     pl.BlockSpec((1, LANES), lambda b, i: (0, 0))]
        args += list(router)
        out_specs = [tile(d),
                     pl.BlockSpec((None, tm * (d // LANES), LANES), lambda b, i: (b, i, 0)),
                     tile(LANES),
                     pl.BlockSpec((1, LANES), lambda b, i: (0, 0))]
        out_shape = [jax.ShapeDtypeStruct((bsz, n, d), F32),
                     jax.ShapeDtypeStruct((bsz, n * (d // LANES), LANES), F32),
                     jax.ShapeDtypeStruct((bsz, n, LANES), F32),
                     jax.ShapeDtypeStruct((1, LANES), F32)]
        scratch = [pltpu.VMEM((1, LANES), F32)]
        sem = ("arbitrary", "arbitrary")
    else:
        out_specs = [tile(d), tile(d)]
        out_shape = [jax.ShapeDtypeStruct((bsz, n, d), F32), jax.ShapeDtypeStruct((bsz, n, d), BF16)]
        sem = ("parallel", "parallel")
    return pl.pallas_call(
        functools.partial(_out_proj_kernel, route),
        grid=(bsz, n // tm),
        in_specs=in_specs,
        out_specs=out_specs,
        out_shape=out_shape,
        scratch_shapes=scratch,
        compiler_params=_cparams(sem),
        name="out_proj_route" if route else "out_proj",
    )(*args)


def _swiglu_acc(h, w1_ref, w3_ref, w2_ref, row_scale=None):
    a = jnp.dot(h, w1_ref[...], preferred_element_type=F32)
    b = jnp.dot(h, w3_ref[...], preferred_element_type=F32)
    act = a * jax.nn.sigmoid(a) * b
    if row_scale is not None:
        act = act * row_scale
    return jnp.dot(act.astype(BF16), w2_ref[...], preferred_element_type=F32)


def _ffn_kernel(h_ref, x_ref, gate_ref, w1_ref, w3_ref, w2_ref, o_ref):
    o_ref[...] = x_ref[...] + gate_ref[...] * _swiglu_acc(h_ref[...], w1_ref, w3_ref, w2_ref)


def _ffn(h, x, gate, w1, w3, w2):
    bsz, n, d = x.shape
    f = w1.shape[1]
    tm = min(512, n)
    tile = pl.BlockSpec((None, tm, d), lambda b, i: (b, i, 0))
    resident = lambda shape: pl.BlockSpec(shape, lambda b, i: (0, 0), pipeline_mode=pl.Buffered(1))
    return pl.pallas_call(
        _ffn_kernel,
        grid=(bsz, n // tm),
        in_specs=[
            tile, tile,
            pl.BlockSpec((None, 1, d), lambda b, i: (b, 0, 0)),
            resident((d, f)), resident((d, f)), resident((f, d)),
        ],
        out_specs=tile,
        out_shape=jax.ShapeDtypeStruct((bsz, n, d), F32),
        compiler_params=_cparams(("parallel", "parallel")),
        name="ffn",
    )(h, x, gate, w1, w3, w2)


MOE_BLK = 512
MOE_TF = 1792


def _dispatch_kernel(d0_ref, d1_ref, zr_ref, hx_ref, hs_ref, zbuf, sem, zsem):
    tm = hx_ref.shape[0]
    base = pl.program_id(0) * tm

    @pl.when(pl.program_id(0) == 0)
    def _():
        zbuf[...] = jnp.zeros_like(zbuf)
        for z in range(zr_ref.shape[0]):
            @pl.when(zr_ref[z] >= 0)
            def _():
                pltpu.make_async_copy(zbuf, hs_ref.at[pl.ds(zr_ref[z], MOE_BLK)], zsem).start()
        for z in range(zr_ref.shape[0]):
            @pl.when(zr_ref[z] >= 0)
            def _():
                pltpu.make_async_copy(zbuf, hs_ref.at[pl.ds(zr_ref[z], MOE_BLK)], zsem).wait()

    def issue(t, carry):
        pltpu.make_async_copy(hx_ref.at[t], hs_ref.at[d0_ref[base + t]], sem).start()
        pltpu.make_async_copy(hx_ref.at[t], hs_ref.at[d1_ref[base + t]], sem).start()
        return carry

    lax.fori_loop(0, tm, issue, 0, unroll=8)
    for _ in range(2):
        pltpu.make_async_copy(hx_ref, hs_ref.at[pl.ds(0, tm)], sem).wait()


def _dispatch(dest0, dest1, zero_rows, hx, n_rows):
    n_tok, s, lanes = hx.shape
    tm = 512
    grid_spec = pltpu.PrefetchScalarGridSpec(
        num_scalar_prefetch=3,
        grid=(n_tok // tm,),
        in_specs=[pl.BlockSpec((tm, s, lanes), lambda i, d0, d1, zr: (i, 0, 0))],
        out_specs=pl.BlockSpec(memory_space=pl.ANY),
        scratch_shapes=[pltpu.VMEM((MOE_BLK, s, lanes), F32),
                        pltpu.SemaphoreType.DMA(()), pltpu.SemaphoreType.DMA(())],
    )
    return pl.pallas_call(
        _dispatch_kernel,
        grid_spec=grid_spec,
        out_shape=jax.ShapeDtypeStruct((n_rows, s, lanes), F32),
        compiler_params=_cparams(("arbitrary",)),
        name="moe_dispatch",
    )(dest0, dest1, zero_rows, hx)


def _experts_kernel(be_ref, nu_ref, hs_ref, w1_ref, w3_ref, w2_ref, y_ref, hb_scr, acc_scr):
    j = pl.program_id(0)
    f = pl.program_id(1)
    blk, d = acc_scr.shape
    ns = d // LANES

    @pl.when(j < nu_ref[0])
    def _():
        @pl.when(f == 0)
        def _():
            for s in range(ns):
                hb_scr[:, s * LANES:(s + 1) * LANES] = (
                    hs_ref[pl.ds(s, blk, stride=ns), :].astype(BF16))
            acc_scr[...] = jnp.zeros_like(acc_scr)

        acc_scr[...] += _swiglu_acc(hb_scr[...], w1_ref, w3_ref, w2_ref)

        @pl.when(f == pl.num_programs(1) - 1)
        def _():
            for s in range(ns):
                y_ref[pl.ds(s, blk, stride=ns), :] = acc_scr[:, s * LANES:(s + 1) * LANES]

    @pl.when((j >= nu_ref[0]) & (f == 0))
    def _():
        y_ref[...] = jnp.zeros_like(y_ref)


def _experts(block_expert, n_used, hs, w1, w3, w2):
    ne, d, f = w1.shape
    s = d // LANES
    n_rows = hs.shape[0] // s
    nf = f // MOE_TF
    n_blocks = n_rows // MOE_BLK

    def row_map(j, f_, be, nu):
        return (jnp.minimum(j, nu[0] - 1), 0)

    def f_idx(j, f_, nu):
        return jnp.where(j < nu[0], f_, nf - 1)

    grid_spec = pltpu.PrefetchScalarGridSpec(
        num_scalar_prefetch=2,
        grid=(n_blocks, nf),
        in_specs=[
            pl.BlockSpec((MOE_BLK * s, LANES), row_map),
            pl.BlockSpec((None, d, MOE_TF), lambda j, f_, be, nu: (be[j], 0, f_idx(j, f_, nu))),
            pl.BlockSpec((None, d, MOE_TF), lambda j, f_, be, nu: (be[j], 0, f_idx(j, f_, nu))),
            pl.BlockSpec((None, MOE_TF, d), lambda j, f_, be, nu: (be[j], f_idx(j, f_, nu), 0)),
        ],
        out_specs=pl.BlockSpec((MOE_BLK * s, LANES), lambda j, f_, be, nu: (j, 0)),
        scratch_shapes=[pltpu.VMEM((MOE_BLK, d), BF16), pltpu.VMEM((MOE_BLK, d), F32)],
    )
    return pl.pallas_call(
        _experts_kernel,
        grid_spec=grid_spec,
        out_shape=jax.ShapeDtypeStruct((n_rows * s, LANES), F32),
        compiler_params=_cparams(("arbitrary", "arbitrary")),
        name="moe_experts",
    )(block_expert, n_used, hs, w1, w3, w2)


def _combine_kernel(d0_ref, d1_ref, x_ref, gate_ref, rt_ref, gfin_ref, y_ref, o_ref, ybuf, sem):
    tm, d = x_ref.shape
    ns = d // LANES
    step = pl.program_id(0) * pl.num_programs(1) + pl.program_id(1)
    nsteps = pl.num_programs(0) * pl.num_programs(1)
    slot = step % 2

    def row(ref, r):
        return ref.at[pl.ds(pl.multiple_of(r * ns, ns), ns)]

    def start_gather(block, into):
        base = block * tm

        def issue(t, carry):
            pltpu.make_async_copy(row(y_ref, d0_ref[base + t]), row(ybuf.at[into, 0], t),
                                  sem.at[into]).start()
            pltpu.make_async_copy(row(y_ref, d1_ref[base + t]), row(ybuf.at[into, 1], t),
                                  sem.at[into]).start()
            return carry

        lax.fori_loop(0, tm, issue, 0, unroll=8)

    @pl.when(step == 0)
    def _():
        start_gather(0, 0)

    @pl.when(step + 1 < nsteps)
    def _():
        start_gather(step + 1, 1 - slot)

    for k in range(2):
        pltpu.make_async_copy(y_ref.at[pl.ds(0, tm * ns)], ybuf.at[slot, k], sem.at[slot]).wait()

    rt = rt_ref[...]
    w0 = rt[:, R_W0:R_W0 + 1]
    w1 = rt[:, R_W1:R_W1 + 1]
    ss = jnp.zeros((tm, 1), F32)
    for s in range(ns):
        cols = slice(s * LANES, (s + 1) * LANES)
        moe = (w0 * ybuf[slot, 0, pl.ds(s, tm, stride=ns), :]
               + w1 * ybuf[slot, 1, pl.ds(s, tm, stride=ns), :])
        xn = x_ref[:, cols] + gate_ref[:, cols] * moe
        ss = ss + jnp.sum(xn * xn, axis=-1, keepdims=True)
        o_ref[:, cols] = xn
    o_ref[...] = o_ref[...] * lax.rsqrt(ss / d + EPS) * gfin_ref[...]


def _combine(dest0, dest1, x, gate, route, g_final, y):
    bsz, n, d = x.shape
    s = d // LANES
    tm = 512
    grid_spec = pltpu.PrefetchScalarGridSpec(
        num_scalar_prefetch=2,
        grid=(bsz, n // tm),
        in_specs=[
            pl.BlockSpec((None, tm, d), lambda b, i, d0, d1: (b, i, 0)),
            pl.BlockSpec((None, 1, d), lambda b, i, d0, d1: (b, 0, 0)),
            pl.BlockSpec((None, tm, LANES), lambda b, i, d0, d1: (b, i, 0)),
            pl.BlockSpec((1, d), lambda b, i, d0, d1: (0, 0)),
            pl.BlockSpec(memory_space=pl.ANY),
        ],
        out_specs=pl.BlockSpec((None, tm, d), lambda b, i, d0, d1: (b, i, 0)),
        scratch_shapes=[pltpu.VMEM((2, 2, tm * s, LANES), F32), pltpu.SemaphoreType.DMA((2,))],
    )
    return pl.pallas_call(
        _combine_kernel,
        grid_spec=grid_spec,
        out_shape=jax.ShapeDtypeStruct((bsz, n, d), F32),
        compiler_params=_cparams(("arbitrary", "arbitrary")),
        name="moe_combine",
    )(dest0, dest1, x, gate, route, g_final, y)


def _moe(hx, x, gate, route, counts, g_final, w1, w3, w2):
    bsz, n, d = x.shape
    ne = w1.shape[0]
    n_tok = bsz * n
    n_rows = -(-(2 * n_tok) // MOE_BLK) * MOE_BLK + ne * MOE_BLK
    n_blocks = n_rows // MOE_BLK
    cnt = counts[0, :ne].astype(jnp.int32)
    padded = (cnt + MOE_BLK - 1) // MOE_BLK * MOE_BLK
    pad_end = jnp.cumsum(padded)
    pad_start = pad_end - padded
    n_used = pad_end[-1] // MOE_BLK
    blk = jnp.minimum(jnp.arange(n_blocks, dtype=jnp.int32), n_used - 1)
    block_expert = jnp.minimum(
        jnp.sum((blk[:, None] * MOE_BLK >= pad_end[None, :]).astype(jnp.int32), axis=1), ne - 1)
    rt = route.reshape(n_tok, LANES)
    e0 = rt[:, R_E0].astype(jnp.int32)
    e1 = rt[:, R_E1].astype(jnp.int32)
    dest0 = pad_start[e0] + rt[:, R_RANK0].astype(jnp.int32)
    dest1 = pad_start[e1] + rt[:, R_RANK1].astype(jnp.int32)

    last_blk = jnp.where(padded > 0, pad_end - MOE_BLK, -1)
    trail = n_used + jnp.arange(ne, dtype=jnp.int32)
    trail = jnp.where(trail < n_blocks, trail * MOE_BLK, -1)
    zero_rows = jnp.concatenate([last_blk, trail]).astype(jnp.int32)

    ns = d // LANES
    hs = _dispatch(dest0, dest1, zero_rows, hx.reshape(n_tok, ns, LANES), n_rows)
    y = _experts(block_expert, n_used.reshape(1).astype(jnp.int32),
                 hs.reshape(n_rows * ns, LANES), w1, w3, w2)
    return _combine(dest0, dest1, x, gate, route, g_final, y)


def _rope_tables(n, hd):
    rows = n // GRID_W
    n_freq = hd // 4
    inv = ROPE_BASE ** (-jnp.arange(n_freq, dtype=F32) / n_freq)
    row_ang = jnp.arange(rows, dtype=F32)[:, None] * inv
    col_ang = jnp.arange(GRID_W, dtype=F32)[:, None] * inv

    def table(of_row, of_col):
        a = jnp.broadcast_to(of_row[:, None, :], (rows, GRID_W, n_freq))
        b = jnp.broadcast_to(of_col[None, :, :], (rows, GRID_W, n_freq))
        return jnp.concatenate([a, b], axis=-1).reshape(n, 2 * n_freq)

    cos = table(jnp.cos(row_ang), jnp.cos(col_ang))
    sin = table(jnp.sin(row_ang), jnp.sin(col_ang))
    return jnp.concatenate([cos, cos], axis=-1), jnp.concatenate([-sin, sin], axis=-1)


def _block_diag_gates(rg_wa, rg_wx):
    w = jnp.stack([rg_wa, rg_wx], axis=1)
    nd, ng, nk, c, _ = w.shape
    per = RG_CG // c
    w = w.reshape(nd, ng, nk // per, per, c, c)
    eye = jnp.eye(per, dtype=w.dtype)
    full = jnp.einsum('dgmpij,pq->dgmpiqj', w, eye)
    return full.reshape(nd, ng, nk // per, RG_CG, RG_CG).astype(BF16)


def _mixer(p, rope, states, conv_w, conv_b, wg, bg, lam, log_g, rg_w):
    rg_out, h_last = _rglru(p, conv_w, conv_b, wg, bg, lam, states[0])
    ret_out, s_last = _retention(p, log_g, rope[0], rope[1], states[1], 2 * rg_w)
    return rg_out, ret_out, (h_last, s_last)


def kernel(x, c, ctx, c_ctx, w_mod, b_mod, g_mix, g_ffn, g_final, w_in, w_out, conv_w, conv_b,
           rg_wa, rg_ba, rg_wx, rg_bx, rg_lam, ret_decay, ffn_w1, ffn_w3, ffn_w2,
           moe_router, moe_router_b, moe_w1, moe_w3, moe_w2):
    bsz, n_lat, d = x.shape
    n_ctx = ctx.shape[1]
    depth = w_mod.shape[0]
    rg_w = rg_lam.shape[2]
    nh = ret_decay.shape[2]
    hd = (w_out.shape[1] - rg_w) // nh
    ne = moe_router.shape[2]
    assert depth == 2, "kernel is written for the two-layer block (dense FFN, then MoE)"

    rows = -(-(bsz + 1) // SUBLANES) * SUBLANES
    c_all = jnp.zeros((rows, d), F32).at[:bsz].set(c).at[bsz].set(c_ctx)
    mod = _modulation(c_all, w_mod, b_mod)
    mod = mod.reshape(depth, rows, N_MOD, d)
    mod_lat = mod[:, :bsz, None]
    mod_ctx = jnp.broadcast_to(mod[:, bsz:bsz + 1, None], mod_lat.shape)

    rope_lat = _rope_tables(n_lat, hd)
    rope_ctx = (jnp.ones((n_ctx, hd), F32), jnp.zeros((n_ctx, hd), F32))
    log_g = jax.nn.log_sigmoid(ret_decay.astype(F32))
    zero_states = (jnp.zeros((bsz, 2, rg_w), F32), jnp.zeros((bsz, 2, nh, hd, hd), F32))

    for l in range(depth):
        last = l == depth - 1
        ml = [mod_lat[l, :, :, j] for j in range(N_MOD)]
        mc = [mod_ctx[l, :, :, j] for j in range(N_MOD)]
        g_m = g_mix[l].reshape(1, d)
        g_f = g_ffn[l].reshape(1, d)
        w_in_l = w_in[l].astype(BF16)
        w_out_l = w_out[l].astype(BF16)
        wg = _block_diag_gates(0.5 * rg_wa[l], 0.5 * rg_wx[l])
        bg = 0.5 * jnp.stack([rg_ba[l], rg_bx[l]], axis=1)
        mix_p = (conv_w[l], conv_b[l], wg, bg, rg_lam[l], log_g[l], rg_w)

        pc = _in_proj(ctx, g_m, mc[0], mc[1], w_in_l)
        rg_c, ret_c, ctx_states = _mixer(pc, rope_ctx, zero_states, *mix_p)
        px = _in_proj(x, g_m, ml[0], ml[1], w_in_l)
        rg_x, ret_x, _ = _mixer(px, rope_lat, ctx_states, *mix_p)

        if l % 2 == 0:
            i = l // 2
            w1, w3, w2 = ffn_w1[i].astype(BF16), ffn_w3[i].astype(BF16), ffn_w2[i].astype(BF16)
            x, hx = _out_proj(x, rg_x, ret_x, w_out_l, ml[2], g_f, ml[3], ml[4])
            x = _ffn(hx, x, ml[5], w1, w3, w2)
            if not last:
                ctx, hc = _out_proj(ctx, rg_c, ret_c, w_out_l, mc[2], g_f, mc[3], mc[4])
                ctx = _ffn(hc, ctx, mc[5], w1, w3, w2)
        else:
            i = l // 2
            wr = jnp.zeros((d, LANES), F32).at[:, :ne].set(moe_router[i])
            wr_hi = wr.astype(BF16)
            wr_lo = (wr - wr_hi.astype(F32)).astype(BF16)
            wr2 = jnp.concatenate([wr_hi, wr_lo], axis=1)
            br = jnp.full((1, LANES), NEG_BIG, F32).at[0, :ne].set(moe_router_b[i])
            x, hx, route, counts = _out_proj(x, rg_x, ret_x, w_out_l, ml[2], g_f, ml[3], ml[4],
                                             router=(wr2, br))
            x = _moe(hx, x, ml[5], route, counts, g_final.reshape(1, d),
                     moe_w1[i].astype(BF16), moe_w3[i].astype(BF16), moe_w2[i].astype(BF16))
    return x
```

```python
import functools

import jax
import jax.numpy as jnp
from jax import lax
from jax.experimental import pallas as pl
from jax.experimental.pallas import tpu as pltpu

F32 = jnp.float32
BF16 = jnp.bfloat16

EPS = 1e-6
RG_C = 8.0
CONV_W = 4
CONV_LEFT = 2
RG_BLOCKS = 8
RET_HEADS = 4
RET_CHUNK = 128
ROPE_BASE = 10000.0
GRID_W = 64
N_EXPERTS = 8
N_MOD = 6

LANES = 128
SUBLANES = 8
VMEM_LIMIT = 56 * 1024 * 1024
NEG_BIG = -1e30
LOG2_E = 1.4426950408889634


def _cparams(sem):
    return pltpu.CompilerParams(dimension_semantics=sem, vmem_limit_bytes=VMEM_LIMIT)


def _modulation_kernel(c_ref, w_ref, b_ref, o_ref):
    c = c_ref[...]
    s = c * jax.nn.sigmoid(c)
    o_ref[...] = jnp.dot(s, w_ref[...], precision=lax.Precision.HIGHEST,
                         preferred_element_type=F32) + b_ref[...]


def _modulation(c_all, w_mod, b_mod):
    depth, d, six_d = w_mod.shape
    rows = c_all.shape[0]
    tn = 1536
    return pl.pallas_call(
        _modulation_kernel,
        grid=(depth, six_d // tn),
        in_specs=[
            pl.BlockSpec((rows, d), lambda l, j: (0, 0)),
            pl.BlockSpec((None, d, tn), lambda l, j: (l, 0, j)),
            pl.BlockSpec((None, 1, tn), lambda l, j: (l, 0, j)),
        ],
        out_specs=pl.BlockSpec((None, rows, tn), lambda l, j: (l, 0, j)),
        out_shape=jax.ShapeDtypeStruct((depth, rows, six_d), F32),
        compiler_params=_cparams(("arbitrary", "arbitrary")),
        name="modulation",
    )(c_all, w_mod, b_mod.reshape(depth, 1, six_d))


def _norm_mod(x, g, shift, scale):
    ms = jnp.mean(x * x, axis=-1, keepdims=True)
    y = x * lax.rsqrt(ms + EPS) * g
    return y * (1.0 + scale) + shift


def _in_proj_kernel(x_ref, g_ref, shift_ref, scale_ref, w_ref, o_ref):
    h = _norm_mod(x_ref[...], g_ref[...], shift_ref[...], scale_ref[...])
    o_ref[...] = jnp.dot(h.astype(BF16), w_ref[...],
                         preferred_element_type=F32).astype(o_ref.dtype)


def _in_proj(x, g, shift, scale, w):
    bsz, n, d = x.shape
    f = w.shape[1]
    tm = min(512, n)
    vec = pl.BlockSpec((None, 1, d), lambda b, i: (b, 0, 0))
    return pl.pallas_call(
        _in_proj_kernel,
        grid=(bsz, n // tm),
        in_specs=[
            pl.BlockSpec((None, tm, d), lambda b, i: (b, i, 0)),
            pl.BlockSpec((1, d), lambda b, i: (0, 0)),
            vec, vec,
            pl.BlockSpec((d, f), lambda b, i: (0, 0)),
        ],
        out_specs=pl.BlockSpec((None, tm, f), lambda b, i: (b, i, 0)),
        out_shape=jax.ShapeDtypeStruct((bsz, n, f), BF16),
        compiler_params=_cparams(("parallel", "parallel")),
        name="in_proj",
    )(x, g, shift, scale, w)


RG_TILE = 256
RG_CG = 256
RG_SEG = RG_TILE // SUBLANES
RG_PITCH = RG_SEG + SUBLANES


def _rglru_kernel(u_ref, yg_ref, cw_ref, cb_ref, wg_ref, bg_ref, lam_ref, h0_ref,
                  o_ref, hl_ref, u_scr, hf_scr, nat_scr):
    n = u_ref.shape[0]
    tt = RG_TILE
    nt = n // tt
    nseg = n // RG_SEG
    cg = u_ref.shape[1]
    lane_cols = [slice(g * LANES, (g + 1) * LANES) for g in range(cg // LANES)]

    zeros = jnp.zeros((RG_PITCH, LANES), F32)
    for g, cols in enumerate(lane_cols):
        u_scr[g, pl.ds(0, RG_PITCH), :] = zeros
        u_scr[g, pl.ds(nseg * RG_PITCH + RG_SEG, SUBLANES), :] = zeros[0:SUBLANES]

    def fill(s, carry):
        src = pl.multiple_of(s * RG_SEG, RG_SEG)
        dst = pl.multiple_of((s + 1) * RG_PITCH, SUBLANES)
        for g, cols in enumerate(lane_cols):
            rows = u_ref[pl.ds(src, RG_SEG), cols].astype(F32)
            u_scr[g, pl.ds(dst, RG_SEG), :] = rows
            u_scr[g, pl.ds(dst - SUBLANES, SUBLANES), :] = rows[0:SUBLANES]
        return carry

    lax.fori_loop(0, nseg, fill, 0)

    seg = RG_SEG
    cw_rows = [[jnp.broadcast_to(cw_ref[k:k + 1, c], (SUBLANES, LANES)) for c in lane_cols]
               for k in range(CONV_W)]
    cb_rows = [jnp.broadcast_to(cb_ref[:, c], (SUBLANES, LANES)) for c in lane_cols]

    def conv_tile(i):
        base = pl.multiple_of((i * SUBLANES + 1) * RG_PITCH, SUBLANES)
        blocks = []
        for j in range(seg):
            lane_groups = []
            for g in range(cg // LANES):
                acc = cb_rows[g]
                for k in range(CONV_W):
                    q = j + k - CONV_LEFT
                    start = base + q if q >= 0 else base - RG_PITCH + seg + q
                    acc = acc + cw_rows[k][g] * u_scr[g, pl.ds(start, SUBLANES, stride=RG_PITCH), :]
                lane_groups.append(acc)
            blocks.append(jnp.concatenate(lane_groups, axis=1))
        return jnp.concatenate(blocks, axis=0)

    def gates(uc, d):
        ub = uc.astype(BF16)
        ta = jnp.tanh(jnp.dot(ub, wg_ref[d, 0], preferred_element_type=F32) + bg_ref[d, 0:1, :])
        ti = jnp.tanh(jnp.dot(ub, wg_ref[d, 1], preferred_element_type=F32) + bg_ref[d, 1:2, :])
        z = -lam_ref[d:d + 1, :]
        sp = jnp.maximum(z, 0.0) + jnp.log1p(jnp.exp(-jnp.abs(z)))
        c2 = (-0.5 * RG_C * LOG2_E) * sp
        a = jnp.exp2(c2 + c2 * ta)
        b = jnp.exp2((0.5 * LOG2_E) * jnp.log(1.0 - a * a)) * ((0.5 + 0.5 * ti) * uc)
        return a, b

    def scan_tile(a, b, carry, reverse):
        steps = range(seg - 1, -1, -1) if reverse else range(seg)
        h = acum = None
        h_loc = [None] * seg
        a_cum = [None] * seg
        for j in steps:
            rows = slice(j * SUBLANES, (j + 1) * SUBLANES)
            h = b[rows] if h is None else a[rows] * h + b[rows]
            acum = a[rows] if acum is None else a[rows] * acum
            h_loc[j], a_cum[j] = h, acum
        order = range(SUBLANES - 1, -1, -1) if reverse else range(SUBLANES)
        c = carry
        c_in = [None] * SUBLANES
        for s in order:
            c_in[s] = c
            c = h[s:s + 1, :] + acum[s:s + 1, :] * c
        c_in = jnp.concatenate(c_in, axis=0)
        out = jnp.concatenate([h_loc[j] + a_cum[j] * c_in for j in range(seg)], axis=0)
        return out, c

    def fwd_body(i, carry):
        t0 = pl.multiple_of(i * tt, tt)
        uc = conv_tile(i)
        a, b = gates(uc, 0)
        h, carry = scan_tile(a, b, carry, False)
        hf_scr[pl.ds(t0, tt), :] = h
        return carry

    hf_last = lax.fori_loop(0, nt, fwd_body, h0_ref[0:1, :], unroll=min(2, nt))

    def bwd_body(i, carry):
        tile = nt - 1 - i
        t0 = pl.multiple_of(tile * tt, tt)
        uc = conv_tile(tile)
        a, b = gates(uc, 1)
        h, carry = scan_tile(a, b, carry, True)
        hsum = hf_scr[pl.ds(t0, tt), :] + h
        for j in range(seg):
            for g, cols in enumerate(lane_cols):
                nat_scr[g, pl.ds(j, SUBLANES, stride=RG_PITCH), :] = (
                    hsum[j * SUBLANES:(j + 1) * SUBLANES, cols])
        hnat = jnp.concatenate(
            [jnp.concatenate([nat_scr[g, pl.ds(s * RG_PITCH, seg), :] for s in range(SUBLANES)],
                             axis=0) for g in range(len(lane_cols))], axis=1)
        yg = yg_ref[pl.ds(t0, tt), :].astype(F32)
        o_ref[pl.ds(t0, tt), :] = (jax.nn.gelu(yg) * hnat).astype(o_ref.dtype)
        return carry

    hb_last = lax.fori_loop(0, nt, bwd_body, h0_ref[1:2, :], unroll=min(2, nt))
    hl_ref[0:1, :] = hf_last
    hl_ref[1:2, :] = hb_last


def _rglru(p, conv_w, conv_b, wg, bg, lam, h0):
    bsz, n, _ = p.shape
    rg_w = lam.shape[1]
    ncg = rg_w // RG_CG
    assert n % RG_TILE == 0 and rg_w % RG_CG == 0
    return pl.pallas_call(
        _rglru_kernel,
        grid=(bsz, ncg),
        in_specs=[
            pl.BlockSpec((None, n, RG_CG), lambda b, c: (b, 0, c)),
            pl.BlockSpec((None, n, RG_CG), lambda b, c: (b, 0, ncg + c)),
            pl.BlockSpec((CONV_W, RG_CG), lambda b, c: (0, c)),
            pl.BlockSpec((1, RG_CG), lambda b, c: (0, c)),
            pl.BlockSpec((2, 2, None, RG_CG, RG_CG), lambda b, c: (0, 0, c, 0, 0)),
            pl.BlockSpec((2, 2, RG_CG), lambda b, c: (0, 0, c)),
            pl.BlockSpec((2, RG_CG), lambda b, c: (0, c)),
            pl.BlockSpec((None, 2, RG_CG), lambda b, c: (b, 0, c)),
        ],
        out_specs=[
            pl.BlockSpec((None, n, RG_CG), lambda b, c: (b, 0, c)),
            pl.BlockSpec((None, 2, RG_CG), lambda b, c: (b, 0, c)),
        ],
        out_shape=[
            jax.ShapeDtypeStruct((bsz, n, rg_w), BF16),
            jax.ShapeDtypeStruct((bsz, 2, rg_w), F32),
        ],
        scratch_shapes=[
            pltpu.VMEM((RG_CG // LANES, (n // RG_SEG + 1) * RG_PITCH, LANES), F32),
            pltpu.VMEM((n, RG_CG), F32),
            pltpu.VMEM((RG_CG // LANES, SUBLANES * RG_PITCH, LANES), F32),
        ],
        compiler_params=_cparams(("parallel", "parallel")),
        name="rglru",
    )(p, p, conv_w, conv_b.reshape(1, rg_w), wg, bg, lam, h0)


RET_UNROLL = 8


def _retention_kernel(lg_ref, q_ref, k_ref, v_ref, g_ref, cs_ref, sn_ref, s0_ref,
                      o_ref, so_ref, kr_scr, st_scr):
    n, hd = q_ref.shape
    L = RET_CHUNK
    nc = n // L
    head = pl.program_id(1)
    lgf = lg_ref[0, head]
    lgb = lg_ref[1, head]

    def rope(t, rows):
        return t * cs_ref[rows, :] + pltpu.roll(t, hd // 2, axis=1) * sn_ref[rows, :]

    kr_scr[...] = (rope(k_ref[...].astype(F32), slice(None)) * (hd ** -0.5)).astype(BF16)

    jc = lax.broadcasted_iota(jnp.int32, (L, 1), 0).astype(F32)
    q_dec_f = jnp.exp((jc + 1.0) * lgf)
    q_dec_b = jnp.exp((L - jc) * lgb)
    k_dec_f = jnp.exp((L - 1.0 - jc) * lgf)
    k_dec_b = jnp.exp(jc * lgb)
    ones = jnp.ones((1, hd), F32)
    chunk_f = jnp.exp(ones * (L * lgf))
    chunk_b = jnp.exp(ones * (L * lgb))
    ii = lax.broadcasted_iota(jnp.int32, (L, L), 0)
    jj = lax.broadcasted_iota(jnp.int32, (L, L), 1)
    diff = (ii - jj).astype(F32)
    dmat = jnp.where(diff >= 0.0, jnp.exp(jnp.maximum(diff, 0.0) * lgf),
                     jnp.exp(jnp.maximum(-diff, 0.0) * lgb))

    tn_dims = (((0,), (0,)), ((), ()))
    nt_dims = (((1,), (1,)), ((), ()))

    def state_update(c, state, k_dec, chunk_dec):
        t0 = pl.multiple_of(c * L, L)
        kd = (kr_scr[pl.ds(t0, L), :].astype(F32) * k_dec).astype(BF16)
        upd = lax.dot_general(kd, v_ref[pl.ds(t0, L), :], tn_dims, preferred_element_type=F32)
        return chunk_dec * state + upd

    def state_body(i, carry):
        f_state, r_state = carry
        cf = i
        cb = nc - 1 - i
        st_scr[cf, :, 0:hd] = f_state.astype(BF16)
        st_scr[cb, :, hd:2 * hd] = r_state.astype(BF16)
        return (state_update(cf, f_state, k_dec_f, chunk_f),
                state_update(cb, r_state, k_dec_b, chunk_b))

    f_fin, r_fin = lax.fori_loop(0, nc, state_body, (s0_ref[0], s0_ref[1]),
                                 unroll=min(RET_UNROLL, nc))
    so_ref[0] = f_fin
    so_ref[1] = r_fin

    def out_body(c, carry):
        t0 = pl.multiple_of(c * L, L)
        qc = rope(q_ref[pl.ds(t0, L), :].astype(F32), pl.ds(t0, L)).astype(BF16)
        kc = kr_scr[pl.ds(t0, L), :]
        vc = v_ref[pl.ds(t0, L), :]
        s = lax.dot_general(qc, kc, nt_dims, preferred_element_type=F32)
        o = jnp.dot((s * dmat).astype(BF16), vc, preferred_element_type=F32)
        cross = jnp.dot(qc, st_scr[c], preferred_element_type=F32)
        o = o + cross[:, 0:hd] * q_dec_f + cross[:, hd:2 * hd] * q_dec_b
        mu = jnp.mean(o, axis=-1, keepdims=True)
        oc = o - mu
        var = jnp.mean(oc * oc, axis=-1, keepdims=True)
        gate = g_ref[pl.ds(t0, L), :].astype(F32)
        o_ref[pl.ds(t0, L), :] = (gate * jax.nn.sigmoid(gate) * (oc * lax.rsqrt(var + EPS))).astype(o_ref.dtype)
        return carry

    lax.fori_loop(0, nc, out_body, 0, unroll=min(RET_UNROLL, nc))


def _retention(p, log_g, cs, sn, s0, col0):
    bsz, n, _ = p.shape
    nh = log_g.shape[1]
    hd = cs.shape[1]
    cb0 = col0 // hd

    def col(which):
        return pl.BlockSpec((None, n, hd), lambda b, h, lg: (b, 0, cb0 + which * nh + h))

    state = pl.BlockSpec((None, 2, None, hd, hd), lambda b, h, lg: (b, 0, h, 0, 0))
    table = pl.BlockSpec((n, hd), lambda b, h, lg: (0, 0))
    grid_spec = pltpu.PrefetchScalarGridSpec(
        num_scalar_prefetch=1,
        grid=(bsz, nh),
        in_specs=[col(0), col(1), col(2), col(3), table, table, state],
        out_specs=[pl.BlockSpec((None, n, hd), lambda b, h, lg: (b, 0, h)), state],
        scratch_shapes=[
            pltpu.VMEM((n, hd), BF16),
            pltpu.VMEM((n // RET_CHUNK, hd, 2 * hd), BF16),
        ],
    )
    return pl.pallas_call(
        _retention_kernel,
        grid_spec=grid_spec,
        out_shape=[
            jax.ShapeDtypeStruct((bsz, n, nh * hd), BF16),
            jax.ShapeDtypeStruct((bsz, 2, nh, hd, hd), F32),
        ],
        compiler_params=_cparams(("parallel", "parallel")),
        name="retention",
    )(log_g, p, p, p, p, cs, sn, s0)


R_RANK0, R_RANK1, R_E0, R_E1, R_W0, R_W1 = range(6)


def _out_proj_kernel(route, x_ref, rg_ref, ret_ref, wo_ref, gate_ref, g_ref, shift_ref, scale_ref,
                     *rest):
    if route:
        wr2_ref, br_ref, xo_ref, h_ref, rt_ref, cnt_ref, carry_scr = rest
    else:
        xo_ref, h_ref = rest
    rg_w = rg_ref.shape[1]
    y = jnp.dot(rg_ref[...], wo_ref[0:rg_w, :], preferred_element_type=F32)
    y = y + jnp.dot(ret_ref[...], wo_ref[rg_w:, :], preferred_element_type=F32)
    xn = x_ref[...] + gate_ref[...] * y
    xo_ref[...] = xn
    hx = _norm_mod(xn, g_ref[...], shift_ref[...], scale_ref[...])
    if not route:
        h_ref[...] = hx.astype(h_ref.dtype)
        return

    tm, d = hx.shape
    for s in range(d // LANES):
        h_ref[pl.ds(s, tm, stride=d // LANES), :] = hx[:, s * LANES:(s + 1) * LANES]

    h_hi = hx.astype(BF16)
    h_lo = (hx - h_hi.astype(F32)).astype(BF16)
    part = jnp.dot(h_hi, wr2_ref[...], preferred_element_type=F32)
    logits = (part[:, :LANES] + part[:, LANES:]
              + jnp.dot(h_lo, wr2_ref[:, :LANES], preferred_element_type=F32) + br_ref[...])
    lane_i = lax.broadcasted_iota(jnp.int32, logits.shape, 1)
    lane = lane_i.astype(F32)
    m0 = jnp.max(logits, axis=-1, keepdims=True)
    i0 = jnp.min(jnp.where(logits == m0, lane, float(LANES)), axis=-1, keepdims=True)
    rest_l = jnp.where(lane == i0, NEG_BIG, logits)
    m1 = jnp.max(rest_l, axis=-1, keepdims=True)
    i1 = jnp.min(jnp.where(rest_l == m1, lane, float(LANES)), axis=-1, keepdims=True)
    e = jnp.exp(m1 - m0)
    w0 = 1.0 / (1.0 + e)
    w1 = e / (1.0 + e)

    @pl.when((pl.program_id(0) == 0) & (pl.program_id(1) == 0))
    def _():
        carry_scr[...] = jnp.zeros_like(carry_scr)

    sel0 = lane == i0
    sel1 = lane == i1
    mask = jnp.where(sel0 | sel1, 1.0, 0.0)
    ri = lax.broadcasted_iota(jnp.int32, (tm, tm), 0)
    ci = lax.broadcasted_iota(jnp.int32, (tm, tm), 1)
    tri = jnp.where(ri > ci, 1.0, 0.0).astype(BF16)
    before = jnp.dot(tri, mask.astype(BF16), preferred_element_type=F32) + carry_scr[...]
    rank0 = jnp.sum(jnp.where(sel0, before, 0.0), axis=-1, keepdims=True)
    rank1 = jnp.sum(jnp.where(sel1, before, 0.0), axis=-1, keepdims=True)
    total = carry_scr[...] + jnp.sum(mask, axis=0, keepdims=True)
    carry_scr[...] = total
    cnt_ref[...] = total
    rec = jnp.zeros_like(logits)
    for idx, val in ((R_RANK0, rank0), (R_RANK1, rank1), (R_E0, i0), (R_E1, i1),
                     (R_W0, w0), (R_W1, w1)):
        rec = jnp.where(lane_i == idx, val, rec)
    rt_ref[...] = rec


def _out_proj(x, rg, ret, w_out, gate, g, shift, scale, router=None):
    bsz, n, d = x.shape
    tm = min(512, n)
    route = router is not None
    vec = pl.BlockSpec((None, 1, d), lambda b, i: (b, 0, 0))
    tile = lambda w: pl.BlockSpec((None, tm, w), lambda b, i: (b, i, 0))
    in_specs = [
        tile(d), tile(rg.shape[2]), tile(ret.shape[2]),
        pl.BlockSpec(w_out.shape, lambda b, i: (0, 0)),
        vec, pl.BlockSpec((1, d), lambda b, i: (0, 0)), vec, vec,
    ]
    args = [x, rg, ret, w_out, gate, g, shift, scale]
    scratch = []
    if route:
        in_specs += [pl.BlockSpec((d, 2 * LANES), lambda b, i: (0, 0)),
                     pl.BlockSpec((1, LANES), lambda b, i: (0, 0))]
        args += list(router)
        out_specs = [tile(d),
                     pl.BlockSpec((None, tm * (d // LANES), LANES), lambda b, i: (b, i, 0)),
                     tile(LANES),
                     pl.BlockSpec((1, LANES), lambda b, i: (0, 0))]
        out_shape = [jax.ShapeDtypeStruct((bsz, n, d), F32),
                     jax.ShapeDtypeStruct((bsz, n * (d // LANES), LANES), F32),
                     jax.ShapeDtypeStruct((bsz, n, LANES), F32),
                     jax.ShapeDtypeStruct((1, LANES), F32)]
        scratch = [pltpu.VMEM((1, LANES), F32)]
        sem = ("arbitrary", "arbitrary")
    else:
        out_specs = [tile(d), tile(d)]
        out_shape = [jax.ShapeDtypeStruct((bsz, n, d), F32), jax.ShapeDtypeStruct((bsz, n, d), BF16)]
        sem = ("parallel", "parallel")
    return pl.pallas_call(
        functools.partial(_out_proj_kernel, route),
        grid=(bsz, n // tm),
        in_specs=in_specs,
        out_specs=out_specs,
        out_shape=out_shape,
        scratch_shapes=scratch,
        compiler_params=_cparams(sem),
        name="out_proj_route" if route else "out_proj",
    )(*args)


def _swiglu_acc(h, w1_ref, w3_ref, w2_ref, row_scale=None):
    a = jnp.dot(h, w1_ref[...], preferred_element_type=F32)
    b = jnp.dot(h, w3_ref[...], preferred_element_type=F32)
    act = a * jax.nn.sigmoid(a) * b
    if row_scale is not None:
        act = act * row_scale
    return jnp.dot(act.astype(BF16), w2_ref[...], preferred_element_type=F32)


def _ffn_kernel(h_ref, x_ref, gate_ref, w1_ref, w3_ref, w2_ref, o_ref):
    o_ref[...] = x_ref[...] + gate_ref[...] * _swiglu_acc(h_ref[...], w1_ref, w3_ref, w2_ref)


def _ffn(h, x, gate, w1, w3, w2):
    bsz, n, d = x.shape
    f = w1.shape[1]
    tm = min(512, n)
    tile = pl.BlockSpec((None, tm, d), lambda b, i: (b, i, 0))
    resident = lambda shape: pl.BlockSpec(shape, lambda b, i: (0, 0), pipeline_mode=pl.Buffered(1))
    return pl.pallas_call(
        _ffn_kernel,
        grid=(bsz, n // tm),
        in_specs=[
            tile, tile,
            pl.BlockSpec((None, 1, d), lambda b, i: (b, 0, 0)),
            resident((d, f)), resident((d, f)), resident((f, d)),
        ],
        out_specs=tile,
        out_shape=jax.ShapeDtypeStruct((bsz, n, d), F32),
        compiler_params=_cparams(("parallel", "parallel")),
        name="ffn",
    )(h, x, gate, w1, w3, w2)


MOE_BLK = 512
MOE_TF = 1792


def _dispatch_kernel(d0_ref, d1_ref, zr_ref, hx_ref, hs_ref, zbuf, sem, zsem):
    tm = hx_ref.shape[0]
    base = pl.program_id(0) * tm

    @pl.when(pl.program_id(0) == 0)
    def _():
        zbuf[...] = jnp.zeros_like(zbuf)
        for z in range(zr_ref.shape[0]):
            @pl.when(zr_ref[z] >= 0)
            def _():
                pltpu.make_async_copy(zbuf, hs_ref.at[pl.ds(zr_ref[z], MOE_BLK)], zsem).start()
        for z in range(zr_ref.shape[0]):
            @pl.when(zr_ref[z] >= 0)
            def _():
                pltpu.make_async_copy(zbuf, hs_ref.at[pl.ds(zr_ref[z], MOE_BLK)], zsem).wait()

    def issue(t, carry):
        pltpu.make_async_copy(hx_ref.at[t], hs_ref.at[d0_ref[base + t]], sem).start(priority=0)
        pltpu.make_async_copy(hx_ref.at[t], hs_ref.at[d1_ref[base + t]], sem).start(priority=1)
        return carry

    lax.fori_loop(0, tm, issue, 0, unroll=8)
    for _ in range(2):
        pltpu.make_async_copy(hx_ref, hs_ref.at[pl.ds(0, tm)], sem).wait()


def _dispatch(dest0, dest1, zero_rows, hx, n_rows):
    n_tok, s, lanes = hx.shape
    tm = 512
    grid_spec = pltpu.PrefetchScalarGridSpec(
        num_scalar_prefetch=3,
        grid=(n_tok // tm,),
        in_specs=[pl.BlockSpec((tm, s, lanes), lambda i, d0, d1, zr: (i, 0, 0))],
        out_specs=pl.BlockSpec(memory_space=pl.ANY),
        scratch_shapes=[pltpu.VMEM((MOE_BLK, s, lanes), F32),
                        pltpu.SemaphoreType.DMA(()), pltpu.SemaphoreType.DMA(())],
    )
    return pl.pallas_call(
        _dispatch_kernel,
        grid_spec=grid_spec,
        out_shape=jax.ShapeDtypeStruct((n_rows, s, lanes), F32),
        compiler_params=_cparams(("arbitrary",)),
        name="moe_dispatch",
    )(dest0, dest1, zero_rows, hx)


def _experts_kernel(nf, be_ref, nu_ref, hs_ref, w1_ref, w3_ref, w2_ref, y_ref, hb_scr, acc_scr):
    assert nf >= 2
    j = pl.program_id(0)
    f = pl.program_id(1)
    blk, d = acc_scr.shape
    ns = d // LANES

    @pl.when(j < nu_ref[0])
    def _():
        @pl.when(f == 0)
        def _():
            for s in range(ns):
                hb_scr[:, s * LANES:(s + 1) * LANES] = (
                    hs_ref[pl.ds(s, blk, stride=ns), :].astype(BF16))

        part = _swiglu_acc(hb_scr[...], w1_ref, w3_ref, w2_ref)

        @pl.when(f == 0)
        def _():
            acc_scr[...] = part

        @pl.when((f > 0) & (f < nf - 1))
        def _():
            acc_scr[...] += part

        @pl.when(f == nf - 1)
        def _():
            total = acc_scr[...] + part
            for s in range(ns):
                y_ref[pl.ds(s, blk, stride=ns), :] = total[:, s * LANES:(s + 1) * LANES]

    @pl.when((j >= nu_ref[0]) & (f == 0))
    def _():
        y_ref[...] = jnp.zeros_like(y_ref)


def _experts(block_expert, n_used, hs, w1, w3, w2):
    ne, d, f = w1.shape
    s = d // LANES
    n_rows = hs.shape[0] // s
    nf = f // MOE_TF
    n_blocks = n_rows // MOE_BLK

    def row_map(j, f_, be, nu):
        return (jnp.minimum(j, nu[0] - 1), 0)

    def f_idx(j, f_, nu):
        return jnp.where(j < nu[0], f_, nf - 1)

    grid_spec = pltpu.PrefetchScalarGridSpec(
        num_scalar_prefetch=2,
        grid=(n_blocks, nf),
        in_specs=[
            pl.BlockSpec((MOE_BLK * s, LANES), row_map),
            pl.BlockSpec((None, d, MOE_TF), lambda j, f_, be, nu: (be[j], 0, f_idx(j, f_, nu))),
            pl.BlockSpec((None, d, MOE_TF), lambda j, f_, be, nu: (be[j], 0, f_idx(j, f_, nu))),
            pl.BlockSpec((None, MOE_TF, d), lambda j, f_, be, nu: (be[j], f_idx(j, f_, nu), 0)),
        ],
        out_specs=pl.BlockSpec((MOE_BLK * s, LANES), lambda j, f_, be, nu: (j, 0)),
        scratch_shapes=[pltpu.VMEM((MOE_BLK, d), BF16), pltpu.VMEM((MOE_BLK, d), F32)],
    )
    return pl.pallas_call(
        functools.partial(_experts_kernel, nf),
        grid_spec=grid_spec,
        out_shape=jax.ShapeDtypeStruct((n_rows * s, LANES), F32),
        compiler_params=_cparams(("arbitrary", "arbitrary")),
        name="moe_experts",
    )(block_expert, n_used, hs, w1, w3, w2)


def _combine_kernel(d0_ref, d1_ref, x_ref, gate_ref, rt_ref, gfin_ref, y_ref, o_ref, ybuf, sem):
    tm, d = x_ref.shape
    ns = d // LANES
    step = pl.program_id(0) * pl.num_programs(1) + pl.program_id(1)
    nsteps = pl.num_programs(0) * pl.num_programs(1)
    slot = step % 2

    def row(ref, r):
        return ref.at[pl.ds(pl.multiple_of(r * ns, ns), ns)]

    def start_gather(block, into):
        base = block * tm

        def issue(t, carry):
            pltpu.make_async_copy(row(y_ref, d0_ref[base + t]), row(ybuf.at[into, 0], t),
                                  sem.at[into]).start(priority=0)
            pltpu.make_async_copy(row(y_ref, d1_ref[base + t]), row(ybuf.at[into, 1], t),
                                  sem.at[into]).start(priority=1)
            return carry

        lax.fori_loop(0, tm, issue, 0, unroll=8)

    @pl.when(step == 0)
    def _():
        start_gather(0, 0)

    @pl.when(step + 1 < nsteps)
    def _():
        start_gather(step + 1, 1 - slot)

    for k in range(2):
        pltpu.make_async_copy(y_ref.at[pl.ds(0, tm * ns)], ybuf.at[slot, k], sem.at[slot]).wait()

    rt = rt_ref[...]
    w0 = rt[:, R_W0:R_W0 + 1]
    w1 = rt[:, R_W1:R_W1 + 1]
    ss = jnp.zeros((tm, 1), F32)
    for s in range(ns):
        cols = slice(s * LANES, (s + 1) * LANES)
        moe = (w0 * ybuf[slot, 0, pl.ds(s, tm, stride=ns), :]
               + w1 * ybuf[slot, 1, pl.ds(s, tm, stride=ns), :])
        xn = x_ref[:, cols] + gate_ref[:, cols] * moe
        ss = ss + jnp.sum(xn * xn, axis=-1, keepdims=True)
        o_ref[:, cols] = xn
    o_ref[...] = o_ref[...] * lax.rsqrt(ss / d + EPS) * gfin_ref[...]


def _combine(dest0, dest1, x, gate, route, g_final, y):
    bsz, n, d = x.shape
    s = d // LANES
    tm = 512
    grid_spec = pltpu.PrefetchScalarGridSpec(
        num_scalar_prefetch=2,
        grid=(bsz, n // tm),
        in_specs=[
            pl.BlockSpec((None, tm, d), lambda b, i, d0, d1: (b, i, 0)),
            pl.BlockSpec((None, 1, d), lambda b, i, d0, d1: (b, 0, 0)),
            pl.BlockSpec((None, tm, LANES), lambda b, i, d0, d1: (b, i, 0)),
            pl.BlockSpec((1, d), lambda b, i, d0, d1: (0, 0)),
            pl.BlockSpec(memory_space=pl.ANY),
        ],
        out_specs=pl.BlockSpec((None, tm, d), lambda b, i, d0, d1: (b, i, 0)),
        scratch_shapes=[pltpu.VMEM((2, 2, tm * s, LANES), F32), pltpu.SemaphoreType.DMA((2,))],
    )
    return pl.pallas_call(
        _combine_kernel,
        grid_spec=grid_spec,
        out_shape=jax.ShapeDtypeStruct((bsz, n, d), F32),
        compiler_params=_cparams(("arbitrary", "arbitrary")),
        name="moe_combine",
    )(dest0, dest1, x, gate, route, g_final, y)


def _moe(hx, x, gate, route, counts, g_final, w1, w3, w2):
    bsz, n, d = x.shape
    ne = w1.shape[0]
    n_tok = bsz * n
    n_rows = -(-(2 * n_tok) // MOE_BLK) * MOE_BLK + ne * MOE_BLK
    n_blocks = n_rows // MOE_BLK
    cnt = counts[0, :ne].astype(jnp.int32)
    padded = (cnt + MOE_BLK - 1) // MOE_BLK * MOE_BLK
    pad_end = jnp.cumsum(padded)
    pad_start = pad_end - padded
    n_used = pad_end[-1] // MOE_BLK
    blk = jnp.minimum(jnp.arange(n_blocks, dtype=jnp.int32), n_used - 1)
    block_expert = jnp.minimum(
        jnp.sum((blk[:, None] * MOE_BLK >= pad_end[None, :]).astype(jnp.int32), axis=1), ne - 1)
    rt = route.reshape(n_tok, LANES)
    e0 = rt[:, R_E0].astype(jnp.int32)
    e1 = rt[:, R_E1].astype(jnp.int32)
    dest0 = pad_start[e0] + rt[:, R_RANK0].astype(jnp.int32)
    dest1 = pad_start[e1] + rt[:, R_RANK1].astype(jnp.int32)

    last_blk = jnp.where(padded > 0, pad_end - MOE_BLK, -1)
    trail = n_used + jnp.arange(ne, dtype=jnp.int32)
    trail = jnp.where(trail < n_blocks, trail * MOE_BLK, -1)
    zero_rows = jnp.concatenate([last_blk, trail]).astype(jnp.int32)

    ns = d // LANES
    hs = _dispatch(dest0, dest1, zero_rows, hx.reshape(n_tok, ns, LANES), n_rows)
    y = _experts(block_expert, n_used.reshape(1).astype(jnp.int32),
                 hs.reshape(n_rows * ns, LANES), w1, w3, w2)
    return _combine(dest0, dest1, x, gate, route, g_final, y)


def _rope_tables(n, hd):
    rows = n // GRID_W
    n_freq = hd // 4
    inv = ROPE_BASE ** (-jnp.arange(n_freq, dtype=F32) / n_freq)
    row_ang = jnp.arange(rows, dtype=F32)[:, None] * inv
    col_ang = jnp.arange(GRID_W, dtype=F32)[:, None] * inv

    def table(of_row, of_col):
        a = jnp.broadcast_to(of_row[:, None, :], (rows, GRID_W, n_freq))
        b = jnp.broadcast_to(of_col[None, :, :], (rows, GRID_W, n_freq))
        return jnp.concatenate([a, b], axis=-1).reshape(n, 2 * n_freq)

    cos = table(jnp.cos(row_ang), jnp.cos(col_ang))
    sin = table(jnp.sin(row_ang), jnp.sin(col_ang))
    return jnp.concatenate([cos, cos], axis=-1), jnp.concatenate([-sin, sin], axis=-1)


def _block_diag_gates(rg_wa, rg_wx):
    w = jnp.stack([rg_wa, rg_wx], axis=1)
    nd, ng, nk, c, _ = w.shape
    per = RG_CG // c
    w = w.reshape(nd, ng, nk // per, per, c, c)
    eye = jnp.eye(per, dtype=w.dtype)
    full = jnp.einsum('dgmpij,pq->dgmpiqj', w, eye)
    return full.reshape(nd, ng, nk // per, RG_CG, RG_CG).astype(BF16)


def _mixer(p, rope, states, conv_w, conv_b, wg, bg, lam, log_g, rg_w):
    rg_out, h_last = _rglru(p, conv_w, conv_b, wg, bg, lam, states[0])
    ret_out, s_last = _retention(p, log_g, rope[0], rope[1], states[1], 2 * rg_w)
    return rg_out, ret_out, (h_last, s_last)


def kernel(x, c, ctx, c_ctx, w_mod, b_mod, g_mix, g_ffn, g_final, w_in, w_out, conv_w, conv_b,
           rg_wa, rg_ba, rg_wx, rg_bx, rg_lam, ret_decay, ffn_w1, ffn_w3, ffn_w2,
           moe_router, moe_router_b, moe_w1, moe_w3, moe_w2):
    bsz, n_lat, d = x.shape
    n_ctx = ctx.shape[1]
    depth = w_mod.shape[0]
    rg_w = rg_lam.shape[2]
    nh = ret_decay.shape[2]
    hd = (w_out.shape[1] - rg_w) // nh
    ne = moe_router.shape[2]
    assert depth == 2, "kernel is written for the two-layer block (dense FFN, then MoE)"

    rows = -(-(bsz + 1) // SUBLANES) * SUBLANES
    c_all = jnp.zeros((rows, d), F32).at[:bsz].set(c).at[bsz].set(c_ctx)
    mod = _modulation(c_all, w_mod, b_mod)
    mod = mod.reshape(depth, rows, N_MOD, d)
    mod_lat = mod[:, :bsz, None]
    mod_ctx = jnp.broadcast_to(mod[:, bsz:bsz + 1, None], mod_lat.shape)

    rope_lat = _rope_tables(n_lat, hd)
    rope_ctx = (jnp.ones((n_ctx, hd), F32), jnp.zeros((n_ctx, hd), F32))
    log_g = jax.nn.log_sigmoid(ret_decay.astype(F32))
    zero_states = (jnp.zeros((bsz, 2, rg_w), F32), jnp.zeros((bsz, 2, nh, hd, hd), F32))

    for l in range(depth):
        last = l == depth - 1
        ml = [mod_lat[l, :, :, j] for j in range(N_MOD)]
        mc = [mod_ctx[l, :, :, j] for j in range(N_MOD)]
        g_m = g_mix[l].reshape(1, d)
        g_f = g_ffn[l].reshape(1, d)
        w_in_l = w_in[l].astype(BF16)
        w_out_l = w_out[l].astype(BF16)
        wg = _block_diag_gates(0.5 * rg_wa[l], 0.5 * rg_wx[l])
        bg = 0.5 * jnp.stack([rg_ba[l], rg_bx[l]], axis=1)
        mix_p = (conv_w[l], conv_b[l], wg, bg, rg_lam[l], log_g[l], rg_w)

        pc = _in_proj(ctx, g_m, mc[0], mc[1], w_in_l)
        rg_c, ret_c, ctx_states = _mixer(pc, rope_ctx, zero_states, *mix_p)
        px = _in_proj(x, g_m, ml[0], ml[1], w_in_l)
        rg_x, ret_x, _ = _mixer(px, rope_lat, ctx_states, *mix_p)

        if l % 2 == 0:
            i = l // 2
            w1, w3, w2 = ffn_w1[i].astype(BF16), ffn_w3[i].astype(BF16), ffn_w2[i].astype(BF16)
            x, hx = _out_proj(x, rg_x, ret_x, w_out_l, ml[2], g_f, ml[3], ml[4])
            x = _ffn(hx, x, ml[5], w1, w3, w2)
            if not last:
                ctx, hc = _out_proj(ctx, rg_c, ret_c, w_out_l, mc[2], g_f, mc[3], mc[4])
                ctx = _ffn(hc, ctx, mc[5], w1, w3, w2)
        else:
            i = l // 2
            wr = jnp.zeros((d, LANES), F32).at[:, :ne].set(moe_router[i])
            wr_hi = wr.astype(BF16)
            wr_lo = (wr - wr_hi.astype(F32)).astype(BF16)
            wr2 = jnp.concatenate([wr_hi, wr_lo], axis=1)
            br = jnp.full((1, LANES), NEG_BIG, F32).at[0, :ne].set(moe_router_b[i])
            x, hx, route, counts = _out_proj(x, rg_x, ret_x, w_out_l, ml[2], g_f, ml[3], ml[4],
                                             router=(wr2, br))
            x = _moe(hx, x, ml[5], route, counts, g_final.reshape(1, d),
                     moe_w1[i].astype(BF16), moe_w3[i].astype(BF16), moe_w2[i].astype(BF16))
    return x
```

```python
import functools

import jax
import jax.numpy as jnp
from jax import lax
from jax.experimental import pallas as pl
from jax.experimental.pallas import tpu as pltpu

F32 = jnp.float32
BF16 = jnp.bfloat16

EPS = 1e-6
RG_C = 8.0
CONV_W = 4
CONV_LEFT = 2
RG_BLOCKS = 8
RET_HEADS = 4
RET_CHUNK = 128
ROPE_BASE = 10000.0
GRID_W = 64
N_EXPERTS = 8
N_MOD = 6

LANES = 128
SUBLANES = 8
VMEM_LIMIT = 56 * 1024 * 1024
NEG_BIG = -1e30
LOG2_E = 1.4426950408889634


def _cparams(sem):
    return pltpu.CompilerParams(dimension_semantics=sem, vmem_limit_bytes=VMEM_LIMIT)


def _modulation_kernel(c_ref, w_ref, b_ref, o_ref):
    c = c_ref[...]
    s = c * jax.nn.sigmoid(c)
    o_ref[...] = jnp.dot(s, w_ref[...], precision=lax.Precision.HIGHEST,
                         preferred_element_type=F32) + b_ref[...]


def _modulation(c_all, w_mod, b_mod):
    depth, d, six_d = w_mod.shape
    rows = c_all.shape[0]
    tn = 1536
    return pl.pallas_call(
        _modulation_kernel,
        grid=(depth, six_d // tn),
        in_specs=[
            pl.BlockSpec((rows, d), lambda l, j: (0, 0)),
            pl.BlockSpec((None, d, tn), lambda l, j: (l, 0, j)),
            pl.BlockSpec((None, 1, tn), lambda l, j: (l, 0, j)),
        ],
        out_specs=pl.BlockSpec((None, rows, tn), lambda l, j: (l, 0, j)),
        out_shape=jax.ShapeDtypeStruct((depth, rows, six_d), F32),
        compiler_params=_cparams(("arbitrary", "arbitrary")),
        name="modulation",
    )(c_all, w_mod, b_mod.reshape(depth, 1, six_d))


def _norm_mod(x, g, shift, scale):
    ms = jnp.mean(x * x, axis=-1, keepdims=True)
    y = x * lax.rsqrt(ms + EPS) * g
    return y * (1.0 + scale) + shift


def _in_proj_kernel(x_ref, g_ref, shift_ref, scale_ref, w_ref, o_ref):
    h = _norm_mod(x_ref[...], g_ref[...], shift_ref[...], scale_ref[...])
    o_ref[...] = jnp.dot(h.astype(BF16), w_ref[...],
                         preferred_element_type=F32).astype(o_ref.dtype)


def _in_proj(x, g, shift, scale, w):
    bsz, n, d = x.shape
    f = w.shape[1]
    tm = min(512, n)
    vec = pl.BlockSpec((None, 1, d), lambda b, i: (b, 0, 0))
    return pl.pallas_call(
        _in_proj_kernel,
        grid=(bsz, n // tm),
        in_specs=[
            pl.BlockSpec((None, tm, d), lambda b, i: (b, i, 0)),
            pl.BlockSpec((1, d), lambda b, i: (0, 0)),
            vec, vec,
            pl.BlockSpec((d, f), lambda b, i: (0, 0)),
        ],
        out_specs=pl.BlockSpec((None, tm, f), lambda b, i: (b, i, 0)),
        out_shape=jax.ShapeDtypeStruct((bsz, n, f), BF16),
        compiler_params=_cparams(("parallel", "parallel")),
        name="in_proj",
    )(x, g, shift, scale, w)


RG_TILE = 256
RG_CG = 256
RG_SEG = RG_TILE // SUBLANES
RG_PITCH = RG_SEG + SUBLANES


def _rglru_kernel(u_ref, yg_ref, cw_ref, cb_ref, wg_ref, bg_ref, lam_ref, h0_ref,
                  o_ref, hl_ref, u_scr, hf_scr, nat_scr):
    n = u_ref.shape[0]
    tt = RG_TILE
    nt = n // tt
    nseg = n // RG_SEG
    cg = u_ref.shape[1]
    lane_cols = [slice(g * LANES, (g + 1) * LANES) for g in range(cg // LANES)]

    zeros = jnp.zeros((RG_PITCH, LANES), F32)
    for g, cols in enumerate(lane_cols):
        u_scr[g, pl.ds(0, RG_PITCH), :] = zeros
        u_scr[g, pl.ds(nseg * RG_PITCH + RG_SEG, SUBLANES), :] = zeros[0:SUBLANES]

    def fill(s, carry):
        src = pl.multiple_of(s * RG_SEG, RG_SEG)
        dst = pl.multiple_of((s + 1) * RG_PITCH, SUBLANES)
        for g, cols in enumerate(lane_cols):
            rows = u_ref[pl.ds(src, RG_SEG), cols].astype(F32)
            u_scr[g, pl.ds(dst, RG_SEG), :] = rows
            u_scr[g, pl.ds(dst - SUBLANES, SUBLANES), :] = rows[0:SUBLANES]
        return carry

    lax.fori_loop(0, nseg, fill, 0)

    seg = RG_SEG
    cw_rows = [[jnp.broadcast_to(cw_ref[k:k + 1, c], (SUBLANES, LANES)) for c in lane_cols]
               for k in range(CONV_W)]
    cb_rows = [jnp.broadcast_to(cb_ref[:, c], (SUBLANES, LANES)) for c in lane_cols]

    def conv_tile(i):
        base = pl.multiple_of((i * SUBLANES + 1) * RG_PITCH, SUBLANES)
        blocks = []
        for j in range(seg):
            lane_groups = []
            for g in range(cg // LANES):
                acc = cb_rows[g]
                for k in range(CONV_W):
                    q = j + k - CONV_LEFT
                    start = base + q if q >= 0 else base - RG_PITCH + seg + q
                    acc = acc + cw_rows[k][g] * u_scr[g, pl.ds(start, SUBLANES, stride=RG_PITCH), :]
                lane_groups.append(acc)
            blocks.append(jnp.concatenate(lane_groups, axis=1))
        return jnp.concatenate(blocks, axis=0)

    def gates(uc, d):
        ub = uc.astype(BF16)
        ta = jnp.tanh(jnp.dot(ub, wg_ref[d, 0], preferred_element_type=F32) + bg_ref[d, 0:1, :])
        ti = jnp.tanh(jnp.dot(ub, wg_ref[d, 1], preferred_element_type=F32) + bg_ref[d, 1:2, :])
        z = -lam_ref[d:d + 1, :]
        sp = jnp.maximum(z, 0.0) + jnp.log1p(jnp.exp(-jnp.abs(z)))
        c2 = (-0.5 * RG_C * LOG2_E) * sp
        a = jnp.exp2(c2 + c2 * ta)
        b = jnp.exp2((0.5 * LOG2_E) * jnp.log(1.0 - a * a)) * ((0.5 + 0.5 * ti) * uc)
        return a, b

    def scan_tile(a, b, carry, reverse):
        steps = range(seg - 1, -1, -1) if reverse else range(seg)
        h = acum = None
        h_loc = [None] * seg
        a_cum = [None] * seg
        for j in steps:
            rows = slice(j * SUBLANES, (j + 1) * SUBLANES)
            h = b[rows] if h is None else a[rows] * h + b[rows]
            acum = a[rows] if acum is None else a[rows] * acum
            h_loc[j], a_cum[j] = h, acum
        order = range(SUBLANES - 1, -1, -1) if reverse else range(SUBLANES)
        c = carry
        c_in = [None] * SUBLANES
        for s in order:
            c_in[s] = c
            c = h[s:s + 1, :] + acum[s:s + 1, :] * c
        c_in = jnp.concatenate(c_in, axis=0)
        out = jnp.concatenate([h_loc[j] + a_cum[j] * c_in for j in range(seg)], axis=0)
        return out, c

    def fwd_body(i, carry):
        t0 = pl.multiple_of(i * tt, tt)
        uc = conv_tile(i)
        a, b = gates(uc, 0)
        h, carry = scan_tile(a, b, carry, False)
        hf_scr[pl.ds(t0, tt), :] = h
        return carry

    hf_last = lax.fori_loop(0, nt, fwd_body, h0_ref[0:1, :], unroll=min(2, nt))

    def bwd_body(i, carry):
        tile = nt - 1 - i
        t0 = pl.multiple_of(tile * tt, tt)
        uc = conv_tile(tile)
        a, b = gates(uc, 1)
        h, carry = scan_tile(a, b, carry, True)
        hsum = hf_scr[pl.ds(t0, tt), :] + h
        for j in range(seg):
            for g, cols in enumerate(lane_cols):
                nat_scr[g, pl.ds(j, SUBLANES, stride=RG_PITCH), :] = (
                    hsum[j * SUBLANES:(j + 1) * SUBLANES, cols])
        hnat = jnp.concatenate(
            [jnp.concatenate([nat_scr[g, pl.ds(s * RG_PITCH, seg), :] for s in range(SUBLANES)],
                             axis=0) for g in range(len(lane_cols))], axis=1)
        yg = yg_ref[pl.ds(t0, tt), :].astype(F32)
        o_ref[pl.ds(t0, tt), :] = (jax.nn.gelu(yg) * hnat).astype(o_ref.dtype)
        return carry

    hb_last = lax.fori_loop(0, nt, bwd_body, h0_ref[1:2, :], unroll=min(2, nt))
    hl_ref[0:1, :] = hf_last
    hl_ref[1:2, :] = hb_last


def _rglru(p, conv_w, conv_b, wg, bg, lam, h0):
    bsz, n, _ = p.shape
    rg_w = lam.shape[1]
    ncg = rg_w // RG_CG
    assert n % RG_TILE == 0 and rg_w % RG_CG == 0
    return pl.pallas_call(
        _rglru_kernel,
        grid=(bsz, ncg),
        in_specs=[
            pl.BlockSpec((None, n, RG_CG), lambda b, c: (b, 0, c)),
            pl.BlockSpec((None, n, RG_CG), lambda b, c: (b, 0, ncg + c)),
            pl.BlockSpec((CONV_W, RG_CG), lambda b, c: (0, c)),
            pl.BlockSpec((1, RG_CG), lambda b, c: (0, c)),
            pl.BlockSpec((2, 2, None, RG_CG, RG_CG), lambda b, c: (0, 0, c, 0, 0)),
            pl.BlockSpec((2, 2, RG_CG), lambda b, c: (0, 0, c)),
            pl.BlockSpec((2, RG_CG), lambda b, c: (0, c)),
            pl.BlockSpec((None, 2, RG_CG), lambda b, c: (b, 0, c)),
        ],
        out_specs=[
            pl.BlockSpec((None, n, RG_CG), lambda b, c: (b, 0, c)),
            pl.BlockSpec((None, 2, RG_CG), lambda b, c: (b, 0, c)),
        ],
        out_shape=[
            jax.ShapeDtypeStruct((bsz, n, rg_w), BF16),
            jax.ShapeDtypeStruct((bsz, 2, rg_w), F32),
        ],
        scratch_shapes=[
            pltpu.VMEM((RG_CG // LANES, (n // RG_SEG + 1) * RG_PITCH, LANES), F32),
            pltpu.VMEM((n, RG_CG), F32),
            pltpu.VMEM((RG_CG // LANES, SUBLANES * RG_PITCH, LANES), F32),
        ],
        compiler_params=_cparams(("parallel", "parallel")),
        name="rglru",
    )(p, p, conv_w, conv_b.reshape(1, rg_w), wg, bg, lam, h0)


RET_UNROLL = 8


def _retention_kernel(lg_ref, q_ref, k_ref, v_ref, g_ref, cs_ref, sn_ref, s0_ref,
                      o_ref, so_ref, kr_scr, st_scr):
    n, hd = q_ref.shape
    L = RET_CHUNK
    nc = n // L
    head = pl.program_id(1)
    lgf = lg_ref[0, head]
    lgb = lg_ref[1, head]

    def rope(t, rows):
        return t * cs_ref[rows, :] + pltpu.roll(t, hd // 2, axis=1) * sn_ref[rows, :]

    kr_scr[...] = (rope(k_ref[...].astype(F32), slice(None)) * (hd ** -0.5)).astype(BF16)

    jc = lax.broadcasted_iota(jnp.int32, (L, 1), 0).astype(F32)
    q_dec_f = jnp.exp((jc + 1.0) * lgf)
    q_dec_b = jnp.exp((L - jc) * lgb)
    k_dec_f = jnp.exp((L - 1.0 - jc) * lgf)
    k_dec_b = jnp.exp(jc * lgb)
    ones = jnp.ones((1, hd), F32)
    chunk_f = jnp.exp(ones * (L * lgf))
    chunk_b = jnp.exp(ones * (L * lgb))
    ii = lax.broadcasted_iota(jnp.int32, (L, L), 0)
    jj = lax.broadcasted_iota(jnp.int32, (L, L), 1)
    diff = (ii - jj).astype(F32)
    dmat = jnp.where(diff >= 0.0, jnp.exp(jnp.maximum(diff, 0.0) * lgf),
                     jnp.exp(jnp.maximum(-diff, 0.0) * lgb))

    tn_dims = (((0,), (0,)), ((), ()))
    nt_dims = (((1,), (1,)), ((), ()))

    def state_update(c, state, k_dec, chunk_dec):
        t0 = pl.multiple_of(c * L, L)
        kd = (kr_scr[pl.ds(t0, L), :].astype(F32) * k_dec).astype(BF16)
        upd = lax.dot_general(kd, v_ref[pl.ds(t0, L), :], tn_dims, preferred_element_type=F32)
        return chunk_dec * state + upd

    def state_body(i, carry):
        f_state, r_state = carry
        cf = i
        cb = nc - 1 - i
        st_scr[cf, :, 0:hd] = f_state.astype(BF16)
        st_scr[cb, :, hd:2 * hd] = r_state.astype(BF16)
        return (state_update(cf, f_state, k_dec_f, chunk_f),
                state_update(cb, r_state, k_dec_b, chunk_b))

    f_fin, r_fin = lax.fori_loop(0, nc, state_body, (s0_ref[0], s0_ref[1]),
                                 unroll=min(RET_UNROLL, nc))
    so_ref[0] = f_fin
    so_ref[1] = r_fin

    def out_body(c, carry):
        t0 = pl.multiple_of(c * L, L)
        qc = rope(q_ref[pl.ds(t0, L), :].astype(F32), pl.ds(t0, L)).astype(BF16)
        kc = kr_scr[pl.ds(t0, L), :]
        vc = v_ref[pl.ds(t0, L), :]
        s = lax.dot_general(qc, kc, nt_dims, preferred_element_type=F32)
        o = jnp.dot((s * dmat).astype(BF16), vc, preferred_element_type=F32)
        cross = jnp.dot(qc, st_scr[c], preferred_element_type=F32)
        o = o + cross[:, 0:hd] * q_dec_f + cross[:, hd:2 * hd] * q_dec_b
        mu = jnp.mean(o, axis=-1, keepdims=True)
        oc = o - mu
        var = jnp.mean(oc * oc, axis=-1, keepdims=True)
        gate = g_ref[pl.ds(t0, L), :].astype(F32)
        o_ref[pl.ds(t0, L), :] = (gate * jax.nn.sigmoid(gate) * (oc * lax.rsqrt(var + EPS))).astype(o_ref.dtype)
        return carry

    lax.fori_loop(0, nc, out_body, 0, unroll=min(RET_UNROLL, nc))


def _retention(p, log_g, cs, sn, s0, col0):
    bsz, n, _ = p.shape
    nh = log_g.shape[1]
    hd = cs.shape[1]
    cb0 = col0 // hd

    def col(which):
        return pl.BlockSpec((None, n, hd), lambda b, h, lg: (b, 0, cb0 + which * nh + h))

    state = pl.BlockSpec((None, 2, None, hd, hd), lambda b, h, lg: (b, 0, h, 0, 0))
    table = pl.BlockSpec((n, hd), lambda b, h, lg: (0, 0))
    grid_spec = pltpu.PrefetchScalarGridSpec(
        num_scalar_prefetch=1,
        grid=(bsz, nh),
        in_specs=[col(0), col(1), col(2), col(3), table, table, state],
        out_specs=[pl.BlockSpec((None, n, hd), lambda b, h, lg: (b, 0, h)), state],
        scratch_shapes=[
            pltpu.VMEM((n, hd), BF16),
            pltpu.VMEM((n // RET_CHUNK, hd, 2 * hd), BF16),
        ],
    )
    return pl.pallas_call(
        _retention_kernel,
        grid_spec=grid_spec,
        out_shape=[
            jax.ShapeDtypeStruct((bsz, n, nh * hd), BF16),
            jax.ShapeDtypeStruct((bsz, 2, nh, hd, hd), F32),
        ],
        compiler_params=_cparams(("parallel", "parallel")),
        name="retention",
    )(log_g, p, p, p, p, cs, sn, s0)


R_RANK0, R_RANK1, R_E0, R_E1, R_W0, R_W1 = range(6)


def _out_proj_kernel(route, x_ref, rg_ref, ret_ref, wo_ref, gate_ref, g_ref, shift_ref, scale_ref,
                     *rest):
    if route:
        wr2_ref, br_ref, xo_ref, h_ref, rt_ref, cnt_ref, carry_scr = rest
    else:
        xo_ref, h_ref = rest
    rg_w = rg_ref.shape[1]
    y = jnp.dot(rg_ref[...], wo_ref[0:rg_w, :], preferred_element_type=F32)
    y = y + jnp.dot(ret_ref[...], wo_ref[rg_w:, :], preferred_element_type=F32)
    xn = x_ref[...] + gate_ref[...] * y
    xo_ref[...] = xn
    hx = _norm_mod(xn, g_ref[...], shift_ref[...], scale_ref[...])
    if not route:
        h_ref[...] = hx.astype(h_ref.dtype)
        return

    tm, d = hx.shape
    for s in range(d // LANES):
        h_ref[pl.ds(s, tm, stride=d // LANES), :] = hx[:, s * LANES:(s + 1) * LANES]

    h_hi = hx.astype(BF16)
    h_lo = (hx - h_hi.astype(F32)).astype(BF16)
    part = jnp.dot(h_hi, wr2_ref[...], preferred_element_type=F32)
    logits = (part[:, :LANES] + part[:, LANES:]
              + jnp.dot(h_lo, wr2_ref[:, :LANES], preferred_element_type=F32) + br_ref[...])
    lane_i = lax.broadcasted_iota(jnp.int32, logits.shape, 1)
    lane = lane_i.astype(F32)
    m0 = jnp.max(logits, axis=-1, keepdims=True)
    i0 = jnp.min(jnp.where(logits == m0, lane, float(LANES)), axis=-1, keepdims=True)
    rest_l = jnp.where(lane == i0, NEG_BIG, logits)
    m1 = jnp.max(rest_l, axis=-1, keepdims=True)
    i1 = jnp.min(jnp.where(rest_l == m1, lane, float(LANES)), axis=-1, keepdims=True)
    e = jnp.exp(m1 - m0)
    w0 = 1.0 / (1.0 + e)
    w1 = e / (1.0 + e)

    @pl.when((pl.program_id(0) == 0) & (pl.program_id(1) == 0))
    def _():
        carry_scr[...] = jnp.zeros_like(carry_scr)

    sel0 = lane == i0
    sel1 = lane == i1
    mask = jnp.where(sel0 | sel1, 1.0, 0.0)
    ri = lax.broadcasted_iota(jnp.int32, (tm, tm), 0)
    ci = lax.broadcasted_iota(jnp.int32, (tm, tm), 1)
    tri = jnp.where(ri > ci, 1.0, 0.0).astype(BF16)
    before = jnp.dot(tri, mask.astype(BF16), preferred_element_type=F32) + carry_scr[...]
    rank0 = jnp.sum(jnp.where(sel0, before, 0.0), axis=-1, keepdims=True)
    rank1 = jnp.sum(jnp.where(sel1, before, 0.0), axis=-1, keepdims=True)
    total = carry_scr[...] + jnp.sum(mask, axis=0, keepdims=True)
    carry_scr[...] = total
    cnt_ref[...] = total
    rec = jnp.zeros_like(logits)
    for idx, val in ((R_RANK0, rank0), (R_RANK1, rank1), (R_E0, i0), (R_E1, i1),
                     (R_W0, w0), (R_W1, w1)):
        rec = jnp.where(lane_i == idx, val, rec)
    rt_ref[...] = rec


def _out_proj(x, rg, ret, w_out, gate, g, shift, scale, router=None):
    bsz, n, d = x.shape
    tm = min(512, n)
    route = router is not None
    vec = pl.BlockSpec((None, 1, d), lambda b, i: (b, 0, 0))
    tile = lambda w: pl.BlockSpec((None, tm, w), lambda b, i: (b, i, 0))
    in_specs = [
        tile(d), tile(rg.shape[2]), tile(ret.shape[2]),
        pl.BlockSpec(w_out.shape, lambda b, i: (0, 0)),
        vec, pl.BlockSpec((1, d), lambda b, i: (0, 0)), vec, vec,
    ]
    args = [x, rg, ret, w_out, gate, g, shift, scale]
    scratch = []
    if route:
        in_specs += [pl.BlockSpec((d, 2 * LANES), lambda b, i: (0, 0)),
                     pl.BlockSpec((1, LANES), lambda b, i: (0, 0))]
        args += list(router)
        out_specs = [tile(d),
                     pl.BlockSpec((None, tm * (d // LANES), LANES), lambda b, i: (b, i, 0)),
                     tile(LANES),
                     pl.BlockSpec((1, LANES), lambda b, i: (0, 0))]
        out_shape = [jax.ShapeDtypeStruct((bsz, n, d), F32),
                     jax.ShapeDtypeStruct((bsz, n * (d // LANES), LANES), F32),
                     jax.ShapeDtypeStruct((bsz, n, LANES), F32),
                     jax.ShapeDtypeStruct((1, LANES), F32)]
        scratch = [pltpu.VMEM((1, LANES), F32)]
        sem = ("arbitrary", "arbitrary")
    else:
        out_specs = [tile(d), tile(d)]
        out_shape = [jax.ShapeDtypeStruct((bsz, n, d), F32), jax.ShapeDtypeStruct((bsz, n, d), BF16)]
        sem = ("parallel", "parallel")
    return pl.pallas_call(
        functools.partial(_out_proj_kernel, route),
        grid=(bsz, n // tm),
        in_specs=in_specs,
        out_specs=out_specs,
        out_shape=out_shape,
        scratch_shapes=scratch,
        compiler_params=_cparams(sem),
        name="out_proj_route" if route else "out_proj",
    )(*args)


def _swiglu_acc(h, w1_ref, w3_ref, w2_ref, row_scale=None):
    a = jnp.dot(h, w1_ref[...], preferred_element_type=F32)
    b = jnp.dot(h, w3_ref[...], preferred_element_type=F32)
    act = a * jax.nn.sigmoid(a) * b
    if row_scale is not None:
        act = act * row_scale
    return jnp.dot(act.astype(BF16), w2_ref[...], preferred_element_type=F32)


def _ffn_kernel(h_ref, x_ref, gate_ref, w1_ref, w3_ref, w2_ref, o_ref):
    o_ref[...] = x_ref[...] + gate_ref[...] * _swiglu_acc(h_ref[...], w1_ref, w3_ref, w2_ref)


def _ffn(h, x, gate, w1, w3, w2):
    bsz, n, d = x.shape
    f = w1.shape[1]
    tm = min(512, n)
    tile = pl.BlockSpec((None, tm, d), lambda b, i: (b, i, 0))
    resident = lambda shape: pl.BlockSpec(shape, lambda b, i: (0, 0), pipeline_mode=pl.Buffered(1))
    return pl.pallas_call(
        _ffn_kernel,
        grid=(bsz, n // tm),
        in_specs=[
            tile, tile,
            pl.BlockSpec((None, 1, d), lambda b, i: (b, 0, 0)),
            resident((d, f)), resident((d, f)), resident((f, d)),
        ],
        out_specs=tile,
        out_shape=jax.ShapeDtypeStruct((bsz, n, d), F32),
        compiler_params=_cparams(("parallel", "parallel")),
        name="ffn",
    )(h, x, gate, w1, w3, w2)


MOE_BLK = 256


def _dispatch_kernel(d0_ref, d1_ref, zr_ref, hx_ref, hs_ref, zbuf, sem, zsem):
    tm = hx_ref.shape[0]
    base = pl.program_id(0) * tm

    @pl.when(pl.program_id(0) == 0)
    def _():
        zbuf[...] = jnp.zeros_like(zbuf)
        for z in range(zr_ref.shape[0]):
            @pl.when(zr_ref[z] >= 0)
            def _():
                pltpu.make_async_copy(zbuf, hs_ref.at[pl.ds(zr_ref[z], MOE_BLK)], zsem).start()
        for z in range(zr_ref.shape[0]):
            @pl.when(zr_ref[z] >= 0)
            def _():
                pltpu.make_async_copy(zbuf, hs_ref.at[pl.ds(zr_ref[z], MOE_BLK)], zsem).wait()

    def issue(t, carry):
        pltpu.make_async_copy(hx_ref.at[t], hs_ref.at[d0_ref[base + t]], sem).start(priority=0)
        pltpu.make_async_copy(hx_ref.at[t], hs_ref.at[d1_ref[base + t]], sem).start(priority=1)
        return carry

    lax.fori_loop(0, tm, issue, 0, unroll=8)
    for _ in range(2):
        pltpu.make_async_copy(hx_ref, hs_ref.at[pl.ds(0, tm)], sem).wait()


def _dispatch(dest0, dest1, zero_rows, hx, n_rows):
    n_tok, s, lanes = hx.shape
    tm = 512
    grid_spec = pltpu.PrefetchScalarGridSpec(
        num_scalar_prefetch=3,
        grid=(n_tok // tm,),
        in_specs=[pl.BlockSpec((tm, s, lanes), lambda i, d0, d1, zr: (i, 0, 0))],
        out_specs=pl.BlockSpec(memory_space=pl.ANY),
        scratch_shapes=[pltpu.VMEM((MOE_BLK, s, lanes), F32),
                        pltpu.SemaphoreType.DMA(()), pltpu.SemaphoreType.DMA(())],
    )
    return pl.pallas_call(
        _dispatch_kernel,
        grid_spec=grid_spec,
        out_shape=jax.ShapeDtypeStruct((n_rows, s, lanes), F32),
        compiler_params=_cparams(("arbitrary",)),
        name="moe_dispatch",
    )(dest0, dest1, zero_rows, hx)


def _experts_kernel(be_ref, nu_ref, hs_ref, w1_ref, w3_ref, w2_ref, y_ref):
    j = pl.program_id(0)
    d = w1_ref.shape[0]
    ns = d // LANES
    blk = hs_ref.shape[0] // ns

    @pl.when(j < nu_ref[0])
    def _():
        h = jnp.concatenate(
            [hs_ref[pl.ds(s, blk, stride=ns), :].astype(BF16) for s in range(ns)], axis=1)
        out = _swiglu_acc(h, w1_ref, w3_ref, w2_ref)
        for s in range(ns):
            y_ref[pl.ds(s, blk, stride=ns), :] = out[:, s * LANES:(s + 1) * LANES]

    @pl.when(j >= nu_ref[0])
    def _():
        y_ref[...] = jnp.zeros_like(y_ref)


def _experts(block_expert, n_used, hs, w1, w3, w2):
    ne, d, f = w1.shape
    s = d // LANES
    n_rows = hs.shape[0] // s
    n_blocks = n_rows // MOE_BLK

    def expert(shape):
        return pl.BlockSpec((None,) + shape, lambda j, be, nu: (be[j], 0, 0),
                            pipeline_mode=pl.Buffered(1))

    grid_spec = pltpu.PrefetchScalarGridSpec(
        num_scalar_prefetch=2,
        grid=(n_blocks,),
        in_specs=[
            pl.BlockSpec((MOE_BLK * s, LANES), lambda j, be, nu: (jnp.minimum(j, nu[0] - 1), 0)),
            expert((d, f)), expert((d, f)), expert((f, d)),
        ],
        out_specs=pl.BlockSpec((MOE_BLK * s, LANES), lambda j, be, nu: (j, 0)),
    )
    return pl.pallas_call(
        _experts_kernel,
        grid_spec=grid_spec,
        out_shape=jax.ShapeDtypeStruct((n_rows * s, LANES), F32),
        compiler_params=_cparams(("arbitrary",)),
        name="moe_experts",
    )(block_expert, n_used, hs, w1, w3, w2)


def _combine_kernel(d0_ref, d1_ref, x_ref, gate_ref, rt_ref, gfin_ref, y_ref, o_ref, ybuf, sem):
    tm, d = x_ref.shape
    ns = d // LANES
    step = pl.program_id(0) * pl.num_programs(1) + pl.program_id(1)
    nsteps = pl.num_programs(0) * pl.num_programs(1)
    slot = step % 2

    def row(ref, r):
        return ref.at[pl.ds(pl.multiple_of(r * ns, ns), ns)]

    def start_gather(block, into):
        base = block * tm

        def issue(t, carry):
            pltpu.make_async_copy(row(y_ref, d0_ref[base + t]), row(ybuf.at[into, 0], t),
                                  sem.at[into]).start(priority=0)
            pltpu.make_async_copy(row(y_ref, d1_ref[base + t]), row(ybuf.at[into, 1], t),
                                  sem.at[into]).start(priority=1)
            return carry

        lax.fori_loop(0, tm, issue, 0, unroll=8)

    @pl.when(step == 0)
    def _():
        start_gather(0, 0)

    @pl.when(step + 1 < nsteps)
    def _():
        start_gather(step + 1, 1 - slot)

    for k in range(2):
        pltpu.make_async_copy(y_ref.at[pl.ds(0, tm * ns)], ybuf.at[slot, k], sem.at[slot]).wait()

    chunk = 64
    for r0 in range(0, tm, chunk):
        rows = pl.ds(r0, chunk)
        rt = rt_ref[rows, :]
        w0 = rt[:, R_W0:R_W0 + 1]
        w1 = rt[:, R_W1:R_W1 + 1]
        ss = jnp.zeros((chunk, 1), F32)
        for s in range(ns):
            cols = slice(s * LANES, (s + 1) * LANES)
            moe = (w0 * ybuf[slot, 0, pl.ds(r0 * ns + s, chunk, stride=ns), :]
                   + w1 * ybuf[slot, 1, pl.ds(r0 * ns + s, chunk, stride=ns), :])
            xn = x_ref[rows, cols] + gate_ref[:, cols] * moe
            ss = ss + jnp.sum(xn * xn, axis=-1, keepdims=True)
            o_ref[rows, cols] = xn
        o_ref[rows, :] = o_ref[rows, :] * lax.rsqrt(ss / d + EPS) * gfin_ref[...]


def _combine(dest0, dest1, x, gate, route, g_final, y):
    bsz, n, d = x.shape
    s = d // LANES
    tm = 512
    grid_spec = pltpu.PrefetchScalarGridSpec(
        num_scalar_prefetch=2,
        grid=(bsz, n // tm),
        in_specs=[
            pl.BlockSpec((None, tm, d), lambda b, i, d0, d1: (b, i, 0)),
            pl.BlockSpec((None, 1, d), lambda b, i, d0, d1: (b, 0, 0)),
            pl.BlockSpec((None, tm, LANES), lambda b, i, d0, d1: (b, i, 0)),
            pl.BlockSpec((1, d), lambda b, i, d0, d1: (0, 0)),
            pl.BlockSpec(memory_space=pl.ANY),
        ],
        out_specs=pl.BlockSpec((None, tm, d), lambda b, i, d0, d1: (b, i, 0)),
        scratch_shapes=[pltpu.VMEM((2, 2, tm * s, LANES), F32), pltpu.SemaphoreType.DMA((2,))],
    )
    return pl.pallas_call(
        _combine_kernel,
        grid_spec=grid_spec,
        out_shape=jax.ShapeDtypeStruct((bsz, n, d), F32),
        compiler_params=_cparams(("arbitrary", "arbitrary")),
        name="moe_combine",
    )(dest0, dest1, x, gate, route, g_final, y)


def _moe(hx, x, gate, route, counts, g_final, w1, w3, w2):
    bsz, n, d = x.shape
    ne = w1.shape[0]
    n_tok = bsz * n
    n_rows = -(-(2 * n_tok) // MOE_BLK) * MOE_BLK + ne * MOE_BLK
    n_blocks = n_rows // MOE_BLK
    cnt = counts[0, :ne].astype(jnp.int32)
    padded = (cnt + MOE_BLK - 1) // MOE_BLK * MOE_BLK
    pad_end = jnp.cumsum(padded)
    pad_start = pad_end - padded
    n_used = pad_end[-1] // MOE_BLK
    blk = jnp.minimum(jnp.arange(n_blocks, dtype=jnp.int32), n_used - 1)
    block_expert = jnp.minimum(
        jnp.sum((blk[:, None] * MOE_BLK >= pad_end[None, :]).astype(jnp.int32), axis=1), ne - 1)
    rt = route.reshape(n_tok, LANES)
    e0 = rt[:, R_E0].astype(jnp.int32)
    e1 = rt[:, R_E1].astype(jnp.int32)
    dest0 = pad_start[e0] + rt[:, R_RANK0].astype(jnp.int32)
    dest1 = pad_start[e1] + rt[:, R_RANK1].astype(jnp.int32)

    last_blk = jnp.where(padded > 0, pad_end - MOE_BLK, -1)
    trail = n_used + jnp.arange(ne, dtype=jnp.int32)
    trail = jnp.where(trail < n_blocks, trail * MOE_BLK, -1)
    zero_rows = jnp.concatenate([last_blk, trail]).astype(jnp.int32)

    ns = d // LANES
    hs = _dispatch(dest0, dest1, zero_rows, hx.reshape(n_tok, ns, LANES), n_rows)
    y = _experts(block_expert, n_used.reshape(1).astype(jnp.int32),
                 hs.reshape(n_rows * ns, LANES), w1, w3, w2)
    return _combine(dest0, dest1, x, gate, route, g_final, y)


def _rope_tables(n, hd):
    rows = n // GRID_W
    n_freq = hd // 4
    inv = ROPE_BASE ** (-jnp.arange(n_freq, dtype=F32) / n_freq)
    row_ang = jnp.arange(rows, dtype=F32)[:, None] * inv
    col_ang = jnp.arange(GRID_W, dtype=F32)[:, None] * inv

    def table(of_row, of_col):
        a = jnp.broadcast_to(of_row[:, None, :], (rows, GRID_W, n_freq))
        b = jnp.broadcast_to(of_col[None, :, :], (rows, GRID_W, n_freq))
        return jnp.concatenate([a, b], axis=-1).reshape(n, 2 * n_freq)

    cos = table(jnp.cos(row_ang), jnp.cos(col_ang))
    sin = table(jnp.sin(row_ang), jnp.sin(col_ang))
    return jnp.concatenate([cos, cos], axis=-1), jnp.concatenate([-sin, sin], axis=-1)


def _block_diag_gates(rg_wa, rg_wx):
    w = jnp.stack([rg_wa, rg_wx], axis=1)
    nd, ng, nk, c, _ = w.shape
    per = RG_CG // c
    w = w.reshape(nd, ng, nk // per, per, c, c)
    eye = jnp.eye(per, dtype=w.dtype)
    full = jnp.einsum('dgmpij,pq->dgmpiqj', w, eye)
    return full.reshape(nd, ng, nk // per, RG_CG, RG_CG).astype(BF16)


def _mixer(p, rope, states, conv_w, conv_b, wg, bg, lam, log_g, rg_w):
    rg_out, h_last = _rglru(p, conv_w, conv_b, wg, bg, lam, states[0])
    ret_out, s_last = _retention(p, log_g, rope[0], rope[1], states[1], 2 * rg_w)
    return rg_out, ret_out, (h_last, s_last)


def kernel(x, c, ctx, c_ctx, w_mod, b_mod, g_mix, g_ffn, g_final, w_in, w_out, conv_w, conv_b,
           rg_wa, rg_ba, rg_wx, rg_bx, rg_lam, ret_decay, ffn_w1, ffn_w3, ffn_w2,
           moe_router, moe_router_b, moe_w1, moe_w3, moe_w2):
    bsz, n_lat, d = x.shape
    n_ctx = ctx.shape[1]
    depth = w_mod.shape[0]
    rg_w = rg_lam.shape[2]
    nh = ret_decay.shape[2]
    hd = (w_out.shape[1] - rg_w) // nh
    ne = moe_router.shape[2]
    assert depth == 2, "kernel is written for the two-layer block (dense FFN, then MoE)"

    rows = -(-(bsz + 1) // SUBLANES) * SUBLANES
    c_all = jnp.zeros((rows, d), F32).at[:bsz].set(c).at[bsz].set(c_ctx)
    mod = _modulation(c_all, w_mod, b_mod)
    mod = mod.reshape(depth, rows, N_MOD, d)
    mod_lat = mod[:, :bsz, None]
    mod_ctx = jnp.broadcast_to(mod[:, bsz:bsz + 1, None], mod_lat.shape)

    rope_lat = _rope_tables(n_lat, hd)
    rope_ctx = (jnp.ones((n_ctx, hd), F32), jnp.zeros((n_ctx, hd), F32))
    log_g = jax.nn.log_sigmoid(ret_decay.astype(F32))
    zero_states = (jnp.zeros((bsz, 2, rg_w), F32), jnp.zeros((bsz, 2, nh, hd, hd), F32))

    for l in range(depth):
        last = l == depth - 1
        ml = [mod_lat[l, :, :, j] for j in range(N_MOD)]
        mc = [mod_ctx[l, :, :, j] for j in range(N_MOD)]
        g_m = g_mix[l].reshape(1, d)
        g_f = g_ffn[l].reshape(1, d)
        w_in_l = w_in[l].astype(BF16)
        w_out_l = w_out[l].astype(BF16)
        wg = _block_diag_gates(0.5 * rg_wa[l], 0.5 * rg_wx[l])
        bg = 0.5 * jnp.stack([rg_ba[l], rg_bx[l]], axis=1)
        mix_p = (conv_w[l], conv_b[l], wg, bg, rg_lam[l], log_g[l], rg_w)

        pc = _in_proj(ctx, g_m, mc[0], mc[1], w_in_l)
        rg_c, ret_c, ctx_states = _mixer(pc, rope_ctx, zero_states, *mix_p)
        px = _in_proj(x, g_m, ml[0], ml[1], w_in_l)
        rg_x, ret_x, _ = _mixer(px, rope_lat, ctx_states, *mix_p)

        if l % 2 == 0:
            i = l // 2
            w1, w3, w2 = ffn_w1[i].astype(BF16), ffn_w3[i].astype(BF16), ffn_w2[i].astype(BF16)
            x, hx = _out_proj(x, rg_x, ret_x, w_out_l, ml[2], g_f, ml[3], ml[4])
            x = _ffn(hx, x, ml[5], w1, w3, w2)
            if not last:
                ctx, hc = _out_proj(ctx, rg_c, ret_c, w_out_l, mc[2], g_f, mc[3], mc[4])
                ctx = _ffn(hc, ctx, mc[5], w1, w3, w2)
        else:
            i = l // 2
            wr = jnp.zeros((d, LANES), F32).at[:, :ne].set(moe_router[i])
            wr_hi = wr.astype(BF16)
            wr_lo = (wr - wr_hi.astype(F32)).astype(BF16)
            wr2 = jnp.concatenate([wr_hi, wr_lo], axis=1)
            br = jnp.full((1, LANES), NEG_BIG, F32).at[0, :ne].set(moe_router_b[i])
            x, hx, route, counts = _out_proj(x, rg_x, ret_x, w_out_l, ml[2], g_f, ml[3], ml[4],
                                             router=(wr2, br))
            x = _moe(hx, x, ml[5], route, counts, g_final.reshape(1, d),
                     moe_w1[i].astype(BF16), moe_w3[i].astype(BF16), moe_w2[i].astype(BF16))
    return x
```

```python
import functools

import jax
import jax.numpy as jnp
from jax import lax
from jax.experimental import pallas as pl
from jax.experimental.pallas import tpu as pltpu

F32 = jnp.float32
BF16 = jnp.bfloat16

EPS = 1e-6
RG_C = 8.0
CONV_W = 4
CONV_LEFT = 2
RG_BLOCKS = 8
RET_HEADS = 4
RET_CHUNK = 128
ROPE_BASE = 10000.0
GRID_W = 64
N_EXPERTS = 8
N_MOD = 6

LANES = 128
SUBLANES = 8
VMEM_LIMIT = 56 * 1024 * 1024
NEG_BIG = -1e30
LOG2_E = 1.4426950408889634


def _cparams(sem):
    return pltpu.CompilerParams(dimension_semantics=sem, vmem_limit_bytes=VMEM_LIMIT)


def _modulation_kernel(c_ref, w_ref, b_ref, o_ref):
    c = c_ref[...]
    s = c * jax.nn.sigmoid(c)
    o_ref[...] = jnp.dot(s, w_ref[...], precision=lax.Precision.HIGHEST,
                         preferred_element_type=F32) + b_ref[...]


def _modulation(c_all, w_mod, b_mod):
    depth, d, six_d = w_mod.shape
    rows = c_all.shape[0]
    tn = 1536
    return pl.pallas_call(
        _modulation_kernel,
        grid=(depth, six_d // tn),
        in_specs=[
            pl.BlockSpec((rows, d), lambda l, j: (0, 0)),
            pl.BlockSpec((None, d, tn), lambda l, j: (l, 0, j)),
            pl.BlockSpec((None, 1, tn), lambda l, j: (l, 0, j)),
        ],
        out_specs=pl.BlockSpec((None, rows, tn), lambda l, j: (l, 0, j)),
        out_shape=jax.ShapeDtypeStruct((depth, rows, six_d), F32),
        compiler_params=_cparams(("arbitrary", "arbitrary")),
        name="modulation",
    )(c_all, w_mod, b_mod.reshape(depth, 1, six_d))


def _norm_mod(x, g, shift, scale):
    ms = jnp.mean(x * x, axis=-1, keepdims=True)
    y = x * lax.rsqrt(ms + EPS) * g
    return y * (1.0 + scale) + shift


def _in_proj_kernel(x_ref, g_ref, shift_ref, scale_ref, w_ref, o_ref):
    h = _norm_mod(x_ref[...], g_ref[...], shift_ref[...], scale_ref[...])
    o_ref[...] = jnp.dot(h.astype(BF16), w_ref[...],
                         preferred_element_type=F32).astype(o_ref.dtype)


def _in_proj(x, g, shift, scale, w):
    bsz, n, d = x.shape
    f = w.shape[1]
    tm = min(512, n)
    vec = pl.BlockSpec((None, 1, d), lambda b, i: (b, 0, 0))
    return pl.pallas_call(
        _in_proj_kernel,
        grid=(bsz, n // tm),
        in_specs=[
            pl.BlockSpec((None, tm, d), lambda b, i: (b, i, 0)),
            pl.BlockSpec((1, d), lambda b, i: (0, 0)),
            vec, vec,
            pl.BlockSpec((d, f), lambda b, i: (0, 0)),
        ],
        out_specs=pl.BlockSpec((None, tm, f), lambda b, i: (b, i, 0)),
        out_shape=jax.ShapeDtypeStruct((bsz, n, f), BF16),
        compiler_params=_cparams(("parallel", "parallel")),
        name="in_proj",
    )(x, g, shift, scale, w)


RG_TILE = 256
RG_CG = 256
RG_SEG = RG_TILE // SUBLANES
RG_PITCH = RG_SEG + SUBLANES


def _rglru_kernel(u_ref, yg_ref, cw_ref, cb_ref, wg_ref, bg_ref, lam_ref, h0_ref,
                  o_ref, hl_ref, u_scr, hf_scr, nat_scr):
    n = u_ref.shape[0]
    tt = RG_TILE
    nt = n // tt
    nseg = n // RG_SEG
    cg = u_ref.shape[1]
    lane_cols = [slice(g * LANES, (g + 1) * LANES) for g in range(cg // LANES)]

    zeros = jnp.zeros((RG_PITCH, LANES), F32)
    for g, cols in enumerate(lane_cols):
        u_scr[g, pl.ds(0, RG_PITCH), :] = zeros
        u_scr[g, pl.ds(nseg * RG_PITCH + RG_SEG, SUBLANES), :] = zeros[0:SUBLANES]

    def fill(s, carry):
        src = pl.multiple_of(s * RG_SEG, RG_SEG)
        dst = pl.multiple_of((s + 1) * RG_PITCH, SUBLANES)
        for g, cols in enumerate(lane_cols):
            rows = u_ref[pl.ds(src, RG_SEG), cols].astype(F32)
            u_scr[g, pl.ds(dst, RG_SEG), :] = rows
            u_scr[g, pl.ds(dst - SUBLANES, SUBLANES), :] = rows[0:SUBLANES]
        return carry

    lax.fori_loop(0, nseg, fill, 0)

    seg = RG_SEG
    cw_rows = [[jnp.broadcast_to(cw_ref[k:k + 1, c], (SUBLANES, LANES)) for c in lane_cols]
               for k in range(CONV_W)]
    cb_rows = [jnp.broadcast_to(cb_ref[:, c], (SUBLANES, LANES)) for c in lane_cols]

    def conv_tile(i):
        base = pl.multiple_of((i * SUBLANES + 1) * RG_PITCH, SUBLANES)
        blocks = []
        for j in range(seg):
            lane_groups = []
            for g in range(cg // LANES):
                acc = cb_rows[g]
                for k in range(CONV_W):
                    q = j + k - CONV_LEFT
                    start = base + q if q >= 0 else base - RG_PITCH + seg + q
                    acc = acc + cw_rows[k][g] * u_scr[g, pl.ds(start, SUBLANES, stride=RG_PITCH), :]
                lane_groups.append(acc)
            blocks.append(jnp.concatenate(lane_groups, axis=1))
        return jnp.concatenate(blocks, axis=0)

    def gates(uc, d):
        ub = uc.astype(BF16)
        ta = jnp.tanh(jnp.dot(ub, wg_ref[d, 0], preferred_element_type=F32) + bg_ref[d, 0:1, :])
        ti = jnp.tanh(jnp.dot(ub, wg_ref[d, 1], preferred_element_type=F32) + bg_ref[d, 1:2, :])
        z = -lam_ref[d:d + 1, :]
        sp = jnp.maximum(z, 0.0) + jnp.log1p(jnp.exp(-jnp.abs(z)))
        c2 = (-0.5 * RG_C * LOG2_E) * sp
        a = jnp.exp2(c2 + c2 * ta)
        b = jnp.exp2((0.5 * LOG2_E) * jnp.log(1.0 - a * a)) * ((0.5 + 0.5 * ti) * uc)
        return a, b

    def scan_tile(a, b, carry, reverse):
        steps = range(seg - 1, -1, -1) if reverse else range(seg)
        h = acum = None
        h_loc = [None] * seg
        a_cum = [None] * seg
        for j in steps:
            rows = slice(j * SUBLANES, (j + 1) * SUBLANES)
            h = b[rows] if h is None else a[rows] * h + b[rows]
            acum = a[rows] if acum is None else a[rows] * acum
            h_loc[j], a_cum[j] = h, acum
        order = range(SUBLANES - 1, -1, -1) if reverse else range(SUBLANES)
        c = carry
        c_in = [None] * SUBLANES
        for s in order:
            c_in[s] = c
            c = h[s:s + 1, :] + acum[s:s + 1, :] * c
        c_in = jnp.concatenate(c_in, axis=0)
        out = jnp.concatenate([h_loc[j] + a_cum[j] * c_in for j in range(seg)], axis=0)
        return out, c

    def fwd_body(i, carry):
        t0 = pl.multiple_of(i * tt, tt)
        uc = conv_tile(i)
        a, b = gates(uc, 0)
        h, carry = scan_tile(a, b, carry, False)
        hf_scr[pl.ds(t0, tt), :] = h
        return carry

    hf_last = lax.fori_loop(0, nt, fwd_body, h0_ref[0:1, :], unroll=min(2, nt))

    def bwd_body(i, carry):
        tile = nt - 1 - i
        t0 = pl.multiple_of(tile * tt, tt)
        uc = conv_tile(tile)
        a, b = gates(uc, 1)
        h, carry = scan_tile(a, b, carry, True)
        hsum = hf_scr[pl.ds(t0, tt), :] + h
        for j in range(seg):
            for g, cols in enumerate(lane_cols):
                nat_scr[g, pl.ds(j, SUBLANES, stride=RG_PITCH), :] = (
                    hsum[j * SUBLANES:(j + 1) * SUBLANES, cols])
        hnat = jnp.concatenate(
            [jnp.concatenate([nat_scr[g, pl.ds(s * RG_PITCH, seg), :] for s in range(SUBLANES)],
                             axis=0) for g in range(len(lane_cols))], axis=1)
        yg = yg_ref[pl.ds(t0, tt), :].astype(F32)
        o_ref[pl.ds(t0, tt), :] = (jax.nn.gelu(yg) * hnat).astype(o_ref.dtype)
        return carry

    hb_last = lax.fori_loop(0, nt, bwd_body, h0_ref[1:2, :], unroll=min(2, nt))
    hl_ref[0:1, :] = hf_last
    hl_ref[1:2, :] = hb_last


def _rglru(p, conv_w, conv_b, wg, bg, lam, h0):
    bsz, n, _ = p.shape
    rg_w = lam.shape[1]
    ncg = rg_w // RG_CG
    assert n % RG_TILE == 0 and rg_w % RG_CG == 0
    return pl.pallas_call(
        _rglru_kernel,
        grid=(bsz, ncg),
        in_specs=[
            pl.BlockSpec((None, n, RG_CG), lambda b, c: (b, 0, c)),
            pl.BlockSpec((None, n, RG_CG), lambda b, c: (b, 0, ncg + c)),
            pl.BlockSpec((CONV_W, RG_CG), lambda b, c: (0, c)),
            pl.BlockSpec((1, RG_CG), lambda b, c: (0, c)),
            pl.BlockSpec((2, 2, None, RG_CG, RG_CG), lambda b, c: (0, 0, c, 0, 0)),
            pl.BlockSpec((2, 2, RG_CG), lambda b, c: (0, 0, c)),
            pl.BlockSpec((2, RG_CG), lambda b, c: (0, c)),
            pl.BlockSpec((None, 2, RG_CG), lambda b, c: (b, 0, c)),
        ],
        out_specs=[
            pl.BlockSpec((None, n, RG_CG), lambda b, c: (b, 0, c)),
            pl.BlockSpec((None, 2, RG_CG), lambda b, c: (b, 0, c)),
        ],
        out_shape=[
            jax.ShapeDtypeStruct((bsz, n, rg_w), BF16),
            jax.ShapeDtypeStruct((bsz, 2, rg_w), F32),
        ],
        scratch_shapes=[
            pltpu.VMEM((RG_CG // LANES, (n // RG_SEG + 1) * RG_PITCH, LANES), F32),
            pltpu.VMEM((n, RG_CG), F32),
            pltpu.VMEM((RG_CG // LANES, SUBLANES * RG_PITCH, LANES), F32),
        ],
        compiler_params=_cparams(("parallel", "parallel")),
        name="rglru",
    )(p, p, conv_w, conv_b.reshape(1, rg_w), wg, bg, lam, h0)


RET_UNROLL = 8


def _retention_kernel(lg_ref, q_ref, k_ref, v_ref, g_ref, cs_ref, sn_ref, s0_ref,
                      o_ref, so_ref, kr_scr, st_scr):
    n, hd = q_ref.shape
    L = RET_CHUNK
    nc = n // L
    head = pl.program_id(1)
    lgf = lg_ref[0, head]
    lgb = lg_ref[1, head]

    def rope(t, rows):
        return t * cs_ref[rows, :] + pltpu.roll(t, hd // 2, axis=1) * sn_ref[rows, :]

    kr_scr[...] = (rope(k_ref[...].astype(F32), slice(None)) * (hd ** -0.5)).astype(BF16)

    jc = lax.broadcasted_iota(jnp.int32, (L, 1), 0).astype(F32)
    q_dec_f = jnp.exp((jc + 1.0) * lgf)
    q_dec_b = jnp.exp((L - jc) * lgb)
    k_dec_f = jnp.exp((L - 1.0 - jc) * lgf)
    k_dec_b = jnp.exp(jc * lgb)
    ones = jnp.ones((1, hd), F32)
    chunk_f = jnp.exp(ones * (L * lgf))
    chunk_b = jnp.exp(ones * (L * lgb))
    ii = lax.broadcasted_iota(jnp.int32, (L, L), 0)
    jj = lax.broadcasted_iota(jnp.int32, (L, L), 1)
    diff = (ii - jj).astype(F32)
    dmat = jnp.where(diff >= 0.0, jnp.exp(jnp.maximum(diff, 0.0) * lgf),
                     jnp.exp(jnp.maximum(-diff, 0.0) * lgb))

    tn_dims = (((0,), (0,)), ((), ()))
    nt_dims = (((1,), (1,)), ((), ()))

    def state_update(c, state, k_dec, chunk_dec):
        t0 = pl.multiple_of(c * L, L)
        kd = (kr_scr[pl.ds(t0, L), :].astype(F32) * k_dec).astype(BF16)
        upd = lax.dot_general(kd, v_ref[pl.ds(t0, L), :], tn_dims, preferred_element_type=F32)
        return chunk_dec * state + upd

    def state_body(i, carry):
        f_state, r_state = carry
        cf = i
        cb = nc - 1 - i
        st_scr[cf, :, 0:hd] = f_state.astype(BF16)
        st_scr[cb, :, hd:2 * hd] = r_state.astype(BF16)
        return (state_update(cf, f_state, k_dec_f, chunk_f),
                state_update(cb, r_state, k_dec_b, chunk_b))

    f_fin, r_fin = lax.fori_loop(0, nc, state_body, (s0_ref[0], s0_ref[1]),
                                 unroll=min(RET_UNROLL, nc))
    so_ref[0] = f_fin
    so_ref[1] = r_fin

    def out_body(c, carry):
        t0 = pl.multiple_of(c * L, L)
        qc = rope(q_ref[pl.ds(t0, L), :].astype(F32), pl.ds(t0, L)).astype(BF16)
        kc = kr_scr[pl.ds(t0, L), :]
        vc = v_ref[pl.ds(t0, L), :]
        s = lax.dot_general(qc, kc, nt_dims, preferred_element_type=F32)
        o = jnp.dot((s * dmat).astype(BF16), vc, preferred_element_type=F32)
        cross = jnp.dot(qc, st_scr[c], preferred_element_type=F32)
        o = o + cross[:, 0:hd] * q_dec_f + cross[:, hd:2 * hd] * q_dec_b
        mu = jnp.mean(o, axis=-1, keepdims=True)
        oc = o - mu
        var = jnp.mean(oc * oc, axis=-1, keepdims=True)
        gate = g_ref[pl.ds(t0, L), :].astype(F32)
        o_ref[pl.ds(t0, L), :] = (gate * jax.nn.sigmoid(gate) * (oc * lax.rsqrt(var + EPS))).astype(o_ref.dtype)
        return carry

    lax.fori_loop(0, nc, out_body, 0, unroll=min(RET_UNROLL, nc))


def _retention(p, log_g, cs, sn, s0, col0):
    bsz, n, _ = p.shape
    nh = log_g.shape[1]
    hd = cs.shape[1]
    cb0 = col0 // hd

    def col(which):
        return pl.BlockSpec((None, n, hd), lambda b, h, lg: (b, 0, cb0 + which * nh + h))

    state = pl.BlockSpec((None, 2, None, hd, hd), lambda b, h, lg: (b, 0, h, 0, 0))
    table = pl.BlockSpec((n, hd), lambda b, h, lg: (0, 0))
    grid_spec = pltpu.PrefetchScalarGridSpec(
        num_scalar_prefetch=1,
        grid=(bsz, nh),
        in_specs=[col(0), col(1), col(2), col(3), table, table, state],
        out_specs=[pl.BlockSpec((None, n, hd), lambda b, h, lg: (b, 0, h)), state],
        scratch_shapes=[
            pltpu.VMEM((n, hd), BF16),
            pltpu.VMEM((n // RET_CHUNK, hd, 2 * hd), BF16),
        ],
    )
    return pl.pallas_call(
        _retention_kernel,
        grid_spec=grid_spec,
        out_shape=[
            jax.ShapeDtypeStruct((bsz, n, nh * hd), BF16),
            jax.ShapeDtypeStruct((bsz, 2, nh, hd, hd), F32),
        ],
        compiler_params=_cparams(("parallel", "parallel")),
        name="retention",
    )(log_g, p, p, p, p, cs, sn, s0)


R_RANK0, R_RANK1, R_E0, R_E1, R_W0, R_W1 = range(6)


def _out_proj_kernel(route, x_ref, rg_ref, ret_ref, wo_ref, gate_ref, g_ref, shift_ref, scale_ref,
                     *rest):
    if route:
        wr2_ref, br_ref, xo_ref, h_ref, rt_ref, rtt_ref, cnt_ref, carry_scr = rest
    else:
        xo_ref, h_ref = rest
    rg_w = rg_ref.shape[1]
    y = jnp.dot(rg_ref[...], wo_ref[0:rg_w, :], preferred_element_type=F32)
    y = y + jnp.dot(ret_ref[...], wo_ref[rg_w:, :], preferred_element_type=F32)
    xn = x_ref[...] + gate_ref[...] * y
    xo_ref[...] = xn
    hx = _norm_mod(xn, g_ref[...], shift_ref[...], scale_ref[...])
    if not route:
        h_ref[...] = hx.astype(h_ref.dtype)
        return

    tm, d = hx.shape
    for s in range(d // LANES):
        h_ref[pl.ds(s, tm, stride=d // LANES), :] = hx[:, s * LANES:(s + 1) * LANES]

    h_hi = hx.astype(BF16)
    h_lo = (hx - h_hi.astype(F32)).astype(BF16)
    part = jnp.dot(h_hi, wr2_ref[...], preferred_element_type=F32)
    logits = (part[:, :LANES] + part[:, LANES:]
              + jnp.dot(h_lo, wr2_ref[:, :LANES], preferred_element_type=F32) + br_ref[...])
    lane_i = lax.broadcasted_iota(jnp.int32, logits.shape, 1)
    lane = lane_i.astype(F32)
    m0 = jnp.max(logits, axis=-1, keepdims=True)
    i0 = jnp.min(jnp.where(logits == m0, lane, float(LANES)), axis=-1, keepdims=True)
    rest_l = jnp.where(lane == i0, NEG_BIG, logits)
    m1 = jnp.max(rest_l, axis=-1, keepdims=True)
    i1 = jnp.min(jnp.where(rest_l == m1, lane, float(LANES)), axis=-1, keepdims=True)
    e = jnp.exp(m1 - m0)
    w0 = 1.0 / (1.0 + e)
    w1 = e / (1.0 + e)

    @pl.when((pl.program_id(0) == 0) & (pl.program_id(1) == 0))
    def _():
        carry_scr[...] = jnp.zeros_like(carry_scr)

    sel0 = lane == i0
    sel1 = lane == i1
    mask = jnp.where(sel0 | sel1, 1.0, 0.0)
    ri = lax.broadcasted_iota(jnp.int32, (tm, tm), 0)
    ci = lax.broadcasted_iota(jnp.int32, (tm, tm), 1)
    tri = jnp.where(ri > ci, 1.0, 0.0).astype(BF16)
    before = jnp.dot(tri, mask.astype(BF16), preferred_element_type=F32) + carry_scr[...]
    rank0 = jnp.sum(jnp.where(sel0, before, 0.0), axis=-1, keepdims=True)
    rank1 = jnp.sum(jnp.where(sel1, before, 0.0), axis=-1, keepdims=True)
    total = carry_scr[...] + jnp.sum(mask, axis=0, keepdims=True)
    carry_scr[...] = total
    cnt_ref[...] = total
    rec = jnp.zeros_like(logits)
    for idx, val in ((R_RANK0, rank0), (R_RANK1, rank1), (R_E0, i0), (R_E1, i1),
                     (R_W0, w0), (R_W1, w1)):
        rec = jnp.where(lane_i == idx, val, rec)
    rt_ref[...] = rec
    rtt_ref[...] = jnp.transpose(rec)[0:rtt_ref.shape[0], :]


def _out_proj(x, rg, ret, w_out, gate, g, shift, scale, router=None):
    bsz, n, d = x.shape
    tm = min(512, n)
    route = router is not None
    vec = pl.BlockSpec((None, 1, d), lambda b, i: (b, 0, 0))
    tile = lambda w: pl.BlockSpec((None, tm, w), lambda b, i: (b, i, 0))
    in_specs = [
        tile(d), tile(rg.shape[2]), tile(ret.shape[2]),
        pl.BlockSpec(w_out.shape, lambda b, i: (0, 0)),
        vec, pl.BlockSpec((1, d), lambda b, i: (0, 0)), vec, vec,
    ]
    args = [x, rg, ret, w_out, gate, g, shift, scale]
    scratch = []
    if route:
        in_specs += [pl.BlockSpec((d, 2 * LANES), lambda b, i: (0, 0)),
                     pl.BlockSpec((1, LANES), lambda b, i: (0, 0))]
        args += list(router)
        out_specs = [tile(d),
                     pl.BlockSpec((None, tm * (d // LANES), LANES), lambda b, i: (b, i, 0)),
                     tile(LANES),
                     pl.BlockSpec((SUBLANES, tm), lambda b, i: (0, b * (n // tm) + i)),
                     pl.BlockSpec((1, LANES), lambda b, i: (0, 0))]
        out_shape = [jax.ShapeDtypeStruct((bsz, n, d), F32),
                     jax.ShapeDtypeStruct((bsz, n * (d // LANES), LANES), F32),
                     jax.ShapeDtypeStruct((bsz, n, LANES), F32),
                     jax.ShapeDtypeStruct((SUBLANES, bsz * n), F32),
                     jax.ShapeDtypeStruct((1, LANES), F32)]
        scratch = [pltpu.VMEM((1, LANES), F32)]
        sem = ("arbitrary", "arbitrary")
    else:
        out_specs = [tile(d), tile(d)]
        out_shape = [jax.ShapeDtypeStruct((bsz, n, d), F32), jax.ShapeDtypeStruct((bsz, n, d), BF16)]
        sem = ("parallel", "parallel")
    return pl.pallas_call(
        functools.partial(_out_proj_kernel, route),
        grid=(bsz, n // tm),
        in_specs=in_specs,
        out_specs=out_specs,
        out_shape=out_shape,
        scratch_shapes=scratch,
        compiler_params=_cparams(sem),
        name="out_proj_route" if route else "out_proj",
    )(*args)


def _swiglu_acc(h, w1_ref, w3_ref, w2_ref, row_scale=None):
    a = jnp.dot(h, w1_ref[...], preferred_element_type=F32)
    b = jnp.dot(h, w3_ref[...], preferred_element_type=F32)
    act = a * jax.nn.sigmoid(a) * b
    if row_scale is not None:
        act = act * row_scale
    return jnp.dot(act.astype(BF16), w2_ref[...], preferred_element_type=F32)


def _ffn_kernel(n_cast, h_ref, x_ref, gate_ref, w1_ref, w3_ref, w2_ref, *rest):
    cast_in, o_ref, cast_out = rest[:n_cast], rest[n_cast], rest[n_cast + 1:]
    o_ref[...] = x_ref[...] + gate_ref[...] * _swiglu_acc(h_ref[...], w1_ref, w3_ref, w2_ref)
    for src, dst in zip(cast_in, cast_out):
        dst[...] = src[...].astype(dst.dtype)


def _ffn(h, x, gate, w1, w3, w2, cast=()):
    bsz, n, d = x.shape
    f = w1.shape[1]
    tm = min(512, n)
    steps = bsz * (n // tm)
    tile = pl.BlockSpec((None, tm, d), lambda b, i: (b, i, 0))
    resident = lambda shape: pl.BlockSpec(shape, lambda b, i: (0, 0), pipeline_mode=pl.Buffered(1))
    slabs = []
    for a in cast:
        rows, cols = a.shape
        assert rows % (steps * 2 * SUBLANES) == 0
        slabs.append(pl.BlockSpec((rows // steps, cols), lambda b, i: (b * (n // tm) + i, 0)))
    out = pl.pallas_call(
        functools.partial(_ffn_kernel, len(cast)),
        grid=(bsz, n // tm),
        in_specs=[
            tile, tile,
            pl.BlockSpec((None, 1, d), lambda b, i: (b, 0, 0)),
            resident((d, f)), resident((d, f)), resident((f, d)),
        ] + slabs,
        out_specs=[tile] + slabs,
        out_shape=[jax.ShapeDtypeStruct((bsz, n, d), F32)]
        + [jax.ShapeDtypeStruct(a.shape, BF16) for a in cast],
        compiler_params=_cparams(("parallel", "parallel")),
        name="ffn",
    )(h, x, gate, w1, w3, w2, *cast)
    return out[0], tuple(out[1:])


MOE_BLK = 256


def _dispatch_kernel(d0_ref, d1_ref, zr_ref, hx_ref, hs_ref, zbuf, sem, zsem):
    tm = hx_ref.shape[0]
    base = pl.program_id(0) * tm

    @pl.when(pl.program_id(0) == 0)
    def _():
        zbuf[...] = jnp.zeros_like(zbuf)
        for z in range(zr_ref.shape[0]):
            @pl.when(zr_ref[z] >= 0)
            def _():
                pltpu.make_async_copy(zbuf, hs_ref.at[pl.ds(zr_ref[z], MOE_BLK)], zsem).start()
        for z in range(zr_ref.shape[0]):
            @pl.when(zr_ref[z] >= 0)
            def _():
                pltpu.make_async_copy(zbuf, hs_ref.at[pl.ds(zr_ref[z], MOE_BLK)], zsem).wait()

    def issue(t, carry):
        pltpu.make_async_copy(hx_ref.at[t], hs_ref.at[d0_ref[base + t]], sem).start(priority=0)
        pltpu.make_async_copy(hx_ref.at[t], hs_ref.at[d1_ref[base + t]], sem).start(priority=1)
        return carry

    lax.fori_loop(0, tm, issue, 0, unroll=8)
    for _ in range(2):
        pltpu.make_async_copy(hx_ref, hs_ref.at[pl.ds(0, tm)], sem).wait()


def _dispatch(dest0, dest1, zero_rows, hx, n_rows):
    n_tok, s, lanes = hx.shape
    tm = 512
    grid_spec = pltpu.PrefetchScalarGridSpec(
        num_scalar_prefetch=3,
        grid=(n_tok // tm,),
        in_specs=[pl.BlockSpec((tm, s, lanes), lambda i, d0, d1, zr: (i, 0, 0))],
        out_specs=pl.BlockSpec(memory_space=pl.ANY),
        scratch_shapes=[pltpu.VMEM((MOE_BLK, s, lanes), F32),
                        pltpu.SemaphoreType.DMA(()), pltpu.SemaphoreType.DMA(())],
    )
    return pl.pallas_call(
        _dispatch_kernel,
        grid_spec=grid_spec,
        out_shape=jax.ShapeDtypeStruct((n_rows, s, lanes), F32),
        compiler_params=_cparams(("arbitrary",)),
        name="moe_dispatch",
    )(dest0, dest1, zero_rows, hx)


def _experts_kernel(be_ref, nu_ref, hs_ref, w1_ref, w3_ref, w2_ref, y_ref):
    j = pl.program_id(0)
    d = w1_ref.shape[0]
    ns = d // LANES
    blk = hs_ref.shape[0] // ns

    @pl.when(j < nu_ref[0])
    def _():
        h = jnp.concatenate(
            [hs_ref[pl.ds(s, blk, stride=ns), :].astype(BF16) for s in range(ns)], axis=1)
        out = _swiglu_acc(h, w1_ref, w3_ref, w2_ref)
        for s in range(ns):
            y_ref[pl.ds(s, blk, stride=ns), :] = out[:, s * LANES:(s + 1) * LANES]

    @pl.when(j >= nu_ref[0])
    def _():
        y_ref[...] = jnp.zeros_like(y_ref)


def _experts(block_expert, n_used, hs, w1, w3, w2):
    ne, d, f = w1.shape
    s = d // LANES
    n_rows = hs.shape[0] // s
    n_blocks = n_rows // MOE_BLK

    def expert(shape):
        return pl.BlockSpec((None,) + shape, lambda j, be, nu: (be[j], 0, 0),
                            pipeline_mode=pl.Buffered(1))

    grid_spec = pltpu.PrefetchScalarGridSpec(
        num_scalar_prefetch=2,
        grid=(n_blocks,),
        in_specs=[
            pl.BlockSpec((MOE_BLK * s, LANES), lambda j, be, nu: (jnp.minimum(j, nu[0] - 1), 0)),
            expert((d, f)), expert((d, f)), expert((f, d)),
        ],
        out_specs=pl.BlockSpec((MOE_BLK * s, LANES), lambda j, be, nu: (j, 0)),
    )
    return pl.pallas_call(
        _experts_kernel,
        grid_spec=grid_spec,
        out_shape=jax.ShapeDtypeStruct((n_rows * s, LANES), F32),
        compiler_params=_cparams(("arbitrary",)),
        name="moe_experts",
    )(block_expert, n_used, hs, w1, w3, w2)


def _combine_kernel(d0_ref, d1_ref, x_ref, gate_ref, rt_ref, gfin_ref, y_ref, o_ref, ybuf, sem):
    tm, d = x_ref.shape
    ns = d // LANES
    step = pl.program_id(0) * pl.num_programs(1) + pl.program_id(1)
    nsteps = pl.num_programs(0) * pl.num_programs(1)
    slot = step % 2

    def row(ref, r):
        return ref.at[pl.ds(pl.multiple_of(r * ns, ns), ns)]

    def start_gather(block, into):
        base = block * tm

        def issue(t, carry):
            pltpu.make_async_copy(row(y_ref, d0_ref[base + t]), row(ybuf.at[into, 0], t),
                                  sem.at[into]).start(priority=0)
            pltpu.make_async_copy(row(y_ref, d1_ref[base + t]), row(ybuf.at[into, 1], t),
                                  sem.at[into]).start(priority=1)
            return carry

        lax.fori_loop(0, tm, issue, 0, unroll=8)

    @pl.when(step == 0)
    def _():
        start_gather(0, 0)

    @pl.when(step + 1 < nsteps)
    def _():
        start_gather(step + 1, 1 - slot)

    for k in range(2):
        pltpu.make_async_copy(y_ref.at[pl.ds(0, tm * ns)], ybuf.at[slot, k], sem.at[slot]).wait()

    chunk = 64
    for r0 in range(0, tm, chunk):
        rows = pl.ds(r0, chunk)
        rt = rt_ref[rows, :]
        w0 = rt[:, R_W0:R_W0 + 1]
        w1 = rt[:, R_W1:R_W1 + 1]
        ss = jnp.zeros((chunk, 1), F32)
        for s in range(ns):
            cols = slice(s * LANES, (s + 1) * LANES)
            moe = (w0 * ybuf[slot, 0, pl.ds(r0 * ns + s, chunk, stride=ns), :]
                   + w1 * ybuf[slot, 1, pl.ds(r0 * ns + s, chunk, stride=ns), :])
            xn = x_ref[rows, cols] + gate_ref[:, cols] * moe
            ss = ss + jnp.sum(xn * xn, axis=-1, keepdims=True)
            o_ref[rows, cols] = xn
        o_ref[rows, :] = o_ref[rows, :] * lax.rsqrt(ss / d + EPS) * gfin_ref[...]


def _combine(dest0, dest1, x, gate, route, g_final, y):
    bsz, n, d = x.shape
    s = d // LANES
    tm = 512
    grid_spec = pltpu.PrefetchScalarGridSpec(
        num_scalar_prefetch=2,
        grid=(bsz, n // tm),
        in_specs=[
            pl.BlockSpec((None, tm, d), lambda b, i, d0, d1: (b, i, 0)),
            pl.BlockSpec((None, 1, d), lambda b, i, d0, d1: (b, 0, 0)),
            pl.BlockSpec((None, tm, LANES), lambda b, i, d0, d1: (b, i, 0)),
            pl.BlockSpec((1, d), lambda b, i, d0, d1: (0, 0)),
            pl.BlockSpec(memory_space=pl.ANY),
        ],
        out_specs=pl.BlockSpec((None, tm, d), lambda b, i, d0, d1: (b, i, 0)),
        scratch_shapes=[pltpu.VMEM((2, 2, tm * s, LANES), F32), pltpu.SemaphoreType.DMA((2,))],
    )
    return pl.pallas_call(
        _combine_kernel,
        grid_spec=grid_spec,
        out_shape=jax.ShapeDtypeStruct((bsz, n, d), F32),
        compiler_params=_cparams(("arbitrary", "arbitrary")),
        name="moe_combine",
    )(dest0, dest1, x, gate, route, g_final, y)


def _moe(hx, x, gate, route, route_t, counts, g_final, w1, w3, w2):
    bsz, n, d = x.shape
    ne = w1.shape[0]
    n_tok = bsz * n
    n_rows = -(-(2 * n_tok) // MOE_BLK) * MOE_BLK + ne * MOE_BLK
    n_blocks = n_rows // MOE_BLK
    cnt = counts[0, :ne].astype(jnp.int32)
    padded = (cnt + MOE_BLK - 1) // MOE_BLK * MOE_BLK
    pad_end = jnp.cumsum(padded)
    pad_start = pad_end - padded
    n_used = pad_end[-1] // MOE_BLK
    blk = jnp.minimum(jnp.arange(n_blocks, dtype=jnp.int32), n_used - 1)
    block_expert = jnp.minimum(
        jnp.sum((blk[:, None] * MOE_BLK >= pad_end[None, :]).astype(jnp.int32), axis=1), ne - 1)
    def dest(e_row, rank_row):
        e = route_t[e_row].astype(jnp.int32)
        start = sum(jnp.where(e == k, pad_start[k], 0) for k in range(ne))
        return start + route_t[rank_row].astype(jnp.int32)

    dest0 = dest(R_E0, R_RANK0)
    dest1 = dest(R_E1, R_RANK1)

    last_blk = jnp.where(padded > 0, pad_end - MOE_BLK, -1)
    trail = n_used + jnp.arange(ne, dtype=jnp.int32)
    trail = jnp.where(trail < n_blocks, trail * MOE_BLK, -1)
    zero_rows = jnp.concatenate([last_blk, trail]).astype(jnp.int32)

    ns = d // LANES
    hs = _dispatch(dest0, dest1, zero_rows, hx.reshape(n_tok, ns, LANES), n_rows)
    y = _experts(block_expert, n_used.reshape(1).astype(jnp.int32),
                 hs.reshape(n_rows * ns, LANES), w1, w3, w2)
    return _combine(dest0, dest1, x, gate, route, g_final, y)


def _rope_tables(n, hd):
    rows = n // GRID_W
    n_freq = hd // 4
    inv = ROPE_BASE ** (-jnp.arange(n_freq, dtype=F32) / n_freq)
    row_ang = jnp.arange(rows, dtype=F32)[:, None] * inv
    col_ang = jnp.arange(GRID_W, dtype=F32)[:, None] * inv

    def table(of_row, of_col):
        a = jnp.broadcast_to(of_row[:, None, :], (rows, GRID_W, n_freq))
        b = jnp.broadcast_to(of_col[None, :, :], (rows, GRID_W, n_freq))
        return jnp.concatenate([a, b], axis=-1).reshape(n, 2 * n_freq)

    cos = table(jnp.cos(row_ang), jnp.cos(col_ang))
    sin = table(jnp.sin(row_ang), jnp.sin(col_ang))
    return jnp.concatenate([cos, cos], axis=-1), jnp.concatenate([-sin, sin], axis=-1)


def _block_diag_gates(rg_wa, rg_wx):
    w = jnp.stack([rg_wa, rg_wx], axis=1)
    nd, ng, nk, c, _ = w.shape
    per = RG_CG // c
    w = w.reshape(nd, ng, nk // per, per, c, c)
    eye = jnp.eye(per, dtype=w.dtype)
    full = jnp.einsum('dgmpij,pq->dgmpiqj', w, eye)
    return full.reshape(nd, ng, nk // per, RG_CG, RG_CG).astype(BF16)


def _mixer(p, rope, states, conv_w, conv_b, wg, bg, lam, log_g, rg_w):
    rg_out, h_last = _rglru(p, conv_w, conv_b, wg, bg, lam, states[0])
    ret_out, s_last = _retention(p, log_g, rope[0], rope[1], states[1], 2 * rg_w)
    return rg_out, ret_out, (h_last, s_last)


def kernel(x, c, ctx, c_ctx, w_mod, b_mod, g_mix, g_ffn, g_final, w_in, w_out, conv_w, conv_b,
           rg_wa, rg_ba, rg_wx, rg_bx, rg_lam, ret_decay, ffn_w1, ffn_w3, ffn_w2,
           moe_router, moe_router_b, moe_w1, moe_w3, moe_w2):
    bsz, n_lat, d = x.shape
    n_ctx = ctx.shape[1]
    depth = w_mod.shape[0]
    rg_w = rg_lam.shape[2]
    nh = ret_decay.shape[2]
    hd = (w_out.shape[1] - rg_w) // nh
    ne = moe_router.shape[2]
    assert depth == 2, "kernel is written for the two-layer block (dense FFN, then MoE)"

    rows = -(-(bsz + 1) // SUBLANES) * SUBLANES
    c_all = jnp.zeros((rows, d), F32).at[:bsz].set(c).at[bsz].set(c_ctx)
    mod = _modulation(c_all, w_mod, b_mod)
    mod = mod.reshape(depth, rows, N_MOD, d)
    mod_lat = mod[:, :bsz, None]
    mod_ctx = jnp.broadcast_to(mod[:, bsz:bsz + 1, None], mod_lat.shape)

    rope_lat = _rope_tables(n_lat, hd)
    rope_ctx = (jnp.ones((n_ctx, hd), F32), jnp.zeros((n_ctx, hd), F32))
    log_g = jax.nn.log_sigmoid(ret_decay.astype(F32))
    zero_states = (jnp.zeros((bsz, 2, rg_w), F32), jnp.zeros((bsz, 2, nh, hd, hd), F32))

    for l in range(depth):
        last = l == depth - 1
        ml = [mod_lat[l, :, :, j] for j in range(N_MOD)]
        mc = [mod_ctx[l, :, :, j] for j in range(N_MOD)]
        g_m = g_mix[l].reshape(1, d)
        g_f = g_ffn[l].reshape(1, d)
        w_in_l = w_in[l].astype(BF16)
        w_out_l = w_out[l].astype(BF16)
        wg = _block_diag_gates(0.5 * rg_wa[l], 0.5 * rg_wx[l])
        bg = 0.5 * jnp.stack([rg_ba[l], rg_bx[l]], axis=1)
        mix_p = (conv_w[l], conv_b[l], wg, bg, rg_lam[l], log_g[l], rg_w)

        pc = _in_proj(ctx, g_m, mc[0], mc[1], w_in_l)
        rg_c, ret_c, ctx_states = _mixer(pc, rope_ctx, zero_states, *mix_p)
        px = _in_proj(x, g_m, ml[0], ml[1], w_in_l)
        rg_x, ret_x, _ = _mixer(px, rope_lat, ctx_states, *mix_p)

        if l % 2 == 0:
            i = l // 2
            w1, w3, w2 = ffn_w1[i].astype(BF16), ffn_w3[i].astype(BF16), ffn_w2[i].astype(BF16)
            x, hx = _out_proj(x, rg_x, ret_x, w_out_l, ml[2], g_f, ml[3], ml[4])
            nxt = moe_w1[i], moe_w3[i], moe_w2[i]
            x, cast = _ffn(hx, x, ml[5], w1, w3, w2,
                           cast=[w.reshape(-1, w.shape[-1]) for w in nxt])
            moe_w = [c.reshape(w.shape) for c, w in zip(cast, nxt)]
            if not last:
                ctx, hc = _out_proj(ctx, rg_c, ret_c, w_out_l, mc[2], g_f, mc[3], mc[4])
                ctx, _ = _ffn(hc, ctx, mc[5], w1, w3, w2)
        else:
            i = l // 2
            wr = jnp.zeros((d, LANES), F32).at[:, :ne].set(moe_router[i])
            wr_hi = wr.astype(BF16)
            wr_lo = (wr - wr_hi.astype(F32)).astype(BF16)
            wr2 = jnp.concatenate([wr_hi, wr_lo], axis=1)
            br = jnp.full((1, LANES), NEG_BIG, F32).at[0, :ne].set(moe_router_b[i])
            x, hx, route, route_t, counts = _out_proj(
                x, rg_x, ret_x, w_out_l, ml[2], g_f, ml[3], ml[4], router=(wr2, br))
            x = _moe(hx, x, ml[5], route, route_t, counts, g_final.reshape(1, d), *moe_w)
    return x
```

```python
import functools

import jax
import jax.numpy as jnp
from jax import lax
from jax.experimental import pallas as pl
from jax.experimental.pallas import tpu as pltpu

F32 = jnp.float32
BF16 = jnp.bfloat16

EPS = 1e-6
RG_C = 8.0
CONV_W = 4
CONV_LEFT = 2
RG_BLOCKS = 8
RET_HEADS = 4
RET_CHUNK = 128
ROPE_BASE = 10000.0
GRID_W = 64
N_EXPERTS = 8
N_MOD = 6

LANES = 128
SUBLANES = 8
VMEM_LIMIT = 56 * 1024 * 1024
NEG_BIG = -1e30
LOG2_E = 1.4426950408889634


def _cparams(sem):
    return pltpu.CompilerParams(dimension_semantics=sem, vmem_limit_bytes=VMEM_LIMIT)


def _modulation_kernel(c_ref, w_ref, b_ref, o_ref):
    c = c_ref[...]
    s = c * jax.nn.sigmoid(c)
    o_ref[...] = jnp.dot(s, w_ref[...], precision=lax.Precision.HIGHEST,
                         preferred_element_type=F32) + b_ref[...]


def _modulation(c_all, w_mod, b_mod):
    depth, d, six_d = w_mod.shape
    rows = c_all.shape[0]
    tn = 1536
    return pl.pallas_call(
        _modulation_kernel,
        grid=(depth, six_d // tn),
        in_specs=[
            pl.BlockSpec((rows, d), lambda l, j: (0, 0)),
            pl.BlockSpec((None, d, tn), lambda l, j: (l, 0, j)),
            pl.BlockSpec((None, 1, tn), lambda l, j: (l, 0, j)),
        ],
        out_specs=pl.BlockSpec((None, rows, tn), lambda l, j: (l, 0, j)),
        out_shape=jax.ShapeDtypeStruct((depth, rows, six_d), F32),
        compiler_params=_cparams(("arbitrary", "arbitrary")),
        name="modulation",
    )(c_all, w_mod, b_mod.reshape(depth, 1, six_d))


def _norm_mod(x, g, shift, scale):
    ms = jnp.mean(x * x, axis=-1, keepdims=True)
    y = x * lax.rsqrt(ms + EPS) * g
    return y * (1.0 + scale) + shift


def _in_proj_kernel(x_ref, g_ref, shift_ref, scale_ref, w_ref, o_ref):
    h = _norm_mod(x_ref[...], g_ref[...], shift_ref[...], scale_ref[...])
    o_ref[...] = jnp.dot(h.astype(BF16), w_ref[...],
                         preferred_element_type=F32).astype(o_ref.dtype)


def _in_proj(x, g, shift, scale, w):
    bsz, n, d = x.shape
    f = w.shape[1]
    tm = min(512, n)
    vec = pl.BlockSpec((None, 1, d), lambda b, i: (b, 0, 0))
    return pl.pallas_call(
        _in_proj_kernel,
        grid=(bsz, n // tm),
        in_specs=[
            pl.BlockSpec((None, tm, d), lambda b, i: (b, i, 0)),
            pl.BlockSpec((1, d), lambda b, i: (0, 0)),
            vec, vec,
            pl.BlockSpec((d, f), lambda b, i: (0, 0)),
        ],
        out_specs=pl.BlockSpec((None, tm, f), lambda b, i: (b, i, 0)),
        out_shape=jax.ShapeDtypeStruct((bsz, n, f), BF16),
        compiler_params=_cparams(("parallel", "parallel")),
        name="in_proj",
    )(x, g, shift, scale, w)


RG_TILE = 256
RG_CG = 256
RG_SEG = RG_TILE // SUBLANES
RG_PITCH = RG_SEG + SUBLANES


def _rglru_kernel(u_ref, yg_ref, cw_ref, cb_ref, wg_ref, bg_ref, lam_ref, h0_ref,
                  o_ref, hl_ref, u_scr, hf_scr, nat_scr):
    n = u_ref.shape[0]
    tt = RG_TILE
    nt = n // tt
    nseg = n // RG_SEG
    cg = u_ref.shape[1]
    lane_cols = [slice(g * LANES, (g + 1) * LANES) for g in range(cg // LANES)]

    zeros = jnp.zeros((RG_PITCH, LANES), F32)
    for g, cols in enumerate(lane_cols):
        u_scr[g, pl.ds(0, RG_PITCH), :] = zeros
        u_scr[g, pl.ds(nseg * RG_PITCH + RG_SEG, SUBLANES), :] = zeros[0:SUBLANES]

    def fill(s, carry):
        src = pl.multiple_of(s * RG_SEG, RG_SEG)
        dst = pl.multiple_of((s + 1) * RG_PITCH, SUBLANES)
        for g, cols in enumerate(lane_cols):
            rows = u_ref[pl.ds(src, RG_SEG), cols].astype(F32)
            u_scr[g, pl.ds(dst, RG_SEG), :] = rows
            u_scr[g, pl.ds(dst - SUBLANES, SUBLANES), :] = rows[0:SUBLANES]
        return carry

    lax.fori_loop(0, nseg, fill, 0)

    seg = RG_SEG
    cw_rows = [[jnp.broadcast_to(cw_ref[k:k + 1, c], (SUBLANES, LANES)) for c in lane_cols]
               for k in range(CONV_W)]
    cb_rows = [jnp.broadcast_to(cb_ref[:, c], (SUBLANES, LANES)) for c in lane_cols]

    def conv_tile(i):
        base = pl.multiple_of((i * SUBLANES + 1) * RG_PITCH, SUBLANES)
        blocks = []
        for j in range(seg):
            lane_groups = []
            for g in range(cg // LANES):
                acc = cb_rows[g]
                for k in range(CONV_W):
                    q = j + k - CONV_LEFT
                    start = base + q if q >= 0 else base - RG_PITCH + seg + q
                    acc = acc + cw_rows[k][g] * u_scr[g, pl.ds(start, SUBLANES, stride=RG_PITCH), :]
                lane_groups.append(acc)
            blocks.append(jnp.concatenate(lane_groups, axis=1))
        return jnp.concatenate(blocks, axis=0)

    def gates(uc, d):
        ub = uc.astype(BF16)
        ta = jnp.tanh(jnp.dot(ub, wg_ref[d, 0], preferred_element_type=F32) + bg_ref[d, 0:1, :])
        ti = jnp.tanh(jnp.dot(ub, wg_ref[d, 1], preferred_element_type=F32) + bg_ref[d, 1:2, :])
        z = -lam_ref[d:d + 1, :]
        sp = jnp.maximum(z, 0.0) + jnp.log1p(jnp.exp(-jnp.abs(z)))
        c2 = (-0.5 * RG_C * LOG2_E) * sp
        a = jnp.exp2(c2 + c2 * ta)
        b = jnp.exp2((0.5 * LOG2_E) * jnp.log(1.0 - a * a)) * ((0.5 + 0.5 * ti) * uc)
        return a, b

    def scan_tile(a, b, carry, reverse):
        steps = range(seg - 1, -1, -1) if reverse else range(seg)
        h = acum = None
        h_loc = [None] * seg
        a_cum = [None] * seg
        for j in steps:
            rows = slice(j * SUBLANES, (j + 1) * SUBLANES)
            h = b[rows] if h is None else a[rows] * h + b[rows]
            acum = a[rows] if acum is None else a[rows] * acum
            h_loc[j], a_cum[j] = h, acum
        order = range(SUBLANES - 1, -1, -1) if reverse else range(SUBLANES)
        c = carry
        c_in = [None] * SUBLANES
        for s in order:
            c_in[s] = c
            c = h[s:s + 1, :] + acum[s:s + 1, :] * c
        c_in = jnp.concatenate(c_in, axis=0)
        out = jnp.concatenate([h_loc[j] + a_cum[j] * c_in for j in range(seg)], axis=0)
        return out, c

    def fwd_body(i, carry):
        t0 = pl.multiple_of(i * tt, tt)
        uc = conv_tile(i)
        a, b = gates(uc, 0)
        h, carry = scan_tile(a, b, carry, False)
        hf_scr[pl.ds(t0, tt), :] = h
        return carry

    hf_last = lax.fori_loop(0, nt, fwd_body, h0_ref[0:1, :], unroll=min(2, nt))

    def bwd_body(i, carry):
        tile = nt - 1 - i
        t0 = pl.multiple_of(tile * tt, tt)
        uc = conv_tile(tile)
        a, b = gates(uc, 1)
        h, carry = scan_tile(a, b, carry, True)
        hsum = hf_scr[pl.ds(t0, tt), :] + h
        for j in range(seg):
            for g, cols in enumerate(lane_cols):
                nat_scr[g, pl.ds(j, SUBLANES, stride=RG_PITCH), :] = (
                    hsum[j * SUBLANES:(j + 1) * SUBLANES, cols])
        hnat = jnp.concatenate(
            [jnp.concatenate([nat_scr[g, pl.ds(s * RG_PITCH, seg), :] for s in range(SUBLANES)],
                             axis=0) for g in range(len(lane_cols))], axis=1)
        yg = yg_ref[pl.ds(t0, tt), :].astype(F32)
        o_ref[pl.ds(t0, tt), :] = (jax.nn.gelu(yg) * hnat).astype(o_ref.dtype)
        return carry

    hb_last = lax.fori_loop(0, nt, bwd_body, h0_ref[1:2, :], unroll=min(2, nt))
    hl_ref[0:1, :] = hf_last
    hl_ref[1:2, :] = hb_last


def _rglru(p, conv_w, conv_b, wg, bg, lam, h0):
    bsz, n, _ = p.shape
    rg_w = lam.shape[1]
    ncg = rg_w // RG_CG
    assert n % RG_TILE == 0 and rg_w % RG_CG == 0
    return pl.pallas_call(
        _rglru_kernel,
        grid=(bsz, ncg),
        in_specs=[
            pl.BlockSpec((None, n, RG_CG), lambda b, c: (b, 0, c)),
            pl.BlockSpec((None, n, RG_CG), lambda b, c: (b, 0, ncg + c)),
            pl.BlockSpec((CONV_W, RG_CG), lambda b, c: (0, c)),
            pl.BlockSpec((1, RG_CG), lambda b, c: (0, c)),
            pl.BlockSpec((2, 2, None, RG_CG, RG_CG), lambda b, c: (0, 0, c, 0, 0)),
            pl.BlockSpec((2, 2, RG_CG), lambda b, c: (0, 0, c)),
            pl.BlockSpec((2, RG_CG), lambda b, c: (0, c)),
            pl.BlockSpec((None, 2, RG_CG), lambda b, c: (b, 0, c)),
        ],
        out_specs=[
            pl.BlockSpec((None, n, RG_CG), lambda b, c: (b, 0, c)),
            pl.BlockSpec((None, 2, RG_CG), lambda b, c: (b, 0, c)),
        ],
        out_shape=[
            jax.ShapeDtypeStruct((bsz, n, rg_w), BF16),
            jax.ShapeDtypeStruct((bsz, 2, rg_w), F32),
        ],
        scratch_shapes=[
            pltpu.VMEM((RG_CG // LANES, (n // RG_SEG + 1) * RG_PITCH, LANES), F32),
            pltpu.VMEM((n, RG_CG), F32),
            pltpu.VMEM((RG_CG // LANES, SUBLANES * RG_PITCH, LANES), F32),
        ],
        compiler_params=_cparams(("parallel", "parallel")),
        name="rglru",
    )(p, p, conv_w, conv_b.reshape(1, rg_w), wg, bg, lam, h0)


RET_UNROLL = 8


def _retention_kernel(lg_ref, q_ref, k_ref, v_ref, g_ref, cs_ref, sn_ref, s0_ref,
                      o_ref, so_ref, kr_scr, st_scr):
    n, hd = q_ref.shape
    L = RET_CHUNK
    nc = n // L
    head = pl.program_id(1)
    lgf = lg_ref[0, head]
    lgb = lg_ref[1, head]

    def rope(t, rows):
        return t * cs_ref[rows, :] + pltpu.roll(t, hd // 2, axis=1) * sn_ref[rows, :]

    kr_scr[...] = (rope(k_ref[...].astype(F32), slice(None)) * (hd ** -0.5)).astype(BF16)

    jc = lax.broadcasted_iota(jnp.int32, (L, 1), 0).astype(F32)
    q_dec_f = jnp.exp((jc + 1.0) * lgf)
    q_dec_b = jnp.exp((L - jc) * lgb)
    k_dec_f = jnp.exp((L - 1.0 - jc) * lgf)
    k_dec_b = jnp.exp(jc * lgb)
    ones = jnp.ones((1, hd), F32)
    chunk_f = jnp.exp(ones * (L * lgf))
    chunk_b = jnp.exp(ones * (L * lgb))
    ii = lax.broadcasted_iota(jnp.int32, (L, L), 0)
    jj = lax.broadcasted_iota(jnp.int32, (L, L), 1)
    diff = (ii - jj).astype(F32)
    dmat = jnp.where(diff >= 0.0, jnp.exp(jnp.maximum(diff, 0.0) * lgf),
                     jnp.exp(jnp.maximum(-diff, 0.0) * lgb))

    tn_dims = (((0,), (0,)), ((), ()))
    nt_dims = (((1,), (1,)), ((), ()))

    def state_update(c, state, k_dec, chunk_dec):
        t0 = pl.multiple_of(c * L, L)
        kd = (kr_scr[pl.ds(t0, L), :].astype(F32) * k_dec).astype(BF16)
        upd = lax.dot_general(kd, v_ref[pl.ds(t0, L), :], tn_dims, preferred_element_type=F32)
        return chunk_dec * state + upd

    def state_body(i, carry):
        f_state, r_state = carry
        cf = i
        cb = nc - 1 - i
        st_scr[cf, :, 0:hd] = f_state.astype(BF16)
        st_scr[cb, :, hd:2 * hd] = r_state.astype(BF16)
        return (state_update(cf, f_state, k_dec_f, chunk_f),
                state_update(cb, r_state, k_dec_b, chunk_b))

    f_fin, r_fin = lax.fori_loop(0, nc, state_body, (s0_ref[0], s0_ref[1]),
                                 unroll=min(RET_UNROLL, nc))
    so_ref[0] = f_fin
    so_ref[1] = r_fin

    def out_body(c, carry):
        t0 = pl.multiple_of(c * L, L)
        qc = rope(q_ref[pl.ds(t0, L), :].astype(F32), pl.ds(t0, L)).astype(BF16)
        kc = kr_scr[pl.ds(t0, L), :]
        vc = v_ref[pl.ds(t0, L), :]
        s = lax.dot_general(qc, kc, nt_dims, preferred_element_type=F32)
        o = jnp.dot((s * dmat).astype(BF16), vc, preferred_element_type=F32)
        cross = jnp.dot(qc, st_scr[c], preferred_element_type=F32)
        o = o + cross[:, 0:hd] * q_dec_f + cross[:, hd:2 * hd] * q_dec_b
        mu = jnp.mean(o, axis=-1, keepdims=True)
        oc = o - mu
        var = jnp.mean(oc * oc, axis=-1, keepdims=True)
        gate = g_ref[pl.ds(t0, L), :].astype(F32)
        o_ref[pl.ds(t0, L), :] = (gate * jax.nn.sigmoid(gate) * (oc * lax.rsqrt(var + EPS))).astype(o_ref.dtype)
        return carry

    lax.fori_loop(0, nc, out_body, 0, unroll=min(RET_UNROLL, nc))


def _retention(p, log_g, cs, sn, s0, col0):
    bsz, n, _ = p.shape
    nh = log_g.shape[1]
    hd = cs.shape[1]
    cb0 = col0 // hd

    def col(which):
        return pl.BlockSpec((None, n, hd), lambda b, h, lg: (b, 0, cb0 + which * nh + h))

    state = pl.BlockSpec((None, 2, None, hd, hd), lambda b, h, lg: (b, 0, h, 0, 0))
    table = pl.BlockSpec((n, hd), lambda b, h, lg: (0, 0))
    grid_spec = pltpu.PrefetchScalarGridSpec(
        num_scalar_prefetch=1,
        grid=(bsz, nh),
        in_specs=[col(0), col(1), col(2), col(3), table, table, state],
        out_specs=[pl.BlockSpec((None, n, hd), lambda b, h, lg: (b, 0, h)), state],
        scratch_shapes=[
            pltpu.VMEM((n, hd), BF16),
            pltpu.VMEM((n // RET_CHUNK, hd, 2 * hd), BF16),
        ],
    )
    return pl.pallas_call(
        _retention_kernel,
        grid_spec=grid_spec,
        out_shape=[
            jax.ShapeDtypeStruct((bsz, n, nh * hd), BF16),
            jax.ShapeDtypeStruct((bsz, 2, nh, hd, hd), F32),
        ],
        compiler_params=_cparams(("parallel", "parallel")),
        name="retention",
    )(log_g, p, p, p, p, cs, sn, s0)


R_RANK0, R_RANK1, R_E0, R_E1, R_W0, R_W1 = range(6)


def _mix_residual(x_ref, rg_ref, ret_ref, wo_ref, gate_ref):
    rg_w = rg_ref.shape[1]
    y = jnp.dot(rg_ref[...], wo_ref[0:rg_w, :], preferred_element_type=F32)
    y = y + jnp.dot(ret_ref[...], wo_ref[rg_w:, :], preferred_element_type=F32)
    return x_ref[...] + gate_ref[...] * y


def _out_proj_kernel(x_ref, rg_ref, ret_ref, wo_ref, gate_ref, g_ref, shift_ref, scale_ref,
                     wr2_ref, br_ref, xo_ref, h_ref, rt_ref, rtt_ref, cnt_ref, carry_scr):
    xn = _mix_residual(x_ref, rg_ref, ret_ref, wo_ref, gate_ref)
    xo_ref[...] = xn
    hx = _norm_mod(xn, g_ref[...], shift_ref[...], scale_ref[...])

    tm, d = hx.shape
    for s in range(d // LANES):
        h_ref[pl.ds(s, tm, stride=d // LANES), :] = hx[:, s * LANES:(s + 1) * LANES]

    h_hi = hx.astype(BF16)
    h_lo = (hx - h_hi.astype(F32)).astype(BF16)
    part = jnp.dot(h_hi, wr2_ref[...], preferred_element_type=F32)
    logits = (part[:, :LANES] + part[:, LANES:]
              + jnp.dot(h_lo, wr2_ref[:, :LANES], preferred_element_type=F32) + br_ref[...])
    lane_i = lax.broadcasted_iota(jnp.int32, logits.shape, 1)
    lane = lane_i.astype(F32)
    m0 = jnp.max(logits, axis=-1, keepdims=True)
    i0 = jnp.min(jnp.where(logits == m0, lane, float(LANES)), axis=-1, keepdims=True)
    rest_l = jnp.where(lane == i0, NEG_BIG, logits)
    m1 = jnp.max(rest_l, axis=-1, keepdims=True)
    i1 = jnp.min(jnp.where(rest_l == m1, lane, float(LANES)), axis=-1, keepdims=True)
    e = jnp.exp(m1 - m0)
    w0 = 1.0 / (1.0 + e)
    w1 = e / (1.0 + e)

    @pl.when((pl.program_id(0) == 0) & (pl.program_id(1) == 0))
    def _():
        carry_scr[...] = jnp.zeros_like(carry_scr)

    sel0 = lane == i0
    sel1 = lane == i1
    mask = jnp.where(sel0 | sel1, 1.0, 0.0)
    ri = lax.broadcasted_iota(jnp.int32, (tm, tm), 0)
    ci = lax.broadcasted_iota(jnp.int32, (tm, tm), 1)
    tri = jnp.where(ri > ci, 1.0, 0.0).astype(BF16)
    before = jnp.dot(tri, mask.astype(BF16), preferred_element_type=F32) + carry_scr[...]
    rank0 = jnp.sum(jnp.where(sel0, before, 0.0), axis=-1, keepdims=True)
    rank1 = jnp.sum(jnp.where(sel1, before, 0.0), axis=-1, keepdims=True)
    total = carry_scr[...] + jnp.sum(mask, axis=0, keepdims=True)
    carry_scr[...] = total
    cnt_ref[...] = total
    rec = jnp.zeros_like(logits)
    for idx, val in ((R_RANK0, rank0), (R_RANK1, rank1), (R_E0, i0), (R_E1, i1),
                     (R_W0, w0), (R_W1, w1)):
        rec = jnp.where(lane_i == idx, val, rec)
    rt_ref[...] = rec
    rtt_ref[...] = jnp.transpose(rec)[0:rtt_ref.shape[0], :]


def _out_proj_route(x, rg, ret, w_out, gate, g, shift, scale, wr2, br):
    bsz, n, d = x.shape
    tm = min(512, n)
    vec = pl.BlockSpec((None, 1, d), lambda b, i: (b, 0, 0))
    tile = lambda w: pl.BlockSpec((None, tm, w), lambda b, i: (b, i, 0))
    const = lambda shape: pl.BlockSpec(shape, lambda b, i: (0, 0))
    return pl.pallas_call(
        _out_proj_kernel,
        grid=(bsz, n // tm),
        in_specs=[
            tile(d), tile(rg.shape[2]), tile(ret.shape[2]), const(w_out.shape),
            vec, const((1, d)), vec, vec, const((d, 2 * LANES)), const((1, LANES)),
        ],
        out_specs=[
            tile(d),
            pl.BlockSpec((None, tm * (d // LANES), LANES), lambda b, i: (b, i, 0)),
            tile(LANES),
            pl.BlockSpec((SUBLANES, tm), lambda b, i: (0, b * (n // tm) + i)),
            const((1, LANES)),
        ],
        out_shape=[
            jax.ShapeDtypeStruct((bsz, n, d), F32),
            jax.ShapeDtypeStruct((bsz, n * (d // LANES), LANES), F32),
            jax.ShapeDtypeStruct((bsz, n, LANES), F32),
            jax.ShapeDtypeStruct((SUBLANES, bsz * n), F32),
            jax.ShapeDtypeStruct((1, LANES), F32),
        ],
        scratch_shapes=[pltpu.VMEM((1, LANES), F32)],
        compiler_params=_cparams(("arbitrary", "arbitrary")),
        name="out_proj_route",
    )(x, rg, ret, w_out, gate, g, shift, scale, wr2, br)


def _swiglu_acc(h, w1_ref, w3_ref, w2_ref, row_scale=None):
    a = jnp.dot(h, w1_ref[...], preferred_element_type=F32)
    b = jnp.dot(h, w3_ref[...], preferred_element_type=F32)
    act = a * jax.nn.sigmoid(a) * b
    if row_scale is not None:
        act = act * row_scale
    return jnp.dot(act.astype(BF16), w2_ref[...], preferred_element_type=F32)


def _mix_ffn_kernel(n_cast, x_ref, rg_ref, ret_ref, wo_ref, gmix_ref, g_ref, shift_ref, scale_ref,
                    gffn_ref, w1_ref, w3_ref, w2_ref, *rest):
    cast_in, o_ref, cast_out = rest[:n_cast], rest[n_cast], rest[n_cast + 1:]
    xn = _mix_residual(x_ref, rg_ref, ret_ref, wo_ref, gmix_ref)
    hx = _norm_mod(xn, g_ref[...], shift_ref[...], scale_ref[...]).astype(BF16)
    o_ref[...] = xn + gffn_ref[...] * _swiglu_acc(hx, w1_ref, w3_ref, w2_ref)
    for src, dst in zip(cast_in, cast_out):
        dst[...] = src[...].astype(dst.dtype)


def _mix_ffn(x, rg, ret, w_out, gate_mix, g, shift, scale, gate_ffn, w1, w3, w2, cast=()):
    bsz, n, d = x.shape
    f = w1.shape[1]
    tm = min(512, n)
    steps = bsz * (n // tm)
    vec = pl.BlockSpec((None, 1, d), lambda b, i: (b, 0, 0))
    tile = lambda w: pl.BlockSpec((None, tm, w), lambda b, i: (b, i, 0))
    resident = lambda shape: pl.BlockSpec(shape, lambda b, i: (0, 0), pipeline_mode=pl.Buffered(1))
    slabs = []
    for a in cast:
        rows, cols = a.shape
        assert rows % (steps * 2 * SUBLANES) == 0
        slabs.append(pl.BlockSpec((rows // steps, cols), lambda b, i: (b * (n // tm) + i, 0)))
    out = pl.pallas_call(
        functools.partial(_mix_ffn_kernel, len(cast)),
        grid=(bsz, n // tm),
        in_specs=[
            tile(d), tile(rg.shape[2]), tile(ret.shape[2]), resident(w_out.shape),
            vec, resident((1, d)), vec, vec, vec,
            resident((d, f)), resident((d, f)), resident((f, d)),
        ] + slabs,
        out_specs=[tile(d)] + slabs,
        out_shape=[jax.ShapeDtypeStruct((bsz, n, d), F32)]
        + [jax.ShapeDtypeStruct(a.shape, BF16) for a in cast],
        compiler_params=_cparams(("parallel", "parallel")),
        name="mix_ffn",
    )(x, rg, ret, w_out, gate_mix, g, shift, scale, gate_ffn, w1, w3, w2, *cast)
    return out[0], tuple(out[1:])


MOE_BLK = 256


def _dispatch_kernel(d0_ref, d1_ref, zr_ref, hx_ref, hs_ref, zbuf, sem, zsem):
    tm = hx_ref.shape[0]
    base = pl.program_id(0) * tm

    @pl.when(pl.program_id(0) == 0)
    def _():
        zbuf[...] = jnp.zeros_like(zbuf)
        for z in range(zr_ref.shape[0]):
            @pl.when(zr_ref[z] >= 0)
            def _():
                pltpu.make_async_copy(zbuf, hs_ref.at[pl.ds(zr_ref[z], MOE_BLK)], zsem).start()
        for z in range(zr_ref.shape[0]):
            @pl.when(zr_ref[z] >= 0)
            def _():
                pltpu.make_async_copy(zbuf, hs_ref.at[pl.ds(zr_ref[z], MOE_BLK)], zsem).wait()

    def issue(t, carry):
        pltpu.make_async_copy(hx_ref.at[t], hs_ref.at[d0_ref[base + t]], sem).start(priority=0)
        pltpu.make_async_copy(hx_ref.at[t], hs_ref.at[d1_ref[base + t]], sem).start(priority=1)
        return carry

    lax.fori_loop(0, tm, issue, 0, unroll=8)
    for _ in range(2):
        pltpu.make_async_copy(hx_ref, hs_ref.at[pl.ds(0, tm)], sem).wait()


def _dispatch(dest0, dest1, zero_rows, hx, n_rows):
    n_tok, s, lanes = hx.shape
    tm = 512
    grid_spec = pltpu.PrefetchScalarGridSpec(
        num_scalar_prefetch=3,
        grid=(n_tok // tm,),
        in_specs=[pl.BlockSpec((tm, s, lanes), lambda i, d0, d1, zr: (i, 0, 0))],
        out_specs=pl.BlockSpec(memory_space=pl.ANY),
        scratch_shapes=[pltpu.VMEM((MOE_BLK, s, lanes), F32),
                        pltpu.SemaphoreType.DMA(()), pltpu.SemaphoreType.DMA(())],
    )
    return pl.pallas_call(
        _dispatch_kernel,
        grid_spec=grid_spec,
        out_shape=jax.ShapeDtypeStruct((n_rows, s, lanes), F32),
        compiler_params=_cparams(("arbitrary",)),
        name="moe_dispatch",
    )(dest0, dest1, zero_rows, hx)


def _experts_kernel(be_ref, nu_ref, hs_ref, w1_ref, w3_ref, w2_ref, y_ref):
    j = pl.program_id(0)
    d = w1_ref.shape[0]
    ns = d // LANES
    blk = hs_ref.shape[0] // ns

    @pl.when(j < nu_ref[0])
    def _():
        h = jnp.concatenate(
            [hs_ref[pl.ds(s, blk, stride=ns), :].astype(BF16) for s in range(ns)], axis=1)
        out = _swiglu_acc(h, w1_ref, w3_ref, w2_ref)
        for s in range(ns):
            y_ref[pl.ds(s, blk, stride=ns), :] = out[:, s * LANES:(s + 1) * LANES]

    @pl.when(j >= nu_ref[0])
    def _():
        y_ref[...] = jnp.zeros_like(y_ref)


def _experts(block_expert, n_used, hs, w1, w3, w2):
    ne, d, f = w1.shape
    s = d // LANES
    n_rows = hs.shape[0] // s
    n_blocks = n_rows // MOE_BLK

    def expert(shape):
        return pl.BlockSpec((None,) + shape, lambda j, be, nu: (be[j], 0, 0),
                            pipeline_mode=pl.Buffered(1))

    grid_spec = pltpu.PrefetchScalarGridSpec(
        num_scalar_prefetch=2,
        grid=(n_blocks,),
        in_specs=[
            pl.BlockSpec((MOE_BLK * s, LANES), lambda j, be, nu: (jnp.minimum(j, nu[0] - 1), 0)),
            expert((d, f)), expert((d, f)), expert((f, d)),
        ],
        out_specs=pl.BlockSpec((MOE_BLK * s, LANES), lambda j, be, nu: (j, 0)),
    )
    return pl.pallas_call(
        _experts_kernel,
        grid_spec=grid_spec,
        out_shape=jax.ShapeDtypeStruct((n_rows * s, LANES), F32),
        compiler_params=_cparams(("arbitrary",)),
        name="moe_experts",
    )(block_expert, n_used, hs, w1, w3, w2)


def _combine_kernel(d0_ref, d1_ref, x_ref, gate_ref, rt_ref, gfin_ref, y_ref, o_ref, ybuf, sem):
    tm, d = x_ref.shape
    ns = d // LANES
    step = pl.program_id(0) * pl.num_programs(1) + pl.program_id(1)
    nsteps = pl.num_programs(0) * pl.num_programs(1)
    slot = step % 2

    def row(ref, r):
        return ref.at[pl.ds(pl.multiple_of(r * ns, ns), ns)]

    def start_gather(block, into):
        base = block * tm

        def issue(t, carry):
            pltpu.make_async_copy(row(y_ref, d0_ref[base + t]), row(ybuf.at[into, 0], t),
                                  sem.at[into]).start(priority=0)
            pltpu.make_async_copy(row(y_ref, d1_ref[base + t]), row(ybuf.at[into, 1], t),
                                  sem.at[into]).start(priority=1)
            return carry

        lax.fori_loop(0, tm, issue, 0, unroll=8)

    @pl.when(step == 0)
    def _():
        start_gather(0, 0)

    @pl.when(step + 1 < nsteps)
    def _():
        start_gather(step + 1, 1 - slot)

    for k in range(2):
        pltpu.make_async_copy(y_ref.at[pl.ds(0, tm * ns)], ybuf.at[slot, k], sem.at[slot]).wait()

    chunk = 64
    for r0 in range(0, tm, chunk):
        rows = pl.ds(r0, chunk)
        rt = rt_ref[rows, :]
        w0 = rt[:, R_W0:R_W0 + 1]
        w1 = rt[:, R_W1:R_W1 + 1]
        ss = jnp.zeros((chunk, 1), F32)
        for s in range(ns):
            cols = slice(s * LANES, (s + 1) * LANES)
            moe = (w0 * ybuf[slot, 0, pl.ds(r0 * ns + s, chunk, stride=ns), :]
                   + w1 * ybuf[slot, 1, pl.ds(r0 * ns + s, chunk, stride=ns), :])
            xn = x_ref[rows, cols] + gate_ref[:, cols] * moe
            ss = ss + jnp.sum(xn * xn, axis=-1, keepdims=True)
            o_ref[rows, cols] = xn
        o_ref[rows, :] = o_ref[rows, :] * lax.rsqrt(ss / d + EPS) * gfin_ref[...]


def _combine(dest0, dest1, x, gate, route, g_final, y):
    bsz, n, d = x.shape
    s = d // LANES
    tm = 512
    grid_spec = pltpu.PrefetchScalarGridSpec(
        num_scalar_prefetch=2,
        grid=(bsz, n // tm),
        in_specs=[
            pl.BlockSpec((None, tm, d), lambda b, i, d0, d1: (b, i, 0)),
            pl.BlockSpec((None, 1, d), lambda b, i, d0, d1: (b, 0, 0)),
            pl.BlockSpec((None, tm, LANES), lambda b, i, d0, d1: (b, i, 0)),
            pl.BlockSpec((1, d), lambda b, i, d0, d1: (0, 0)),
            pl.BlockSpec(memory_space=pl.ANY),
        ],
        out_specs=pl.BlockSpec((None, tm, d), lambda b, i, d0, d1: (b, i, 0)),
        scratch_shapes=[pltpu.VMEM((2, 2, tm * s, LANES), F32), pltpu.SemaphoreType.DMA((2,))],
    )
    return pl.pallas_call(
        _combine_kernel,
        grid_spec=grid_spec,
        out_shape=jax.ShapeDtypeStruct((bsz, n, d), F32),
        compiler_params=_cparams(("arbitrary", "arbitrary")),
        name="moe_combine",
    )(dest0, dest1, x, gate, route, g_final, y)


def _moe(hx, x, gate, route, route_t, counts, g_final, w1, w3, w2):
    bsz, n, d = x.shape
    ne = w1.shape[0]
    n_tok = bsz * n
    n_rows = -(-(2 * n_tok) // MOE_BLK) * MOE_BLK + ne * MOE_BLK
    n_blocks = n_rows // MOE_BLK
    cnt = counts[0, :ne].astype(jnp.int32)
    padded = (cnt + MOE_BLK - 1) // MOE_BLK * MOE_BLK
    pad_end = jnp.cumsum(padded)
    pad_start = pad_end - padded
    n_used = pad_end[-1] // MOE_BLK
    blk = jnp.minimum(jnp.arange(n_blocks, dtype=jnp.int32), n_used - 1)
    block_expert = jnp.minimum(
        jnp.sum((blk[:, None] * MOE_BLK >= pad_end[None, :]).astype(jnp.int32), axis=1), ne - 1)
    def dest(e_row, rank_row):
        e = route_t[e_row].astype(jnp.int32)
        start = sum(jnp.where(e == k, pad_start[k], 0) for k in range(ne))
        return start + route_t[rank_row].astype(jnp.int32)

    dest0 = dest(R_E0, R_RANK0)
    dest1 = dest(R_E1, R_RANK1)

    last_blk = jnp.where(padded > 0, pad_end - MOE_BLK, -1)
    trail = n_used + jnp.arange(ne, dtype=jnp.int32)
    trail = jnp.where(trail < n_blocks, trail * MOE_BLK, -1)
    zero_rows = jnp.concatenate([last_blk, trail]).astype(jnp.int32)

    ns = d // LANES
    hs = _dispatch(dest0, dest1, zero_rows, hx.reshape(n_tok, ns, LANES), n_rows)
    y = _experts(block_expert, n_used.reshape(1).astype(jnp.int32),
                 hs.reshape(n_rows * ns, LANES), w1, w3, w2)
    return _combine(dest0, dest1, x, gate, route, g_final, y)


def _rope_tables(n, hd):
    rows = n // GRID_W
    n_freq = hd // 4
    inv = ROPE_BASE ** (-jnp.arange(n_freq, dtype=F32) / n_freq)
    row_ang = jnp.arange(rows, dtype=F32)[:, None] * inv
    col_ang = jnp.arange(GRID_W, dtype=F32)[:, None] * inv

    def table(of_row, of_col):
        a = jnp.broadcast_to(of_row[:, None, :], (rows, GRID_W, n_freq))
        b = jnp.broadcast_to(of_col[None, :, :], (rows, GRID_W, n_freq))
        return jnp.concatenate([a, b], axis=-1).reshape(n, 2 * n_freq)

    cos = table(jnp.cos(row_ang), jnp.cos(col_ang))
    sin = table(jnp.sin(row_ang), jnp.sin(col_ang))
    return jnp.concatenate([cos, cos], axis=-1), jnp.concatenate([-sin, sin], axis=-1)


def _block_diag_gates(rg_wa, rg_wx):
    w = jnp.stack([rg_wa, rg_wx], axis=1)
    nd, ng, nk, c, _ = w.shape
    per = RG_CG // c
    w = w.reshape(nd, ng, nk // per, per, c, c)
    eye = jnp.eye(per, dtype=w.dtype)
    full = jnp.einsum('dgmpij,pq->dgmpiqj', w, eye)
    return full.reshape(nd, ng, nk // per, RG_CG, RG_CG).astype(BF16)


def _mixer(p, rope, states, conv_w, conv_b, wg, bg, lam, log_g, rg_w):
    rg_out, h_last = _rglru(p, conv_w, conv_b, wg, bg, lam, states[0])
    ret_out, s_last = _retention(p, log_g, rope[0], rope[1], states[1], 2 * rg_w)
    return rg_out, ret_out, (h_last, s_last)


def kernel(x, c, ctx, c_ctx, w_mod, b_mod, g_mix, g_ffn, g_final, w_in, w_out, conv_w, conv_b,
           rg_wa, rg_ba, rg_wx, rg_bx, rg_lam, ret_decay, ffn_w1, ffn_w3, ffn_w2,
           moe_router, moe_router_b, moe_w1, moe_w3, moe_w2):
    bsz, n_lat, d = x.shape
    n_ctx = ctx.shape[1]
    depth = w_mod.shape[0]
    rg_w = rg_lam.shape[2]
    nh = ret_decay.shape[2]
    hd = (w_out.shape[1] - rg_w) // nh
    ne = moe_router.shape[2]
    assert depth == 2, "kernel is written for the two-layer block (dense FFN, then MoE)"

    rows = -(-(bsz + 1) // SUBLANES) * SUBLANES
    c_all = jnp.zeros((rows, d), F32).at[:bsz].set(c).at[bsz].set(c_ctx)
    mod = _modulation(c_all, w_mod, b_mod)
    mod = mod.reshape(depth, rows, N_MOD, d)
    mod_lat = mod[:, :bsz, None]
    mod_ctx = jnp.broadcast_to(mod[:, bsz:bsz + 1, None], mod_lat.shape)

    rope_lat = _rope_tables(n_lat, hd)
    rope_ctx = (jnp.ones((n_ctx, hd), F32), jnp.zeros((n_ctx, hd), F32))
    log_g = jax.nn.log_sigmoid(ret_decay.astype(F32))
    zero_states = (jnp.zeros((bsz, 2, rg_w), F32), jnp.zeros((bsz, 2, nh, hd, hd), F32))

    for l in range(depth):
        last = l == depth - 1
        ml = [mod_lat[l, :, :, j] for j in range(N_MOD)]
        mc = [mod_ctx[l, :, :, j] for j in range(N_MOD)]
        g_m = g_mix[l].reshape(1, d)
        g_f = g_ffn[l].reshape(1, d)
        w_in_l = w_in[l].astype(BF16)
        w_out_l = w_out[l].astype(BF16)
        wg = _block_diag_gates(0.5 * rg_wa[l], 0.5 * rg_wx[l])
        bg = 0.5 * jnp.stack([rg_ba[l], rg_bx[l]], axis=1)
        mix_p = (conv_w[l], conv_b[l], wg, bg, rg_lam[l], log_g[l], rg_w)

        pc = _in_proj(ctx, g_m, mc[0], mc[1], w_in_l)
        rg_c, ret_c, ctx_states = _mixer(pc, rope_ctx, zero_states, *mix_p)
        px = _in_proj(x, g_m, ml[0], ml[1], w_in_l)
        rg_x, ret_x, _ = _mixer(px, rope_lat, ctx_states, *mix_p)

        if l % 2 == 0:
            i = l // 2
            w1, w3, w2 = ffn_w1[i].astype(BF16), ffn_w3[i].astype(BF16), ffn_w2[i].astype(BF16)
            nxt = moe_w1[i], moe_w3[i], moe_w2[i]
            x, cast = _mix_ffn(x, rg_x, ret_x, w_out_l, ml[2], g_f, ml[3], ml[4], ml[5],
                               w1, w3, w2, cast=[w.reshape(-1, w.shape[-1]) for w in nxt])
            moe_w = [c.reshape(w.shape) for c, w in zip(cast, nxt)]
            if not last:
                ctx, _ = _mix_ffn(ctx, rg_c, ret_c, w_out_l, mc[2], g_f, mc[3], mc[4], mc[5],
                                  w1, w3, w2)
        else:
            i = l // 2
            wr = jnp.zeros((d, LANES), F32).at[:, :ne].set(moe_router[i])
            wr_hi = wr.astype(BF16)
            wr_lo = (wr - wr_hi.astype(F32)).astype(BF16)
            wr2 = jnp.concatenate([wr_hi, wr_lo], axis=1)
            br = jnp.full((1, LANES), NEG_BIG, F32).at[0, :ne].set(moe_router_b[i])
            x, hx, route, route_t, counts = _out_proj_route(
                x, rg_x, ret_x, w_out_l, ml[2], g_f, ml[3], ml[4], wr2, br)
            x = _moe(hx, x, ml[5], route, route_t, counts, g_final.reshape(1, d), *moe_w)
    return x
```

```python
import functools

import jax
import jax.numpy as jnp
from jax import lax
from jax.experimental import pallas as pl
from jax.experimental.pallas import tpu as pltpu

F32 = jnp.float32
BF16 = jnp.bfloat16

EPS = 1e-6
RG_C = 8.0
CONV_W = 4
CONV_LEFT = 2
RG_BLOCKS = 8
RET_HEADS = 4
RET_CHUNK = 256
ROPE_BASE = 10000.0
GRID_W = 64
N_EXPERTS = 8
N_MOD = 6

LANES = 128
SUBLANES = 8
VMEM_LIMIT = 56 * 1024 * 1024
NEG_BIG = -1e30
LOG2_E = 1.4426950408889634


def _cparams(sem):
    return pltpu.CompilerParams(dimension_semantics=sem, vmem_limit_bytes=VMEM_LIMIT)


def _modulation_kernel(c_ref, w_ref, b_ref, o_ref):
    c = c_ref[...]
    s = c * jax.nn.sigmoid(c)
    o_ref[...] = jnp.dot(s, w_ref[...], precision=lax.Precision.HIGHEST,
                         preferred_element_type=F32) + b_ref[...]


def _modulation(c_all, w_mod, b_mod):
    depth, d, six_d = w_mod.shape
    rows = c_all.shape[0]
    tn = 1536
    return pl.pallas_call(
        _modulation_kernel,
        grid=(depth, six_d // tn),
        in_specs=[
            pl.BlockSpec((rows, d), lambda l, j: (0, 0)),
            pl.BlockSpec((None, d, tn), lambda l, j: (l, 0, j)),
            pl.BlockSpec((None, 1, tn), lambda l, j: (l, 0, j)),
        ],
        out_specs=pl.BlockSpec((None, rows, tn), lambda l, j: (l, 0, j)),
        out_shape=jax.ShapeDtypeStruct((depth, rows, six_d), F32),
        compiler_params=_cparams(("arbitrary", "arbitrary")),
        name="modulation",
    )(c_all, w_mod, b_mod.reshape(depth, 1, six_d))


def _norm_mod(x, g, shift, scale):
    ms = jnp.mean(x * x, axis=-1, keepdims=True)
    y = x * lax.rsqrt(ms + EPS) * g
    return y * (1.0 + scale) + shift


def _in_proj_kernel(x_ref, g_ref, shift_ref, scale_ref, w_ref, o_ref):
    h = _norm_mod(x_ref[...], g_ref[...], shift_ref[...], scale_ref[...])
    o_ref[...] = jnp.dot(h.astype(BF16), w_ref[...],
                         preferred_element_type=F32).astype(o_ref.dtype)


def _in_proj(x, g, shift, scale, w):
    bsz, n, d = x.shape
    f = w.shape[1]
    tm = min(512, n)
    vec = pl.BlockSpec((None, 1, d), lambda b, i: (b, 0, 0))
    return pl.pallas_call(
        _in_proj_kernel,
        grid=(bsz, n // tm),
        in_specs=[
            pl.BlockSpec((None, tm, d), lambda b, i: (b, i, 0)),
            pl.BlockSpec((1, d), lambda b, i: (0, 0)),
            vec, vec,
            pl.BlockSpec((d, f), lambda b, i: (0, 0)),
        ],
        out_specs=pl.BlockSpec((None, tm, f), lambda b, i: (b, i, 0)),
        out_shape=jax.ShapeDtypeStruct((bsz, n, f), BF16),
        compiler_params=_cparams(("parallel", "parallel")),
        name="in_proj",
    )(x, g, shift, scale, w)


RG_TILE = 256
RG_CG = 256
RG_SEG = RG_TILE // SUBLANES
RG_PITCH = RG_SEG + SUBLANES


def _rglru_kernel(u_ref, yg_ref, cw_ref, cb_ref, wg_ref, bg_ref, lam_ref, h0_ref,
                  o_ref, hl_ref, u_scr, hf_scr, ab_scr, nat_scr):
    n = u_ref.shape[0]
    tt = RG_TILE
    nt = n // tt
    nseg = n // RG_SEG
    cg = u_ref.shape[1]
    lane_cols = [slice(g * LANES, (g + 1) * LANES) for g in range(cg // LANES)]

    zeros = jnp.zeros((RG_PITCH, LANES), F32)
    for g, cols in enumerate(lane_cols):
        u_scr[g, pl.ds(0, RG_PITCH), :] = zeros
        u_scr[g, pl.ds(nseg * RG_PITCH + RG_SEG, SUBLANES), :] = zeros[0:SUBLANES]

    def fill(s, carry):
        src = pl.multiple_of(s * RG_SEG, RG_SEG)
        dst = pl.multiple_of((s + 1) * RG_PITCH, SUBLANES)
        for g, cols in enumerate(lane_cols):
            rows = u_ref[pl.ds(src, RG_SEG), cols].astype(F32)
            u_scr[g, pl.ds(dst, RG_SEG), :] = rows
            u_scr[g, pl.ds(dst - SUBLANES, SUBLANES), :] = rows[0:SUBLANES]
        return carry

    lax.fori_loop(0, nseg, fill, 0)

    seg = RG_SEG
    cw_rows = [[jnp.broadcast_to(cw_ref[k:k + 1, c], (SUBLANES, LANES)) for c in lane_cols]
               for k in range(CONV_W)]
    cb_rows = [jnp.broadcast_to(cb_ref[:, c], (SUBLANES, LANES)) for c in lane_cols]

    def conv_tile(i):
        base = pl.multiple_of((i * SUBLANES + 1) * RG_PITCH, SUBLANES)
        blocks = []
        for j in range(seg):
            lane_groups = []
            for g in range(cg // LANES):
                acc = cb_rows[g]
                for k in range(CONV_W):
                    q = j + k - CONV_LEFT
                    start = base + q if q >= 0 else base - RG_PITCH + seg + q
                    acc = acc + cw_rows[k][g] * u_scr[g, pl.ds(start, SUBLANES, stride=RG_PITCH), :]
                lane_groups.append(acc)
            blocks.append(jnp.concatenate(lane_groups, axis=1))
        return jnp.concatenate(blocks, axis=0)

    def gates(uc, d):
        ub = uc.astype(BF16)
        ta = jnp.tanh(jnp.dot(ub, wg_ref[d, 0], preferred_element_type=F32) + bg_ref[d, 0:1, :])
        ti = jnp.tanh(jnp.dot(ub, wg_ref[d, 1], preferred_element_type=F32) + bg_ref[d, 1:2, :])
        z = -lam_ref[d:d + 1, :]
        sp = jnp.maximum(z, 0.0) + jnp.log1p(jnp.exp(-jnp.abs(z)))
        c2 = (-0.5 * RG_C * LOG2_E) * sp
        a = jnp.exp2(c2 + c2 * ta)
        b = jnp.exp2((0.5 * LOG2_E) * jnp.log(1.0 - a * a)) * ((0.5 + 0.5 * ti) * uc)
        return a, b

    def scan_tile(a, b, carry, reverse):
        steps = range(seg - 1, -1, -1) if reverse else range(seg)
        h = acum = None
        h_loc = [None] * seg
        a_cum = [None] * seg
        for j in steps:
            rows = slice(j * SUBLANES, (j + 1) * SUBLANES)
            h = b[rows] if h is None else a[rows] * h + b[rows]
            acum = a[rows] if acum is None else a[rows] * acum
            h_loc[j], a_cum[j] = h, acum
        order = range(SUBLANES - 1, -1, -1) if reverse else range(SUBLANES)
        c = carry
        c_in = [None] * SUBLANES
        for s in order:
            c_in[s] = c
            c = h[s:s + 1, :] + acum[s:s + 1, :] * c
        c_in = jnp.concatenate(c_in, axis=0)
        out = jnp.concatenate([h_loc[j] + a_cum[j] * c_in for j in range(seg)], axis=0)
        return out, c

    def fwd_body(i, carry):
        t0 = pl.multiple_of(i * tt, tt)
        uc = conv_tile(i)
        a, b = gates(uc, 0)
        h, carry = scan_tile(a, b, carry, False)
        hf_scr[pl.ds(t0, tt), :] = h
        a, b = gates(uc, 1)
        ab_scr[0, pl.ds(t0, tt), :] = a
        ab_scr[1, pl.ds(t0, tt), :] = b
        return carry

    hf_last = lax.fori_loop(0, nt, fwd_body, h0_ref[0:1, :], unroll=min(2, nt))

    def bwd_body(i, carry):
        tile = nt - 1 - i
        t0 = pl.multiple_of(tile * tt, tt)
        h, carry = scan_tile(ab_scr[0, pl.ds(t0, tt), :], ab_scr[1, pl.ds(t0, tt), :], carry, True)
        hsum = hf_scr[pl.ds(t0, tt), :] + h
        for j in range(seg):
            for g, cols in enumerate(lane_cols):
                nat_scr[g, pl.ds(j, SUBLANES, stride=RG_PITCH), :] = (
                    hsum[j * SUBLANES:(j + 1) * SUBLANES, cols])
        hnat = jnp.concatenate(
            [jnp.concatenate([nat_scr[g, pl.ds(s * RG_PITCH, seg), :] for s in range(SUBLANES)],
                             axis=0) for g in range(len(lane_cols))], axis=1)
        yg = yg_ref[pl.ds(t0, tt), :].astype(F32)
        o_ref[pl.ds(t0, tt), :] = (jax.nn.gelu(yg) * hnat).astype(o_ref.dtype)
        return carry

    hb_last = lax.fori_loop(0, nt, bwd_body, h0_ref[1:2, :], unroll=min(2, nt))
    hl_ref[0:1, :] = hf_last
    hl_ref[1:2, :] = hb_last


def _rglru(p, conv_w, conv_b, wg, bg, lam, h0):
    bsz, n, _ = p.shape
    rg_w = lam.shape[1]
    ncg = rg_w // RG_CG
    assert n % RG_TILE == 0 and rg_w % RG_CG == 0
    return pl.pallas_call(
        _rglru_kernel,
        grid=(bsz, ncg),
        in_specs=[
            pl.BlockSpec((None, n, RG_CG), lambda b, c: (b, 0, c)),
            pl.BlockSpec((None, n, RG_CG), lambda b, c: (b, 0, ncg + c)),
            pl.BlockSpec((CONV_W, RG_CG), lambda b, c: (0, c)),
            pl.BlockSpec((1, RG_CG), lambda b, c: (0, c)),
            pl.BlockSpec((2, 2, None, RG_CG, RG_CG), lambda b, c: (0, 0, c, 0, 0)),
            pl.BlockSpec((2, 2, RG_CG), lambda b, c: (0, 0, c)),
            pl.BlockSpec((2, RG_CG), lambda b, c: (0, c)),
            pl.BlockSpec((None, 2, RG_CG), lambda b, c: (b, 0, c)),
        ],
        out_specs=[
            pl.BlockSpec((None, n, RG_CG), lambda b, c: (b, 0, c)),
            pl.BlockSpec((None, 2, RG_CG), lambda b, c: (b, 0, c)),
        ],
        out_shape=[
            jax.ShapeDtypeStruct((bsz, n, rg_w), BF16),
            jax.ShapeDtypeStruct((bsz, 2, rg_w), F32),
        ],
        scratch_shapes=[
            pltpu.VMEM((RG_CG // LANES, (n // RG_SEG + 1) * RG_PITCH, LANES), F32),
            pltpu.VMEM((n, RG_CG), F32),
            pltpu.VMEM((2, n, RG_CG), F32),
            pltpu.VMEM((RG_CG // LANES, SUBLANES * RG_PITCH, LANES), F32),
        ],
        compiler_params=_cparams(("parallel", "parallel")),
        name="rglru",
    )(p, p, conv_w, conv_b.reshape(1, rg_w), wg, bg, lam, h0)


RET_UNROLL = 4


def _retention_kernel(lg_ref, q_ref, k_ref, v_ref, g_ref, cs_ref, sn_ref, s0_ref,
                      o_ref, so_ref, kr_scr, st_scr):
    n, hd = q_ref.shape
    L = RET_CHUNK
    nc = n // L
    head = pl.program_id(1)
    lgf = lg_ref[0, head]
    lgb = lg_ref[1, head]

    def rope(t, rows):
        return t * cs_ref[rows, :] + pltpu.roll(t, hd // 2, axis=1) * sn_ref[rows, :]

    kr_scr[...] = (rope(k_ref[...].astype(F32), slice(None)) * (hd ** -0.5)).astype(BF16)

    jc = lax.broadcasted_iota(jnp.int32, (L, 1), 0).astype(F32)
    q_dec_f = jnp.exp((jc + 1.0) * lgf)
    q_dec_b = jnp.exp((L - jc) * lgb)
    k_dec_f = jnp.exp((L - 1.0 - jc) * lgf)
    k_dec_b = jnp.exp(jc * lgb)
    ones = jnp.ones((1, hd), F32)
    chunk_f = jnp.exp(ones * (L * lgf))
    chunk_b = jnp.exp(ones * (L * lgb))
    ii = lax.broadcasted_iota(jnp.int32, (L, L), 0)
    jj = lax.broadcasted_iota(jnp.int32, (L, L), 1)
    diff = (ii - jj).astype(F32)
    dmat = jnp.where(diff >= 0.0, jnp.exp(jnp.maximum(diff, 0.0) * lgf),
                     jnp.exp(jnp.maximum(-diff, 0.0) * lgb))

    tn_dims = (((0,), (0,)), ((), ()))
    nt_dims = (((1,), (1,)), ((), ()))

    def state_update(c, state, k_dec, chunk_dec):
        t0 = pl.multiple_of(c * L, L)
        kd = (kr_scr[pl.ds(t0, L), :].astype(F32) * k_dec).astype(BF16)
        upd = lax.dot_general(kd, v_ref[pl.ds(t0, L), :], tn_dims, preferred_element_type=F32)
        return chunk_dec * state + upd

    def state_body(i, carry):
        f_state, r_state = carry
        cf = i
        cb = nc - 1 - i
        st_scr[cf, :, 0:hd] = f_state.astype(BF16)
        st_scr[cb, :, hd:2 * hd] = r_state.astype(BF16)
        return (state_update(cf, f_state, k_dec_f, chunk_f),
                state_update(cb, r_state, k_dec_b, chunk_b))

    f_fin, r_fin = lax.fori_loop(0, nc, state_body, (s0_ref[0], s0_ref[1]),
                                 unroll=min(RET_UNROLL, nc))
    so_ref[0] = f_fin
    so_ref[1] = r_fin

    def out_body(c, carry):
        t0 = pl.multiple_of(c * L, L)
        qc = rope(q_ref[pl.ds(t0, L), :].astype(F32), pl.ds(t0, L)).astype(BF16)
        kc = kr_scr[pl.ds(t0, L), :]
        vc = v_ref[pl.ds(t0, L), :]
        s = lax.dot_general(qc, kc, nt_dims, preferred_element_type=F32)
        o = jnp.dot((s * dmat).astype(BF16), vc, preferred_element_type=F32)
        cross = jnp.dot(qc, st_scr[c], preferred_element_type=F32)
        o = o + cross[:, 0:hd] * q_dec_f + cross[:, hd:2 * hd] * q_dec_b
        mu = jnp.mean(o, axis=-1, keepdims=True)
        oc = o - mu
        var = jnp.mean(oc * oc, axis=-1, keepdims=True)
        gate = g_ref[pl.ds(t0, L), :].astype(F32)
        o_ref[pl.ds(t0, L), :] = (gate * jax.nn.sigmoid(gate) * (oc * lax.rsqrt(var + EPS))).astype(o_ref.dtype)
        return carry

    lax.fori_loop(0, nc, out_body, 0, unroll=min(RET_UNROLL, nc))


def _retention(p, log_g, cs, sn, s0, col0):
    bsz, n, _ = p.shape
    nh = log_g.shape[1]
    hd = cs.shape[1]
    cb0 = col0 // hd

    def col(which):
        return pl.BlockSpec((None, n, hd), lambda b, h, lg: (b, 0, cb0 + which * nh + h))

    state = pl.BlockSpec((None, 2, None, hd, hd), lambda b, h, lg: (b, 0, h, 0, 0))
    table = pl.BlockSpec((n, hd), lambda b, h, lg: (0, 0))
    grid_spec = pltpu.PrefetchScalarGridSpec(
        num_scalar_prefetch=1,
        grid=(bsz, nh),
        in_specs=[col(0), col(1), col(2), col(3), table, table, state],
        out_specs=[pl.BlockSpec((None, n, hd), lambda b, h, lg: (b, 0, h)), state],
        scratch_shapes=[
            pltpu.VMEM((n, hd), BF16),
            pltpu.VMEM((n // RET_CHUNK, hd, 2 * hd), BF16),
        ],
    )
    return pl.pallas_call(
        _retention_kernel,
        grid_spec=grid_spec,
        out_shape=[
            jax.ShapeDtypeStruct((bsz, n, nh * hd), BF16),
            jax.ShapeDtypeStruct((bsz, 2, nh, hd, hd), F32),
        ],
        compiler_params=_cparams(("parallel", "parallel")),
        name="retention",
    )(log_g, p, p, p, p, cs, sn, s0)


R_RANK0, R_RANK1, R_E0, R_E1, R_W0, R_W1 = range(6)


def _mix_residual(x_ref, rg_ref, ret_ref, wo_ref, gate_ref):
    rg_w = rg_ref.shape[1]
    y = jnp.dot(rg_ref[...], wo_ref[0:rg_w, :], preferred_element_type=F32)
    y = y + jnp.dot(ret_ref[...], wo_ref[rg_w:, :], preferred_element_type=F32)
    return x_ref[...] + gate_ref[...] * y


def _out_proj_kernel(x_ref, rg_ref, ret_ref, wo_ref, gate_ref, g_ref, shift_ref, scale_ref,
                     wr2_ref, br_ref, xo_ref, h_ref, rt_ref, rtt_ref, cnt_ref, carry_scr):
    xn = _mix_residual(x_ref, rg_ref, ret_ref, wo_ref, gate_ref)
    xo_ref[...] = xn
    hx = _norm_mod(xn, g_ref[...], shift_ref[...], scale_ref[...])

    tm, d = hx.shape
    for s in range(d // LANES):
        h_ref[pl.ds(s, tm, stride=d // LANES), :] = hx[:, s * LANES:(s + 1) * LANES]

    h_hi = hx.astype(BF16)
    h_lo = (hx - h_hi.astype(F32)).astype(BF16)
    part = jnp.dot(h_hi, wr2_ref[...], preferred_element_type=F32)
    logits = (part[:, :LANES] + part[:, LANES:]
              + jnp.dot(h_lo, wr2_ref[:, :LANES], preferred_element_type=F32) + br_ref[...])
    lane_i = lax.broadcasted_iota(jnp.int32, logits.shape, 1)
    lane = lane_i.astype(F32)
    m0 = jnp.max(logits, axis=-1, keepdims=True)
    i0 = jnp.min(jnp.where(logits == m0, lane, float(LANES)), axis=-1, keepdims=True)
    rest_l = jnp.where(lane == i0, NEG_BIG, logits)
    m1 = jnp.max(rest_l, axis=-1, keepdims=True)
    i1 = jnp.min(jnp.where(rest_l == m1, lane, float(LANES)), axis=-1, keepdims=True)
    e = jnp.exp(m1 - m0)
    w0 = 1.0 / (1.0 + e)
    w1 = e / (1.0 + e)

    @pl.when((pl.program_id(0) == 0) & (pl.program_id(1) == 0))
    def _():
        carry_scr[...] = jnp.zeros_like(carry_scr)

    sel0 = lane == i0
    sel1 = lane == i1
    mask = jnp.where(sel0 | sel1, 1.0, 0.0)
    ri = lax.broadcasted_iota(jnp.int32, (tm, tm), 0)
    ci = lax.broadcasted_iota(jnp.int32, (tm, tm), 1)
    tri = jnp.where(ri > ci, 1.0, 0.0).astype(BF16)
    before = jnp.dot(tri, mask.astype(BF16), preferred_element_type=F32) + carry_scr[...]
    rank0 = jnp.sum(jnp.where(sel0, before, 0.0), axis=-1, keepdims=True)
    rank1 = jnp.sum(jnp.where(sel1, before, 0.0), axis=-1, keepdims=True)
    total = carry_scr[...] + jnp.sum(mask, axis=0, keepdims=True)
    carry_scr[...] = total
    cnt_ref[...] = total
    rec = jnp.zeros_like(logits)
    for idx, val in ((R_RANK0, rank0), (R_RANK1, rank1), (R_E0, i0), (R_E1, i1),
                     (R_W0, w0), (R_W1, w1)):
        rec = jnp.where(lane_i == idx, val, rec)
    rt_ref[...] = rec
    rtt_ref[...] = jnp.transpose(rec)[0:rtt_ref.shape[0], :]


def _out_proj_route(x, rg, ret, w_out, gate, g, shift, scale, wr2, br):
    bsz, n, d = x.shape
    tm = min(512, n)
    vec = pl.BlockSpec((None, 1, d), lambda b, i: (b, 0, 0))
    tile = lambda w: pl.BlockSpec((None, tm, w), lambda b, i: (b, i, 0))
    const = lambda shape: pl.BlockSpec(shape, lambda b, i: (0, 0))
    return pl.pallas_call(
        _out_proj_kernel,
        grid=(bsz, n // tm),
        in_specs=[
            tile(d), tile(rg.shape[2]), tile(ret.shape[2]), const(w_out.shape),
            vec, const((1, d)), vec, vec, const((d, 2 * LANES)), const((1, LANES)),
        ],
        out_specs=[
            tile(d),
            pl.BlockSpec((None, tm * (d // LANES), LANES), lambda b, i: (b, i, 0)),
            tile(LANES),
            pl.BlockSpec((SUBLANES, tm), lambda b, i: (0, b * (n // tm) + i)),
            const((1, LANES)),
        ],
        out_shape=[
            jax.ShapeDtypeStruct((bsz, n, d), F32),
            jax.ShapeDtypeStruct((bsz, n * (d // LANES), LANES), F32),
            jax.ShapeDtypeStruct((bsz, n, LANES), F32),
            jax.ShapeDtypeStruct((SUBLANES, bsz * n), F32),
            jax.ShapeDtypeStruct((1, LANES), F32),
        ],
        scratch_shapes=[pltpu.VMEM((1, LANES), F32)],
        compiler_params=_cparams(("arbitrary", "arbitrary")),
        name="out_proj_route",
    )(x, rg, ret, w_out, gate, g, shift, scale, wr2, br)


def _swiglu_acc(h, w1_ref, w3_ref, w2_ref, row_scale=None):
    a = jnp.dot(h, w1_ref[...], preferred_element_type=F32)
    b = jnp.dot(h, w3_ref[...], preferred_element_type=F32)
    act = a * jax.nn.sigmoid(a) * b
    if row_scale is not None:
        act = act * row_scale
    return jnp.dot(act.astype(BF16), w2_ref[...], preferred_element_type=F32)


def _mix_ffn_kernel(n_cast, x_ref, rg_ref, ret_ref, wo_ref, gmix_ref, g_ref, shift_ref, scale_ref,
                    gffn_ref, w1_ref, w3_ref, w2_ref, *rest):
    cast_in, o_ref, cast_out = rest[:n_cast], rest[n_cast], rest[n_cast + 1:]
    xn = _mix_residual(x_ref, rg_ref, ret_ref, wo_ref, gmix_ref)
    hx = _norm_mod(xn, g_ref[...], shift_ref[...], scale_ref[...]).astype(BF16)
    o_ref[...] = xn + gffn_ref[...] * _swiglu_acc(hx, w1_ref, w3_ref, w2_ref)
    for src, dst in zip(cast_in, cast_out):
        dst[...] = src[...].astype(dst.dtype)


def _mix_ffn(x, rg, ret, w_out, gate_mix, g, shift, scale, gate_ffn, w1, w3, w2, cast=()):
    bsz, n, d = x.shape
    f = w1.shape[1]
    tm = min(512, n)
    steps = bsz * (n // tm)
    vec = pl.BlockSpec((None, 1, d), lambda b, i: (b, 0, 0))
    tile = lambda w: pl.BlockSpec((None, tm, w), lambda b, i: (b, i, 0))
    resident = lambda shape: pl.BlockSpec(shape, lambda b, i: (0, 0), pipeline_mode=pl.Buffered(1))
    slabs = []
    for a in cast:
        rows, cols = a.shape
        assert rows % (steps * 2 * SUBLANES) == 0
        slabs.append(pl.BlockSpec((rows // steps, cols), lambda b, i: (b * (n // tm) + i, 0)))
    out = pl.pallas_call(
        functools.partial(_mix_ffn_kernel, len(cast)),
        grid=(bsz, n // tm),
        in_specs=[
            tile(d), tile(rg.shape[2]), tile(ret.shape[2]), resident(w_out.shape),
            vec, resident((1, d)), vec, vec, vec,
            resident((d, f)), resident((d, f)), resident((f, d)),
        ] + slabs,
        out_specs=[tile(d)] + slabs,
        out_shape=[jax.ShapeDtypeStruct((bsz, n, d), F32)]
        + [jax.ShapeDtypeStruct(a.shape, BF16) for a in cast],
        compiler_params=_cparams(("parallel", "parallel")),
        name="mix_ffn",
    )(x, rg, ret, w_out, gate_mix, g, shift, scale, gate_ffn, w1, w3, w2, *cast)
    return out[0], tuple(out[1:])


MOE_BLK = 256
MOE_TOKENS = 1024


def _dispatch_kernel(d0_ref, d1_ref, zr_ref, hx_ref, hs_ref, zbuf, sem, zsem):
    tm = hx_ref.shape[0]
    base = pl.program_id(0) * tm

    @pl.when(pl.program_id(0) == 0)
    def _():
        zbuf[...] = jnp.zeros_like(zbuf)
        for z in range(zr_ref.shape[0]):
            @pl.when(zr_ref[z] >= 0)
            def _():
                pltpu.make_async_copy(zbuf, hs_ref.at[pl.ds(zr_ref[z], MOE_BLK)], zsem).start()
        for z in range(zr_ref.shape[0]):
            @pl.when(zr_ref[z] >= 0)
            def _():
                pltpu.make_async_copy(zbuf, hs_ref.at[pl.ds(zr_ref[z], MOE_BLK)], zsem).wait()

    def issue(t, carry):
        pltpu.make_async_copy(hx_ref.at[t], hs_ref.at[d0_ref[base + t]], sem).start(priority=0)
        pltpu.make_async_copy(hx_ref.at[t], hs_ref.at[d1_ref[base + t]], sem).start(priority=1)
        return carry

    lax.fori_loop(0, tm, issue, 0, unroll=8)
    for _ in range(2):
        pltpu.make_async_copy(hx_ref, hs_ref.at[pl.ds(0, tm)], sem).wait()


def _dispatch(dest0, dest1, zero_rows, hx, n_rows):
    n_tok, s, lanes = hx.shape
    tm = min(MOE_TOKENS, n_tok)
    grid_spec = pltpu.PrefetchScalarGridSpec(
        num_scalar_prefetch=3,
        grid=(n_tok // tm,),
        in_specs=[pl.BlockSpec((tm, s, lanes), lambda i, d0, d1, zr: (i, 0, 0))],
        out_specs=pl.BlockSpec(memory_space=pl.ANY),
        scratch_shapes=[pltpu.VMEM((MOE_BLK, s, lanes), F32),
                        pltpu.SemaphoreType.DMA(()), pltpu.SemaphoreType.DMA(())],
    )
    return pl.pallas_call(
        _dispatch_kernel,
        grid_spec=grid_spec,
        out_shape=jax.ShapeDtypeStruct((n_rows, s, lanes), F32),
        compiler_params=_cparams(("arbitrary",)),
        name="moe_dispatch",
    )(dest0, dest1, zero_rows, hx)


def _experts_kernel(be_ref, nu_ref, hs_ref, w1_ref, w3_ref, w2_ref, y_ref):
    j = pl.program_id(0)
    d = w1_ref.shape[0]
    ns = d // LANES
    blk = hs_ref.shape[0] // ns

    @pl.when(j < nu_ref[0])
    def _():
        h = jnp.concatenate(
            [hs_ref[pl.ds(s, blk, stride=ns), :].astype(BF16) for s in range(ns)], axis=1)
        out = _swiglu_acc(h, w1_ref, w3_ref, w2_ref)
        for s in range(ns):
            y_ref[pl.ds(s, blk, stride=ns), :] = out[:, s * LANES:(s + 1) * LANES]

    @pl.when(j >= nu_ref[0])
    def _():
        y_ref[...] = jnp.zeros_like(y_ref)


def _experts(block_expert, n_used, hs, w1, w3, w2):
    ne, d, f = w1.shape
    s = d // LANES
    n_rows = hs.shape[0] // s
    n_blocks = n_rows // MOE_BLK

    def expert(shape):
        return pl.BlockSpec((None,) + shape, lambda j, be, nu: (be[j], 0, 0),
                            pipeline_mode=pl.Buffered(1))

    grid_spec = pltpu.PrefetchScalarGridSpec(
        num_scalar_prefetch=2,
        grid=(n_blocks,),
        in_specs=[
            pl.BlockSpec((MOE_BLK * s, LANES), lambda j, be, nu: (jnp.minimum(j, nu[0] - 1), 0)),
            expert((d, f)), expert((d, f)), expert((f, d)),
        ],
        out_specs=pl.BlockSpec((MOE_BLK * s, LANES), lambda j, be, nu: (j, 0)),
    )
    return pl.pallas_call(
        _experts_kernel,
        grid_spec=grid_spec,
        out_shape=jax.ShapeDtypeStruct((n_rows * s, LANES), F32),
        compiler_params=_cparams(("arbitrary",)),
        name="moe_experts",
    )(block_expert, n_used, hs, w1, w3, w2)


def _combine_kernel(d0_ref, d1_ref, x_ref, gate_ref, rt_ref, gfin_ref, y_ref, o_ref, ybuf, sem):
    tm, d = x_ref.shape
    ns = d // LANES
    step = pl.program_id(0) * pl.num_programs(1) + pl.program_id(1)
    nsteps = pl.num_programs(0) * pl.num_programs(1)
    slot = step % 2

    def row(ref, r):
        return ref.at[pl.ds(pl.multiple_of(r * ns, ns), ns)]

    def start_gather(block, into):
        base = block * tm

        def issue(t, carry):
            pltpu.make_async_copy(row(y_ref, d0_ref[base + t]), row(ybuf.at[into, 0], t),
                                  sem.at[into]).start(priority=0)
            pltpu.make_async_copy(row(y_ref, d1_ref[base + t]), row(ybuf.at[into, 1], t),
                                  sem.at[into]).start(priority=1)
            return carry

        lax.fori_loop(0, tm, issue, 0, unroll=8)

    @pl.when(step == 0)
    def _():
        start_gather(0, 0)

    @pl.when(step + 1 < nsteps)
    def _():
        start_gather(step + 1, 1 - slot)

    for k in range(2):
        pltpu.make_async_copy(y_ref.at[pl.ds(0, tm * ns)], ybuf.at[slot, k], sem.at[slot]).wait()

    chunk = 64
    for r0 in range(0, tm, chunk):
        rows = pl.ds(r0, chunk)
        rt = rt_ref[rows, :]
        w0 = rt[:, R_W0:R_W0 + 1]
        w1 = rt[:, R_W1:R_W1 + 1]
        ss = jnp.zeros((chunk, 1), F32)
        for s in range(ns):
            cols = slice(s * LANES, (s + 1) * LANES)
            moe = (w0 * ybuf[slot, 0, pl.ds(r0 * ns + s, chunk, stride=ns), :]
                   + w1 * ybuf[slot, 1, pl.ds(r0 * ns + s, chunk, stride=ns), :])
            xn = x_ref[rows, cols] + gate_ref[:, cols] * moe
            ss = ss + jnp.sum(xn * xn, axis=-1, keepdims=True)
            o_ref[rows, cols] = xn
        o_ref[rows, :] = o_ref[rows, :] * lax.rsqrt(ss / d + EPS) * gfin_ref[...]


def _combine(dest0, dest1, x, gate, route, g_final, y):
    bsz, n, d = x.shape
    s = d // LANES
    tm = min(MOE_TOKENS, n)
    grid_spec = pltpu.PrefetchScalarGridSpec(
        num_scalar_prefetch=2,
        grid=(bsz, n // tm),
        in_specs=[
            pl.BlockSpec((None, tm, d), lambda b, i, d0, d1: (b, i, 0)),
            pl.BlockSpec((None, 1, d), lambda b, i, d0, d1: (b, 0, 0)),
            pl.BlockSpec((None, tm, LANES), lambda b, i, d0, d1: (b, i, 0)),
            pl.BlockSpec((1, d), lambda b, i, d0, d1: (0, 0)),
            pl.BlockSpec(memory_space=pl.ANY),
        ],
        out_specs=pl.BlockSpec((None, tm, d), lambda b, i, d0, d1: (b, i, 0)),
        scratch_shapes=[pltpu.VMEM((2, 2, tm * s, LANES), F32), pltpu.SemaphoreType.DMA((2,))],
    )
    return pl.pallas_call(
        _combine_kernel,
        grid_spec=grid_spec,
        out_shape=jax.ShapeDtypeStruct((bsz, n, d), F32),
        compiler_params=_cparams(("arbitrary", "arbitrary")),
        name="moe_combine",
    )(dest0, dest1, x, gate, route, g_final, y)


def _moe(hx, x, gate, route, route_t, counts, g_final, w1, w3, w2):
    bsz, n, d = x.shape
    ne = w1.shape[0]
    n_tok = bsz * n
    n_rows = -(-(2 * n_tok) // MOE_BLK) * MOE_BLK + ne * MOE_BLK
    n_blocks = n_rows // MOE_BLK
    cnt = counts[0, :ne].astype(jnp.int32)
    padded = (cnt + MOE_BLK - 1) // MOE_BLK * MOE_BLK
    pad_end = jnp.cumsum(padded)
    pad_start = pad_end - padded
    n_used = pad_end[-1] // MOE_BLK
    blk = jnp.minimum(jnp.arange(n_blocks, dtype=jnp.int32), n_used - 1)
    block_expert = jnp.minimum(
        jnp.sum((blk[:, None] * MOE_BLK >= pad_end[None, :]).astype(jnp.int32), axis=1), ne - 1)
    def dest(e_row, rank_row):
        e = route_t[e_row].astype(jnp.int32)
        start = sum(jnp.where(e == k, pad_start[k], 0) for k in range(ne))
        return start + route_t[rank_row].astype(jnp.int32)

    dest0 = dest(R_E0, R_RANK0)
    dest1 = dest(R_E1, R_RANK1)

    last_blk = jnp.where(padded > 0, pad_end - MOE_BLK, -1)
    trail = n_used + jnp.arange(ne, dtype=jnp.int32)
    trail = jnp.where(trail < n_blocks, trail * MOE_BLK, -1)
    zero_rows = jnp.concatenate([last_blk, trail]).astype(jnp.int32)

    ns = d // LANES
    hs = _dispatch(dest0, dest1, zero_rows, hx.reshape(n_tok, ns, LANES), n_rows)
    y = _experts(block_expert, n_used.reshape(1).astype(jnp.int32),
                 hs.reshape(n_rows * ns, LANES), w1, w3, w2)
    return _combine(dest0, dest1, x, gate, route, g_final, y)


def _rope_tables(n, hd):
    rows = n // GRID_W
    n_freq = hd // 4
    inv = ROPE_BASE ** (-jnp.arange(n_freq, dtype=F32) / n_freq)
    row_ang = jnp.arange(rows, dtype=F32)[:, None] * inv
    col_ang = jnp.arange(GRID_W, dtype=F32)[:, None] * inv

    def table(of_row, of_col):
        a = jnp.broadcast_to(of_row[:, None, :], (rows, GRID_W, n_freq))
        b = jnp.broadcast_to(of_col[None, :, :], (rows, GRID_W, n_freq))
        return jnp.concatenate([a, b], axis=-1).reshape(n, 2 * n_freq)

    cos = table(jnp.cos(row_ang), jnp.cos(col_ang))
    sin = table(jnp.sin(row_ang), jnp.sin(col_ang))
    return jnp.concatenate([cos, cos], axis=-1), jnp.concatenate([-sin, sin], axis=-1)


def _block_diag_gates(rg_wa, rg_wx):
    w = jnp.stack([rg_wa, rg_wx], axis=1)
    nd, ng, nk, c, _ = w.shape
    per = RG_CG // c
    w = w.reshape(nd, ng, nk // per, per, c, c)
    eye = jnp.eye(per, dtype=w.dtype)
    full = jnp.einsum('dgmpij,pq->dgmpiqj', w, eye)
    return full.reshape(nd, ng, nk // per, RG_CG, RG_CG).astype(BF16)


def _mixer(p, rope, states, conv_w, conv_b, wg, bg, lam, log_g, rg_w):
    rg_out, h_last = _rglru(p, conv_w, conv_b, wg, bg, lam, states[0])
    ret_out, s_last = _retention(p, log_g, rope[0], rope[1], states[1], 2 * rg_w)
    return rg_out, ret_out, (h_last, s_last)


def kernel(x, c, ctx, c_ctx, w_mod, b_mod, g_mix, g_ffn, g_final, w_in, w_out, conv_w, conv_b,
           rg_wa, rg_ba, rg_wx, rg_bx, rg_lam, ret_decay, ffn_w1, ffn_w3, ffn_w2,
           moe_router, moe_router_b, moe_w1, moe_w3, moe_w2):
    bsz, n_lat, d = x.shape
    n_ctx = ctx.shape[1]
    depth = w_mod.shape[0]
    rg_w = rg_lam.shape[2]
    nh = ret_decay.shape[2]
    hd = (w_out.shape[1] - rg_w) // nh
    ne = moe_router.shape[2]
    assert depth == 2, "kernel is written for the two-layer block (dense FFN, then MoE)"

    rows = -(-(bsz + 1) // SUBLANES) * SUBLANES
    c_all = jnp.zeros((rows, d), F32).at[:bsz].set(c).at[bsz].set(c_ctx)
    mod = _modulation(c_all, w_mod, b_mod)
    mod = mod.reshape(depth, rows, N_MOD, d)
    mod_lat = mod[:, :bsz, None]
    mod_ctx = jnp.broadcast_to(mod[:, bsz:bsz + 1, None], mod_lat.shape)

    rope_lat = _rope_tables(n_lat, hd)
    rope_ctx = (jnp.ones((n_ctx, hd), F32), jnp.zeros((n_ctx, hd), F32))
    log_g = jax.nn.log_sigmoid(ret_decay.astype(F32))
    zero_states = (jnp.zeros((bsz, 2, rg_w), F32), jnp.zeros((bsz, 2, nh, hd, hd), F32))

    for l in range(depth):
        last = l == depth - 1
        ml = [mod_lat[l, :, :, j] for j in range(N_MOD)]
        mc = [mod_ctx[l, :, :, j] for j in range(N_MOD)]
        g_m = g_mix[l].reshape(1, d)
        g_f = g_ffn[l].reshape(1, d)
        w_in_l = w_in[l].astype(BF16)
        w_out_l = w_out[l].astype(BF16)
        wg = _block_diag_gates(0.5 * rg_wa[l], 0.5 * rg_wx[l])
        bg = 0.5 * jnp.stack([rg_ba[l], rg_bx[l]], axis=1)
        mix_p = (conv_w[l], conv_b[l], wg, bg, rg_lam[l], log_g[l], rg_w)

        pc = _in_proj(ctx, g_m, mc[0], mc[1], w_in_l)
        rg_c, ret_c, ctx_states = _mixer(pc, rope_ctx, zero_states, *mix_p)
        px = _in_proj(x, g_m, ml[0], ml[1], w_in_l)
        rg_x, ret_x, _ = _mixer(px, rope_lat, ctx_states, *mix_p)

        if l % 2 == 0:
            i = l // 2
            w1, w3, w2 = ffn_w1[i].astype(BF16), ffn_w3[i].astype(BF16), ffn_w2[i].astype(BF16)
            nxt = moe_w1[i], moe_w3[i], moe_w2[i]
            x, cast = _mix_ffn(x, rg_x, ret_x, w_out_l, ml[2], g_f, ml[3], ml[4], ml[5],
                               w1, w3, w2, cast=[w.reshape(-1, w.shape[-1]) for w in nxt])
            moe_w = [c.reshape(w.shape) for c, w in zip(cast, nxt)]
            if not last:
                ctx, _ = _mix_ffn(ctx, rg_c, ret_c, w_out_l, mc[2], g_f, mc[3], mc[4], mc[5],
                                  w1, w3, w2)
        else:
            i = l // 2
            wr = jnp.zeros((d, LANES), F32).at[:, :ne].set(moe_router[i])
            wr_hi = wr.astype(BF16)
            wr_lo = (wr - wr_hi.astype(F32)).astype(BF16)
            wr2 = jnp.concatenate([wr_hi, wr_lo], axis=1)
            br = jnp.full((1, LANES), NEG_BIG, F32).at[0, :ne].set(moe_router_b[i])
            x, hx, route, route_t, counts = _out_proj_route(
                x, rg_x, ret_x, w_out_l, ml[2], g_f, ml[3], ml[4], wr2, br)
            x = _moe(hx, x, ml[5], route, route_t, counts, g_final.reshape(1, d), *moe_w)
    return x
```

```python
import functools

import jax
import jax.numpy as jnp
from jax import lax
from jax.experimental import pallas as pl
from jax.experimental.pallas import tpu as pltpu

F32 = jnp.float32
BF16 = jnp.bfloat16

EPS = 1e-6
RG_C = 8.0
CONV_W = 4
CONV_LEFT = 2
RET_CHUNK = 256
ROPE_BASE = 10000.0
GRID_W = 64
N_MOD = 6

ROW_TILE = 512

LANES = 128
SUBLANES = 8
VMEM_LIMIT = 56 * 1024 * 1024
NEG_BIG = -1e30
LOG2_E = 1.4426950408889634


def _cparams(sem):
    return pltpu.CompilerParams(dimension_semantics=sem, vmem_limit_bytes=VMEM_LIMIT)


def _modulation_kernel(c_ref, w_ref, b_ref, o_ref):
    c = c_ref[...]
    s = c * jax.nn.sigmoid(c)
    o_ref[...] = jnp.dot(s, w_ref[...], precision=lax.Precision.HIGHEST,
                         preferred_element_type=F32) + b_ref[...]


def _modulation(c_all, w_mod, b_mod):
    depth, d, six_d = w_mod.shape
    rows = c_all.shape[0]
    tn = six_d // 4
    return pl.pallas_call(
        _modulation_kernel,
        grid=(depth, six_d // tn),
        in_specs=[
            pl.BlockSpec((rows, d), lambda l, j: (0, 0)),
            pl.BlockSpec((None, d, tn), lambda l, j: (l, 0, j)),
            pl.BlockSpec((None, 1, tn), lambda l, j: (l, 0, j)),
        ],
        out_specs=pl.BlockSpec((None, rows, tn), lambda l, j: (l, 0, j)),
        out_shape=jax.ShapeDtypeStruct((depth, rows, six_d), F32),
        compiler_params=_cparams(("arbitrary", "arbitrary")),
        name="modulation",
    )(c_all, w_mod, b_mod.reshape(depth, 1, six_d))


def _norm_mod(x, g, shift, scale):
    ms = jnp.mean(x * x, axis=-1, keepdims=True)
    y = x * lax.rsqrt(ms + EPS) * g
    return y * (1.0 + scale) + shift


def _in_proj_kernel(x_ref, g_ref, shift_ref, scale_ref, w_ref, o_ref):
    h = _norm_mod(x_ref[...], g_ref[...], shift_ref[...], scale_ref[...])
    o_ref[...] = jnp.dot(h.astype(BF16), w_ref[...],
                         preferred_element_type=F32).astype(o_ref.dtype)


def _in_proj(x, g, shift, scale, w):
    bsz, n, d = x.shape
    f = w.shape[1]
    tm = min(ROW_TILE, n)
    vec = pl.BlockSpec((None, 1, d), lambda b, i: (b, 0, 0))
    return pl.pallas_call(
        _in_proj_kernel,
        grid=(bsz, n // tm),
        in_specs=[
            pl.BlockSpec((None, tm, d), lambda b, i: (b, i, 0)),
            pl.BlockSpec((1, d), lambda b, i: (0, 0)),
            vec, vec,
            pl.BlockSpec((d, f), lambda b, i: (0, 0)),
        ],
        out_specs=pl.BlockSpec((None, tm, f), lambda b, i: (b, i, 0)),
        out_shape=jax.ShapeDtypeStruct((bsz, n, f), BF16),
        compiler_params=_cparams(("parallel", "parallel")),
        name="in_proj",
    )(x, g, shift, scale, w)


RG_TILE = 256
RG_CG = 256
RG_SEG = RG_TILE // SUBLANES
RG_PITCH = RG_SEG + SUBLANES


def _rglru_kernel(u_ref, yg_ref, cw_ref, cb_ref, wg_ref, bg_ref, lam_ref, h0_ref,
                  o_ref, hl_ref, u_scr, hf_scr, ab_scr, nat_scr):
    n = u_ref.shape[0]
    tt = RG_TILE
    nt = n // tt
    nseg = n // RG_SEG
    cg = u_ref.shape[1]
    lane_cols = [slice(g * LANES, (g + 1) * LANES) for g in range(cg // LANES)]

    zeros = jnp.zeros((RG_PITCH, LANES), F32)
    for g, cols in enumerate(lane_cols):
        u_scr[g, pl.ds(0, RG_PITCH), :] = zeros
        u_scr[g, pl.ds(nseg * RG_PITCH + RG_SEG, SUBLANES), :] = zeros[0:SUBLANES]

    def fill(s, carry):
        src = pl.multiple_of(s * RG_SEG, RG_SEG)
        dst = pl.multiple_of((s + 1) * RG_PITCH, SUBLANES)
        for g, cols in enumerate(lane_cols):
            rows = u_ref[pl.ds(src, RG_SEG), cols].astype(F32)
            u_scr[g, pl.ds(dst, RG_SEG), :] = rows
            u_scr[g, pl.ds(dst - SUBLANES, SUBLANES), :] = rows[0:SUBLANES]
        return carry

    lax.fori_loop(0, nseg, fill, 0)

    seg = RG_SEG
    cw_rows = [[jnp.broadcast_to(cw_ref[k:k + 1, c], (SUBLANES, LANES)) for c in lane_cols]
               for k in range(CONV_W)]
    cb_rows = [jnp.broadcast_to(cb_ref[:, c], (SUBLANES, LANES)) for c in lane_cols]

    def conv_tile(i):
        base = pl.multiple_of((i * SUBLANES + 1) * RG_PITCH, SUBLANES)
        blocks = []
        for j in range(seg):
            lane_groups = []
            for g in range(cg // LANES):
                acc = cb_rows[g]
                for k in range(CONV_W):
                    q = j + k - CONV_LEFT
                    start = base + q if q >= 0 else base - RG_PITCH + seg + q
                    acc = acc + cw_rows[k][g] * u_scr[g, pl.ds(start, SUBLANES, stride=RG_PITCH), :]
                lane_groups.append(acc)
            blocks.append(jnp.concatenate(lane_groups, axis=1))
        return jnp.concatenate(blocks, axis=0)

    def gates(uc, d):
        ub = uc.astype(BF16)
        ta = jnp.tanh(jnp.dot(ub, wg_ref[d, 0], preferred_element_type=F32) + bg_ref[d, 0:1, :])
        ti = jnp.tanh(jnp.dot(ub, wg_ref[d, 1], preferred_element_type=F32) + bg_ref[d, 1:2, :])
        z = -lam_ref[d:d + 1, :]
        sp = jnp.maximum(z, 0.0) + jnp.log1p(jnp.exp(-jnp.abs(z)))
        c2 = (-0.5 * RG_C * LOG2_E) * sp
        a = jnp.exp2(c2 + c2 * ta)
        b = jnp.exp2((0.5 * LOG2_E) * jnp.log(1.0 - a * a)) * ((0.5 + 0.5 * ti) * uc)
        return a, b

    def scan_tile(a, b, carry, reverse):
        steps = range(seg - 1, -1, -1) if reverse else range(seg)
        h = acum = None
        h_loc = [None] * seg
        a_cum = [None] * seg
        for j in steps:
            rows = slice(j * SUBLANES, (j + 1) * SUBLANES)
            h = b[rows] if h is None else a[rows] * h + b[rows]
            acum = a[rows] if acum is None else a[rows] * acum
            h_loc[j], a_cum[j] = h, acum
        order = range(SUBLANES - 1, -1, -1) if reverse else range(SUBLANES)
        c = carry
        c_in = [None] * SUBLANES
        for s in order:
            c_in[s] = c
            c = h[s:s + 1, :] + acum[s:s + 1, :] * c
        c_in = jnp.concatenate(c_in, axis=0)
        out = jnp.concatenate([h_loc[j] + a_cum[j] * c_in for j in range(seg)], axis=0)
        return out, c

    def fwd_body(i, carry):
        t0 = pl.multiple_of(i * tt, tt)
        uc = conv_tile(i)
        a, b = gates(uc, 0)
        h, carry = scan_tile(a, b, carry, False)
        hf_scr[pl.ds(t0, tt), :] = h
        a, b = gates(uc, 1)
        ab_scr[0, pl.ds(t0, tt), :] = a
        ab_scr[1, pl.ds(t0, tt), :] = b
        return carry

    hf_last = lax.fori_loop(0, nt, fwd_body, h0_ref[0:1, :], unroll=min(2, nt))

    def bwd_body(i, carry):
        tile = nt - 1 - i
        t0 = pl.multiple_of(tile * tt, tt)
        h, carry = scan_tile(ab_scr[0, pl.ds(t0, tt), :], ab_scr[1, pl.ds(t0, tt), :], carry, True)
        hsum = hf_scr[pl.ds(t0, tt), :] + h
        for j in range(seg):
            for g, cols in enumerate(lane_cols):
                nat_scr[g, pl.ds(j, SUBLANES, stride=RG_PITCH), :] = (
                    hsum[j * SUBLANES:(j + 1) * SUBLANES, cols])
        hnat = jnp.concatenate(
            [jnp.concatenate([nat_scr[g, pl.ds(s * RG_PITCH, seg), :] for s in range(SUBLANES)],
                             axis=0) for g in range(len(lane_cols))], axis=1)
        yg = yg_ref[pl.ds(t0, tt), :].astype(F32)
        o_ref[pl.ds(t0, tt), :] = (jax.nn.gelu(yg) * hnat).astype(o_ref.dtype)
        return carry

    hb_last = lax.fori_loop(0, nt, bwd_body, h0_ref[1:2, :], unroll=min(2, nt))
    hl_ref[0:1, :] = hf_last
    hl_ref[1:2, :] = hb_last


def _rglru(p, conv_w, conv_b, wg, bg, lam, h0):
    bsz, n, _ = p.shape
    rg_w = lam.shape[1]
    ncg = rg_w // RG_CG
    assert n % RG_TILE == 0 and rg_w % RG_CG == 0
    return pl.pallas_call(
        _rglru_kernel,
        grid=(bsz, ncg),
        in_specs=[
            pl.BlockSpec((None, n, RG_CG), lambda b, c: (b, 0, c)),
            pl.BlockSpec((None, n, RG_CG), lambda b, c: (b, 0, ncg + c)),
            pl.BlockSpec((CONV_W, RG_CG), lambda b, c: (0, c)),
            pl.BlockSpec((1, RG_CG), lambda b, c: (0, c)),
            pl.BlockSpec((2, 2, None, RG_CG, RG_CG), lambda b, c: (0, 0, c, 0, 0)),
            pl.BlockSpec((2, 2, RG_CG), lambda b, c: (0, 0, c)),
            pl.BlockSpec((2, RG_CG), lambda b, c: (0, c)),
            pl.BlockSpec((None, 2, RG_CG), lambda b, c: (b, 0, c)),
        ],
        out_specs=[
            pl.BlockSpec((None, n, RG_CG), lambda b, c: (b, 0, c)),
            pl.BlockSpec((None, 2, RG_CG), lambda b, c: (b, 0, c)),
        ],
        out_shape=[
            jax.ShapeDtypeStruct((bsz, n, rg_w), BF16),
            jax.ShapeDtypeStruct((bsz, 2, rg_w), F32),
        ],
        scratch_shapes=[
            pltpu.VMEM((RG_CG // LANES, (n // RG_SEG + 1) * RG_PITCH, LANES), F32),
            pltpu.VMEM((n, RG_CG), F32),
            pltpu.VMEM((2, n, RG_CG), F32),
            pltpu.VMEM((RG_CG // LANES, SUBLANES * RG_PITCH, LANES), F32),
        ],
        compiler_params=_cparams(("parallel", "parallel")),
        name="rglru",
    )(p, p, conv_w, conv_b.reshape(1, rg_w), wg, bg, lam, h0)


RET_UNROLL = 4


def _retention_kernel(lg_ref, q_ref, k_ref, v_ref, g_ref, cs_ref, sn_ref, s0_ref,
                      o_ref, so_ref, kr_scr, st_scr):
    n, hd = q_ref.shape
    L = RET_CHUNK
    nc = n // L
    head = pl.program_id(1)
    lgf = lg_ref[0, head]
    lgb = lg_ref[1, head]

    def rope(t, rows):
        return t * cs_ref[rows, :] + pltpu.roll(t, hd // 2, axis=1) * sn_ref[rows, :]

    kr_scr[...] = (rope(k_ref[...].astype(F32), slice(None)) * (hd ** -0.5)).astype(BF16)

    jc = lax.broadcasted_iota(jnp.int32, (L, 1), 0).astype(F32)
    q_dec_f = jnp.exp((jc + 1.0) * lgf)
    q_dec_b = jnp.exp((L - jc) * lgb)
    k_dec_f = jnp.exp((L - 1.0 - jc) * lgf)
    k_dec_b = jnp.exp(jc * lgb)
    ones = jnp.ones((1, hd), F32)
    chunk_f = jnp.exp(ones * (L * lgf))
    chunk_b = jnp.exp(ones * (L * lgb))
    ii = lax.broadcasted_iota(jnp.int32, (L, L), 0)
    jj = lax.broadcasted_iota(jnp.int32, (L, L), 1)
    diff = (ii - jj).astype(F32)
    dmat = jnp.where(diff >= 0.0, jnp.exp(jnp.maximum(diff, 0.0) * lgf),
                     jnp.exp(jnp.maximum(-diff, 0.0) * lgb))

    tn_dims = (((0,), (0,)), ((), ()))
    nt_dims = (((1,), (1,)), ((), ()))

    def state_update(c, state, k_dec, chunk_dec):
        t0 = pl.multiple_of(c * L, L)
        kd = (kr_scr[pl.ds(t0, L), :].astype(F32) * k_dec).astype(BF16)
        upd = lax.dot_general(kd, v_ref[pl.ds(t0, L), :], tn_dims, preferred_element_type=F32)
        return chunk_dec * state + upd

    def state_body(i, carry):
        f_state, r_state = carry
        cf = i
        cb = nc - 1 - i
        st_scr[cf, :, 0:hd] = f_state.astype(BF16)
        st_scr[cb, :, hd:2 * hd] = r_state.astype(BF16)
        return (state_update(cf, f_state, k_dec_f, chunk_f),
                state_update(cb, r_state, k_dec_b, chunk_b))

    f_fin, r_fin = lax.fori_loop(0, nc, state_body, (s0_ref[0], s0_ref[1]),
                                 unroll=min(RET_UNROLL, nc))
    so_ref[0] = f_fin
    so_ref[1] = r_fin

    def out_body(c, carry):
        t0 = pl.multiple_of(c * L, L)
        qc = rope(q_ref[pl.ds(t0, L), :].astype(F32), pl.ds(t0, L)).astype(BF16)
        kc = kr_scr[pl.ds(t0, L), :]
        vc = v_ref[pl.ds(t0, L), :]
        s = lax.dot_general(qc, kc, nt_dims, preferred_element_type=F32)
        o = jnp.dot((s * dmat).astype(BF16), vc, preferred_element_type=F32)
        cross = jnp.dot(qc, st_scr[c], preferred_element_type=F32)
        o = o + cross[:, 0:hd] * q_dec_f + cross[:, hd:2 * hd] * q_dec_b
        mu = jnp.mean(o, axis=-1, keepdims=True)
        oc = o - mu
        var = jnp.mean(oc * oc, axis=-1, keepdims=True)
        gate = g_ref[pl.ds(t0, L), :].astype(F32)
        o_ref[pl.ds(t0, L), :] = (gate * jax.nn.sigmoid(gate) * (oc * lax.rsqrt(var + EPS))).astype(o_ref.dtype)
        return carry

    lax.fori_loop(0, nc, out_body, 0, unroll=min(RET_UNROLL, nc))


def _retention(p, log_g, cs, sn, s0, col0):
    bsz, n, _ = p.shape
    nh = log_g.shape[1]
    hd = cs.shape[1]
    cb0 = col0 // hd

    def col(which):
        return pl.BlockSpec((None, n, hd), lambda b, h, lg: (b, 0, cb0 + which * nh + h))

    state = pl.BlockSpec((None, 2, None, hd, hd), lambda b, h, lg: (b, 0, h, 0, 0))
    table = pl.BlockSpec((n, hd), lambda b, h, lg: (0, 0))
    grid_spec = pltpu.PrefetchScalarGridSpec(
        num_scalar_prefetch=1,
        grid=(bsz, nh),
        in_specs=[col(0), col(1), col(2), col(3), table, table, state],
        out_specs=[pl.BlockSpec((None, n, hd), lambda b, h, lg: (b, 0, h)), state],
        scratch_shapes=[
            pltpu.VMEM((n, hd), BF16),
            pltpu.VMEM((n // RET_CHUNK, hd, 2 * hd), BF16),
        ],
    )
    return pl.pallas_call(
        _retention_kernel,
        grid_spec=grid_spec,
        out_shape=[
            jax.ShapeDtypeStruct((bsz, n, nh * hd), BF16),
            jax.ShapeDtypeStruct((bsz, 2, nh, hd, hd), F32),
        ],
        compiler_params=_cparams(("parallel", "parallel")),
        name="retention",
    )(log_g, p, p, p, p, cs, sn, s0)


R_RANK0, R_RANK1, R_E0, R_E1, R_W0, R_W1 = range(6)


def _mix_residual(x_ref, rg_ref, ret_ref, wo_ref, gate_ref, rows=slice(None)):
    rg_w = rg_ref.shape[1]
    y = jnp.dot(rg_ref[rows, :], wo_ref[0:rg_w, :], preferred_element_type=F32)
    y = y + jnp.dot(ret_ref[rows, :], wo_ref[rg_w:, :], preferred_element_type=F32)
    return x_ref[rows, :] + gate_ref[...] * y


ROUTE_ROWS = 512


def _out_proj_kernel(x_ref, rg_ref, ret_ref, wo_ref, gate_ref, g_ref, shift_ref, scale_ref,
                     wr2_ref, br_ref, xo_ref, h_ref, rt_ref, rtt_ref, cnt_ref, carry_scr):
    tm, d = x_ref.shape
    ns = d // LANES
    nr = min(ROUTE_ROWS, tm)

    @pl.when((pl.program_id(0) == 0) & (pl.program_id(1) == 0))
    def _():
        carry_scr[...] = jnp.zeros_like(carry_scr)

    erow = lax.broadcasted_iota(jnp.int32, (SUBLANES, nr), 0).astype(F32)
    ri = lax.broadcasted_iota(jnp.int32, (nr, nr), 0)
    ci = lax.broadcasted_iota(jnp.int32, (nr, nr), 1)
    earlier = jnp.where(ri < ci, 1.0, 0.0).astype(BF16)
    count = carry_scr[:, 0:1]

    for r0 in range(0, tm, nr):
        rows = pl.ds(r0, nr)
        xn = _mix_residual(x_ref, rg_ref, ret_ref, wo_ref, gate_ref, rows)
        xo_ref[rows, :] = xn
        hx = _norm_mod(xn, g_ref[...], shift_ref[...], scale_ref[...])

        for s in range(ns):
            h_ref[pl.ds(r0 * ns + s, nr, stride=ns), :] = hx[:, s * LANES:(s + 1) * LANES]

        h_hi = hx.astype(BF16)
        h_lo = (hx - h_hi.astype(F32)).astype(BF16)
        part = jnp.dot(h_hi, wr2_ref[...], preferred_element_type=F32)
        logits = (part[:, :LANES] + part[:, LANES:]
                  + jnp.dot(h_lo, wr2_ref[:, :LANES], preferred_element_type=F32) + br_ref[...])
        lt = jnp.transpose(logits)[0:SUBLANES, :]
        m0 = jnp.max(lt, axis=0, keepdims=True)
        i0 = jnp.min(jnp.where(lt == m0, erow, float(SUBLANES)), axis=0, keepdims=True)
        rest_l = jnp.where(erow == i0, NEG_BIG, lt)
        m1 = jnp.max(rest_l, axis=0, keepdims=True)
        i1 = jnp.min(jnp.where(rest_l == m1, erow, float(SUBLANES)), axis=0, keepdims=True)
        e = jnp.exp(m1 - m0)
        w0 = 1.0 / (1.0 + e)
        w1 = e / (1.0 + e)

        sel0 = erow == i0
        sel1 = erow == i1
        mask = jnp.where(sel0 | sel1, 1.0, 0.0)
        before = jnp.dot(mask.astype(BF16), earlier, preferred_element_type=F32) + count
        rank0 = jnp.sum(jnp.where(sel0, before, 0.0), axis=0, keepdims=True)
        rank1 = jnp.sum(jnp.where(sel1, before, 0.0), axis=0, keepdims=True)
        count = count + jnp.sum(mask, axis=1, keepdims=True)
        rec = jnp.zeros((SUBLANES, nr), F32)
        for idx, val in enumerate((rank0, rank1, i0, i1, w0, w1)):
            rec = jnp.where(erow == idx, val, rec)
        rtt_ref[:, r0:r0 + nr] = rec
        rt_ref[rows, :] = jnp.transpose(
            jnp.concatenate([rec, jnp.zeros((LANES - SUBLANES, nr), F32)], axis=0))

    carry_scr[...] = jnp.broadcast_to(count, carry_scr.shape)
    on_diag = (lax.broadcasted_iota(jnp.int32, (SUBLANES, LANES), 0)
               == lax.broadcasted_iota(jnp.int32, (SUBLANES, LANES), 1))
    cnt_ref[...] = jnp.sum(jnp.where(on_diag, count, 0.0), axis=0, keepdims=True)


def _out_proj_route(x, rg, ret, w_out, gate, g, shift, scale, wr2, br):
    bsz, n, d = x.shape
    tm = min(ROUTE_ROWS, n)
    vec = pl.BlockSpec((None, 1, d), lambda b, i: (b, 0, 0))
    tile = lambda w: pl.BlockSpec((None, tm, w), lambda b, i: (b, i, 0))
    const = lambda shape: pl.BlockSpec(shape, lambda b, i: (0, 0))
    return pl.pallas_call(
        _out_proj_kernel,
        grid=(bsz, n // tm),
        in_specs=[
            tile(d), tile(rg.shape[2]), tile(ret.shape[2]), const(w_out.shape),
            vec, const((1, d)), vec, vec, const((d, 2 * LANES)), const((1, LANES)),
        ],
        out_specs=[
            tile(d),
            pl.BlockSpec((None, tm * (d // LANES), LANES), lambda b, i: (b, i, 0)),
            tile(LANES),
            pl.BlockSpec((SUBLANES, tm), lambda b, i: (0, b * (n // tm) + i)),
            const((1, LANES)),
        ],
        out_shape=[
            jax.ShapeDtypeStruct((bsz, n, d), F32),
            jax.ShapeDtypeStruct((bsz, n * (d // LANES), LANES), F32),
            jax.ShapeDtypeStruct((bsz, n, LANES), F32),
            jax.ShapeDtypeStruct((SUBLANES, bsz * n), F32),
            jax.ShapeDtypeStruct((1, LANES), F32),
        ],
        scratch_shapes=[pltpu.VMEM((SUBLANES, LANES), F32)],
        compiler_params=_cparams(("arbitrary", "arbitrary")),
        name="out_proj_route",
    )(x, rg, ret, w_out, gate, g, shift, scale, wr2, br)


def _swiglu_acc(h, w1_ref, w3_ref, w2_ref):
    a = jnp.dot(h, w1_ref[...], preferred_element_type=F32)
    b = jnp.dot(h, w3_ref[...], preferred_element_type=F32)
    act = a * jax.nn.sigmoid(a) * b
    return jnp.dot(act.astype(BF16), w2_ref[...], preferred_element_type=F32)


def _mix_ffn_kernel(n_cast, x_ref, rg_ref, ret_ref, wo_ref, gmix_ref, g_ref, shift_ref, scale_ref,
                    gffn_ref, w1_ref, w3_ref, w2_ref, *rest):
    cast_in, o_ref, cast_out = rest[:n_cast], rest[n_cast], rest[n_cast + 1:]
    xn = _mix_residual(x_ref, rg_ref, ret_ref, wo_ref, gmix_ref)
    hx = _norm_mod(xn, g_ref[...], shift_ref[...], scale_ref[...]).astype(BF16)
    o_ref[...] = xn + gffn_ref[...] * _swiglu_acc(hx, w1_ref, w3_ref, w2_ref)
    for src, dst in zip(cast_in, cast_out):
        dst[...] = src[...].astype(dst.dtype)


def _mix_ffn(x, rg, ret, w_out, gate_mix, g, shift, scale, gate_ffn, w1, w3, w2, cast=()):
    bsz, n, d = x.shape
    f = w1.shape[1]
    tm = min(ROW_TILE, n)
    steps = bsz * (n // tm)
    vec = pl.BlockSpec((None, 1, d), lambda b, i: (b, 0, 0))
    tile = lambda w: pl.BlockSpec((None, tm, w), lambda b, i: (b, i, 0))
    resident = lambda shape: pl.BlockSpec(shape, lambda b, i: (0, 0), pipeline_mode=pl.Buffered(1))
    slabs = []
    for a in cast:
        rows, cols = a.shape
        assert rows % (steps * 2 * SUBLANES) == 0
        slabs.append(pl.BlockSpec((rows // steps, cols), lambda b, i: (b * (n // tm) + i, 0)))
    out = pl.pallas_call(
        functools.partial(_mix_ffn_kernel, len(cast)),
        grid=(bsz, n // tm),
        in_specs=[
            tile(d), tile(rg.shape[2]), tile(ret.shape[2]), resident(w_out.shape),
            vec, resident((1, d)), vec, vec, vec,
            resident((d, f)), resident((d, f)), resident((f, d)),
        ] + slabs,
        out_specs=[tile(d)] + slabs,
        out_shape=[jax.ShapeDtypeStruct((bsz, n, d), F32)]
        + [jax.ShapeDtypeStruct(a.shape, BF16) for a in cast],
        compiler_params=_cparams(("parallel", "parallel")),
        name="mix_ffn",
    )(x, rg, ret, w_out, gate_mix, g, shift, scale, gate_ffn, w1, w3, w2, *cast)
    return out[0], tuple(out[1:])


MOE_BLK = 256
MOE_TOKENS = 1024


def _dispatch_kernel(d0_ref, d1_ref, zr_ref, hx_ref, hs_ref, zbuf, sem, zsem):
    tm = hx_ref.shape[0]
    base = pl.program_id(0) * tm

    @pl.when(pl.program_id(0) == 0)
    def _():
        zbuf[...] = jnp.zeros_like(zbuf)
        for z in range(zr_ref.shape[0]):
            @pl.when(zr_ref[z] >= 0)
            def _():
                pltpu.make_async_copy(zbuf, hs_ref.at[pl.ds(zr_ref[z], MOE_BLK)], zsem).start()
        for z in range(zr_ref.shape[0]):
            @pl.when(zr_ref[z] >= 0)
            def _():
                pltpu.make_async_copy(zbuf, hs_ref.at[pl.ds(zr_ref[z], MOE_BLK)], zsem).wait()

    def issue(t, carry):
        pltpu.make_async_copy(hx_ref.at[t], hs_ref.at[d0_ref[base + t]], sem).start(priority=0)
        pltpu.make_async_copy(hx_ref.at[t], hs_ref.at[d1_ref[base + t]], sem).start(priority=1)
        return carry

    lax.fori_loop(0, tm, issue, 0, unroll=8)
    for _ in range(2):
        pltpu.make_async_copy(hx_ref, hs_ref.at[pl.ds(0, tm)], sem).wait()


def _dispatch(dest0, dest1, zero_rows, hx, n_rows):
    n_tok, s, lanes = hx.shape
    tm = min(MOE_TOKENS, n_tok)
    grid_spec = pltpu.PrefetchScalarGridSpec(
        num_scalar_prefetch=3,
        grid=(n_tok // tm,),
        in_specs=[pl.BlockSpec((tm, s, lanes), lambda i, d0, d1, zr: (i, 0, 0))],
        out_specs=pl.BlockSpec(memory_space=pl.ANY),
        scratch_shapes=[pltpu.VMEM((MOE_BLK, s, lanes), F32),
                        pltpu.SemaphoreType.DMA(()), pltpu.SemaphoreType.DMA(())],
    )
    return pl.pallas_call(
        _dispatch_kernel,
        grid_spec=grid_spec,
        out_shape=jax.ShapeDtypeStruct((n_rows, s, lanes), F32),
        compiler_params=_cparams(("arbitrary",)),
        name="moe_dispatch",
    )(dest0, dest1, zero_rows, hx)


def _experts_kernel(be_ref, nu_ref, hs_ref, w1_ref, w3_ref, w2_ref, y_ref):
    j = pl.program_id(0)
    d = w1_ref.shape[0]
    ns = d // LANES
    blk = hs_ref.shape[0] // ns

    @pl.when(j < nu_ref[0])
    def _():
        h = jnp.concatenate(
            [hs_ref[pl.ds(s, blk, stride=ns), :].astype(BF16) for s in range(ns)], axis=1)
        out = _swiglu_acc(h, w1_ref, w3_ref, w2_ref)
        for s in range(ns):
            y_ref[pl.ds(s, blk, stride=ns), :] = out[:, s * LANES:(s + 1) * LANES]

    @pl.when(j >= nu_ref[0])
    def _():
        y_ref[...] = jnp.zeros_like(y_ref)


def _experts(block_expert, n_used, hs, w1, w3, w2):
    ne, d, f = w1.shape
    s = d // LANES
    n_rows = hs.shape[0] // s
    n_blocks = n_rows // MOE_BLK

    def expert(shape):
        return pl.BlockSpec((None,) + shape, lambda j, be, nu: (be[j], 0, 0),
                            pipeline_mode=pl.Buffered(1))

    grid_spec = pltpu.PrefetchScalarGridSpec(
        num_scalar_prefetch=2,
        grid=(n_blocks,),
        in_specs=[
            pl.BlockSpec((MOE_BLK * s, LANES), lambda j, be, nu: (jnp.minimum(j, nu[0] - 1), 0)),
            expert((d, f)), expert((d, f)), expert((f, d)),
        ],
        out_specs=pl.BlockSpec((MOE_BLK * s, LANES), lambda j, be, nu: (j, 0)),
    )
    return pl.pallas_call(
        _experts_kernel,
        grid_spec=grid_spec,
        out_shape=jax.ShapeDtypeStruct((n_rows * s, LANES), F32),
        compiler_params=_cparams(("arbitrary",)),
        name="moe_experts",
    )(block_expert, n_used, hs, w1, w3, w2)


def _combine_kernel(d0_ref, d1_ref, x_ref, gate_ref, rt_ref, gfin_ref, y_ref, o_ref, ybuf, sem):
    tm, d = x_ref.shape
    ns = d // LANES
    step = pl.program_id(0) * pl.num_programs(1) + pl.program_id(1)
    nsteps = pl.num_programs(0) * pl.num_programs(1)
    slot = step % 2

    def row(ref, r):
        return ref.at[pl.ds(pl.multiple_of(r * ns, ns), ns)]

    def start_gather(block, into):
        base = block * tm

        def issue(t, carry):
            pltpu.make_async_copy(row(y_ref, d0_ref[base + t]), row(ybuf.at[into, 0], t),
                                  sem.at[into]).start(priority=0)
            pltpu.make_async_copy(row(y_ref, d1_ref[base + t]), row(ybuf.at[into, 1], t),
                                  sem.at[into]).start(priority=1)
            return carry

        lax.fori_loop(0, tm, issue, 0, unroll=8)

    @pl.when(step == 0)
    def _():
        start_gather(0, 0)

    @pl.when(step + 1 < nsteps)
    def _():
        start_gather(step + 1, 1 - slot)

    for k in range(2):
        pltpu.make_async_copy(y_ref.at[pl.ds(0, tm * ns)], ybuf.at[slot, k], sem.at[slot]).wait()

    chunk = 64
    for r0 in range(0, tm, chunk):
        rows = pl.ds(r0, chunk)
        rt = rt_ref[rows, :]
        w0 = rt[:, R_W0:R_W0 + 1]
        w1 = rt[:, R_W1:R_W1 + 1]
        ss = jnp.zeros((chunk, 1), F32)
        for s in range(ns):
            cols = slice(s * LANES, (s + 1) * LANES)
            moe = (w0 * ybuf[slot, 0, pl.ds(r0 * ns + s, chunk, stride=ns), :]
                   + w1 * ybuf[slot, 1, pl.ds(r0 * ns + s, chunk, stride=ns), :])
            xn = x_ref[rows, cols] + gate_ref[:, cols] * moe
            ss = ss + jnp.sum(xn * xn, axis=-1, keepdims=True)
            o_ref[rows, cols] = xn
        o_ref[rows, :] = o_ref[rows, :] * lax.rsqrt(ss / d + EPS) * gfin_ref[...]


def _combine(dest0, dest1, x, gate, route, g_final, y):
    bsz, n, d = x.shape
    s = d // LANES
    tm = min(MOE_TOKENS // 2, n)
    grid_spec = pltpu.PrefetchScalarGridSpec(
        num_scalar_prefetch=2,
        grid=(bsz, n // tm),
        in_specs=[
            pl.BlockSpec((None, tm, d), lambda b, i, d0, d1: (b, i, 0)),
            pl.BlockSpec((None, 1, d), lambda b, i, d0, d1: (b, 0, 0)),
            pl.BlockSpec((None, tm, LANES), lambda b, i, d0, d1: (b, i, 0)),
            pl.BlockSpec((1, d), lambda b, i, d0, d1: (0, 0)),
            pl.BlockSpec(memory_space=pl.ANY),
        ],
        out_specs=pl.BlockSpec((None, tm, d), lambda b, i, d0, d1: (b, i, 0)),
        scratch_shapes=[pltpu.VMEM((2, 2, tm * s, LANES), F32), pltpu.SemaphoreType.DMA((2,))],
    )
    return pl.pallas_call(
        _combine_kernel,
        grid_spec=grid_spec,
        out_shape=jax.ShapeDtypeStruct((bsz, n, d), F32),
        compiler_params=_cparams(("arbitrary", "arbitrary")),
        name="moe_combine",
    )(dest0, dest1, x, gate, route, g_final, y)


def _moe(hx, x, gate, route, route_t, counts, g_final, w1, w3, w2):
    bsz, n, d = x.shape
    ne = w1.shape[0]
    n_tok = bsz * n
    n_rows = -(-(2 * n_tok) // MOE_BLK) * MOE_BLK + ne * MOE_BLK
    n_blocks = n_rows // MOE_BLK
    cnt = counts[0, :ne].astype(jnp.int32)
    padded = (cnt + MOE_BLK - 1) // MOE_BLK * MOE_BLK
    pad_end = jnp.cumsum(padded)
    pad_start = pad_end - padded
    n_used = pad_end[-1] // MOE_BLK
    blk = jnp.minimum(jnp.arange(n_blocks, dtype=jnp.int32), n_used - 1)
    block_expert = jnp.minimum(
        jnp.sum((blk[:, None] * MOE_BLK >= pad_end[None, :]).astype(jnp.int32), axis=1), ne - 1)
    def dest(e_row, rank_row):
        e = route_t[e_row].astype(jnp.int32)
        start = sum(jnp.where(e == k, pad_start[k], 0) for k in range(ne))
        return start + route_t[rank_row].astype(jnp.int32)

    dest0 = dest(R_E0, R_RANK0)
    dest1 = dest(R_E1, R_RANK1)

    last_blk = jnp.where(padded > 0, pad_end - MOE_BLK, -1)
    trail = n_used + jnp.arange(ne, dtype=jnp.int32)
    trail = jnp.where(trail < n_blocks, trail * MOE_BLK, -1)
    zero_rows = jnp.concatenate([last_blk, trail]).astype(jnp.int32)

    ns = d // LANES
    hs = _dispatch(dest0, dest1, zero_rows, hx.reshape(n_tok, ns, LANES), n_rows)
    y = _experts(block_expert, n_used.reshape(1).astype(jnp.int32),
                 hs.reshape(n_rows * ns, LANES), w1, w3, w2)
    return _combine(dest0, dest1, x, gate, route, g_final, y)


def _rope_tables(n, hd):
    rows = n // GRID_W
    n_freq = hd // 4
    inv = ROPE_BASE ** (-jnp.arange(n_freq, dtype=F32) / n_freq)
    row_ang = jnp.arange(rows, dtype=F32)[:, None] * inv
    col_ang = jnp.arange(GRID_W, dtype=F32)[:, None] * inv

    def table(of_row, of_col):
        a = jnp.broadcast_to(of_row[:, None, :], (rows, GRID_W, n_freq))
        b = jnp.broadcast_to(of_col[None, :, :], (rows, GRID_W, n_freq))
        return jnp.concatenate([a, b], axis=-1).reshape(n, 2 * n_freq)

    cos = table(jnp.cos(row_ang), jnp.cos(col_ang))
    sin = table(jnp.sin(row_ang), jnp.sin(col_ang))
    return jnp.concatenate([cos, cos], axis=-1), jnp.concatenate([-sin, sin], axis=-1)


def _block_diag_gates(rg_wa, rg_wx):
    w = jnp.stack([rg_wa, rg_wx], axis=1)
    nd, ng, nk, c, _ = w.shape
    per = RG_CG // c
    w = w.reshape(nd, ng, nk // per, per, c, c)
    eye = jnp.eye(per, dtype=w.dtype)
    full = jnp.einsum('dgmpij,pq->dgmpiqj', w, eye)
    return full.reshape(nd, ng, nk // per, RG_CG, RG_CG).astype(BF16)


def _mixer(p, rope, states, conv_w, conv_b, wg, bg, lam, log_g, rg_w):
    rg_out, h_last = _rglru(p, conv_w, conv_b, wg, bg, lam, states[0])
    ret_out, s_last = _retention(p, log_g, rope[0], rope[1], states[1], 2 * rg_w)
    return rg_out, ret_out, (h_last, s_last)


def kernel(x, c, ctx, c_ctx, w_mod, b_mod, g_mix, g_ffn, g_final, w_in, w_out, conv_w, conv_b,
           rg_wa, rg_ba, rg_wx, rg_bx, rg_lam, ret_decay, ffn_w1, ffn_w3, ffn_w2,
           moe_router, moe_router_b, moe_w1, moe_w3, moe_w2):
    bsz, n_lat, d = x.shape
    n_ctx = ctx.shape[1]
    depth = w_mod.shape[0]
    rg_w = rg_lam.shape[2]
    nh = ret_decay.shape[2]
    hd = (w_out.shape[1] - rg_w) // nh
    ne = moe_router.shape[2]
    assert depth == 2, "kernel is written for the two-layer block (dense FFN, then MoE)"

    rows = -(-(bsz + 1) // SUBLANES) * SUBLANES
    c_all = jnp.zeros((rows, d), F32).at[:bsz].set(c).at[bsz].set(c_ctx)
    mod = _modulation(c_all, w_mod, b_mod)
    mod = mod.reshape(depth, rows, N_MOD, d)
    mod_lat = mod[:, :bsz, None]
    mod_ctx = jnp.broadcast_to(mod[:, bsz:bsz + 1, None], mod_lat.shape)

    rope_lat = _rope_tables(n_lat, hd)
    rope_ctx = (jnp.ones((n_ctx, hd), F32), jnp.zeros((n_ctx, hd), F32))
    log_g = jax.nn.log_sigmoid(ret_decay.astype(F32))
    zero_states = (jnp.zeros((bsz, 2, rg_w), F32), jnp.zeros((bsz, 2, nh, hd, hd), F32))

    for l in range(depth):
        last = l == depth - 1
        ml = [mod_lat[l, :, :, j] for j in range(N_MOD)]
        mc = [mod_ctx[l, :, :, j] for j in range(N_MOD)]
        g_m = g_mix[l].reshape(1, d)
        g_f = g_ffn[l].reshape(1, d)
        w_in_l = w_in[l].astype(BF16)
        w_out_l = w_out[l].astype(BF16)
        wg = _block_diag_gates(0.5 * rg_wa[l], 0.5 * rg_wx[l])
        bg = 0.5 * jnp.stack([rg_ba[l], rg_bx[l]], axis=1)
        mix_p = (conv_w[l], conv_b[l], wg, bg, rg_lam[l], log_g[l], rg_w)

        pc = _in_proj(ctx, g_m, mc[0], mc[1], w_in_l)
        rg_c, ret_c, ctx_states = _mixer(pc, rope_ctx, zero_states, *mix_p)
        px = _in_proj(x, g_m, ml[0], ml[1], w_in_l)
        rg_x, ret_x, _ = _mixer(px, rope_lat, ctx_states, *mix_p)

        if l % 2 == 0:
            i = l // 2
            w1, w3, w2 = ffn_w1[i].astype(BF16), ffn_w3[i].astype(BF16), ffn_w2[i].astype(BF16)
            nxt = moe_w1[i], moe_w3[i], moe_w2[i]
            x, cast = _mix_ffn(x, rg_x, ret_x, w_out_l, ml[2], g_f, ml[3], ml[4], ml[5],
                               w1, w3, w2, cast=[w.reshape(-1, w.shape[-1]) for w in nxt])
            moe_w = [c.reshape(w.shape) for c, w in zip(cast, nxt)]
            if not last:
                ctx, _ = _mix_ffn(ctx, rg_c, ret_c, w_out_l, mc[2], g_f, mc[3], mc[4], mc[5],
                                  w1, w3, w2)
        else:
            i = l // 2
            assert ne <= SUBLANES
            wr = jnp.zeros((d, LANES), F32).at[:, :ne].set(moe_router[i])
            wr_hi = wr.astype(BF16)
            wr_lo = (wr - wr_hi.astype(F32)).astype(BF16)
            wr2 = jnp.concatenate([wr_hi, wr_lo], axis=1)
            br = jnp.full((1, LANES), NEG_BIG, F32).at[0, :ne].set(moe_router_b[i])
            x, hx, route, route_t, counts = _out_proj_route(
                x, rg_x, ret_x, w_out_l, ml[2], g_f, ml[3], ml[4], wr2, br)
            x = _moe(hx, x, ml[5], route, route_t, counts, g_final.reshape(1, d), *moe_w)
    return x
```

```python
import functools

import jax
import jax.numpy as jnp
from jax import lax
from jax.experimental import pallas as pl
from jax.experimental.pallas import tpu as pltpu

F32 = jnp.float32
BF16 = jnp.bfloat16

EPS = 1e-6
RG_C = 8.0
CONV_W = 4
CONV_LEFT = 2
RET_CHUNK = 256
ROPE_BASE = 10000.0
GRID_W = 64
N_MOD = 6

ROW_TILE = 512

LANES = 128
SUBLANES = 8
VMEM_LIMIT = 56 * 1024 * 1024
NEG_BIG = -1e30
LOG2_E = 1.4426950408889634


def _cparams(sem):
    return pltpu.CompilerParams(dimension_semantics=sem, vmem_limit_bytes=VMEM_LIMIT)


def _modulation_kernel(c_ref, w_ref, b_ref, o_ref):
    c = c_ref[...]
    s = c * jax.nn.sigmoid(c)
    o_ref[...] = jnp.dot(s, w_ref[...], precision=lax.Precision.HIGHEST,
                         preferred_element_type=F32) + b_ref[...]


def _modulation(c_all, w_mod, b_mod):
    depth, d, six_d = w_mod.shape
    rows = c_all.shape[0]
    tn = six_d // 4
    return pl.pallas_call(
        _modulation_kernel,
        grid=(depth, six_d // tn),
        in_specs=[
            pl.BlockSpec((rows, d), lambda l, j: (0, 0)),
            pl.BlockSpec((None, d, tn), lambda l, j: (l, 0, j)),
            pl.BlockSpec((None, 1, tn), lambda l, j: (l, 0, j)),
        ],
        out_specs=pl.BlockSpec((None, rows, tn), lambda l, j: (l, 0, j)),
        out_shape=jax.ShapeDtypeStruct((depth, rows, six_d), F32),
        compiler_params=_cparams(("arbitrary", "arbitrary")),
        name="modulation",
    )(c_all, w_mod, b_mod.reshape(depth, 1, six_d))


def _norm_mod(x, g, shift, scale):
    ms = jnp.mean(x * x, axis=-1, keepdims=True)
    y = x * lax.rsqrt(ms + EPS) * g
    return y * (1.0 + scale) + shift


def _in_proj_kernel(x_ref, g_ref, shift_ref, scale_ref, w_ref, o_ref):
    h = _norm_mod(x_ref[...], g_ref[...], shift_ref[...], scale_ref[...])
    o_ref[...] = jnp.dot(h.astype(BF16), w_ref[...],
                         preferred_element_type=F32).astype(o_ref.dtype)


def _in_proj(x, g, shift, scale, w):
    bsz, n, d = x.shape
    f = w.shape[1]
    tm = min(ROW_TILE, n)
    vec = pl.BlockSpec((None, 1, d), lambda b, i: (b, 0, 0))
    return pl.pallas_call(
        _in_proj_kernel,
        grid=(bsz, n // tm),
        in_specs=[
            pl.BlockSpec((None, tm, d), lambda b, i: (b, i, 0)),
            pl.BlockSpec((1, d), lambda b, i: (0, 0)),
            vec, vec,
            pl.BlockSpec((d, f), lambda b, i: (0, 0)),
        ],
        out_specs=pl.BlockSpec((None, tm, f), lambda b, i: (b, i, 0)),
        out_shape=jax.ShapeDtypeStruct((bsz, n, f), BF16),
        compiler_params=_cparams(("parallel", "parallel")),
        name="in_proj",
    )(x, g, shift, scale, w)


RG_TILE = 256
RG_CG = 256
RG_SEG = RG_TILE // SUBLANES
RG_PITCH = RG_SEG + SUBLANES


def _rglru_kernel(u_ref, yg_ref, cw_ref, cb_ref, wg_ref, bg_ref, lam_ref, h0_ref,
                  o_ref, hl_ref, u_scr, hf_scr, ab_scr, nat_scr):
    n = u_ref.shape[0]
    tt = RG_TILE
    nt = n // tt
    nseg = n // RG_SEG
    cg = u_ref.shape[1]
    lane_cols = [slice(g * LANES, (g + 1) * LANES) for g in range(cg // LANES)]

    zeros = jnp.zeros((RG_PITCH, LANES), F32)
    for g, cols in enumerate(lane_cols):
        u_scr[g, pl.ds(0, RG_PITCH), :] = zeros
        u_scr[g, pl.ds(nseg * RG_PITCH + RG_SEG, SUBLANES), :] = zeros[0:SUBLANES]

    def fill(s, carry):
        src = pl.multiple_of(s * RG_SEG, RG_SEG)
        dst = pl.multiple_of((s + 1) * RG_PITCH, SUBLANES)
        for g, cols in enumerate(lane_cols):
            rows = u_ref[pl.ds(src, RG_SEG), cols].astype(F32)
            u_scr[g, pl.ds(dst, RG_SEG), :] = rows
            u_scr[g, pl.ds(dst - SUBLANES, SUBLANES), :] = rows[0:SUBLANES]
        return carry

    lax.fori_loop(0, nseg, fill, 0)

    seg = RG_SEG
    cw_rows = [[jnp.broadcast_to(cw_ref[k:k + 1, c], (SUBLANES, LANES)) for c in lane_cols]
               for k in range(CONV_W)]
    cb_rows = [jnp.broadcast_to(cb_ref[:, c], (SUBLANES, LANES)) for c in lane_cols]

    def conv_tile(i):
        base = pl.multiple_of((i * SUBLANES + 1) * RG_PITCH, SUBLANES)
        blocks = []
        for j in range(seg):
            lane_groups = []
            for g in range(cg // LANES):
                acc = cb_rows[g]
                for k in range(CONV_W):
                    q = j + k - CONV_LEFT
                    start = base + q if q >= 0 else base - RG_PITCH + seg + q
                    acc = acc + cw_rows[k][g] * u_scr[g, pl.ds(start, SUBLANES, stride=RG_PITCH), :]
                lane_groups.append(acc)
            blocks.append(jnp.concatenate(lane_groups, axis=1))
        return jnp.concatenate(blocks, axis=0)

    def gates(uc, d):
        ub = uc.astype(BF16)
        ta = jnp.tanh(jnp.dot(ub, wg_ref[d, 0], preferred_element_type=F32) + bg_ref[d, 0:1, :])
        ti = jnp.tanh(jnp.dot(ub, wg_ref[d, 1], preferred_element_type=F32) + bg_ref[d, 1:2, :])
        z = -lam_ref[d:d + 1, :]
        sp = jnp.maximum(z, 0.0) + jnp.log1p(jnp.exp(-jnp.abs(z)))
        c2 = (-0.5 * RG_C * LOG2_E) * sp
        a = jnp.exp2(c2 + c2 * ta)
        b = jnp.exp2((0.5 * LOG2_E) * jnp.log(1.0 - a * a)) * ((0.5 + 0.5 * ti) * uc)
        return a, b

    def scan_tile(a, b, carry, reverse):
        steps = range(seg - 1, -1, -1) if reverse else range(seg)
        h = acum = None
        h_loc = [None] * seg
        a_cum = [None] * seg
        for j in steps:
            rows = slice(j * SUBLANES, (j + 1) * SUBLANES)
            h = b[rows] if h is None else a[rows] * h + b[rows]
            acum = a[rows] if acum is None else a[rows] * acum
            h_loc[j], a_cum[j] = h, acum
        order = range(SUBLANES - 1, -1, -1) if reverse else range(SUBLANES)
        c = carry
        c_in = [None] * SUBLANES
        for s in order:
            c_in[s] = c
            c = h[s:s + 1, :] + acum[s:s + 1, :] * c
        c_in = jnp.concatenate(c_in, axis=0)
        out = jnp.concatenate([h_loc[j] + a_cum[j] * c_in for j in range(seg)], axis=0)
        return out, c

    def fwd_body(i, carry):
        t0 = pl.multiple_of(i * tt, tt)
        uc = conv_tile(i)
        a, b = gates(uc, 0)
        h, carry = scan_tile(a, b, carry, False)
        hf_scr[pl.ds(t0, tt), :] = h
        a, b = gates(uc, 1)
        ab_scr[0, pl.ds(t0, tt), :] = a
        ab_scr[1, pl.ds(t0, tt), :] = b
        return carry

    hf_last = lax.fori_loop(0, nt, fwd_body, h0_ref[0:1, :], unroll=min(2, nt))

    def bwd_body(i, carry):
        tile = nt - 1 - i
        t0 = pl.multiple_of(tile * tt, tt)
        h, carry = scan_tile(ab_scr[0, pl.ds(t0, tt), :], ab_scr[1, pl.ds(t0, tt), :], carry, True)
        hsum = hf_scr[pl.ds(t0, tt), :] + h
        for j in range(seg):
            for g, cols in enumerate(lane_cols):
                nat_scr[g, pl.ds(j, SUBLANES, stride=RG_PITCH), :] = (
                    hsum[j * SUBLANES:(j + 1) * SUBLANES, cols])
        hnat = jnp.concatenate(
            [jnp.concatenate([nat_scr[g, pl.ds(s * RG_PITCH, seg), :] for s in range(SUBLANES)],
                             axis=0) for g in range(len(lane_cols))], axis=1)
        yg = yg_ref[pl.ds(t0, tt), :].astype(F32)
        o_ref[pl.ds(t0, tt), :] = (jax.nn.gelu(yg) * hnat).astype(o_ref.dtype)
        return carry

    hb_last = lax.fori_loop(0, nt, bwd_body, h0_ref[1:2, :], unroll=min(2, nt))
    hl_ref[0:1, :] = hf_last
    hl_ref[1:2, :] = hb_last


def _rglru(p, conv_w, conv_b, wg, bg, lam, h0):
    bsz, n, _ = p.shape
    rg_w = lam.shape[1]
    ncg = rg_w // RG_CG
    assert n % RG_TILE == 0 and rg_w % RG_CG == 0
    return pl.pallas_call(
        _rglru_kernel,
        grid=(bsz, ncg),
        in_specs=[
            pl.BlockSpec((None, n, RG_CG), lambda b, c: (b, 0, c)),
            pl.BlockSpec((None, n, RG_CG), lambda b, c: (b, 0, ncg + c)),
            pl.BlockSpec((CONV_W, RG_CG), lambda b, c: (0, c)),
            pl.BlockSpec((1, RG_CG), lambda b, c: (0, c)),
            pl.BlockSpec((2, 2, None, RG_CG, RG_CG), lambda b, c: (0, 0, c, 0, 0)),
            pl.BlockSpec((2, 2, RG_CG), lambda b, c: (0, 0, c)),
            pl.BlockSpec((2, RG_CG), lambda b, c: (0, c)),
            pl.BlockSpec((None, 2, RG_CG), lambda b, c: (b, 0, c)),
        ],
        out_specs=[
            pl.BlockSpec((None, n, RG_CG), lambda b, c: (b, 0, c)),
            pl.BlockSpec((None, 2, RG_CG), lambda b, c: (b, 0, c)),
        ],
        out_shape=[
            jax.ShapeDtypeStruct((bsz, n, rg_w), BF16),
            jax.ShapeDtypeStruct((bsz, 2, rg_w), F32),
        ],
        scratch_shapes=[
            pltpu.VMEM((RG_CG // LANES, (n // RG_SEG + 1) * RG_PITCH, LANES), F32),
            pltpu.VMEM((n, RG_CG), F32),
            pltpu.VMEM((2, n, RG_CG), F32),
            pltpu.VMEM((RG_CG // LANES, SUBLANES * RG_PITCH, LANES), F32),
        ],
        compiler_params=_cparams(("parallel", "parallel")),
        name="rglru",
    )(p, p, conv_w, conv_b.reshape(1, rg_w), wg, bg, lam, h0)


RET_UNROLL = 4


def _retention_kernel(lg_ref, q_ref, k_ref, v_ref, g_ref, cs_ref, sn_ref, s0_ref,
                      o_ref, so_ref, kr_scr, st_scr):
    n, hd = q_ref.shape
    L = RET_CHUNK
    nc = n // L
    head = pl.program_id(1)
    lgf = lg_ref[0, head]
    lgb = lg_ref[1, head]

    def rope(t, rows):
        return t * cs_ref[rows, :] + pltpu.roll(t, hd // 2, axis=1) * sn_ref[rows, :]

    kr_scr[...] = (rope(k_ref[...].astype(F32), slice(None)) * (hd ** -0.5)).astype(BF16)

    jc = lax.broadcasted_iota(jnp.int32, (L, 1), 0).astype(F32)
    q_dec_f = jnp.exp((jc + 1.0) * lgf)
    q_dec_b = jnp.exp((L - jc) * lgb)
    k_dec_f = jnp.exp((L - 1.0 - jc) * lgf)
    k_dec_b = jnp.exp(jc * lgb)
    ones = jnp.ones((1, hd), F32)
    chunk_f = jnp.exp(ones * (L * lgf))
    chunk_b = jnp.exp(ones * (L * lgb))
    ii = lax.broadcasted_iota(jnp.int32, (L, L), 0)
    jj = lax.broadcasted_iota(jnp.int32, (L, L), 1)
    diff = (ii - jj).astype(F32)
    dmat = jnp.where(diff >= 0.0, jnp.exp(jnp.maximum(diff, 0.0) * lgf),
                     jnp.exp(jnp.maximum(-diff, 0.0) * lgb))

    tn_dims = (((0,), (0,)), ((), ()))
    nt_dims = (((1,), (1,)), ((), ()))

    def state_update(c, state, k_dec, chunk_dec):
        t0 = pl.multiple_of(c * L, L)
        kd = (kr_scr[pl.ds(t0, L), :].astype(F32) * k_dec).astype(BF16)
        upd = lax.dot_general(kd, v_ref[pl.ds(t0, L), :], tn_dims, preferred_element_type=F32)
        return chunk_dec * state + upd

    def state_body(i, carry):
        f_state, r_state = carry
        cf = i
        cb = nc - 1 - i
        st_scr[cf, :, 0:hd] = f_state.astype(BF16)
        st_scr[cb, :, hd:2 * hd] = r_state.astype(BF16)
        return (state_update(cf, f_state, k_dec_f, chunk_f),
                state_update(cb, r_state, k_dec_b, chunk_b))

    f_fin, r_fin = lax.fori_loop(0, nc, state_body, (s0_ref[0], s0_ref[1]),
                                 unroll=min(RET_UNROLL, nc))
    so_ref[0] = f_fin
    so_ref[1] = r_fin

    def out_body(c, carry):
        t0 = pl.multiple_of(c * L, L)
        qc = rope(q_ref[pl.ds(t0, L), :].astype(F32), pl.ds(t0, L)).astype(BF16)
        kc = kr_scr[pl.ds(t0, L), :]
        vc = v_ref[pl.ds(t0, L), :]
        s = lax.dot_general(qc, kc, nt_dims, preferred_element_type=F32)
        o = jnp.dot((s * dmat).astype(BF16), vc, preferred_element_type=F32)
        cross = jnp.dot(qc, st_scr[c], preferred_element_type=F32)
        o = o + cross[:, 0:hd] * q_dec_f + cross[:, hd:2 * hd] * q_dec_b
        mu = jnp.mean(o, axis=-1, keepdims=True)
        oc = o - mu
        var = jnp.mean(oc * oc, axis=-1, keepdims=True)
        gate = g_ref[pl.ds(t0, L), :].astype(F32)
        o_ref[pl.ds(t0, L), :] = (gate * jax.nn.sigmoid(gate) * (oc * lax.rsqrt(var + EPS))).astype(o_ref.dtype)
        return carry

    lax.fori_loop(0, nc, out_body, 0, unroll=min(RET_UNROLL, nc))


def _retention(p, log_g, cs, sn, s0, col0):
    bsz, n, _ = p.shape
    nh = log_g.shape[1]
    hd = cs.shape[1]
    cb0 = col0 // hd

    def col(which):
        return pl.BlockSpec((None, n, hd), lambda b, h, lg: (b, 0, cb0 + which * nh + h))

    state = pl.BlockSpec((None, 2, None, hd, hd), lambda b, h, lg: (b, 0, h, 0, 0))
    table = pl.BlockSpec((n, hd), lambda b, h, lg: (0, 0))
    grid_spec = pltpu.PrefetchScalarGridSpec(
        num_scalar_prefetch=1,
        grid=(bsz, nh),
        in_specs=[col(0), col(1), col(2), col(3), table, table, state],
        out_specs=[pl.BlockSpec((None, n, hd), lambda b, h, lg: (b, 0, h)), state],
        scratch_shapes=[
            pltpu.VMEM((n, hd), BF16),
            pltpu.VMEM((n // RET_CHUNK, hd, 2 * hd), BF16),
        ],
    )
    return pl.pallas_call(
        _retention_kernel,
        grid_spec=grid_spec,
        out_shape=[
            jax.ShapeDtypeStruct((bsz, n, nh * hd), BF16),
            jax.ShapeDtypeStruct((bsz, 2, nh, hd, hd), F32),
        ],
        compiler_params=_cparams(("parallel", "parallel")),
        name="retention",
    )(log_g, p, p, p, p, cs, sn, s0)


R_RANK0, R_RANK1, R_E0, R_E1, R_W0, R_W1 = range(6)


def _mix_residual(x_ref, rg_ref, ret_ref, wo_ref, gate_ref, rows=slice(None)):
    rg_w = rg_ref.shape[1]
    y = jnp.dot(rg_ref[rows, :], wo_ref[0:rg_w, :], preferred_element_type=F32)
    y = y + jnp.dot(ret_ref[rows, :], wo_ref[rg_w:, :], preferred_element_type=F32)
    return x_ref[rows, :] + gate_ref[...] * y


ROUTE_ROWS = 512


def _out_proj_kernel(x_ref, rg_ref, ret_ref, wo_ref, gate_ref, g_ref, shift_ref, scale_ref,
                     wr2_ref, br_ref, xo_ref, h_ref, rt_ref, rtt_ref, cnt_ref, carry_scr):
    tm, d = x_ref.shape
    ns = d // LANES
    nr = min(ROUTE_ROWS, tm)

    @pl.when((pl.program_id(0) == 0) & (pl.program_id(1) == 0))
    def _():
        carry_scr[...] = jnp.zeros_like(carry_scr)

    erow = lax.broadcasted_iota(jnp.int32, (SUBLANES, nr), 0).astype(F32)
    ri = lax.broadcasted_iota(jnp.int32, (nr, nr), 0)
    ci = lax.broadcasted_iota(jnp.int32, (nr, nr), 1)
    earlier = jnp.where(ri < ci, 1.0, 0.0).astype(BF16)
    count = carry_scr[:, 0:1]

    for r0 in range(0, tm, nr):
        rows = pl.ds(r0, nr)
        xn = _mix_residual(x_ref, rg_ref, ret_ref, wo_ref, gate_ref, rows)
        xo_ref[rows, :] = xn
        hx = _norm_mod(xn, g_ref[...], shift_ref[...], scale_ref[...])

        for s in range(ns):
            h_ref[pl.ds(r0 * ns + s, nr, stride=ns), :] = hx[:, s * LANES:(s + 1) * LANES]

        h_hi = hx.astype(BF16)
        h_lo = (hx - h_hi.astype(F32)).astype(BF16)
        part = jnp.dot(h_hi, wr2_ref[...], preferred_element_type=F32)
        logits = (part[:, :LANES] + part[:, LANES:]
                  + jnp.dot(h_lo, wr2_ref[:, :LANES], preferred_element_type=F32) + br_ref[...])
        lt = jnp.transpose(logits)[0:SUBLANES, :]
        m0 = jnp.max(lt, axis=0, keepdims=True)
        i0 = jnp.min(jnp.where(lt == m0, erow, float(SUBLANES)), axis=0, keepdims=True)
        rest_l = jnp.where(erow == i0, NEG_BIG, lt)
        m1 = jnp.max(rest_l, axis=0, keepdims=True)
        i1 = jnp.min(jnp.where(rest_l == m1, erow, float(SUBLANES)), axis=0, keepdims=True)
        e = jnp.exp(m1 - m0)
        w0 = 1.0 / (1.0 + e)
        w1 = e / (1.0 + e)

        sel0 = erow == i0
        sel1 = erow == i1
        mask = jnp.where(sel0 | sel1, 1.0, 0.0)
        before = jnp.dot(mask.astype(BF16), earlier, preferred_element_type=F32) + count
        rank0 = jnp.sum(jnp.where(sel0, before, 0.0), axis=0, keepdims=True)
        rank1 = jnp.sum(jnp.where(sel1, before, 0.0), axis=0, keepdims=True)
        count = count + jnp.sum(mask, axis=1, keepdims=True)
        rec = jnp.zeros((SUBLANES, nr), F32)
        for idx, val in enumerate((rank0, rank1, i0, i1, w0, w1)):
            rec = jnp.where(erow == idx, val, rec)
        rtt_ref[:, r0:r0 + nr] = rec
        rt_ref[rows, :] = jnp.transpose(
            jnp.concatenate([rec, jnp.zeros((LANES - SUBLANES, nr), F32)], axis=0))

    carry_scr[...] = jnp.broadcast_to(count, carry_scr.shape)
    on_diag = (lax.broadcasted_iota(jnp.int32, (SUBLANES, LANES), 0)
               == lax.broadcasted_iota(jnp.int32, (SUBLANES, LANES), 1))
    cnt_ref[...] = jnp.sum(jnp.where(on_diag, count, 0.0), axis=0, keepdims=True)


def _out_proj_route(x, rg, ret, w_out, gate, g, shift, scale, wr2, br):
    bsz, n, d = x.shape
    tm = min(ROUTE_ROWS, n)
    vec = pl.BlockSpec((None, 1, d), lambda b, i: (b, 0, 0))
    tile = lambda w: pl.BlockSpec((None, tm, w), lambda b, i: (b, i, 0))
    const = lambda shape: pl.BlockSpec(shape, lambda b, i: (0, 0))
    return pl.pallas_call(
        _out_proj_kernel,
        grid=(bsz, n // tm),
        in_specs=[
            tile(d), tile(rg.shape[2]), tile(ret.shape[2]), const(w_out.shape),
            vec, const((1, d)), vec, vec, const((d, 2 * LANES)), const((1, LANES)),
        ],
        out_specs=[
            tile(d),
            pl.BlockSpec((None, tm * (d // LANES), LANES), lambda b, i: (b, i, 0)),
            tile(LANES),
            pl.BlockSpec((SUBLANES, tm), lambda b, i: (0, b * (n // tm) + i)),
            const((1, LANES)),
        ],
        out_shape=[
            jax.ShapeDtypeStruct((bsz, n, d), F32),
            jax.ShapeDtypeStruct((bsz, n * (d // LANES), LANES), F32),
            jax.ShapeDtypeStruct((bsz, n, LANES), F32),
            jax.ShapeDtypeStruct((SUBLANES, bsz * n), F32),
            jax.ShapeDtypeStruct((1, LANES), F32),
        ],
        scratch_shapes=[pltpu.VMEM((SUBLANES, LANES), F32)],
        compiler_params=_cparams(("arbitrary", "arbitrary")),
        name="out_proj_route",
    )(x, rg, ret, w_out, gate, g, shift, scale, wr2, br)


def _swiglu_acc(h, w1_ref, w3_ref, w2_ref):
    a = jnp.dot(h, w1_ref[...], preferred_element_type=F32)
    b = jnp.dot(h, w3_ref[...], preferred_element_type=F32)
    act = a * jax.nn.sigmoid(a) * b
    return jnp.dot(act.astype(BF16), w2_ref[...], preferred_element_type=F32)


def _mix_ffn_kernel(n_cast, x_ref, rg_ref, ret_ref, wo_ref, gmix_ref, g_ref, shift_ref, scale_ref,
                    gffn_ref, w1_ref, w3_ref, w2_ref, *rest):
    cast_in, o_ref, cast_out = rest[:n_cast], rest[n_cast], rest[n_cast + 1:]
    xn = _mix_residual(x_ref, rg_ref, ret_ref, wo_ref, gmix_ref)
    hx = _norm_mod(xn, g_ref[...], shift_ref[...], scale_ref[...]).astype(BF16)
    o_ref[...] = xn + gffn_ref[...] * _swiglu_acc(hx, w1_ref, w3_ref, w2_ref)
    for src, dst in zip(cast_in, cast_out):
        dst[...] = src[...].astype(dst.dtype)


def _mix_ffn(x, rg, ret, w_out, gate_mix, g, shift, scale, gate_ffn, w1, w3, w2, cast=()):
    bsz, n, d = x.shape
    f = w1.shape[1]
    tm = min(ROW_TILE, n)
    steps = bsz * (n // tm)
    vec = pl.BlockSpec((None, 1, d), lambda b, i: (b, 0, 0))
    tile = lambda w: pl.BlockSpec((None, tm, w), lambda b, i: (b, i, 0))
    resident = lambda shape: pl.BlockSpec(shape, lambda b, i: (0, 0), pipeline_mode=pl.Buffered(1))
    slabs = []
    for a in cast:
        rows, cols = a.shape
        assert rows % (steps * 2 * SUBLANES) == 0
        slabs.append(pl.BlockSpec((rows // steps, cols), lambda b, i: (b * (n // tm) + i, 0)))
    out = pl.pallas_call(
        functools.partial(_mix_ffn_kernel, len(cast)),
        grid=(bsz, n // tm),
        in_specs=[
            tile(d), tile(rg.shape[2]), tile(ret.shape[2]), resident(w_out.shape),
            vec, resident((1, d)), vec, vec, vec,
            resident((d, f)), resident((d, f)), resident((f, d)),
        ] + slabs,
        out_specs=[tile(d)] + slabs,
        out_shape=[jax.ShapeDtypeStruct((bsz, n, d), F32)]
        + [jax.ShapeDtypeStruct(a.shape, BF16) for a in cast],
        compiler_params=_cparams(("parallel", "parallel")),
        name="mix_ffn",
    )(x, rg, ret, w_out, gate_mix, g, shift, scale, gate_ffn, w1, w3, w2, *cast)
    return out[0], tuple(out[1:])


MOE_BLK = 256
MOE_TOKENS = 1024


def _dispatch_kernel(nsteps, d0_ref, d1_ref, zr_ref, hx_ref, hs_ref, zbuf, stage, sem, zsem):
    tm = hx_ref.shape[0]
    step = pl.program_id(0)
    base = step * tm
    slot = step % 2

    @pl.when(pl.program_id(0) == 0)
    def _():
        zbuf[...] = jnp.zeros_like(zbuf)
        for z in range(zr_ref.shape[0]):
            @pl.when(zr_ref[z] >= 0)
            def _():
                pltpu.make_async_copy(zbuf, hs_ref.at[pl.ds(zr_ref[z], MOE_BLK)], zsem).start()
        for z in range(zr_ref.shape[0]):
            @pl.when(zr_ref[z] >= 0)
            def _():
                pltpu.make_async_copy(zbuf, hs_ref.at[pl.ds(zr_ref[z], MOE_BLK)], zsem).wait()

    def drain(which):
        for _ in range(2):
            pltpu.make_async_copy(stage.at[which], hs_ref.at[pl.ds(0, tm)], sem.at[which]).wait()

    @pl.when(step >= 2)
    def _():
        drain(slot)

    stage[slot] = hx_ref[...]
    src = stage.at[slot]

    def issue(t, carry):
        pltpu.make_async_copy(src.at[t], hs_ref.at[d0_ref[base + t]], sem.at[slot]).start(priority=0)
        pltpu.make_async_copy(src.at[t], hs_ref.at[d1_ref[base + t]], sem.at[slot]).start(priority=1)
        return carry

    lax.fori_loop(0, tm, issue, 0, unroll=8)

    @pl.when(step == nsteps - 1)
    def _():
        drain(slot)
        if nsteps >= 2:
            drain(1 - slot)


def _dispatch(dest0, dest1, zero_rows, hx, n_rows):
    n_tok, s, lanes = hx.shape
    tm = min(MOE_TOKENS, n_tok)
    grid_spec = pltpu.PrefetchScalarGridSpec(
        num_scalar_prefetch=3,
        grid=(n_tok // tm,),
        in_specs=[pl.BlockSpec((tm, s, lanes), lambda i, d0, d1, zr: (i, 0, 0))],
        out_specs=pl.BlockSpec(memory_space=pl.ANY),
        scratch_shapes=[pltpu.VMEM((MOE_BLK, s, lanes), F32),
                        pltpu.VMEM((2, tm, s, lanes), F32),
                        pltpu.SemaphoreType.DMA((2,)), pltpu.SemaphoreType.DMA(())],
    )
    return pl.pallas_call(
        functools.partial(_dispatch_kernel, n_tok // tm),
        grid_spec=grid_spec,
        out_shape=jax.ShapeDtypeStruct((n_rows, s, lanes), F32),
        compiler_params=_cparams(("arbitrary",)),
        name="moe_dispatch",
    )(dest0, dest1, zero_rows, hx)


def _experts_kernel(be_ref, nu_ref, hs_ref, w1_ref, w3_ref, w2_ref, y_ref):
    j = pl.program_id(0)
    d = w1_ref.shape[0]
    ns = d // LANES
    blk = hs_ref.shape[0] // ns

    @pl.when(j < nu_ref[0])
    def _():
        h = jnp.concatenate(
            [hs_ref[pl.ds(s, blk, stride=ns), :].astype(BF16) for s in range(ns)], axis=1)
        out = _swiglu_acc(h, w1_ref, w3_ref, w2_ref)
        for s in range(ns):
            y_ref[pl.ds(s, blk, stride=ns), :] = out[:, s * LANES:(s + 1) * LANES]

    @pl.when(j >= nu_ref[0])
    def _():
        y_ref[...] = jnp.zeros_like(y_ref)


def _experts(block_expert, n_used, hs, w1, w3, w2):
    ne, d, f = w1.shape
    s = d // LANES
    n_rows = hs.shape[0] // s
    n_blocks = n_rows // MOE_BLK

    def expert(shape):
        return pl.BlockSpec((None,) + shape, lambda j, be, nu: (be[j], 0, 0),
                            pipeline_mode=pl.Buffered(1))

    grid_spec = pltpu.PrefetchScalarGridSpec(
        num_scalar_prefetch=2,
        grid=(n_blocks,),
        in_specs=[
            pl.BlockSpec((MOE_BLK * s, LANES), lambda j, be, nu: (jnp.minimum(j, nu[0] - 1), 0)),
            expert((d, f)), expert((d, f)), expert((f, d)),
        ],
        out_specs=pl.BlockSpec((MOE_BLK * s, LANES), lambda j, be, nu: (j, 0)),
    )
    return pl.pallas_call(
        _experts_kernel,
        grid_spec=grid_spec,
        out_shape=jax.ShapeDtypeStruct((n_rows * s, LANES), F32),
        compiler_params=_cparams(("arbitrary",)),
        name="moe_experts",
    )(block_expert, n_used, hs, w1, w3, w2)


def _combine_kernel(d0_ref, d1_ref, x_ref, gate_ref, rt_ref, gfin_ref, y_ref, o_ref, ybuf, sem):
    tm, d = x_ref.shape
    ns = d // LANES
    step = pl.program_id(0) * pl.num_programs(1) + pl.program_id(1)
    nsteps = pl.num_programs(0) * pl.num_programs(1)
    slot = step % 2

    def row(ref, r):
        return ref.at[pl.ds(pl.multiple_of(r * ns, ns), ns)]

    def start_gather(block, into):
        base = block * tm

        def issue(t, carry):
            pltpu.make_async_copy(row(y_ref, d0_ref[base + t]), row(ybuf.at[into, 0], t),
                                  sem.at[into]).start(priority=0)
            pltpu.make_async_copy(row(y_ref, d1_ref[base + t]), row(ybuf.at[into, 1], t),
                                  sem.at[into]).start(priority=1)
            return carry

        lax.fori_loop(0, tm, issue, 0, unroll=8)

    @pl.when(step == 0)
    def _():
        start_gather(0, 0)

    @pl.when(step + 1 < nsteps)
    def _():
        start_gather(step + 1, 1 - slot)

    for k in range(2):
        pltpu.make_async_copy(y_ref.at[pl.ds(0, tm * ns)], ybuf.at[slot, k], sem.at[slot]).wait()

    chunk = 64
    for r0 in range(0, tm, chunk):
        rows = pl.ds(r0, chunk)
        rt = rt_ref[rows, :]
        w0 = rt[:, R_W0:R_W0 + 1]
        w1 = rt[:, R_W1:R_W1 + 1]
        ss = jnp.zeros((chunk, 1), F32)
        for s in range(ns):
            cols = slice(s * LANES, (s + 1) * LANES)
            moe = (w0 * ybuf[slot, 0, pl.ds(r0 * ns + s, chunk, stride=ns), :]
                   + w1 * ybuf[slot, 1, pl.ds(r0 * ns + s, chunk, stride=ns), :])
            xn = x_ref[rows, cols] + gate_ref[:, cols] * moe
            ss = ss + jnp.sum(xn * xn, axis=-1, keepdims=True)
            o_ref[rows, cols] = xn
        o_ref[rows, :] = o_ref[rows, :] * lax.rsqrt(ss / d + EPS) * gfin_ref[...]


def _combine(dest0, dest1, x, gate, route, g_final, y):
    bsz, n, d = x.shape
    s = d // LANES
    tm = min(MOE_TOKENS // 2, n)
    grid_spec = pltpu.PrefetchScalarGridSpec(
        num_scalar_prefetch=2,
        grid=(bsz, n // tm),
        in_specs=[
            pl.BlockSpec((None, tm, d), lambda b, i, d0, d1: (b, i, 0)),
            pl.BlockSpec((None, 1, d), lambda b, i, d0, d1: (b, 0, 0)),
            pl.BlockSpec((None, tm, LANES), lambda b, i, d0, d1: (b, i, 0)),
            pl.BlockSpec((1, d), lambda b, i, d0, d1: (0, 0)),
            pl.BlockSpec(memory_space=pl.ANY),
        ],
        out_specs=pl.BlockSpec((None, tm, d), lambda b, i, d0, d1: (b, i, 0)),
        scratch_shapes=[pltpu.VMEM((2, 2, tm * s, LANES), F32), pltpu.SemaphoreType.DMA((2,))],
    )
    return pl.pallas_call(
        _combine_kernel,
        grid_spec=grid_spec,
        out_shape=jax.ShapeDtypeStruct((bsz, n, d), F32),
        compiler_params=_cparams(("arbitrary", "arbitrary")),
        name="moe_combine",
    )(dest0, dest1, x, gate, route, g_final, y)


def _moe(hx, x, gate, route, route_t, counts, g_final, w1, w3, w2):
    bsz, n, d = x.shape
    ne = w1.shape[0]
    n_tok = bsz * n
    n_rows = -(-(2 * n_tok) // MOE_BLK) * MOE_BLK + ne * MOE_BLK
    n_blocks = n_rows // MOE_BLK
    cnt = counts[0, :ne].astype(jnp.int32)
    padded = (cnt + MOE_BLK - 1) // MOE_BLK * MOE_BLK
    pad_end = jnp.cumsum(padded)
    pad_start = pad_end - padded
    n_used = pad_end[-1] // MOE_BLK
    blk = jnp.minimum(jnp.arange(n_blocks, dtype=jnp.int32), n_used - 1)
    block_expert = jnp.minimum(
        jnp.sum((blk[:, None] * MOE_BLK >= pad_end[None, :]).astype(jnp.int32), axis=1), ne - 1)
    def dest(e_row, rank_row):
        e = route_t[e_row].astype(jnp.int32)
        start = sum(jnp.where(e == k, pad_start[k], 0) for k in range(ne))
        return start + route_t[rank_row].astype(jnp.int32)

    dest0 = dest(R_E0, R_RANK0)
    dest1 = dest(R_E1, R_RANK1)

    last_blk = jnp.where(padded > 0, pad_end - MOE_BLK, -1)
    trail = n_used + jnp.arange(ne, dtype=jnp.int32)
    trail = jnp.where(trail < n_blocks, trail * MOE_BLK, -1)
    zero_rows = jnp.concatenate([last_blk, trail]).astype(jnp.int32)

    ns = d // LANES
    hs = _dispatch(dest0, dest1, zero_rows, hx.reshape(n_tok, ns, LANES), n_rows)
    y = _experts(block_expert, n_used.reshape(1).astype(jnp.int32),
                 hs.reshape(n_rows * ns, LANES), w1, w3, w2)
    return _combine(dest0, dest1, x, gate, route, g_final, y)


def _rope_tables(n, hd):
    rows = n // GRID_W
    n_freq = hd // 4
    inv = ROPE_BASE ** (-jnp.arange(n_freq, dtype=F32) / n_freq)
    row_ang = jnp.arange(rows, dtype=F32)[:, None] * inv
    col_ang = jnp.arange(GRID_W, dtype=F32)[:, None] * inv

    def table(of_row, of_col):
        a = jnp.broadcast_to(of_row[:, None, :], (rows, GRID_W, n_freq))
        b = jnp.broadcast_to(of_col[None, :, :], (rows, GRID_W, n_freq))
        return jnp.concatenate([a, b], axis=-1).reshape(n, 2 * n_freq)

    cos = table(jnp.cos(row_ang), jnp.cos(col_ang))
    sin = table(jnp.sin(row_ang), jnp.sin(col_ang))
    return jnp.concatenate([cos, cos], axis=-1), jnp.concatenate([-sin, sin], axis=-1)


def _block_diag_gates(rg_wa, rg_wx):
    w = jnp.stack([rg_wa, rg_wx], axis=1)
    nd, ng, nk, c, _ = w.shape
    per = RG_CG // c
    w = w.reshape(nd, ng, nk // per, per, c, c)
    eye = jnp.eye(per, dtype=w.dtype)
    full = jnp.einsum('dgmpij,pq->dgmpiqj', w, eye)
    return full.reshape(nd, ng, nk // per, RG_CG, RG_CG).astype(BF16)


def _mixer(p, rope, states, conv_w, conv_b, wg, bg, lam, log_g, rg_w):
    rg_out, h_last = _rglru(p, conv_w, conv_b, wg, bg, lam, states[0])
    ret_out, s_last = _retention(p, log_g, rope[0], rope[1], states[1], 2 * rg_w)
    return rg_out, ret_out, (h_last, s_last)


def kernel(x, c, ctx, c_ctx, w_mod, b_mod, g_mix, g_ffn, g_final, w_in, w_out, conv_w, conv_b,
           rg_wa, rg_ba, rg_wx, rg_bx, rg_lam, ret_decay, ffn_w1, ffn_w3, ffn_w2,
           moe_router, moe_router_b, moe_w1, moe_w3, moe_w2):
    bsz, n_lat, d = x.shape
    n_ctx = ctx.shape[1]
    depth = w_mod.shape[0]
    rg_w = rg_lam.shape[2]
    nh = ret_decay.shape[2]
    hd = (w_out.shape[1] - rg_w) // nh
    ne = moe_router.shape[2]
    assert depth == 2, "kernel is written for the two-layer block (dense FFN, then MoE)"

    rows = -(-(bsz + 1) // SUBLANES) * SUBLANES
    c_all = jnp.zeros((rows, d), F32).at[:bsz].set(c).at[bsz].set(c_ctx)
    mod = _modulation(c_all, w_mod, b_mod)
    mod = mod.reshape(depth, rows, N_MOD, d)
    mod_lat = mod[:, :bsz, None]
    mod_ctx = jnp.broadcast_to(mod[:, bsz:bsz + 1, None], mod_lat.shape)

    rope_lat = _rope_tables(n_lat, hd)
    rope_ctx = (jnp.ones((n_ctx, hd), F32), jnp.zeros((n_ctx, hd), F32))
    log_g = jax.nn.log_sigmoid(ret_decay.astype(F32))
    zero_states = (jnp.zeros((bsz, 2, rg_w), F32), jnp.zeros((bsz, 2, nh, hd, hd), F32))

    for l in range(depth):
        last = l == depth - 1
        ml = [mod_lat[l, :, :, j] for j in range(N_MOD)]
        mc = [mod_ctx[l, :, :, j] for j in range(N_MOD)]
        g_m = g_mix[l].reshape(1, d)
        g_f = g_ffn[l].reshape(1, d)
        w_in_l = w_in[l].astype(BF16)
        w_out_l = w_out[l].astype(BF16)
        wg = _block_diag_gates(0.5 * rg_wa[l], 0.5 * rg_wx[l])
        bg = 0.5 * jnp.stack([rg_ba[l], rg_bx[l]], axis=1)
        mix_p = (conv_w[l], conv_b[l], wg, bg, rg_lam[l], log_g[l], rg_w)

        pc = _in_proj(ctx, g_m, mc[0], mc[1], w_in_l)
        rg_c, ret_c, ctx_states = _mixer(pc, rope_ctx, zero_states, *mix_p)
        px = _in_proj(x, g_m, ml[0], ml[1], w_in_l)
        rg_x, ret_x, _ = _mixer(px, rope_lat, ctx_states, *mix_p)

        if l % 2 == 0:
            i = l // 2
            w1, w3, w2 = ffn_w1[i].astype(BF16), ffn_w3[i].astype(BF16), ffn_w2[i].astype(BF16)
            nxt = moe_w1[i], moe_w3[i], moe_w2[i]
            x, cast = _mix_ffn(x, rg_x, ret_x, w_out_l, ml[2], g_f, ml[3], ml[4], ml[5],
                               w1, w3, w2, cast=[w.reshape(-1, w.shape[-1]) for w in nxt])
            moe_w = [c.reshape(w.shape) for c, w in zip(cast, nxt)]
            if not last:
                ctx, _ = _mix_ffn(ctx, rg_c, ret_c, w_out_l, mc[2], g_f, mc[3], mc[4], mc[5],
                                  w1, w3, w2)
        else:
            i = l // 2
            assert ne <= SUBLANES
            wr = jnp.zeros((d, LANES), F32).at[:, :ne].set(moe_router[i])
            wr_hi = wr.astype(BF16)
            wr_lo = (wr - wr_hi.astype(F32)).astype(BF16)
            wr2 = jnp.concatenate([wr_hi, wr_lo], axis=1)
            br = jnp.full((1, LANES), NEG_BIG, F32).at[0, :ne].set(moe_router_b[i])
            x, hx, route, route_t, counts = _out_proj_route(
                x, rg_x, ret_x, w_out_l, ml[2], g_f, ml[3], ml[4], wr2, br)
            x = _moe(hx, x, ml[5], route, route_t, counts, g_final.reshape(1, d), *moe_w)
    return x
```

```python
import functools

import jax
import jax.numpy as jnp
from jax import lax
from jax.experimental import pallas as pl
from jax.experimental.pallas import tpu as pltpu

F32 = jnp.float32
BF16 = jnp.bfloat16

EPS = 1e-6
RG_C = 8.0
CONV_W = 4
CONV_LEFT = 2
RET_CHUNK = 256
ROPE_BASE = 10000.0
GRID_W = 64
N_MOD = 6

ROW_TILE = 512

LANES = 128
SUBLANES = 8
VMEM_LIMIT = 56 * 1024 * 1024
NEG_BIG = -1e30
LOG2_E = 1.4426950408889634


def _cparams(sem):
    return pltpu.CompilerParams(dimension_semantics=sem, vmem_limit_bytes=VMEM_LIMIT)


def _modulation_kernel(c_ref, w_ref, b_ref, o_ref):
    c = c_ref[...]
    s = c * jax.nn.sigmoid(c)
    o_ref[...] = jnp.dot(s, w_ref[...], precision=lax.Precision.HIGHEST,
                         preferred_element_type=F32) + b_ref[...]


def _modulation(c_all, w_mod, b_mod):
    depth, d, six_d = w_mod.shape
    rows = c_all.shape[0]
    tn = six_d // 4
    return pl.pallas_call(
        _modulation_kernel,
        grid=(depth, six_d // tn),
        in_specs=[
            pl.BlockSpec((rows, d), lambda l, j: (0, 0)),
            pl.BlockSpec((None, d, tn), lambda l, j: (l, 0, j)),
            pl.BlockSpec((None, 1, tn), lambda l, j: (l, 0, j)),
        ],
        out_specs=pl.BlockSpec((None, rows, tn), lambda l, j: (l, 0, j)),
        out_shape=jax.ShapeDtypeStruct((depth, rows, six_d), F32),
        compiler_params=_cparams(("arbitrary", "arbitrary")),
        name="modulation",
    )(c_all, w_mod, b_mod.reshape(depth, 1, six_d))


def _norm_mod(x, g, shift, scale):
    ms = jnp.mean(x * x, axis=-1, keepdims=True)
    y = x * lax.rsqrt(ms + EPS) * g
    return y * (1.0 + scale) + shift


def _in_proj_kernel(x_ref, g_ref, shift_ref, scale_ref, w_ref, o_ref):
    h = _norm_mod(x_ref[...], g_ref[...], shift_ref[...], scale_ref[...])
    o_ref[...] = jnp.dot(h.astype(BF16), w_ref[...],
                         preferred_element_type=F32).astype(o_ref.dtype)


def _in_proj(x, g, shift, scale, w):
    bsz, n, d = x.shape
    f = w.shape[1]
    tm = min(2 * ROW_TILE, n)
    vec = pl.BlockSpec((None, 1, d), lambda b, i: (b, 0, 0))
    return pl.pallas_call(
        _in_proj_kernel,
        grid=(bsz, n // tm),
        in_specs=[
            pl.BlockSpec((None, tm, d), lambda b, i: (b, i, 0)),
            pl.BlockSpec((1, d), lambda b, i: (0, 0)),
            vec, vec,
            pl.BlockSpec((d, f), lambda b, i: (0, 0)),
        ],
        out_specs=pl.BlockSpec((None, tm, f), lambda b, i: (b, i, 0)),
        out_shape=jax.ShapeDtypeStruct((bsz, n, f), BF16),
        compiler_params=_cparams(("parallel", "parallel")),
        name="in_proj",
    )(x, g, shift, scale, w)


RG_TILE = 256
RG_CG = 256
RG_SEG = RG_TILE // SUBLANES
RG_PITCH = RG_SEG + SUBLANES


def _rglru_kernel(u_ref, yg_ref, cw_ref, cb_ref, wg_ref, bg_ref, lam_ref, h0_ref,
                  o_ref, hl_ref, u_scr, hf_scr, ab_scr, nat_scr):
    n = u_ref.shape[0]
    tt = RG_TILE
    nt = n // tt
    nseg = n // RG_SEG
    cg = u_ref.shape[1]
    lane_cols = [slice(g * LANES, (g + 1) * LANES) for g in range(cg // LANES)]

    zeros = jnp.zeros((RG_PITCH, LANES), F32)
    for g, cols in enumerate(lane_cols):
        u_scr[g, pl.ds(0, RG_PITCH), :] = zeros
        u_scr[g, pl.ds(nseg * RG_PITCH + RG_SEG, SUBLANES), :] = zeros[0:SUBLANES]

    def fill(s, carry):
        src = pl.multiple_of(s * RG_SEG, RG_SEG)
        dst = pl.multiple_of((s + 1) * RG_PITCH, SUBLANES)
        for g, cols in enumerate(lane_cols):
            rows = u_ref[pl.ds(src, RG_SEG), cols].astype(F32)
            u_scr[g, pl.ds(dst, RG_SEG), :] = rows
            u_scr[g, pl.ds(dst - SUBLANES, SUBLANES), :] = rows[0:SUBLANES]
        return carry

    lax.fori_loop(0, nseg, fill, 0)

    seg = RG_SEG
    cw_rows = [[jnp.broadcast_to(cw_ref[k:k + 1, c], (SUBLANES, LANES)) for c in lane_cols]
               for k in range(CONV_W)]
    cb_rows = [jnp.broadcast_to(cb_ref[:, c], (SUBLANES, LANES)) for c in lane_cols]

    def conv_tile(i):
        base = pl.multiple_of((i * SUBLANES + 1) * RG_PITCH, SUBLANES)
        blocks = []
        for j in range(seg):
            lane_groups = []
            for g in range(cg // LANES):
                acc = cb_rows[g]
                for k in range(CONV_W):
                    q = j + k - CONV_LEFT
                    start = base + q if q >= 0 else base - RG_PITCH + seg + q
                    acc = acc + cw_rows[k][g] * u_scr[g, pl.ds(start, SUBLANES, stride=RG_PITCH), :]
                lane_groups.append(acc)
            blocks.append(jnp.concatenate(lane_groups, axis=1))
        return jnp.concatenate(blocks, axis=0)

    def gates(uc, d):
        ub = uc.astype(BF16)
        ta = jnp.tanh(jnp.dot(ub, wg_ref[d, 0], preferred_element_type=F32) + bg_ref[d, 0:1, :])
        ti = jnp.tanh(jnp.dot(ub, wg_ref[d, 1], preferred_element_type=F32) + bg_ref[d, 1:2, :])
        z = -lam_ref[d:d + 1, :]
        sp = jnp.maximum(z, 0.0) + jnp.log1p(jnp.exp(-jnp.abs(z)))
        c2 = (-0.5 * RG_C * LOG2_E) * sp
        a = jnp.exp2(c2 + c2 * ta)
        b = jnp.exp2((0.5 * LOG2_E) * jnp.log(1.0 - a * a)) * ((0.5 + 0.5 * ti) * uc)
        return a, b

    def scan_tile(a, b, carry, reverse):
        steps = range(seg - 1, -1, -1) if reverse else range(seg)
        h = acum = None
        h_loc = [None] * seg
        a_cum = [None] * seg
        for j in steps:
            rows = slice(j * SUBLANES, (j + 1) * SUBLANES)
            h = b[rows] if h is None else a[rows] * h + b[rows]
            acum = a[rows] if acum is None else a[rows] * acum
            h_loc[j], a_cum[j] = h, acum
        order = range(SUBLANES - 1, -1, -1) if reverse else range(SUBLANES)
        c = carry
        c_in = [None] * SUBLANES
        for s in order:
            c_in[s] = c
            c = h[s:s + 1, :] + acum[s:s + 1, :] * c
        c_in = jnp.concatenate(c_in, axis=0)
        out = jnp.concatenate([h_loc[j] + a_cum[j] * c_in for j in range(seg)], axis=0)
        return out, c

    def fwd_body(i, carry):
        t0 = pl.multiple_of(i * tt, tt)
        uc = conv_tile(i)
        a, b = gates(uc, 0)
        h, carry = scan_tile(a, b, carry, False)
        hf_scr[pl.ds(t0, tt), :] = h
        a, b = gates(uc, 1)
        ab_scr[0, pl.ds(t0, tt), :] = a
        ab_scr[1, pl.ds(t0, tt), :] = b
        return carry

    hf_last = lax.fori_loop(0, nt, fwd_body, h0_ref[0:1, :], unroll=min(2, nt))

    def bwd_body(i, carry):
        tile = nt - 1 - i
        t0 = pl.multiple_of(tile * tt, tt)
        h, carry = scan_tile(ab_scr[0, pl.ds(t0, tt), :], ab_scr[1, pl.ds(t0, tt), :], carry, True)
        hsum = hf_scr[pl.ds(t0, tt), :] + h
        for j in range(seg):
            for g, cols in enumerate(lane_cols):
                nat_scr[g, pl.ds(j, SUBLANES, stride=RG_PITCH), :] = (
                    hsum[j * SUBLANES:(j + 1) * SUBLANES, cols])
        hnat = jnp.concatenate(
            [jnp.concatenate([nat_scr[g, pl.ds(s * RG_PITCH, seg), :] for s in range(SUBLANES)],
                             axis=0) for g in range(len(lane_cols))], axis=1)
        yg = yg_ref[pl.ds(t0, tt), :].astype(F32)
        o_ref[pl.ds(t0, tt), :] = (jax.nn.gelu(yg) * hnat).astype(o_ref.dtype)
        return carry

    hb_last = lax.fori_loop(0, nt, bwd_body, h0_ref[1:2, :], unroll=min(2, nt))
    hl_ref[0:1, :] = hf_last
    hl_ref[1:2, :] = hb_last


def _rglru(p, conv_w, conv_b, wg, bg, lam, h0):
    bsz, n, _ = p.shape
    rg_w = lam.shape[1]
    ncg = rg_w // RG_CG
    assert n % RG_TILE == 0 and rg_w % RG_CG == 0
    return pl.pallas_call(
        _rglru_kernel,
        grid=(bsz, ncg),
        in_specs=[
            pl.BlockSpec((None, n, RG_CG), lambda b, c: (b, 0, c)),
            pl.BlockSpec((None, n, RG_CG), lambda b, c: (b, 0, ncg + c)),
            pl.BlockSpec((CONV_W, RG_CG), lambda b, c: (0, c)),
            pl.BlockSpec((1, RG_CG), lambda b, c: (0, c)),
            pl.BlockSpec((2, 2, None, RG_CG, RG_CG), lambda b, c: (0, 0, c, 0, 0)),
            pl.BlockSpec((2, 2, RG_CG), lambda b, c: (0, 0, c)),
            pl.BlockSpec((2, RG_CG), lambda b, c: (0, c)),
            pl.BlockSpec((None, 2, RG_CG), lambda b, c: (b, 0, c)),
        ],
        out_specs=[
            pl.BlockSpec((None, n, RG_CG), lambda b, c: (b, 0, c)),
            pl.BlockSpec((None, 2, RG_CG), lambda b, c: (b, 0, c)),
        ],
        out_shape=[
            jax.ShapeDtypeStruct((bsz, n, rg_w), BF16),
            jax.ShapeDtypeStruct((bsz, 2, rg_w), F32),
        ],
        scratch_shapes=[
            pltpu.VMEM((RG_CG // LANES, (n // RG_SEG + 1) * RG_PITCH, LANES), F32),
            pltpu.VMEM((n, RG_CG), F32),
            pltpu.VMEM((2, n, RG_CG), F32),
            pltpu.VMEM((RG_CG // LANES, SUBLANES * RG_PITCH, LANES), F32),
        ],
        compiler_params=_cparams(("parallel", "parallel")),
        name="rglru",
    )(p, p, conv_w, conv_b.reshape(1, rg_w), wg, bg, lam, h0)


RET_UNROLL = 4


def _retention_kernel(lg_ref, q_ref, k_ref, v_ref, g_ref, cs_ref, sn_ref, s0_ref,
                      o_ref, so_ref, kr_scr, st_scr):
    n, hd = q_ref.shape
    L = RET_CHUNK
    nc = n // L
    head = pl.program_id(1)
    lgf = lg_ref[0, head]
    lgb = lg_ref[1, head]

    def rope(t, rows):
        return t * cs_ref[rows, :] + pltpu.roll(t, hd // 2, axis=1) * sn_ref[rows, :]

    kr_scr[...] = (rope(k_ref[...].astype(F32), slice(None)) * (hd ** -0.5)).astype(BF16)

    jc = lax.broadcasted_iota(jnp.int32, (L, 1), 0).astype(F32)
    q_dec_f = jnp.exp((jc + 1.0) * lgf)
    q_dec_b = jnp.exp((L - jc) * lgb)
    k_dec_f = jnp.exp((L - 1.0 - jc) * lgf)
    k_dec_b = jnp.exp(jc * lgb)
    ones = jnp.ones((1, hd), F32)
    chunk_f = jnp.exp(ones * (L * lgf))
    chunk_b = jnp.exp(ones * (L * lgb))
    ii = lax.broadcasted_iota(jnp.int32, (L, L), 0)
    jj = lax.broadcasted_iota(jnp.int32, (L, L), 1)
    diff = (ii - jj).astype(F32)
    dmat = jnp.where(diff >= 0.0, jnp.exp(jnp.maximum(diff, 0.0) * lgf),
                     jnp.exp(jnp.maximum(-diff, 0.0) * lgb))

    tn_dims = (((0,), (0,)), ((), ()))
    nt_dims = (((1,), (1,)), ((), ()))

    def state_update(c, state, k_dec, chunk_dec):
        t0 = pl.multiple_of(c * L, L)
        kd = (kr_scr[pl.ds(t0, L), :].astype(F32) * k_dec).astype(BF16)
        upd = lax.dot_general(kd, v_ref[pl.ds(t0, L), :], tn_dims, preferred_element_type=F32)
        return chunk_dec * state + upd

    def state_body(i, carry):
        f_state, r_state = carry
        cf = i
        cb = nc - 1 - i
        st_scr[cf, :, 0:hd] = f_state.astype(BF16)
        st_scr[cb, :, hd:2 * hd] = r_state.astype(BF16)
        return (state_update(cf, f_state, k_dec_f, chunk_f),
                state_update(cb, r_state, k_dec_b, chunk_b))

    f_fin, r_fin = lax.fori_loop(0, nc, state_body, (s0_ref[0], s0_ref[1]),
                                 unroll=min(RET_UNROLL, nc))
    so_ref[0] = f_fin
    so_ref[1] = r_fin

    def out_body(c, carry):
        t0 = pl.multiple_of(c * L, L)
        qc = rope(q_ref[pl.ds(t0, L), :].astype(F32), pl.ds(t0, L)).astype(BF16)
        kc = kr_scr[pl.ds(t0, L), :]
        vc = v_ref[pl.ds(t0, L), :]
        s = lax.dot_general(qc, kc, nt_dims, preferred_element_type=F32)
        o = jnp.dot((s * dmat).astype(BF16), vc, preferred_element_type=F32)
        cross = jnp.dot(qc, st_scr[c], preferred_element_type=F32)
        o = o + cross[:, 0:hd] * q_dec_f + cross[:, hd:2 * hd] * q_dec_b
        mu = jnp.mean(o, axis=-1, keepdims=True)
        oc = o - mu
        var = jnp.mean(oc * oc, axis=-1, keepdims=True)
        gate = g_ref[pl.ds(t0, L), :].astype(F32)
        o_ref[pl.ds(t0, L), :] = (gate * jax.nn.sigmoid(gate) * (oc * lax.rsqrt(var + EPS))).astype(o_ref.dtype)
        return carry

    lax.fori_loop(0, nc, out_body, 0, unroll=min(RET_UNROLL, nc))


def _retention(p, log_g, cs, sn, s0, col0):
    bsz, n, _ = p.shape
    nh = log_g.shape[1]
    hd = cs.shape[1]
    cb0 = col0 // hd

    def col(which):
        return pl.BlockSpec((None, n, hd), lambda b, h, lg: (b, 0, cb0 + which * nh + h))

    state = pl.BlockSpec((None, 2, None, hd, hd), lambda b, h, lg: (b, 0, h, 0, 0))
    table = pl.BlockSpec((n, hd), lambda b, h, lg: (0, 0))
    grid_spec = pltpu.PrefetchScalarGridSpec(
        num_scalar_prefetch=1,
        grid=(bsz, nh),
        in_specs=[col(0), col(1), col(2), col(3), table, table, state],
        out_specs=[pl.BlockSpec((None, n, hd), lambda b, h, lg: (b, 0, h)), state],
        scratch_shapes=[
            pltpu.VMEM((n, hd), BF16),
            pltpu.VMEM((n // RET_CHUNK, hd, 2 * hd), BF16),
        ],
    )
    return pl.pallas_call(
        _retention_kernel,
        grid_spec=grid_spec,
        out_shape=[
            jax.ShapeDtypeStruct((bsz, n, nh * hd), BF16),
            jax.ShapeDtypeStruct((bsz, 2, nh, hd, hd), F32),
        ],
        compiler_params=_cparams(("parallel", "parallel")),
        name="retention",
    )(log_g, p, p, p, p, cs, sn, s0)


R_RANK0, R_RANK1, R_E0, R_E1, R_W0, R_W1 = range(6)


def _mix_residual(x_ref, rg_ref, ret_ref, wo_ref, gate_ref, rows=slice(None)):
    rg_w = rg_ref.shape[1]
    y = jnp.dot(rg_ref[rows, :], wo_ref[0:rg_w, :], preferred_element_type=F32)
    y = y + jnp.dot(ret_ref[rows, :], wo_ref[rg_w:, :], preferred_element_type=F32)
    return x_ref[rows, :] + gate_ref[...] * y


ROUTE_ROWS = 512


def _out_proj_kernel(x_ref, rg_ref, ret_ref, wo_ref, gate_ref, g_ref, shift_ref, scale_ref,
                     wr2_ref, br_ref, xo_ref, h_ref, rt_ref, rtt_ref, cnt_ref, carry_scr):
    tm, d = x_ref.shape
    ns = d // LANES
    nr = min(ROUTE_ROWS, tm)

    @pl.when((pl.program_id(0) == 0) & (pl.program_id(1) == 0))
    def _():
        carry_scr[...] = jnp.zeros_like(carry_scr)

    erow = lax.broadcasted_iota(jnp.int32, (SUBLANES, nr), 0).astype(F32)
    ri = lax.broadcasted_iota(jnp.int32, (nr, nr), 0)
    ci = lax.broadcasted_iota(jnp.int32, (nr, nr), 1)
    earlier = jnp.where(ri < ci, 1.0, 0.0).astype(BF16)
    count = carry_scr[:, 0:1]

    for r0 in range(0, tm, nr):
        rows = pl.ds(r0, nr)
        xn = _mix_residual(x_ref, rg_ref, ret_ref, wo_ref, gate_ref, rows)
        xo_ref[rows, :] = xn
        hx = _norm_mod(xn, g_ref[...], shift_ref[...], scale_ref[...])

        for s in range(ns):
            h_ref[pl.ds(r0 * ns + s, nr, stride=ns), :] = hx[:, s * LANES:(s + 1) * LANES]

        h_hi = hx.astype(BF16)
        h_lo = (hx - h_hi.astype(F32)).astype(BF16)
        part = jnp.dot(h_hi, wr2_ref[...], preferred_element_type=F32)
        logits = (part[:, :LANES] + part[:, LANES:]
                  + jnp.dot(h_lo, wr2_ref[:, :LANES], preferred_element_type=F32) + br_ref[...])
        lt = jnp.transpose(logits)[0:SUBLANES, :]
        m0 = jnp.max(lt, axis=0, keepdims=True)
        i0 = jnp.min(jnp.where(lt == m0, erow, float(SUBLANES)), axis=0, keepdims=True)
        rest_l = jnp.where(erow == i0, NEG_BIG, lt)
        m1 = jnp.max(rest_l, axis=0, keepdims=True)
        i1 = jnp.min(jnp.where(rest_l == m1, erow, float(SUBLANES)), axis=0, keepdims=True)
        e = jnp.exp(m1 - m0)
        w0 = 1.0 / (1.0 + e)
        w1 = e / (1.0 + e)

        sel0 = erow == i0
        sel1 = erow == i1
        mask = jnp.where(sel0 | sel1, 1.0, 0.0)
        before = jnp.dot(mask.astype(BF16), earlier, preferred_element_type=F32) + count
        rank0 = jnp.sum(jnp.where(sel0, before, 0.0), axis=0, keepdims=True)
        rank1 = jnp.sum(jnp.where(sel1, before, 0.0), axis=0, keepdims=True)
        count = count + jnp.sum(mask, axis=1, keepdims=True)
        rec = jnp.zeros((SUBLANES, nr), F32)
        for idx, val in enumerate((rank0, rank1, i0, i1, w0, w1)):
            rec = jnp.where(erow == idx, val, rec)
        rtt_ref[:, r0:r0 + nr] = rec
        rt_ref[rows, :] = jnp.transpose(
            jnp.concatenate([rec, jnp.zeros((LANES - SUBLANES, nr), F32)], axis=0))

    carry_scr[...] = jnp.broadcast_to(count, carry_scr.shape)
    on_diag = (lax.broadcasted_iota(jnp.int32, (SUBLANES, LANES), 0)
               == lax.broadcasted_iota(jnp.int32, (SUBLANES, LANES), 1))
    cnt_ref[...] = jnp.sum(jnp.where(on_diag, count, 0.0), axis=0, keepdims=True)


def _out_proj_route(x, rg, ret, w_out, gate, g, shift, scale, wr2, br):
    bsz, n, d = x.shape
    tm = min(ROUTE_ROWS, n)
    vec = pl.BlockSpec((None, 1, d), lambda b, i: (b, 0, 0))
    tile = lambda w: pl.BlockSpec((None, tm, w), lambda b, i: (b, i, 0))
    const = lambda shape: pl.BlockSpec(shape, lambda b, i: (0, 0))
    return pl.pallas_call(
        _out_proj_kernel,
        grid=(bsz, n // tm),
        in_specs=[
            tile(d), tile(rg.shape[2]), tile(ret.shape[2]), const(w_out.shape),
            vec, const((1, d)), vec, vec, const((d, 2 * LANES)), const((1, LANES)),
        ],
        out_specs=[
            tile(d),
            pl.BlockSpec((None, tm * (d // LANES), LANES), lambda b, i: (b, i, 0)),
            tile(LANES),
            pl.BlockSpec((SUBLANES, tm), lambda b, i: (0, b * (n // tm) + i)),
            const((1, LANES)),
        ],
        out_shape=[
            jax.ShapeDtypeStruct((bsz, n, d), F32),
            jax.ShapeDtypeStruct((bsz, n * (d // LANES), LANES), F32),
            jax.ShapeDtypeStruct((bsz, n, LANES), F32),
            jax.ShapeDtypeStruct((SUBLANES, bsz * n), F32),
            jax.ShapeDtypeStruct((1, LANES), F32),
        ],
        scratch_shapes=[pltpu.VMEM((SUBLANES, LANES), F32)],
        compiler_params=_cparams(("arbitrary", "arbitrary")),
        name="out_proj_route",
    )(x, rg, ret, w_out, gate, g, shift, scale, wr2, br)


def _swiglu_acc(h, w1_ref, w3_ref, w2_ref):
    a = jnp.dot(h, w1_ref[...], preferred_element_type=F32)
    b = jnp.dot(h, w3_ref[...], preferred_element_type=F32)
    act = a * jax.nn.sigmoid(a) * b
    return jnp.dot(act.astype(BF16), w2_ref[...], preferred_element_type=F32)


def _mix_ffn_kernel(n_cast, x_ref, rg_ref, ret_ref, wo_ref, gmix_ref, g_ref, shift_ref, scale_ref,
                    gffn_ref, w1_ref, w3_ref, w2_ref, *rest):
    cast_in, o_ref, cast_out = rest[:n_cast], rest[n_cast], rest[n_cast + 1:]
    xn = _mix_residual(x_ref, rg_ref, ret_ref, wo_ref, gmix_ref)
    hx = _norm_mod(xn, g_ref[...], shift_ref[...], scale_ref[...]).astype(BF16)
    o_ref[...] = xn + gffn_ref[...] * _swiglu_acc(hx, w1_ref, w3_ref, w2_ref)
    for src, dst in zip(cast_in, cast_out):
        dst[...] = src[...].astype(dst.dtype)


def _mix_ffn(x, rg, ret, w_out, gate_mix, g, shift, scale, gate_ffn, w1, w3, w2, cast=()):
    bsz, n, d = x.shape
    f = w1.shape[1]
    tm = min(ROW_TILE, n)
    steps = bsz * (n // tm)
    vec = pl.BlockSpec((None, 1, d), lambda b, i: (b, 0, 0))
    tile = lambda w: pl.BlockSpec((None, tm, w), lambda b, i: (b, i, 0))
    resident = lambda shape: pl.BlockSpec(shape, lambda b, i: (0, 0), pipeline_mode=pl.Buffered(1))
    slabs = []
    for a in cast:
        rows, cols = a.shape
        assert rows % (steps * 2 * SUBLANES) == 0
        slabs.append(pl.BlockSpec((rows // steps, cols), lambda b, i: (b * (n // tm) + i, 0)))
    out = pl.pallas_call(
        functools.partial(_mix_ffn_kernel, len(cast)),
        grid=(bsz, n // tm),
        in_specs=[
            tile(d), tile(rg.shape[2]), tile(ret.shape[2]), resident(w_out.shape),
            vec, resident((1, d)), vec, vec, vec,
            resident((d, f)), resident((d, f)), resident((f, d)),
        ] + slabs,
        out_specs=[tile(d)] + slabs,
        out_shape=[jax.ShapeDtypeStruct((bsz, n, d), F32)]
        + [jax.ShapeDtypeStruct(a.shape, BF16) for a in cast],
        compiler_params=_cparams(("parallel", "parallel")),
        name="mix_ffn",
    )(x, rg, ret, w_out, gate_mix, g, shift, scale, gate_ffn, w1, w3, w2, *cast)
    return out[0], tuple(out[1:])


MOE_BLK = 512
MOE_TOKENS = 1024


def _dispatch_kernel(nsteps, d0_ref, d1_ref, zr_ref, hx_ref, hs_ref, zbuf, stage, sem, zsem):
    tm = hx_ref.shape[0]
    step = pl.program_id(0)
    base = step * tm
    slot = step % 2

    @pl.when(pl.program_id(0) == 0)
    def _():
        zbuf[...] = jnp.zeros_like(zbuf)
        for z in range(zr_ref.shape[0]):
            @pl.when(zr_ref[z] >= 0)
            def _():
                pltpu.make_async_copy(zbuf, hs_ref.at[pl.ds(zr_ref[z], MOE_BLK)], zsem).start()
        for z in range(zr_ref.shape[0]):
            @pl.when(zr_ref[z] >= 0)
            def _():
                pltpu.make_async_copy(zbuf, hs_ref.at[pl.ds(zr_ref[z], MOE_BLK)], zsem).wait()

    def drain(which):
        for _ in range(2):
            pltpu.make_async_copy(stage.at[which], hs_ref.at[pl.ds(0, tm)], sem.at[which]).wait()

    @pl.when(step >= 2)
    def _():
        drain(slot)

    stage[slot] = hx_ref[...]
    src = stage.at[slot]

    def issue(t, carry):
        pltpu.make_async_copy(src.at[t], hs_ref.at[d0_ref[base + t]], sem.at[slot]).start(priority=0)
        pltpu.make_async_copy(src.at[t], hs_ref.at[d1_ref[base + t]], sem.at[slot]).start(priority=1)
        return carry

    lax.fori_loop(0, tm, issue, 0, unroll=8)

    @pl.when(step == nsteps - 1)
    def _():
        drain(slot)
        if nsteps >= 2:
            drain(1 - slot)


def _dispatch(dest0, dest1, zero_rows, hx, n_rows):
    n_tok, s, lanes = hx.shape
    tm = min(MOE_TOKENS, n_tok)
    grid_spec = pltpu.PrefetchScalarGridSpec(
        num_scalar_prefetch=3,
        grid=(n_tok // tm,),
        in_specs=[pl.BlockSpec((tm, s, lanes), lambda i, d0, d1, zr: (i, 0, 0))],
        out_specs=pl.BlockSpec(memory_space=pl.ANY),
        scratch_shapes=[pltpu.VMEM((MOE_BLK, s, lanes), F32),
                        pltpu.VMEM((2, tm, s, lanes), F32),
                        pltpu.SemaphoreType.DMA((2,)), pltpu.SemaphoreType.DMA(())],
    )
    return pl.pallas_call(
        functools.partial(_dispatch_kernel, n_tok // tm),
        grid_spec=grid_spec,
        out_shape=jax.ShapeDtypeStruct((n_rows, s, lanes), F32),
        compiler_params=_cparams(("arbitrary",)),
        name="moe_dispatch",
    )(dest0, dest1, zero_rows, hx)


def _experts_kernel(be_ref, nu_ref, hs_ref, w1_ref, w3_ref, w2_ref, y_ref):
    j = pl.program_id(0)
    d = w1_ref.shape[0]
    ns = d // LANES
    blk = hs_ref.shape[0] // ns

    @pl.when(j < nu_ref[0])
    def _():
        h = jnp.concatenate(
            [hs_ref[pl.ds(s, blk, stride=ns), :].astype(BF16) for s in range(ns)], axis=1)
        out = _swiglu_acc(h, w1_ref, w3_ref, w2_ref)
        for s in range(ns):
            y_ref[pl.ds(s, blk, stride=ns), :] = out[:, s * LANES:(s + 1) * LANES]

    @pl.when(j >= nu_ref[0])
    def _():
        y_ref[...] = jnp.zeros_like(y_ref)


def _experts(block_expert, n_used, hs, w1, w3, w2):
    ne, d, f = w1.shape
    s = d // LANES
    n_rows = hs.shape[0] // s
    n_blocks = n_rows // MOE_BLK

    def expert(shape):
        return pl.BlockSpec((None,) + shape, lambda j, be, nu: (be[j], 0, 0),
                            pipeline_mode=pl.Buffered(1))

    grid_spec = pltpu.PrefetchScalarGridSpec(
        num_scalar_prefetch=2,
        grid=(n_blocks,),
        in_specs=[
            pl.BlockSpec((MOE_BLK * s, LANES), lambda j, be, nu: (jnp.minimum(j, nu[0] - 1), 0)),
            expert((d, f)), expert((d, f)), expert((f, d)),
        ],
        out_specs=pl.BlockSpec((MOE_BLK * s, LANES), lambda j, be, nu: (j, 0)),
    )
    return pl.pallas_call(
        _experts_kernel,
        grid_spec=grid_spec,
        out_shape=jax.ShapeDtypeStruct((n_rows * s, LANES), F32),
        compiler_params=_cparams(("arbitrary",)),
        name="moe_experts",
    )(block_expert, n_used, hs, w1, w3, w2)


def _combine_kernel(d0_ref, d1_ref, x_ref, gate_ref, rt_ref, gfin_ref, y_ref, o_ref, ybuf, sem):
    tm, d = x_ref.shape
    ns = d // LANES
    step = pl.program_id(0) * pl.num_programs(1) + pl.program_id(1)
    nsteps = pl.num_programs(0) * pl.num_programs(1)
    slot = step % 2

    def row(ref, r):
        return ref.at[pl.ds(pl.multiple_of(r * ns, ns), ns)]

    def start_gather(block, into):
        base = block * tm

        def issue(t, carry):
            pltpu.make_async_copy(row(y_ref, d0_ref[base + t]), row(ybuf.at[into, 0], t),
                                  sem.at[into]).start(priority=0)
            pltpu.make_async_copy(row(y_ref, d1_ref[base + t]), row(ybuf.at[into, 1], t),
                                  sem.at[into]).start(priority=1)
            return carry

        lax.fori_loop(0, tm, issue, 0, unroll=8)

    @pl.when(step == 0)
    def _():
        start_gather(0, 0)

    @pl.when(step + 1 < nsteps)
    def _():
        start_gather(step + 1, 1 - slot)

    for k in range(2):
        pltpu.make_async_copy(y_ref.at[pl.ds(0, tm * ns)], ybuf.at[slot, k], sem.at[slot]).wait()

    chunk = 64
    for r0 in range(0, tm, chunk):
        rows = pl.ds(r0, chunk)
        rt = rt_ref[rows, :]
        w0 = rt[:, R_W0:R_W0 + 1]
        w1 = rt[:, R_W1:R_W1 + 1]
        ss = jnp.zeros((chunk, 1), F32)
        for s in range(ns):
            cols = slice(s * LANES, (s + 1) * LANES)
            moe = (w0 * ybuf[slot, 0, pl.ds(r0 * ns + s, chunk, stride=ns), :]
                   + w1 * ybuf[slot, 1, pl.ds(r0 * ns + s, chunk, stride=ns), :])
            xn = x_ref[rows, cols] + gate_ref[:, cols] * moe
            ss = ss + jnp.sum(xn * xn, axis=-1, keepdims=True)
            o_ref[rows, cols] = xn
        o_ref[rows, :] = o_ref[rows, :] * lax.rsqrt(ss / d + EPS) * gfin_ref[...]


def _combine(dest0, dest1, x, gate, route, g_final, y):
    bsz, n, d = x.shape
    s = d // LANES
    tm = min(MOE_TOKENS // 2, n)
    grid_spec = pltpu.PrefetchScalarGridSpec(
        num_scalar_prefetch=2,
        grid=(bsz, n // tm),
        in_specs=[
            pl.BlockSpec((None, tm, d), lambda b, i, d0, d1: (b, i, 0)),
            pl.BlockSpec((None, 1, d), lambda b, i, d0, d1: (b, 0, 0)),
            pl.BlockSpec((None, tm, LANES), lambda b, i, d0, d1: (b, i, 0)),
            pl.BlockSpec((1, d), lambda b, i, d0, d1: (0, 0)),
            pl.BlockSpec(memory_space=pl.ANY),
        ],
        out_specs=pl.BlockSpec((None, tm, d), lambda b, i, d0, d1: (b, i, 0)),
        scratch_shapes=[pltpu.VMEM((2, 2, tm * s, LANES), F32), pltpu.SemaphoreType.DMA((2,))],
    )
    return pl.pallas_call(
        _combine_kernel,
        grid_spec=grid_spec,
        out_shape=jax.ShapeDtypeStruct((bsz, n, d), F32),
        compiler_params=_cparams(("arbitrary", "arbitrary")),
        name="moe_combine",
    )(dest0, dest1, x, gate, route, g_final, y)


def _moe(hx, x, gate, route, route_t, counts, g_final, w1, w3, w2):
    bsz, n, d = x.shape
    ne = w1.shape[0]
    n_tok = bsz * n
    n_rows = -(-(2 * n_tok) // MOE_BLK) * MOE_BLK + ne * MOE_BLK
    n_blocks = n_rows // MOE_BLK
    cnt = counts[0, :ne].astype(jnp.int32)
    padded = (cnt + MOE_BLK - 1) // MOE_BLK * MOE_BLK
    pad_end = jnp.cumsum(padded)
    pad_start = pad_end - padded
    n_used = pad_end[-1] // MOE_BLK
    blk = jnp.minimum(jnp.arange(n_blocks, dtype=jnp.int32), n_used - 1)
    block_expert = jnp.minimum(
        jnp.sum((blk[:, None] * MOE_BLK >= pad_end[None, :]).astype(jnp.int32), axis=1), ne - 1)
    def dest(e_row, rank_row):
        e = route_t[e_row].astype(jnp.int32)
        start = sum(jnp.where(e == k, pad_start[k], 0) for k in range(ne))
        return start + route_t[rank_row].astype(jnp.int32)

    dest0 = dest(R_E0, R_RANK0)
    dest1 = dest(R_E1, R_RANK1)

    last_blk = jnp.where(padded > 0, pad_end - MOE_BLK, -1)
    trail = n_used + jnp.arange(ne, dtype=jnp.int32)
    trail = jnp.where(trail < n_blocks, trail * MOE_BLK, -1)
    zero_rows = jnp.concatenate([last_blk, trail]).astype(jnp.int32)

    ns = d // LANES
    hs = _dispatch(dest0, dest1, zero_rows, hx.reshape(n_tok, ns, LANES), n_rows)
    y = _experts(block_expert, n_used.reshape(1).astype(jnp.int32),
                 hs.reshape(n_rows * ns, LANES), w1, w3, w2)
    return _combine(dest0, dest1, x, gate, route, g_final, y)


def _rope_tables(n, hd):
    rows = n // GRID_W
    n_freq = hd // 4
    inv = ROPE_BASE ** (-jnp.arange(n_freq, dtype=F32) / n_freq)
    row_ang = jnp.arange(rows, dtype=F32)[:, None] * inv
    col_ang = jnp.arange(GRID_W, dtype=F32)[:, None] * inv

    def table(of_row, of_col):
        a = jnp.broadcast_to(of_row[:, None, :], (rows, GRID_W, n_freq))
        b = jnp.broadcast_to(of_col[None, :, :], (rows, GRID_W, n_freq))
        return jnp.concatenate([a, b], axis=-1).reshape(n, 2 * n_freq)

    cos = table(jnp.cos(row_ang), jnp.cos(col_ang))
    sin = table(jnp.sin(row_ang), jnp.sin(col_ang))
    return jnp.concatenate([cos, cos], axis=-1), jnp.concatenate([-sin, sin], axis=-1)


def _block_diag_gates(rg_wa, rg_wx):
    w = jnp.stack([rg_wa, rg_wx], axis=1)
    nd, ng, nk, c, _ = w.shape
    per = RG_CG // c
    w = w.reshape(nd, ng, nk // per, per, c, c)
    eye = jnp.eye(per, dtype=w.dtype)
    full = jnp.einsum('dgmpij,pq->dgmpiqj', w, eye)
    return full.reshape(nd, ng, nk // per, RG_CG, RG_CG).astype(BF16)


def _mixer(p, rope, states, conv_w, conv_b, wg, bg, lam, log_g, rg_w):
    rg_out, h_last = _rglru(p, conv_w, conv_b, wg, bg, lam, states[0])
    ret_out, s_last = _retention(p, log_g, rope[0], rope[1], states[1], 2 * rg_w)
    return rg_out, ret_out, (h_last, s_last)


def kernel(x, c, ctx, c_ctx, w_mod, b_mod, g_mix, g_ffn, g_final, w_in, w_out, conv_w, conv_b,
           rg_wa, rg_ba, rg_wx, rg_bx, rg_lam, ret_decay, ffn_w1, ffn_w3, ffn_w2,
           moe_router, moe_router_b, moe_w1, moe_w3, moe_w2):
    bsz, n_lat, d = x.shape
    n_ctx = ctx.shape[1]
    depth = w_mod.shape[0]
    rg_w = rg_lam.shape[2]
    nh = ret_decay.shape[2]
    hd = (w_out.shape[1] - rg_w) // nh
    ne = moe_router.shape[2]
    assert depth == 2, "kernel is written for the two-layer block (dense FFN, then MoE)"

    rows = -(-(bsz + 1) // SUBLANES) * SUBLANES
    c_all = jnp.zeros((rows, d), F32).at[:bsz].set(c).at[bsz].set(c_ctx)
    mod = _modulation(c_all, w_mod, b_mod)
    mod = mod.reshape(depth, rows, N_MOD, d)
    mod_lat = mod[:, :bsz, None]
    mod_ctx = jnp.broadcast_to(mod[:, bsz:bsz + 1, None], mod_lat.shape)

    rope_lat = _rope_tables(n_lat, hd)
    rope_ctx = (jnp.ones((n_ctx, hd), F32), jnp.zeros((n_ctx, hd), F32))
    log_g = jax.nn.log_sigmoid(ret_decay.astype(F32))
    zero_states = (jnp.zeros((bsz, 2, rg_w), F32), jnp.zeros((bsz, 2, nh, hd, hd), F32))

    for l in range(depth):
        last = l == depth - 1
        ml = [mod_lat[l, :, :, j] for j in range(N_MOD)]
        mc = [mod_ctx[l, :, :, j] for j in range(N_MOD)]
        g_m = g_mix[l].reshape(1, d)
        g_f = g_ffn[l].reshape(1, d)
        w_in_l = w_in[l].astype(BF16)
        w_out_l = w_out[l].astype(BF16)
        wg = _block_diag_gates(0.5 * rg_wa[l], 0.5 * rg_wx[l])
        bg = 0.5 * jnp.stack([rg_ba[l], rg_bx[l]], axis=1)
        mix_p = (conv_w[l], conv_b[l], wg, bg, rg_lam[l], log_g[l], rg_w)

        pc = _in_proj(ctx, g_m, mc[0], mc[1], w_in_l)
        rg_c, ret_c, ctx_states = _mixer(pc, rope_ctx, zero_states, *mix_p)
        px = _in_proj(x, g_m, ml[0], ml[1], w_in_l)
        rg_x, ret_x, _ = _mixer(px, rope_lat, ctx_states, *mix_p)

        if l % 2 == 0:
            i = l // 2
            w1, w3, w2 = ffn_w1[i].astype(BF16), ffn_w3[i].astype(BF16), ffn_w2[i].astype(BF16)
            nxt = moe_w1[i], moe_w3[i], moe_w2[i]
            x, cast = _mix_ffn(x, rg_x, ret_x, w_out_l, ml[2], g_f, ml[3], ml[4], ml[5],
                               w1, w3, w2, cast=[w.reshape(-1, w.shape[-1]) for w in nxt])
            moe_w = [c.reshape(w.shape) for c, w in zip(cast, nxt)]
            if not last:
                ctx, _ = _mix_ffn(ctx, rg_c, ret_c, w_out_l, mc[2], g_f, mc[3], mc[4], mc[5],
                                  w1, w3, w2)
        else:
            i = l // 2
            assert ne <= SUBLANES
            wr = jnp.zeros((d, LANES), F32).at[:, :ne].set(moe_router[i])
            wr_hi = wr.astype(BF16)
            wr_lo = (wr - wr_hi.astype(F32)).astype(BF16)
            wr2 = jnp.concatenate([wr_hi, wr_lo], axis=1)
            br = jnp.full((1, LANES), NEG_BIG, F32).at[0, :ne].set(moe_router_b[i])
            x, hx, route, route_t, counts = _out_proj_route(
                x, rg_x, ret_x, w_out_l, ml[2], g_f, ml[3], ml[4], wr2, br)
            x = _moe(hx, x, ml[5], route, route_t, counts, g_final.reshape(1, d), *moe_w)
    return x
```

```python
import functools

import jax
import jax.numpy as jnp
from jax import lax
from jax.experimental import pallas as pl
from jax.experimental.pallas import tpu as pltpu

F32 = jnp.float32
BF16 = jnp.bfloat16

EPS = 1e-6
RG_C = 8.0
CONV_W = 4
CONV_LEFT = 2
RET_CHUNK = 256
ROPE_BASE = 10000.0
GRID_W = 64
N_MOD = 6

ROW_TILE = 512

LANES = 128
SUBLANES = 8
VMEM_LIMIT = 56 * 1024 * 1024
NEG_BIG = -1e30
LOG2_E = 1.4426950408889634


def _cparams(sem):
    return pltpu.CompilerParams(dimension_semantics=sem, vmem_limit_bytes=VMEM_LIMIT)


def _modulation_kernel(c_ref, w_ref, b_ref, o_ref):
    c = c_ref[...]
    s = c * jax.nn.sigmoid(c)
    o_ref[...] = jnp.dot(s, w_ref[...], precision=lax.Precision.HIGHEST,
                         preferred_element_type=F32) + b_ref[...]


def _modulation(c_all, w_mod, b_mod):
    depth, d, six_d = w_mod.shape
    rows = c_all.shape[0]
    tn = six_d // 4
    return pl.pallas_call(
        _modulation_kernel,
        grid=(depth, six_d // tn),
        in_specs=[
            pl.BlockSpec((rows, d), lambda l, j: (0, 0)),
            pl.BlockSpec((None, d, tn), lambda l, j: (l, 0, j)),
            pl.BlockSpec((None, 1, tn), lambda l, j: (l, 0, j)),
        ],
        out_specs=pl.BlockSpec((None, rows, tn), lambda l, j: (l, 0, j)),
        out_shape=jax.ShapeDtypeStruct((depth, rows, six_d), F32),
        compiler_params=_cparams(("arbitrary", "arbitrary")),
        name="modulation",
    )(c_all, w_mod, b_mod.reshape(depth, 1, six_d))


def _norm_mod(x, g, shift, scale):
    ms = jnp.mean(x * x, axis=-1, keepdims=True)
    y = x * lax.rsqrt(ms + EPS) * g
    return y * (1.0 + scale) + shift


def _in_proj_kernel(x_ref, g_ref, shift_ref, scale_ref, w_ref, o_ref):
    h = _norm_mod(x_ref[...], g_ref[...], shift_ref[...], scale_ref[...])
    o_ref[...] = jnp.dot(h.astype(BF16), w_ref[...],
                         preferred_element_type=F32).astype(o_ref.dtype)


def _in_proj(x, g, shift, scale, w):
    bsz, n, d = x.shape
    f = w.shape[1]
    tm = min(2 * ROW_TILE, n)
    vec = pl.BlockSpec((None, 1, d), lambda b, i: (b, 0, 0))
    return pl.pallas_call(
        _in_proj_kernel,
        grid=(bsz, n // tm),
        in_specs=[
            pl.BlockSpec((None, tm, d), lambda b, i: (b, i, 0)),
            pl.BlockSpec((1, d), lambda b, i: (0, 0)),
            vec, vec,
            pl.BlockSpec((d, f), lambda b, i: (0, 0)),
        ],
        out_specs=pl.BlockSpec((None, tm, f), lambda b, i: (b, i, 0)),
        out_shape=jax.ShapeDtypeStruct((bsz, n, f), BF16),
        compiler_params=_cparams(("parallel", "parallel")),
        name="in_proj",
    )(x, g, shift, scale, w)


RG_TILE = 256
RG_CG = 256
RG_UNROLL = 4
RG_SEG = RG_TILE // SUBLANES
RG_PITCH = RG_SEG + SUBLANES


def _rglru_kernel(u_ref, yg_ref, cw_ref, cb_ref, wg_ref, bg_ref, lam_ref, h0_ref,
                  o_ref, hl_ref, u_scr, hf_scr, ab_scr, nat_scr):
    n = u_ref.shape[0]
    tt = RG_TILE
    nt = n // tt
    nseg = n // RG_SEG
    cg = u_ref.shape[1]
    lane_cols = [slice(g * LANES, (g + 1) * LANES) for g in range(cg // LANES)]

    zeros = jnp.zeros((RG_PITCH, LANES), F32)
    for g, cols in enumerate(lane_cols):
        u_scr[g, pl.ds(0, RG_PITCH), :] = zeros
        u_scr[g, pl.ds(nseg * RG_PITCH + RG_SEG, SUBLANES), :] = zeros[0:SUBLANES]

    def fill(s, carry):
        src = pl.multiple_of(s * RG_SEG, RG_SEG)
        dst = pl.multiple_of((s + 1) * RG_PITCH, SUBLANES)
        for g, cols in enumerate(lane_cols):
            rows = u_ref[pl.ds(src, RG_SEG), cols].astype(F32)
            u_scr[g, pl.ds(dst, RG_SEG), :] = rows
            u_scr[g, pl.ds(dst - SUBLANES, SUBLANES), :] = rows[0:SUBLANES]
        return carry

    lax.fori_loop(0, nseg, fill, 0)

    seg = RG_SEG
    cw_rows = [[jnp.broadcast_to(cw_ref[k:k + 1, c], (SUBLANES, LANES)) for c in lane_cols]
               for k in range(CONV_W)]
    cb_rows = [jnp.broadcast_to(cb_ref[:, c], (SUBLANES, LANES)) for c in lane_cols]

    def conv_tile(i):
        base = pl.multiple_of((i * SUBLANES + 1) * RG_PITCH, SUBLANES)
        blocks = []
        for j in range(seg):
            lane_groups = []
            for g in range(cg // LANES):
                acc = cb_rows[g]
                for k in range(CONV_W):
                    q = j + k - CONV_LEFT
                    start = base + q if q >= 0 else base - RG_PITCH + seg + q
                    acc = acc + cw_rows[k][g] * u_scr[g, pl.ds(start, SUBLANES, stride=RG_PITCH), :]
                lane_groups.append(acc)
            blocks.append(jnp.concatenate(lane_groups, axis=1))
        return jnp.concatenate(blocks, axis=0)

    def gates(uc, d):
        ub = uc.astype(BF16)
        ta = jnp.tanh(jnp.dot(ub, wg_ref[d, 0], preferred_element_type=F32) + bg_ref[d, 0:1, :])
        ti = jnp.tanh(jnp.dot(ub, wg_ref[d, 1], preferred_element_type=F32) + bg_ref[d, 1:2, :])
        z = -lam_ref[d:d + 1, :]
        sp = jnp.maximum(z, 0.0) + jnp.log1p(jnp.exp(-jnp.abs(z)))
        c2 = (-0.5 * RG_C * LOG2_E) * sp
        a = jnp.exp2(c2 + c2 * ta)
        b = jnp.exp2((0.5 * LOG2_E) * jnp.log(1.0 - a * a)) * ((0.5 + 0.5 * ti) * uc)
        return a, b

    def scan_tile(a, b, carry, reverse):
        steps = range(seg - 1, -1, -1) if reverse else range(seg)
        h = acum = None
        h_loc = [None] * seg
        a_cum = [None] * seg
        for j in steps:
            rows = slice(j * SUBLANES, (j + 1) * SUBLANES)
            h = b[rows] if h is None else a[rows] * h + b[rows]
            acum = a[rows] if acum is None else a[rows] * acum
            h_loc[j], a_cum[j] = h, acum
        order = range(SUBLANES - 1, -1, -1) if reverse else range(SUBLANES)
        c = carry
        c_in = [None] * SUBLANES
        for s in order:
            c_in[s] = c
            c = h[s:s + 1, :] + acum[s:s + 1, :] * c
        c_in = jnp.concatenate(c_in, axis=0)
        out = jnp.concatenate([h_loc[j] + a_cum[j] * c_in for j in range(seg)], axis=0)
        return out, c

    def fwd_body(i, carry):
        t0 = pl.multiple_of(i * tt, tt)
        uc = conv_tile(i)
        a, b = gates(uc, 0)
        h, carry = scan_tile(a, b, carry, False)
        hf_scr[pl.ds(t0, tt), :] = h
        a, b = gates(uc, 1)
        ab_scr[0, pl.ds(t0, tt), :] = a
        ab_scr[1, pl.ds(t0, tt), :] = b
        return carry

    hf_last = lax.fori_loop(0, nt, fwd_body, h0_ref[0:1, :], unroll=min(RG_UNROLL, nt))

    def bwd_body(i, carry):
        tile = nt - 1 - i
        t0 = pl.multiple_of(tile * tt, tt)
        h, carry = scan_tile(ab_scr[0, pl.ds(t0, tt), :], ab_scr[1, pl.ds(t0, tt), :], carry, True)
        hsum = hf_scr[pl.ds(t0, tt), :] + h
        for j in range(seg):
            for g, cols in enumerate(lane_cols):
                nat_scr[g, pl.ds(j, SUBLANES, stride=RG_PITCH), :] = (
                    hsum[j * SUBLANES:(j + 1) * SUBLANES, cols])
        hnat = jnp.concatenate(
            [jnp.concatenate([nat_scr[g, pl.ds(s * RG_PITCH, seg), :] for s in range(SUBLANES)],
                             axis=0) for g in range(len(lane_cols))], axis=1)
        yg = yg_ref[pl.ds(t0, tt), :].astype(F32)
        o_ref[pl.ds(t0, tt), :] = (jax.nn.gelu(yg) * hnat).astype(o_ref.dtype)
        return carry

    hb_last = lax.fori_loop(0, nt, bwd_body, h0_ref[1:2, :], unroll=min(RG_UNROLL, nt))
    hl_ref[0:1, :] = hf_last
    hl_ref[1:2, :] = hb_last


def _rglru(p, conv_w, conv_b, wg, bg, lam, h0):
    bsz, n, _ = p.shape
    rg_w = lam.shape[1]
    ncg = rg_w // RG_CG
    assert n % RG_TILE == 0 and rg_w % RG_CG == 0
    return pl.pallas_call(
        _rglru_kernel,
        grid=(bsz, ncg),
        in_specs=[
            pl.BlockSpec((None, n, RG_CG), lambda b, c: (b, 0, c)),
            pl.BlockSpec((None, n, RG_CG), lambda b, c: (b, 0, ncg + c)),
            pl.BlockSpec((CONV_W, RG_CG), lambda b, c: (0, c)),
            pl.BlockSpec((1, RG_CG), lambda b, c: (0, c)),
            pl.BlockSpec((2, 2, None, RG_CG, RG_CG), lambda b, c: (0, 0, c, 0, 0)),
            pl.BlockSpec((2, 2, RG_CG), lambda b, c: (0, 0, c)),
            pl.BlockSpec((2, RG_CG), lambda b, c: (0, c)),
            pl.BlockSpec((None, 2, RG_CG), lambda b, c: (b, 0, c)),
        ],
        out_specs=[
            pl.BlockSpec((None, n, RG_CG), lambda b, c: (b, 0, c)),
            pl.BlockSpec((None, 2, RG_CG), lambda b, c: (b, 0, c)),
        ],
        out_shape=[
            jax.ShapeDtypeStruct((bsz, n, rg_w), BF16),
            jax.ShapeDtypeStruct((bsz, 2, rg_w), F32),
        ],
        scratch_shapes=[
            pltpu.VMEM((RG_CG // LANES, (n // RG_SEG + 1) * RG_PITCH, LANES), F32),
            pltpu.VMEM((n, RG_CG), F32),
            pltpu.VMEM((2, n, RG_CG), F32),
            pltpu.VMEM((RG_CG // LANES, SUBLANES * RG_PITCH, LANES), F32),
        ],
        compiler_params=_cparams(("parallel", "parallel")),
        name="rglru",
    )(p, p, conv_w, conv_b.reshape(1, rg_w), wg, bg, lam, h0)


RET_UNROLL = 16


def _retention_kernel(lg_ref, q_ref, k_ref, v_ref, g_ref, cs_ref, sn_ref, s0_ref,
                      o_ref, so_ref, kr_scr, st_scr):
    n, hd = q_ref.shape
    L = RET_CHUNK
    nc = n // L
    head = pl.program_id(1)
    lgf = lg_ref[0, head]
    lgb = lg_ref[1, head]

    def rope(t, rows):
        return t * cs_ref[rows, :] + pltpu.roll(t, hd // 2, axis=1) * sn_ref[rows, :]

    kr_scr[...] = (rope(k_ref[...].astype(F32), slice(None)) * (hd ** -0.5)).astype(BF16)

    jc = lax.broadcasted_iota(jnp.int32, (L, 1), 0).astype(F32)
    q_dec_f = jnp.exp((jc + 1.0) * lgf)
    q_dec_b = jnp.exp((L - jc) * lgb)
    k_dec_f = jnp.exp((L - 1.0 - jc) * lgf)
    k_dec_b = jnp.exp(jc * lgb)
    ones = jnp.ones((1, hd), F32)
    chunk_f = jnp.exp(ones * (L * lgf))
    chunk_b = jnp.exp(ones * (L * lgb))
    ii = lax.broadcasted_iota(jnp.int32, (L, L), 0)
    jj = lax.broadcasted_iota(jnp.int32, (L, L), 1)
    diff = (ii - jj).astype(F32)
    dmat = jnp.where(diff >= 0.0, jnp.exp(jnp.maximum(diff, 0.0) * lgf),
                     jnp.exp(jnp.maximum(-diff, 0.0) * lgb))

    tn_dims = (((0,), (0,)), ((), ()))
    nt_dims = (((1,), (1,)), ((), ()))

    def state_update(c, state, k_dec, chunk_dec):
        t0 = pl.multiple_of(c * L, L)
        kd = (kr_scr[pl.ds(t0, L), :].astype(F32) * k_dec).astype(BF16)
        upd = lax.dot_general(kd, v_ref[pl.ds(t0, L), :], tn_dims, preferred_element_type=F32)
        return chunk_dec * state + upd

    def state_body(i, carry):
        f_state, r_state = carry
        cf = i
        cb = nc - 1 - i
        st_scr[cf, :, 0:hd] = f_state.astype(BF16)
        st_scr[cb, :, hd:2 * hd] = r_state.astype(BF16)
        return (state_update(cf, f_state, k_dec_f, chunk_f),
                state_update(cb, r_state, k_dec_b, chunk_b))

    f_fin, r_fin = lax.fori_loop(0, nc, state_body, (s0_ref[0], s0_ref[1]),
                                 unroll=min(RET_UNROLL, nc))
    so_ref[0] = f_fin
    so_ref[1] = r_fin

    def out_body(c, carry):
        t0 = pl.multiple_of(c * L, L)
        qc = rope(q_ref[pl.ds(t0, L), :].astype(F32), pl.ds(t0, L)).astype(BF16)
        kc = kr_scr[pl.ds(t0, L), :]
        vc = v_ref[pl.ds(t0, L), :]
        s = lax.dot_general(qc, kc, nt_dims, preferred_element_type=F32)
        o = jnp.dot((s * dmat).astype(BF16), vc, preferred_element_type=F32)
        cross = jnp.dot(qc, st_scr[c], preferred_element_type=F32)
        o = o + cross[:, 0:hd] * q_dec_f + cross[:, hd:2 * hd] * q_dec_b
        mu = jnp.mean(o, axis=-1, keepdims=True)
        oc = o - mu
        var = jnp.mean(oc * oc, axis=-1, keepdims=True)
        gate = g_ref[pl.ds(t0, L), :].astype(F32)
        o_ref[pl.ds(t0, L), :] = (gate * jax.nn.sigmoid(gate) * (oc * lax.rsqrt(var + EPS))).astype(o_ref.dtype)
        return carry

    lax.fori_loop(0, nc, out_body, 0, unroll=min(RET_UNROLL, nc))


def _retention(p, log_g, cs, sn, s0, col0):
    bsz, n, _ = p.shape
    nh = log_g.shape[1]
    hd = cs.shape[1]
    cb0 = col0 // hd

    def col(which):
        return pl.BlockSpec((None, n, hd), lambda b, h, lg: (b, 0, cb0 + which * nh + h))

    state = pl.BlockSpec((None, 2, None, hd, hd), lambda b, h, lg: (b, 0, h, 0, 0))
    table = pl.BlockSpec((n, hd), lambda b, h, lg: (0, 0))
    grid_spec = pltpu.PrefetchScalarGridSpec(
        num_scalar_prefetch=1,
        grid=(bsz, nh),
        in_specs=[col(0), col(1), col(2), col(3), table, table, state],
        out_specs=[pl.BlockSpec((None, n, hd), lambda b, h, lg: (b, 0, h)), state],
        scratch_shapes=[
            pltpu.VMEM((n, hd), BF16),
            pltpu.VMEM((n // RET_CHUNK, hd, 2 * hd), BF16),
        ],
    )
    return pl.pallas_call(
        _retention_kernel,
        grid_spec=grid_spec,
        out_shape=[
            jax.ShapeDtypeStruct((bsz, n, nh * hd), BF16),
            jax.ShapeDtypeStruct((bsz, 2, nh, hd, hd), F32),
        ],
        compiler_params=_cparams(("parallel", "parallel")),
        name="retention",
    )(log_g, p, p, p, p, cs, sn, s0)


R_RANK0, R_RANK1, R_E0, R_E1, R_W0, R_W1 = range(6)


def _mix_residual(x_ref, rg_ref, ret_ref, wo_ref, gate_ref, rows=slice(None)):
    rg_w = rg_ref.shape[1]
    y = jnp.dot(rg_ref[rows, :], wo_ref[0:rg_w, :], preferred_element_type=F32)
    y = y + jnp.dot(ret_ref[rows, :], wo_ref[rg_w:, :], preferred_element_type=F32)
    return x_ref[rows, :] + gate_ref[...] * y


ROUTE_ROWS = 512


def _out_proj_kernel(x_ref, rg_ref, ret_ref, wo_ref, gate_ref, g_ref, shift_ref, scale_ref,
                     wr2_ref, br_ref, xo_ref, h_ref, rt_ref, rtt_ref, cnt_ref, carry_scr):
    tm, d = x_ref.shape
    ns = d // LANES
    nr = min(ROUTE_ROWS, tm)

    @pl.when((pl.program_id(0) == 0) & (pl.program_id(1) == 0))
    def _():
        carry_scr[...] = jnp.zeros_like(carry_scr)

    erow = lax.broadcasted_iota(jnp.int32, (SUBLANES, nr), 0).astype(F32)
    ri = lax.broadcasted_iota(jnp.int32, (nr, nr), 0)
    ci = lax.broadcasted_iota(jnp.int32, (nr, nr), 1)
    earlier = jnp.where(ri < ci, 1.0, 0.0).astype(BF16)
    count = carry_scr[:, 0:1]

    for r0 in range(0, tm, nr):
        rows = pl.ds(r0, nr)
        xn = _mix_residual(x_ref, rg_ref, ret_ref, wo_ref, gate_ref, rows)
        xo_ref[rows, :] = xn
        hx = _norm_mod(xn, g_ref[...], shift_ref[...], scale_ref[...])

        for s in range(ns):
            h_ref[pl.ds(r0 * ns + s, nr, stride=ns), :] = hx[:, s * LANES:(s + 1) * LANES]

        h_hi = hx.astype(BF16)
        h_lo = (hx - h_hi.astype(F32)).astype(BF16)
        part = jnp.dot(h_hi, wr2_ref[...], preferred_element_type=F32)
        logits = (part[:, :LANES] + part[:, LANES:]
                  + jnp.dot(h_lo, wr2_ref[:, :LANES], preferred_element_type=F32) + br_ref[...])
        lt = jnp.transpose(logits)[0:SUBLANES, :]
        m0 = jnp.max(lt, axis=0, keepdims=True)
        i0 = jnp.min(jnp.where(lt == m0, erow, float(SUBLANES)), axis=0, keepdims=True)
        rest_l = jnp.where(erow == i0, NEG_BIG, lt)
        m1 = jnp.max(rest_l, axis=0, keepdims=True)
        i1 = jnp.min(jnp.where(rest_l == m1, erow, float(SUBLANES)), axis=0, keepdims=True)
        e = jnp.exp(m1 - m0)
        w0 = 1.0 / (1.0 + e)
        w1 = e / (1.0 + e)

        sel0 = erow == i0
        sel1 = erow == i1
        mask = jnp.where(sel0 | sel1, 1.0, 0.0)
        before = jnp.dot(mask.astype(BF16), earlier, preferred_element_type=F32) + count
        rank0 = jnp.sum(jnp.where(sel0, before, 0.0), axis=0, keepdims=True)
        rank1 = jnp.sum(jnp.where(sel1, before, 0.0), axis=0, keepdims=True)
        count = count + jnp.sum(mask, axis=1, keepdims=True)
        rec = jnp.zeros((SUBLANES, nr), F32)
        for idx, val in enumerate((rank0, rank1, i0, i1, w0, w1)):
            rec = jnp.where(erow == idx, val, rec)
        rtt_ref[:, r0:r0 + nr] = rec
        rt_ref[rows, :] = jnp.transpose(
            jnp.concatenate([rec, jnp.zeros((LANES - SUBLANES, nr), F32)], axis=0))

    carry_scr[...] = jnp.broadcast_to(count, carry_scr.shape)
    on_diag = (lax.broadcasted_iota(jnp.int32, (SUBLANES, LANES), 0)
               == lax.broadcasted_iota(jnp.int32, (SUBLANES, LANES), 1))
    cnt_ref[...] = jnp.sum(jnp.where(on_diag, count, 0.0), axis=0, keepdims=True)


def _out_proj_route(x, rg, ret, w_out, gate, g, shift, scale, wr2, br):
    bsz, n, d = x.shape
    tm = min(ROUTE_ROWS, n)
    vec = pl.BlockSpec((None, 1, d), lambda b, i: (b, 0, 0))
    tile = lambda w: pl.BlockSpec((None, tm, w), lambda b, i: (b, i, 0))
    const = lambda shape: pl.BlockSpec(shape, lambda b, i: (0, 0))
    return pl.pallas_call(
        _out_proj_kernel,
        grid=(bsz, n // tm),
        in_specs=[
            tile(d), tile(rg.shape[2]), tile(ret.shape[2]), const(w_out.shape),
            vec, const((1, d)), vec, vec, const((d, 2 * LANES)), const((1, LANES)),
        ],
        out_specs=[
            tile(d),
            pl.BlockSpec((None, tm * (d // LANES), LANES), lambda b, i: (b, i, 0)),
            tile(LANES),
            pl.BlockSpec((SUBLANES, tm), lambda b, i: (0, b * (n // tm) + i)),
            const((1, LANES)),
        ],
        out_shape=[
            jax.ShapeDtypeStruct((bsz, n, d), F32),
            jax.ShapeDtypeStruct((bsz, n * (d // LANES), LANES), F32),
            jax.ShapeDtypeStruct((bsz, n, LANES), F32),
            jax.ShapeDtypeStruct((SUBLANES, bsz * n), F32),
            jax.ShapeDtypeStruct((1, LANES), F32),
        ],
        scratch_shapes=[pltpu.VMEM((SUBLANES, LANES), F32)],
        compiler_params=_cparams(("arbitrary", "arbitrary")),
        name="out_proj_route",
    )(x, rg, ret, w_out, gate, g, shift, scale, wr2, br)


def _swiglu_acc(h, w1_ref, w3_ref, w2_ref):
    a = jnp.dot(h, w1_ref[...], preferred_element_type=F32)
    b = jnp.dot(h, w3_ref[...], preferred_element_type=F32)
    act = a * jax.nn.sigmoid(a) * b
    return jnp.dot(act.astype(BF16), w2_ref[...], preferred_element_type=F32)


def _mix_ffn_kernel(n_cast, x_ref, rg_ref, ret_ref, wo_ref, gmix_ref, g_ref, shift_ref, scale_ref,
                    gffn_ref, w1_ref, w3_ref, w2_ref, *rest):
    cast_in, o_ref, cast_out = rest[:n_cast], rest[n_cast], rest[n_cast + 1:]
    xn = _mix_residual(x_ref, rg_ref, ret_ref, wo_ref, gmix_ref)
    hx = _norm_mod(xn, g_ref[...], shift_ref[...], scale_ref[...]).astype(BF16)
    o_ref[...] = xn + gffn_ref[...] * _swiglu_acc(hx, w1_ref, w3_ref, w2_ref)
    for src, dst in zip(cast_in, cast_out):
        dst[...] = src[...].astype(dst.dtype)


def _mix_ffn(x, rg, ret, w_out, gate_mix, g, shift, scale, gate_ffn, w1, w3, w2, cast=()):
    bsz, n, d = x.shape
    f = w1.shape[1]
    tm = min(ROW_TILE, n)
    steps = bsz * (n // tm)
    vec = pl.BlockSpec((None, 1, d), lambda b, i: (b, 0, 0))
    tile = lambda w: pl.BlockSpec((None, tm, w), lambda b, i: (b, i, 0))
    resident = lambda shape: pl.BlockSpec(shape, lambda b, i: (0, 0), pipeline_mode=pl.Buffered(1))
    slabs = []
    for a in cast:
        rows, cols = a.shape
        assert rows % (steps * 2 * SUBLANES) == 0
        slabs.append(pl.BlockSpec((rows // steps, cols), lambda b, i: (b * (n // tm) + i, 0)))
    out = pl.pallas_call(
        functools.partial(_mix_ffn_kernel, len(cast)),
        grid=(bsz, n // tm),
        in_specs=[
            tile(d), tile(rg.shape[2]), tile(ret.shape[2]), resident(w_out.shape),
            vec, resident((1, d)), vec, vec, vec,
            resident((d, f)), resident((d, f)), resident((f, d)),
        ] + slabs,
        out_specs=[tile(d)] + slabs,
        out_shape=[jax.ShapeDtypeStruct((bsz, n, d), F32)]
        + [jax.ShapeDtypeStruct(a.shape, BF16) for a in cast],
        compiler_params=_cparams(("parallel", "parallel")),
        name="mix_ffn",
    )(x, rg, ret, w_out, gate_mix, g, shift, scale, gate_ffn, w1, w3, w2, *cast)
    return out[0], tuple(out[1:])


MOE_BLK = 256
MOE_TOKENS = 1024


def _dispatch_kernel(nsteps, d0_ref, d1_ref, zr_ref, hx_ref, hs_ref, zbuf, stage, sem, zsem):
    tm = hx_ref.shape[0]
    step = pl.program_id(0)
    base = step * tm
    slot = step % 2

    @pl.when(pl.program_id(0) == 0)
    def _():
        zbuf[...] = jnp.zeros_like(zbuf)
        for z in range(zr_ref.shape[0]):
            @pl.when(zr_ref[z] >= 0)
            def _():
                pltpu.make_async_copy(zbuf, hs_ref.at[pl.ds(zr_ref[z], MOE_BLK)], zsem).start()
        for z in range(zr_ref.shape[0]):
            @pl.when(zr_ref[z] >= 0)
            def _():
                pltpu.make_async_copy(zbuf, hs_ref.at[pl.ds(zr_ref[z], MOE_BLK)], zsem).wait()

    def drain(which):
        for _ in range(2):
            pltpu.make_async_copy(stage.at[which], hs_ref.at[pl.ds(0, tm)], sem.at[which]).wait()

    @pl.when(step >= 2)
    def _():
        drain(slot)

    stage[slot] = hx_ref[...]
    src = stage.at[slot]

    def issue(t, carry):
        pltpu.make_async_copy(src.at[t], hs_ref.at[d0_ref[base + t]], sem.at[slot]).start(priority=0)
        pltpu.make_async_copy(src.at[t], hs_ref.at[d1_ref[base + t]], sem.at[slot]).start(priority=1)
        return carry

    lax.fori_loop(0, tm, issue, 0, unroll=8)

    @pl.when(step == nsteps - 1)
    def _():
        drain(slot)
        if nsteps >= 2:
            drain(1 - slot)


def _dispatch(dest0, dest1, zero_rows, hx, n_rows):
    n_tok, s, lanes = hx.shape
    tm = min(MOE_TOKENS, n_tok)
    grid_spec = pltpu.PrefetchScalarGridSpec(
        num_scalar_prefetch=3,
        grid=(n_tok // tm,),
        in_specs=[pl.BlockSpec((tm, s, lanes), lambda i, d0, d1, zr: (i, 0, 0))],
        out_specs=pl.BlockSpec(memory_space=pl.ANY),
        scratch_shapes=[pltpu.VMEM((MOE_BLK, s, lanes), F32),
                        pltpu.VMEM((2, tm, s, lanes), F32),
                        pltpu.SemaphoreType.DMA((2,)), pltpu.SemaphoreType.DMA(())],
    )
    return pl.pallas_call(
        functools.partial(_dispatch_kernel, n_tok // tm),
        grid_spec=grid_spec,
        out_shape=jax.ShapeDtypeStruct((n_rows, s, lanes), F32),
        compiler_params=_cparams(("arbitrary",)),
        name="moe_dispatch",
    )(dest0, dest1, zero_rows, hx)


def _experts_kernel(be_ref, nu_ref, hs_ref, w1_ref, w3_ref, w2_ref, y_ref):
    j = pl.program_id(0)
    d = w1_ref.shape[0]
    ns = d // LANES
    blk = hs_ref.shape[0] // ns

    @pl.when(j < nu_ref[0])
    def _():
        h = jnp.concatenate(
            [hs_ref[pl.ds(s, blk, stride=ns), :].astype(BF16) for s in range(ns)], axis=1)
        out = _swiglu_acc(h, w1_ref, w3_ref, w2_ref)
        for s in range(ns):
            y_ref[pl.ds(s, blk, stride=ns), :] = out[:, s * LANES:(s + 1) * LANES]

    @pl.when(j >= nu_ref[0])
    def _():
        y_ref[...] = jnp.zeros_like(y_ref)


def _experts(block_expert, n_used, hs, w1, w3, w2):
    ne, d, f = w1.shape
    s = d // LANES
    n_rows = hs.shape[0] // s
    n_blocks = n_rows // MOE_BLK

    def expert(shape):
        return pl.BlockSpec((None,) + shape, lambda j, be, nu: (be[j], 0, 0),
                            pipeline_mode=pl.Buffered(1))

    grid_spec = pltpu.PrefetchScalarGridSpec(
        num_scalar_prefetch=2,
        grid=(n_blocks,),
        in_specs=[
            pl.BlockSpec((MOE_BLK * s, LANES), lambda j, be, nu: (jnp.minimum(j, nu[0] - 1), 0)),
            expert((d, f)), expert((d, f)), expert((f, d)),
        ],
        out_specs=pl.BlockSpec((MOE_BLK * s, LANES), lambda j, be, nu: (j, 0)),
    )
    return pl.pallas_call(
        _experts_kernel,
        grid_spec=grid_spec,
        out_shape=jax.ShapeDtypeStruct((n_rows * s, LANES), F32),
        compiler_params=_cparams(("arbitrary",)),
        name="moe_experts",
    )(block_expert, n_used, hs, w1, w3, w2)


def _combine_kernel(d0_ref, d1_ref, x_ref, gate_ref, rt_ref, gfin_ref, y_ref, o_ref, ybuf, sem):
    tm, d = x_ref.shape
    ns = d // LANES
    step = pl.program_id(0) * pl.num_programs(1) + pl.program_id(1)
    nsteps = pl.num_programs(0) * pl.num_programs(1)
    slot = step % 2

    def row(ref, r):
        return ref.at[pl.ds(pl.multiple_of(r * ns, ns), ns)]

    def start_gather(block, into):
        base = block * tm

        def issue(t, carry):
            pltpu.make_async_copy(row(y_ref, d0_ref[base + t]), row(ybuf.at[into, 0], t),
                                  sem.at[into]).start(priority=0)
            pltpu.make_async_copy(row(y_ref, d1_ref[base + t]), row(ybuf.at[into, 1], t),
                                  sem.at[into]).start(priority=1)
            return carry

        lax.fori_loop(0, tm, issue, 0, unroll=8)

    @pl.when(step == 0)
    def _():
        start_gather(0, 0)

    @pl.when(step + 1 < nsteps)
    def _():
        start_gather(step + 1, 1 - slot)

    for k in range(2):
        pltpu.make_async_copy(y_ref.at[pl.ds(0, tm * ns)], ybuf.at[slot, k], sem.at[slot]).wait()

    chunk = 64
    for r0 in range(0, tm, chunk):
        rows = pl.ds(r0, chunk)
        rt = rt_ref[rows, :]
        w0 = rt[:, R_W0:R_W0 + 1]
        w1 = rt[:, R_W1:R_W1 + 1]
        ss = jnp.zeros((chunk, 1), F32)
        for s in range(ns):
            cols = slice(s * LANES, (s + 1) * LANES)
            moe = (w0 * ybuf[slot, 0, pl.ds(r0 * ns + s, chunk, stride=ns), :]
                   + w1 * ybuf[slot, 1, pl.ds(r0 * ns + s, chunk, stride=ns), :])
            xn = x_ref[rows, cols] + gate_ref[:, cols] * moe
            ss = ss + jnp.sum(xn * xn, axis=-1, keepdims=True)
            o_ref[rows, cols] = xn
        o_ref[rows, :] = o_ref[rows, :] * lax.rsqrt(ss / d + EPS) * gfin_ref[...]


def _combine(dest0, dest1, x, gate, route, g_final, y):
    bsz, n, d = x.shape
    s = d // LANES
    tm = min(MOE_TOKENS // 2, n)
    grid_spec = pltpu.PrefetchScalarGridSpec(
        num_scalar_prefetch=2,
        grid=(bsz, n // tm),
        in_specs=[
            pl.BlockSpec((None, tm, d), lambda b, i, d0, d1: (b, i, 0)),
            pl.BlockSpec((None, 1, d), lambda b, i, d0, d1: (b, 0, 0)),
            pl.BlockSpec((None, tm, LANES), lambda b, i, d0, d1: (b, i, 0)),
            pl.BlockSpec((1, d), lambda b, i, d0, d1: (0, 0)),
            pl.BlockSpec(memory_space=pl.ANY),
        ],
        out_specs=pl.BlockSpec((None, tm, d), lambda b, i, d0, d1: (b, i, 0)),
        scratch_shapes=[pltpu.VMEM((2, 2, tm * s, LANES), F32), pltpu.SemaphoreType.DMA((2,))],
    )
    return pl.pallas_call(
        _combine_kernel,
        grid_spec=grid_spec,
        out_shape=jax.ShapeDtypeStruct((bsz, n, d), F32),
        compiler_params=_cparams(("arbitrary", "arbitrary")),
        name="moe_combine",
    )(dest0, dest1, x, gate, route, g_final, y)


def _moe(hx, x, gate, route, route_t, counts, g_final, w1, w3, w2):
    bsz, n, d = x.shape
    ne = w1.shape[0]
    n_tok = bsz * n
    n_rows = -(-(2 * n_tok) // MOE_BLK) * MOE_BLK + ne * MOE_BLK
    n_blocks = n_rows // MOE_BLK
    cnt = counts[0, :ne].astype(jnp.int32)
    padded = (cnt + MOE_BLK - 1) // MOE_BLK * MOE_BLK
    pad_end = jnp.cumsum(padded)
    pad_start = pad_end - padded
    n_used = pad_end[-1] // MOE_BLK
    blk = jnp.minimum(jnp.arange(n_blocks, dtype=jnp.int32), n_used - 1)
    block_expert = jnp.minimum(
        jnp.sum((blk[:, None] * MOE_BLK >= pad_end[None, :]).astype(jnp.int32), axis=1), ne - 1)
    def dest(e_row, rank_row):
        e = route_t[e_row].astype(jnp.int32)
        start = sum(jnp.where(e == k, pad_start[k], 0) for k in range(ne))
        return start + route_t[rank_row].astype(jnp.int32)

    dest0 = dest(R_E0, R_RANK0)
    dest1 = dest(R_E1, R_RANK1)

    last_blk = jnp.where(padded > 0, pad_end - MOE_BLK, -1)
    trail = n_used + jnp.arange(ne, dtype=jnp.int32)
    trail = jnp.where(trail < n_blocks, trail * MOE_BLK, -1)
    zero_rows = jnp.concatenate([last_blk, trail]).astype(jnp.int32)

    ns = d // LANES
    hs = _dispatch(dest0, dest1, zero_rows, hx.reshape(n_tok, ns, LANES), n_rows)
    y = _experts(block_expert, n_used.reshape(1).astype(jnp.int32),
                 hs.reshape(n_rows * ns, LANES), w1, w3, w2)
    return _combine(dest0, dest1, x, gate, route, g_final, y)


def _rope_tables(n, hd):
    rows = n // GRID_W
    n_freq = hd // 4
    inv = ROPE_BASE ** (-jnp.arange(n_freq, dtype=F32) / n_freq)
    row_ang = jnp.arange(rows, dtype=F32)[:, None] * inv
    col_ang = jnp.arange(GRID_W, dtype=F32)[:, None] * inv

    def table(of_row, of_col):
        a = jnp.broadcast_to(of_row[:, None, :], (rows, GRID_W, n_freq))
        b = jnp.broadcast_to(of_col[None, :, :], (rows, GRID_W, n_freq))
        return jnp.concatenate([a, b], axis=-1).reshape(n, 2 * n_freq)

    cos = table(jnp.cos(row_ang), jnp.cos(col_ang))
    sin = table(jnp.sin(row_ang), jnp.sin(col_ang))
    return jnp.concatenate([cos, cos], axis=-1), jnp.concatenate([-sin, sin], axis=-1)


def _block_diag_gates(rg_wa, rg_wx):
    w = jnp.stack([rg_wa, rg_wx], axis=1)
    nd, ng, nk, c, _ = w.shape
    per = RG_CG // c
    w = w.reshape(nd, ng, nk // per, per, c, c)
    eye = jnp.eye(per, dtype=w.dtype)
    full = jnp.einsum('dgmpij,pq->dgmpiqj', w, eye)
    return full.reshape(nd, ng, nk // per, RG_CG, RG_CG).astype(BF16)


def _mixer(p, rope, states, conv_w, conv_b, wg, bg, lam, log_g, rg_w):
    rg_out, h_last = _rglru(p, conv_w, conv_b, wg, bg, lam, states[0])
    ret_out, s_last = _retention(p, log_g, rope[0], rope[1], states[1], 2 * rg_w)
    return rg_out, ret_out, (h_last, s_last)


def kernel(x, c, ctx, c_ctx, w_mod, b_mod, g_mix, g_ffn, g_final, w_in, w_out, conv_w, conv_b,
           rg_wa, rg_ba, rg_wx, rg_bx, rg_lam, ret_decay, ffn_w1, ffn_w3, ffn_w2,
           moe_router, moe_router_b, moe_w1, moe_w3, moe_w2):
    bsz, n_lat, d = x.shape
    n_ctx = ctx.shape[1]
    depth = w_mod.shape[0]
    rg_w = rg_lam.shape[2]
    nh = ret_decay.shape[2]
    hd = (w_out.shape[1] - rg_w) // nh
    ne = moe_router.shape[2]
    assert depth == 2, "kernel is written for the two-layer block (dense FFN, then MoE)"

    rows = -(-(bsz + 1) // SUBLANES) * SUBLANES
    c_all = jnp.zeros((rows, d), F32).at[:bsz].set(c).at[bsz].set(c_ctx)
    mod = _modulation(c_all, w_mod, b_mod)
    mod = mod.reshape(depth, rows, N_MOD, d)
    mod_lat = mod[:, :bsz, None]
    mod_ctx = jnp.broadcast_to(mod[:, bsz:bsz + 1, None], mod_lat.shape)

    rope_lat = _rope_tables(n_lat, hd)
    rope_ctx = (jnp.ones((n_ctx, hd), F32), jnp.zeros((n_ctx, hd), F32))
    log_g = jax.nn.log_sigmoid(ret_decay.astype(F32))
    zero_states = (jnp.zeros((bsz, 2, rg_w), F32), jnp.zeros((bsz, 2, nh, hd, hd), F32))

    for l in range(depth):
        last = l == depth - 1
        ml = [mod_lat[l, :, :, j] for j in range(N_MOD)]
        mc = [mod_ctx[l, :, :, j] for j in range(N_MOD)]
        g_m = g_mix[l].reshape(1, d)
        g_f = g_ffn[l].reshape(1, d)
        w_in_l = w_in[l].astype(BF16)
        w_out_l = w_out[l].astype(BF16)
        wg = _block_diag_gates(0.5 * rg_wa[l], 0.5 * rg_wx[l])
        bg = 0.5 * jnp.stack([rg_ba[l], rg_bx[l]], axis=1)
        mix_p = (conv_w[l], conv_b[l], wg, bg, rg_lam[l], log_g[l], rg_w)

        pc = _in_proj(ctx, g_m, mc[0], mc[1], w_in_l)
        rg_c, ret_c, ctx_states = _mixer(pc, rope_ctx, zero_states, *mix_p)
        px = _in_proj(x, g_m, ml[0], ml[1], w_in_l)
        rg_x, ret_x, _ = _mixer(px, rope_lat, ctx_states, *mix_p)

        if l % 2 == 0:
            i = l // 2
            w1, w3, w2 = ffn_w1[i].astype(BF16), ffn_w3[i].astype(BF16), ffn_w2[i].astype(BF16)
            nxt = moe_w1[i], moe_w3[i], moe_w2[i]
            x, cast = _mix_ffn(x, rg_x, ret_x, w_out_l, ml[2], g_f, ml[3], ml[4], ml[5],
                               w1, w3, w2, cast=[w.reshape(-1, w.shape[-1]) for w in nxt])
            moe_w = [c.reshape(w.shape) for c, w in zip(cast, nxt)]
            if not last:
                ctx, _ = _mix_ffn(ctx, rg_c, ret_c, w_out_l, mc[2], g_f, mc[3], mc[4], mc[5],
                                  w1, w3, w2)
        else:
            i = l // 2
            assert ne <= SUBLANES
            wr = jnp.zeros((d, LANES), F32).at[:, :ne].set(moe_router[i])
            wr_hi = wr.astype(BF16)
            wr_lo = (wr - wr_hi.astype(F32)).astype(BF16)
            wr2 = jnp.concatenate([wr_hi, wr_lo], axis=1)
            br = jnp.full((1, LANES), NEG_BIG, F32).at[0, :ne].set(moe_router_b[i])
            x, hx, route, route_t, counts = _out_proj_route(
                x, rg_x, ret_x, w_out_l, ml[2], g_f, ml[3], ml[4], wr2, br)
            x = _moe(hx, x, ml[5], route, route_t, counts, g_final.reshape(1, d), *moe_w)
    return x
```

```python
import functools

import jax
import jax.numpy as jnp
from jax import lax
from jax.experimental import pallas as pl
from jax.experimental.pallas import tpu as pltpu

F32 = jnp.float32
BF16 = jnp.bfloat16

EPS = 1e-6
RG_C = 8.0
CONV_W = 4
CONV_LEFT = 2
RET_CHUNK = 256
ROPE_BASE = 10000.0
GRID_W = 64
N_MOD = 6

ROW_TILE = 512

LANES = 128
SUBLANES = 8
VMEM_LIMIT = 56 * 1024 * 1024
NEG_BIG = -1e30
LOG2_E = 1.4426950408889634


def _cparams(sem):
    return pltpu.CompilerParams(dimension_semantics=sem, vmem_limit_bytes=VMEM_LIMIT)


def _modulation_kernel(c_ref, w_ref, b_ref, o_ref):
    c = c_ref[...]
    s = c * jax.nn.sigmoid(c)
    o_ref[...] = jnp.dot(s, w_ref[...], precision=lax.Precision.HIGHEST,
                         preferred_element_type=F32) + b_ref[...]


def _modulation(c_all, w_mod, b_mod):
    depth, d, six_d = w_mod.shape
    rows = c_all.shape[0]
    tn = six_d // 4
    return pl.pallas_call(
        _modulation_kernel,
        grid=(depth, six_d // tn),
        in_specs=[
            pl.BlockSpec((rows, d), lambda l, j: (0, 0)),
            pl.BlockSpec((None, d, tn), lambda l, j: (l, 0, j)),
            pl.BlockSpec((None, 1, tn), lambda l, j: (l, 0, j)),
        ],
        out_specs=pl.BlockSpec((None, rows, tn), lambda l, j: (l, 0, j)),
        out_shape=jax.ShapeDtypeStruct((depth, rows, six_d), F32),
        compiler_params=_cparams(("arbitrary", "arbitrary")),
        name="modulation",
    )(c_all, w_mod, b_mod.reshape(depth, 1, six_d))


def _norm_mod(x, g, shift, scale):
    ms = jnp.mean(x * x, axis=-1, keepdims=True)
    y = x * lax.rsqrt(ms + EPS) * g
    return y * (1.0 + scale) + shift


def _in_proj_kernel(x_ref, g_ref, shift_ref, scale_ref, w_ref, o_ref):
    h = _norm_mod(x_ref[...], g_ref[...], shift_ref[...], scale_ref[...])
    o_ref[...] = jnp.dot(h.astype(BF16), w_ref[...],
                         preferred_element_type=F32).astype(o_ref.dtype)


def _in_proj(x, g, shift, scale, w):
    bsz, n, d = x.shape
    f = w.shape[1]
    tm = min(2 * ROW_TILE, n)
    vec = pl.BlockSpec((None, 1, d), lambda b, i: (b, 0, 0))
    return pl.pallas_call(
        _in_proj_kernel,
        grid=(bsz, n // tm),
        in_specs=[
            pl.BlockSpec((None, tm, d), lambda b, i: (b, i, 0)),
            pl.BlockSpec((1, d), lambda b, i: (0, 0)),
            vec, vec,
            pl.BlockSpec((d, f), lambda b, i: (0, 0)),
        ],
        out_specs=pl.BlockSpec((None, tm, f), lambda b, i: (b, i, 0)),
        out_shape=jax.ShapeDtypeStruct((bsz, n, f), BF16),
        compiler_params=_cparams(("parallel", "parallel")),
        name="in_proj",
    )(x, g, shift, scale, w)


RG_TILE = 256
RG_CG = 256
RG_UNROLL = 4
RG_SEG = RG_TILE // SUBLANES
RG_PITCH = RG_SEG + SUBLANES


def _rglru_kernel(u_ref, yg_ref, cw_ref, cb_ref, wg_ref, bg_ref, lam_ref, h0_ref,
                  o_ref, hl_ref, u_scr, hf_scr, ab_scr, nat_scr):
    n = u_ref.shape[0]
    tt = RG_TILE
    nt = n // tt
    nseg = n // RG_SEG
    cg = u_ref.shape[1]
    lane_cols = [slice(g * LANES, (g + 1) * LANES) for g in range(cg // LANES)]

    zeros = jnp.zeros((RG_PITCH, LANES), F32)
    for g, cols in enumerate(lane_cols):
        u_scr[g, pl.ds(0, RG_PITCH), :] = zeros
        u_scr[g, pl.ds(nseg * RG_PITCH + RG_SEG, SUBLANES), :] = zeros[0:SUBLANES]

    def fill(s, carry):
        src = pl.multiple_of(s * RG_SEG, RG_SEG)
        dst = pl.multiple_of((s + 1) * RG_PITCH, SUBLANES)
        for g, cols in enumerate(lane_cols):
            rows = u_ref[pl.ds(src, RG_SEG), cols].astype(F32)
            u_scr[g, pl.ds(dst, RG_SEG), :] = rows
            u_scr[g, pl.ds(dst - SUBLANES, SUBLANES), :] = rows[0:SUBLANES]
        return carry

    lax.fori_loop(0, nseg, fill, 0)

    seg = RG_SEG
    cw_rows = [[jnp.broadcast_to(cw_ref[k:k + 1, c], (SUBLANES, LANES)) for c in lane_cols]
               for k in range(CONV_W)]
    cb_rows = [jnp.broadcast_to(cb_ref[:, c], (SUBLANES, LANES)) for c in lane_cols]

    def conv_tile(i):
        base = pl.multiple_of((i * SUBLANES + 1) * RG_PITCH, SUBLANES)
        blocks = []
        for j in range(seg):
            lane_groups = []
            for g in range(cg // LANES):
                acc = cb_rows[g]
                for k in range(CONV_W):
                    q = j + k - CONV_LEFT
                    start = base + q if q >= 0 else base - RG_PITCH + seg + q
                    acc = acc + cw_rows[k][g] * u_scr[g, pl.ds(start, SUBLANES, stride=RG_PITCH), :]
                lane_groups.append(acc)
            blocks.append(jnp.concatenate(lane_groups, axis=1))
        return jnp.concatenate(blocks, axis=0)

    def gates(uc, d):
        ub = uc.astype(BF16)
        ta = jnp.tanh(jnp.dot(ub, wg_ref[d, 0], preferred_element_type=F32) + bg_ref[d, 0:1, :])
        ti = jnp.tanh(jnp.dot(ub, wg_ref[d, 1], preferred_element_type=F32) + bg_ref[d, 1:2, :])
        z = -lam_ref[d:d + 1, :]
        sp = jnp.maximum(z, 0.0) + jnp.log1p(jnp.exp(-jnp.abs(z)))
        c2 = (-0.5 * RG_C * LOG2_E) * sp
        a = jnp.exp2(c2 + c2 * ta)
        b = jnp.exp2((0.5 * LOG2_E) * jnp.log(1.0 - a * a)) * ((0.5 + 0.5 * ti) * uc)
        return a, b

    def scan_tile(a, b, carry, reverse):
        steps = range(seg - 1, -1, -1) if reverse else range(seg)
        h = acum = None
        h_loc = [None] * seg
        a_cum = [None] * seg
        for j in steps:
            rows = slice(j * SUBLANES, (j + 1) * SUBLANES)
            h = b[rows] if h is None else a[rows] * h + b[rows]
            acum = a[rows] if acum is None else a[rows] * acum
            h_loc[j], a_cum[j] = h, acum
        order = range(SUBLANES - 1, -1, -1) if reverse else range(SUBLANES)
        c = carry
        c_in = [None] * SUBLANES
        for s in order:
            c_in[s] = c
            c = h[s:s + 1, :] + acum[s:s + 1, :] * c
        c_in = jnp.concatenate(c_in, axis=0)
        out = jnp.concatenate([h_loc[j] + a_cum[j] * c_in for j in range(seg)], axis=0)
        return out, c

    def fwd_body(i, carry):
        t0 = pl.multiple_of(i * tt, tt)
        uc = conv_tile(i)
        a, b = gates(uc, 0)
        h, carry = scan_tile(a, b, carry, False)
        hf_scr[pl.ds(t0, tt), :] = h
        a, b = gates(uc, 1)
        ab_scr[0, pl.ds(t0, tt), :] = a
        ab_scr[1, pl.ds(t0, tt), :] = b
        return carry

    hf_last = lax.fori_loop(0, nt, fwd_body, h0_ref[0:1, :], unroll=min(RG_UNROLL, nt))

    def bwd_body(i, carry):
        tile = nt - 1 - i
        t0 = pl.multiple_of(tile * tt, tt)
        h, carry = scan_tile(ab_scr[0, pl.ds(t0, tt), :], ab_scr[1, pl.ds(t0, tt), :], carry, True)
        hsum = hf_scr[pl.ds(t0, tt), :] + h
        for j in range(seg):
            for g, cols in enumerate(lane_cols):
                nat_scr[g, pl.ds(j, SUBLANES, stride=RG_PITCH), :] = (
                    hsum[j * SUBLANES:(j + 1) * SUBLANES, cols])
        hnat = jnp.concatenate(
            [jnp.concatenate([nat_scr[g, pl.ds(s * RG_PITCH, seg), :] for s in range(SUBLANES)],
                             axis=0) for g in range(len(lane_cols))], axis=1)
        yg = yg_ref[pl.ds(t0, tt), :].astype(F32)
        o_ref[pl.ds(t0, tt), :] = (jax.nn.gelu(yg) * hnat).astype(o_ref.dtype)
        return carry

    hb_last = lax.fori_loop(0, nt, bwd_body, h0_ref[1:2, :], unroll=min(RG_UNROLL, nt))
    hl_ref[0:1, :] = hf_last
    hl_ref[1:2, :] = hb_last


def _rglru(p, conv_w, conv_b, wg, bg, lam, h0):
    bsz, n, _ = p.shape
    rg_w = lam.shape[1]
    ncg = rg_w // RG_CG
    assert n % RG_TILE == 0 and rg_w % RG_CG == 0
    return pl.pallas_call(
        _rglru_kernel,
        grid=(bsz, ncg),
        in_specs=[
            pl.BlockSpec((None, n, RG_CG), lambda b, c: (b, 0, c)),
            pl.BlockSpec((None, n, RG_CG), lambda b, c: (b, 0, ncg + c)),
            pl.BlockSpec((CONV_W, RG_CG), lambda b, c: (0, c)),
            pl.BlockSpec((1, RG_CG), lambda b, c: (0, c)),
            pl.BlockSpec((2, 2, None, RG_CG, RG_CG), lambda b, c: (0, 0, c, 0, 0)),
            pl.BlockSpec((2, 2, RG_CG), lambda b, c: (0, 0, c)),
            pl.BlockSpec((2, RG_CG), lambda b, c: (0, c)),
            pl.BlockSpec((None, 2, RG_CG), lambda b, c: (b, 0, c)),
        ],
        out_specs=[
            pl.BlockSpec((None, n, RG_CG), lambda b, c: (b, 0, c)),
            pl.BlockSpec((None, 2, RG_CG), lambda b, c: (b, 0, c)),
        ],
        out_shape=[
            jax.ShapeDtypeStruct((bsz, n, rg_w), BF16),
            jax.ShapeDtypeStruct((bsz, 2, rg_w), F32),
        ],
        scratch_shapes=[
            pltpu.VMEM((RG_CG // LANES, (n // RG_SEG + 1) * RG_PITCH, LANES), F32),
            pltpu.VMEM((n, RG_CG), F32),
            pltpu.VMEM((2, n, RG_CG), F32),
            pltpu.VMEM((RG_CG // LANES, SUBLANES * RG_PITCH, LANES), F32),
        ],
        compiler_params=_cparams(("parallel", "parallel")),
        name="rglru",
    )(p, p, conv_w, conv_b.reshape(1, rg_w), wg, bg, lam, h0)


RET_UNROLL = 16


def _retention_kernel(lg_ref, q_ref, k_ref, v_ref, g_ref, cs_ref, sn_ref, s0_ref,
                      o_ref, so_ref, kr_scr, st_scr):
    n, hd = q_ref.shape
    L = RET_CHUNK
    nc = n // L
    head = pl.program_id(1)
    lgf = lg_ref[0, head]
    lgb = lg_ref[1, head]

    def rope(t, rows):
        return t * cs_ref[rows, :] + pltpu.roll(t, hd // 2, axis=1) * sn_ref[rows, :]

    kr_scr[...] = (rope(k_ref[...].astype(F32), slice(None)) * (hd ** -0.5)).astype(BF16)

    jc = lax.broadcasted_iota(jnp.int32, (L, 1), 0).astype(F32)
    q_dec_f = jnp.exp((jc + 1.0) * lgf)
    q_dec_b = jnp.exp((L - jc) * lgb)
    k_dec_f = jnp.exp((L - 1.0 - jc) * lgf)
    k_dec_b = jnp.exp(jc * lgb)
    ones = jnp.ones((1, hd), F32)
    chunk_f = jnp.exp(ones * (L * lgf))
    chunk_b = jnp.exp(ones * (L * lgb))
    ii = lax.broadcasted_iota(jnp.int32, (L, L), 0)
    jj = lax.broadcasted_iota(jnp.int32, (L, L), 1)
    diff = (ii - jj).astype(F32)
    dmat = jnp.where(diff >= 0.0, jnp.exp(jnp.maximum(diff, 0.0) * lgf),
                     jnp.exp(jnp.maximum(-diff, 0.0) * lgb))

    tn_dims = (((0,), (0,)), ((), ()))
    nt_dims = (((1,), (1,)), ((), ()))

    def state_update(c, state, k_dec, chunk_dec):
        t0 = pl.multiple_of(c * L, L)
        kd = (kr_scr[pl.ds(t0, L), :].astype(F32) * k_dec).astype(BF16)
        upd = lax.dot_general(kd, v_ref[pl.ds(t0, L), :], tn_dims, preferred_element_type=F32)
        return chunk_dec * state + upd

    def state_body(i, carry):
        f_state, r_state = carry
        cf = i
        cb = nc - 1 - i
        st_scr[cf, :, 0:hd] = f_state.astype(BF16)
        st_scr[cb, :, hd:2 * hd] = r_state.astype(BF16)
        return (state_update(cf, f_state, k_dec_f, chunk_f),
                state_update(cb, r_state, k_dec_b, chunk_b))

    f_fin, r_fin = lax.fori_loop(0, nc, state_body, (s0_ref[0], s0_ref[1]),
                                 unroll=min(RET_UNROLL, nc))
    so_ref[0] = f_fin
    so_ref[1] = r_fin

    def out_body(c, carry):
        t0 = pl.multiple_of(c * L, L)
        qc = rope(q_ref[pl.ds(t0, L), :].astype(F32), pl.ds(t0, L)).astype(BF16)
        kc = kr_scr[pl.ds(t0, L), :]
        vc = v_ref[pl.ds(t0, L), :]
        s = lax.dot_general(qc, kc, nt_dims, preferred_element_type=F32)
        o = jnp.dot((s * dmat).astype(BF16), vc, preferred_element_type=F32)
        cross = jnp.dot(qc, st_scr[c], preferred_element_type=F32)
        o = o + cross[:, 0:hd] * q_dec_f + cross[:, hd:2 * hd] * q_dec_b
        mu = jnp.mean(o, axis=-1, keepdims=True)
        oc = o - mu
        var = jnp.mean(oc * oc, axis=-1, keepdims=True)
        gate = g_ref[pl.ds(t0, L), :].astype(F32)
        o_ref[pl.ds(t0, L), :] = (gate * jax.nn.sigmoid(gate) * (oc * lax.rsqrt(var + EPS))).astype(o_ref.dtype)
        return carry

    lax.fori_loop(0, nc, out_body, 0, unroll=min(RET_UNROLL, nc))


def _retention(p, log_g, cs, sn, s0, col0):
    bsz, n, _ = p.shape
    nh = log_g.shape[1]
    hd = cs.shape[1]
    cb0 = col0 // hd

    def col(which):
        return pl.BlockSpec((None, n, hd), lambda b, h, lg: (b, 0, cb0 + which * nh + h))

    state = pl.BlockSpec((None, 2, None, hd, hd), lambda b, h, lg: (b, 0, h, 0, 0))
    table = pl.BlockSpec((n, hd), lambda b, h, lg: (0, 0))
    grid_spec = pltpu.PrefetchScalarGridSpec(
        num_scalar_prefetch=1,
        grid=(bsz, nh),
        in_specs=[col(0), col(1), col(2), col(3), table, table, state],
        out_specs=[pl.BlockSpec((None, n, hd), lambda b, h, lg: (b, 0, h)), state],
        scratch_shapes=[
            pltpu.VMEM((n, hd), BF16),
            pltpu.VMEM((n // RET_CHUNK, hd, 2 * hd), BF16),
        ],
    )
    return pl.pallas_call(
        _retention_kernel,
        grid_spec=grid_spec,
        out_shape=[
            jax.ShapeDtypeStruct((bsz, n, nh * hd), BF16),
            jax.ShapeDtypeStruct((bsz, 2, nh, hd, hd), F32),
        ],
        compiler_params=_cparams(("parallel", "parallel")),
        name="retention",
    )(log_g, p, p, p, p, cs, sn, s0)


R_RANK0, R_RANK1, R_E0, R_E1, R_W0, R_W1 = range(6)


def _mix_residual(x_ref, rg_ref, ret_ref, wo_ref, gate_ref, rows=slice(None)):
    rg_w = rg_ref.shape[1]
    y = jnp.dot(rg_ref[rows, :], wo_ref[0:rg_w, :], preferred_element_type=F32)
    y = y + jnp.dot(ret_ref[rows, :], wo_ref[rg_w:, :], preferred_element_type=F32)
    return x_ref[rows, :] + gate_ref[...] * y


ROUTE_ROWS = 512


def _out_proj_kernel(x_ref, rg_ref, ret_ref, wo_ref, gate_ref, g_ref, shift_ref, scale_ref,
                     wr2_ref, br_ref, xo_ref, h_ref, rt_ref, rtt_ref, cnt_ref, carry_scr):
    tm, d = x_ref.shape
    ns = d // LANES
    nr = min(ROUTE_ROWS, tm)

    @pl.when((pl.program_id(0) == 0) & (pl.program_id(1) == 0))
    def _():
        carry_scr[...] = jnp.zeros_like(carry_scr)

    erow = lax.broadcasted_iota(jnp.int32, (SUBLANES, nr), 0).astype(F32)
    ri = lax.broadcasted_iota(jnp.int32, (nr, nr), 0)
    ci = lax.broadcasted_iota(jnp.int32, (nr, nr), 1)
    earlier = jnp.where(ri < ci, 1.0, 0.0).astype(BF16)
    count = carry_scr[:, 0:1]

    for r0 in range(0, tm, nr):
        rows = pl.ds(r0, nr)
        xn = _mix_residual(x_ref, rg_ref, ret_ref, wo_ref, gate_ref, rows)
        xo_ref[rows, :] = xn
        hx = _norm_mod(xn, g_ref[...], shift_ref[...], scale_ref[...])

        for s in range(ns):
            h_ref[pl.ds(r0 * ns + s, nr, stride=ns), :] = hx[:, s * LANES:(s + 1) * LANES]

        h_hi = hx.astype(BF16)
        h_lo = (hx - h_hi.astype(F32)).astype(BF16)
        part = jnp.dot(h_hi, wr2_ref[...], preferred_element_type=F32)
        logits = (part[:, :LANES] + part[:, LANES:]
                  + jnp.dot(h_lo, wr2_ref[:, :LANES], preferred_element_type=F32) + br_ref[...])
        lt = jnp.transpose(logits)[0:SUBLANES, :]
        m0 = jnp.max(lt, axis=0, keepdims=True)
        i0 = jnp.min(jnp.where(lt == m0, erow, float(SUBLANES)), axis=0, keepdims=True)
        rest_l = jnp.where(erow == i0, NEG_BIG, lt)
        m1 = jnp.max(rest_l, axis=0, keepdims=True)
        i1 = jnp.min(jnp.where(rest_l == m1, erow, float(SUBLANES)), axis=0, keepdims=True)
        e = jnp.exp(m1 - m0)
        w0 = 1.0 / (1.0 + e)
        w1 = e / (1.0 + e)

        sel0 = erow == i0
        sel1 = erow == i1
        mask = jnp.where(sel0 | sel1, 1.0, 0.0)
        before = jnp.dot(mask.astype(BF16), earlier, preferred_element_type=F32) + count
        rank0 = jnp.sum(jnp.where(sel0, before, 0.0), axis=0, keepdims=True)
        rank1 = jnp.sum(jnp.where(sel1, before, 0.0), axis=0, keepdims=True)
        count = count + jnp.sum(mask, axis=1, keepdims=True)
        rec = jnp.zeros((SUBLANES, nr), F32)
        for idx, val in enumerate((rank0, rank1, i0, i1, w0, w1)):
            rec = jnp.where(erow == idx, val, rec)
        rtt_ref[:, r0:r0 + nr] = rec
        rt_ref[rows, :] = jnp.transpose(
            jnp.concatenate([rec, jnp.zeros((LANES - SUBLANES, nr), F32)], axis=0))

    carry_scr[...] = jnp.broadcast_to(count, carry_scr.shape)
    on_diag = (lax.broadcasted_iota(jnp.int32, (SUBLANES, LANES), 0)
               == lax.broadcasted_iota(jnp.int32, (SUBLANES, LANES), 1))
    cnt_ref[...] = jnp.sum(jnp.where(on_diag, count, 0.0), axis=0, keepdims=True)


def _out_proj_route(x, rg, ret, w_out, gate, g, shift, scale, wr2, br):
    bsz, n, d = x.shape
    tm = min(ROUTE_ROWS, n)
    vec = pl.BlockSpec((None, 1, d), lambda b, i: (b, 0, 0))
    tile = lambda w: pl.BlockSpec((None, tm, w), lambda b, i: (b, i, 0))
    const = lambda shape: pl.BlockSpec(shape, lambda b, i: (0, 0))
    return pl.pallas_call(
        _out_proj_kernel,
        grid=(bsz, n // tm),
        in_specs=[
            tile(d), tile(rg.shape[2]), tile(ret.shape[2]), const(w_out.shape),
            vec, const((1, d)), vec, vec, const((d, 2 * LANES)), const((1, LANES)),
        ],
        out_specs=[
            tile(d),
            pl.BlockSpec((None, tm * (d // LANES), LANES), lambda b, i: (b, i, 0)),
            tile(LANES),
            pl.BlockSpec((SUBLANES, tm), lambda b, i: (0, b * (n // tm) + i)),
            const((1, LANES)),
        ],
        out_shape=[
            jax.ShapeDtypeStruct((bsz, n, d), F32),
            jax.ShapeDtypeStruct((bsz, n * (d // LANES), LANES), F32),
            jax.ShapeDtypeStruct((bsz, n, LANES), F32),
            jax.ShapeDtypeStruct((SUBLANES, bsz * n), F32),
            jax.ShapeDtypeStruct((1, LANES), F32),
        ],
        scratch_shapes=[pltpu.VMEM((SUBLANES, LANES), F32)],
        compiler_params=_cparams(("arbitrary", "arbitrary")),
        name="out_proj_route",
    )(x, rg, ret, w_out, gate, g, shift, scale, wr2, br)


def _swiglu_acc(h, w1_ref, w3_ref, w2_ref):
    a = jnp.dot(h, w1_ref[...], preferred_element_type=F32)
    b = jnp.dot(h, w3_ref[...], preferred_element_type=F32)
    act = a * jax.nn.sigmoid(a) * b
    return jnp.dot(act.astype(BF16), w2_ref[...], preferred_element_type=F32)


def _mix_ffn_kernel(n_cast, x_ref, rg_ref, ret_ref, wo_ref, gmix_ref, g_ref, shift_ref, scale_ref,
                    gffn_ref, w1_ref, w3_ref, w2_ref, *rest):
    cast_in, o_ref, cast_out = rest[:n_cast], rest[n_cast], rest[n_cast + 1:]
    xn = _mix_residual(x_ref, rg_ref, ret_ref, wo_ref, gmix_ref)
    hx = _norm_mod(xn, g_ref[...], shift_ref[...], scale_ref[...]).astype(BF16)
    o_ref[...] = xn + gffn_ref[...] * _swiglu_acc(hx, w1_ref, w3_ref, w2_ref)
    for src, dst in zip(cast_in, cast_out):
        dst[...] = src[...].astype(dst.dtype)


def _mix_ffn(x, rg, ret, w_out, gate_mix, g, shift, scale, gate_ffn, w1, w3, w2, cast=()):
    bsz, n, d = x.shape
    f = w1.shape[1]
    tm = min(ROW_TILE, n)
    steps = bsz * (n // tm)
    vec = pl.BlockSpec((None, 1, d), lambda b, i: (b, 0, 0))
    tile = lambda w: pl.BlockSpec((None, tm, w), lambda b, i: (b, i, 0))
    resident = lambda shape: pl.BlockSpec(shape, lambda b, i: (0, 0), pipeline_mode=pl.Buffered(1))
    slabs = []
    for a in cast:
        rows, cols = a.shape
        assert rows % (steps * 2 * SUBLANES) == 0
        slabs.append(pl.BlockSpec((rows // steps, cols), lambda b, i: (b * (n // tm) + i, 0)))
    out = pl.pallas_call(
        functools.partial(_mix_ffn_kernel, len(cast)),
        grid=(bsz, n // tm),
        in_specs=[
            tile(d), tile(rg.shape[2]), tile(ret.shape[2]), resident(w_out.shape),
            vec, resident((1, d)), vec, vec, vec,
            resident((d, f)), resident((d, f)), resident((f, d)),
        ] + slabs,
        out_specs=[tile(d)] + slabs,
        out_shape=[jax.ShapeDtypeStruct((bsz, n, d), F32)]
        + [jax.ShapeDtypeStruct(a.shape, BF16) for a in cast],
        compiler_params=_cparams(("parallel", "parallel")),
        name="mix_ffn",
    )(x, rg, ret, w_out, gate_mix, g, shift, scale, gate_ffn, w1, w3, w2, *cast)
    return out[0], tuple(out[1:])


MOE_BLK = 256
MOE_TOKENS = 1024
COMBINE_ROWS = 256


def _dispatch_kernel(nsteps, d0_ref, d1_ref, zr_ref, hx_ref, hs_ref, zbuf, stage, sem, zsem):
    tm = hx_ref.shape[0]
    step = pl.program_id(0)
    base = step * tm
    slot = step % 2

    @pl.when(pl.program_id(0) == 0)
    def _():
        zbuf[...] = jnp.zeros_like(zbuf)
        for z in range(zr_ref.shape[0]):
            @pl.when(zr_ref[z] >= 0)
            def _():
                pltpu.make_async_copy(zbuf, hs_ref.at[pl.ds(zr_ref[z], MOE_BLK)], zsem).start()
        for z in range(zr_ref.shape[0]):
            @pl.when(zr_ref[z] >= 0)
            def _():
                pltpu.make_async_copy(zbuf, hs_ref.at[pl.ds(zr_ref[z], MOE_BLK)], zsem).wait()

    def drain(which):
        for _ in range(2):
            pltpu.make_async_copy(stage.at[which], hs_ref.at[pl.ds(0, tm)], sem.at[which]).wait()

    @pl.when(step >= 2)
    def _():
        drain(slot)

    stage[slot] = hx_ref[...]
    src = stage.at[slot]

    def issue(t, carry):
        pltpu.make_async_copy(src.at[t], hs_ref.at[d0_ref[base + t]], sem.at[slot]).start(priority=0)
        pltpu.make_async_copy(src.at[t], hs_ref.at[d1_ref[base + t]], sem.at[slot]).start(priority=1)
        return carry

    lax.fori_loop(0, tm, issue, 0, unroll=8)

    @pl.when(step == nsteps - 1)
    def _():
        drain(slot)
        if nsteps >= 2:
            drain(1 - slot)


def _dispatch(dest0, dest1, zero_rows, hx, n_rows):
    n_tok, s, lanes = hx.shape
    tm = min(MOE_TOKENS, n_tok)
    grid_spec = pltpu.PrefetchScalarGridSpec(
        num_scalar_prefetch=3,
        grid=(n_tok // tm,),
        in_specs=[pl.BlockSpec((tm, s, lanes), lambda i, d0, d1, zr: (i, 0, 0))],
        out_specs=pl.BlockSpec(memory_space=pl.ANY),
        scratch_shapes=[pltpu.VMEM((MOE_BLK, s, lanes), F32),
                        pltpu.VMEM((2, tm, s, lanes), F32),
                        pltpu.SemaphoreType.DMA((2,)), pltpu.SemaphoreType.DMA(())],
    )
    return pl.pallas_call(
        functools.partial(_dispatch_kernel, n_tok // tm),
        grid_spec=grid_spec,
        out_shape=jax.ShapeDtypeStruct((n_rows, s, lanes), F32),
        compiler_params=_cparams(("arbitrary",)),
        name="moe_dispatch",
    )(dest0, dest1, zero_rows, hx)


def _experts_kernel(be_ref, nu_ref, hs_ref, w1_ref, w3_ref, w2_ref, y_ref):
    j = pl.program_id(0)
    d = w1_ref.shape[0]
    ns = d // LANES
    blk = hs_ref.shape[0] // ns

    @pl.when(j < nu_ref[0])
    def _():
        h = jnp.concatenate(
            [hs_ref[pl.ds(s, blk, stride=ns), :].astype(BF16) for s in range(ns)], axis=1)
        out = _swiglu_acc(h, w1_ref, w3_ref, w2_ref)
        for s in range(ns):
            y_ref[pl.ds(s, blk, stride=ns), :] = out[:, s * LANES:(s + 1) * LANES]

    @pl.when(j >= nu_ref[0])
    def _():
        y_ref[...] = jnp.zeros_like(y_ref)


def _experts(block_expert, n_used, hs, w1, w3, w2):
    ne, d, f = w1.shape
    s = d // LANES
    n_rows = hs.shape[0] // s
    n_blocks = n_rows // MOE_BLK

    def expert(shape):
        return pl.BlockSpec((None,) + shape, lambda j, be, nu: (be[j], 0, 0),
                            pipeline_mode=pl.Buffered(1))

    grid_spec = pltpu.PrefetchScalarGridSpec(
        num_scalar_prefetch=2,
        grid=(n_blocks,),
        in_specs=[
            pl.BlockSpec((MOE_BLK * s, LANES), lambda j, be, nu: (jnp.minimum(j, nu[0] - 1), 0)),
            expert((d, f)), expert((d, f)), expert((f, d)),
        ],
        out_specs=pl.BlockSpec((MOE_BLK * s, LANES), lambda j, be, nu: (j, 0)),
    )
    return pl.pallas_call(
        _experts_kernel,
        grid_spec=grid_spec,
        out_shape=jax.ShapeDtypeStruct((n_rows * s, LANES), F32),
        compiler_params=_cparams(("arbitrary",)),
        name="moe_experts",
    )(block_expert, n_used, hs, w1, w3, w2)


def _combine_kernel(d0_ref, d1_ref, x_ref, gate_ref, rt_ref, gfin_ref, y_ref, o_ref, ybuf, sem):
    tm, d = x_ref.shape
    ns = d // LANES
    step = pl.program_id(0) * pl.num_programs(1) + pl.program_id(1)
    nsteps = pl.num_programs(0) * pl.num_programs(1)
    slot = step % 2

    def row(ref, r):
        return ref.at[pl.ds(pl.multiple_of(r * ns, ns), ns)]

    def start_gather(block, into):
        base = block * tm

        def issue(t, carry):
            pltpu.make_async_copy(row(y_ref, d0_ref[base + t]), row(ybuf.at[into, 0], t),
                                  sem.at[into]).start(priority=0)
            pltpu.make_async_copy(row(y_ref, d1_ref[base + t]), row(ybuf.at[into, 1], t),
                                  sem.at[into]).start(priority=1)
            return carry

        lax.fori_loop(0, tm, issue, 0, unroll=8)

    @pl.when(step == 0)
    def _():
        start_gather(0, 0)

    @pl.when(step + 1 < nsteps)
    def _():
        start_gather(step + 1, 1 - slot)

    for k in range(2):
        pltpu.make_async_copy(y_ref.at[pl.ds(0, tm * ns)], ybuf.at[slot, k], sem.at[slot]).wait()

    chunk = min(COMBINE_ROWS, tm)
    for r0 in range(0, tm, chunk):
        rows = pl.ds(r0, chunk)
        rt = rt_ref[rows, :]
        w0 = jnp.broadcast_to(rt[:, R_W0:R_W0 + 1], (chunk, LANES))
        w1 = jnp.broadcast_to(rt[:, R_W1:R_W1 + 1], (chunk, LANES))
        sq = jnp.zeros((chunk, LANES), F32)
        for s in range(ns):
            cols = slice(s * LANES, (s + 1) * LANES)
            moe = (w0 * ybuf[slot, 0, pl.ds(r0 * ns + s, chunk, stride=ns), :]
                   + w1 * ybuf[slot, 1, pl.ds(r0 * ns + s, chunk, stride=ns), :])
            xn = x_ref[rows, cols] + gate_ref[:, cols] * moe
            sq = sq + xn * xn
            o_ref[rows, cols] = xn
        ss = jnp.sum(sq, axis=-1, keepdims=True)
        o_ref[rows, :] = o_ref[rows, :] * lax.rsqrt(ss / d + EPS) * gfin_ref[...]


def _combine(dest0, dest1, x, gate, route, g_final, y):
    bsz, n, d = x.shape
    s = d // LANES
    tm = min(MOE_TOKENS // 2, n)
    grid_spec = pltpu.PrefetchScalarGridSpec(
        num_scalar_prefetch=2,
        grid=(bsz, n // tm),
        in_specs=[
            pl.BlockSpec((None, tm, d), lambda b, i, d0, d1: (b, i, 0)),
            pl.BlockSpec((None, 1, d), lambda b, i, d0, d1: (b, 0, 0)),
            pl.BlockSpec((None, tm, LANES), lambda b, i, d0, d1: (b, i, 0)),
            pl.BlockSpec((1, d), lambda b, i, d0, d1: (0, 0)),
            pl.BlockSpec(memory_space=pl.ANY),
        ],
        out_specs=pl.BlockSpec((None, tm, d), lambda b, i, d0, d1: (b, i, 0)),
        scratch_shapes=[pltpu.VMEM((2, 2, tm * s, LANES), F32), pltpu.SemaphoreType.DMA((2,))],
    )
    return pl.pallas_call(
        _combine_kernel,
        grid_spec=grid_spec,
        out_shape=jax.ShapeDtypeStruct((bsz, n, d), F32),
        compiler_params=_cparams(("arbitrary", "arbitrary")),
        name="moe_combine",
    )(dest0, dest1, x, gate, route, g_final, y)


def _moe(hx, x, gate, route, route_t, counts, g_final, w1, w3, w2):
    bsz, n, d = x.shape
    ne = w1.shape[0]
    n_tok = bsz * n
    n_rows = -(-(2 * n_tok) // MOE_BLK) * MOE_BLK + ne * MOE_BLK
    n_blocks = n_rows // MOE_BLK
    cnt = counts[0, :ne].astype(jnp.int32)
    padded = (cnt + MOE_BLK - 1) // MOE_BLK * MOE_BLK
    pad_end = jnp.cumsum(padded)
    pad_start = pad_end - padded
    n_used = pad_end[-1] // MOE_BLK
    blk = jnp.minimum(jnp.arange(n_blocks, dtype=jnp.int32), n_used - 1)
    block_expert = jnp.minimum(
        jnp.sum((blk[:, None] * MOE_BLK >= pad_end[None, :]).astype(jnp.int32), axis=1), ne - 1)
    def dest(e_row, rank_row):
        e = route_t[e_row].astype(jnp.int32)
        start = sum(jnp.where(e == k, pad_start[k], 0) for k in range(ne))
        return start + route_t[rank_row].astype(jnp.int32)

    dest0 = dest(R_E0, R_RANK0)
    dest1 = dest(R_E1, R_RANK1)

    last_blk = jnp.where(padded > 0, pad_end - MOE_BLK, -1)
    trail = n_used + jnp.arange(ne, dtype=jnp.int32)
    trail = jnp.where(trail < n_blocks, trail * MOE_BLK, -1)
    zero_rows = jnp.concatenate([last_blk, trail]).astype(jnp.int32)

    ns = d // LANES
    hs = _dispatch(dest0, dest1, zero_rows, hx.reshape(n_tok, ns, LANES), n_rows)
    y = _experts(block_expert, n_used.reshape(1).astype(jnp.int32),
                 hs.reshape(n_rows * ns, LANES), w1, w3, w2)
    return _combine(dest0, dest1, x, gate, route, g_final, y)


def _rope_tables(n, hd):
    rows = n // GRID_W
    n_freq = hd // 4
    inv = ROPE_BASE ** (-jnp.arange(n_freq, dtype=F32) / n_freq)
    row_ang = jnp.arange(rows, dtype=F32)[:, None] * inv
    col_ang = jnp.arange(GRID_W, dtype=F32)[:, None] * inv

    def table(of_row, of_col):
        a = jnp.broadcast_to(of_row[:, None, :], (rows, GRID_W, n_freq))
        b = jnp.broadcast_to(of_col[None, :, :], (rows, GRID_W, n_freq))
        return jnp.concatenate([a, b], axis=-1).reshape(n, 2 * n_freq)

    cos = table(jnp.cos(row_ang), jnp.cos(col_ang))
    sin = table(jnp.sin(row_ang), jnp.sin(col_ang))
    return jnp.concatenate([cos, cos], axis=-1), jnp.concatenate([-sin, sin], axis=-1)


def _block_diag_gates(rg_wa, rg_wx):
    w = jnp.stack([rg_wa, rg_wx], axis=1)
    nd, ng, nk, c, _ = w.shape
    per = RG_CG // c
    w = w.reshape(nd, ng, nk // per, per, c, c)
    eye = jnp.eye(per, dtype=w.dtype)
    full = jnp.einsum('dgmpij,pq->dgmpiqj', w, eye)
    return full.reshape(nd, ng, nk // per, RG_CG, RG_CG).astype(BF16)


def _mixer(p, rope, states, conv_w, conv_b, wg, bg, lam, log_g, rg_w):
    rg_out, h_last = _rglru(p, conv_w, conv_b, wg, bg, lam, states[0])
    ret_out, s_last = _retention(p, log_g, rope[0], rope[1], states[1], 2 * rg_w)
    return rg_out, ret_out, (h_last, s_last)


def kernel(x, c, ctx, c_ctx, w_mod, b_mod, g_mix, g_ffn, g_final, w_in, w_out, conv_w, conv_b,
           rg_wa, rg_ba, rg_wx, rg_bx, rg_lam, ret_decay, ffn_w1, ffn_w3, ffn_w2,
           moe_router, moe_router_b, moe_w1, moe_w3, moe_w2):
    bsz, n_lat, d = x.shape
    n_ctx = ctx.shape[1]
    depth = w_mod.shape[0]
    rg_w = rg_lam.shape[2]
    nh = ret_decay.shape[2]
    hd = (w_out.shape[1] - rg_w) // nh
    ne = moe_router.shape[2]
    assert depth == 2, "kernel is written for the two-layer block (dense FFN, then MoE)"

    rows = -(-(bsz + 1) // SUBLANES) * SUBLANES
    c_all = jnp.zeros((rows, d), F32).at[:bsz].set(c).at[bsz].set(c_ctx)
    mod = _modulation(c_all, w_mod, b_mod)
    mod = mod.reshape(depth, rows, N_MOD, d)
    mod_lat = mod[:, :bsz, None]
    mod_ctx = jnp.broadcast_to(mod[:, bsz:bsz + 1, None], mod_lat.shape)

    rope_lat = _rope_tables(n_lat, hd)
    rope_ctx = (jnp.ones((n_ctx, hd), F32), jnp.zeros((n_ctx, hd), F32))
    log_g = jax.nn.log_sigmoid(ret_decay.astype(F32))
    zero_states = (jnp.zeros((bsz, 2, rg_w), F32), jnp.zeros((bsz, 2, nh, hd, hd), F32))

    for l in range(depth):
        last = l == depth - 1
        ml = [mod_lat[l, :, :, j] for j in range(N_MOD)]
        mc = [mod_ctx[l, :, :, j] for j in range(N_MOD)]
        g_m = g_mix[l].reshape(1, d)
        g_f = g_ffn[l].reshape(1, d)
        w_in_l = w_in[l].astype(BF16)
        w_out_l = w_out[l].astype(BF16)
        wg = _block_diag_gates(0.5 * rg_wa[l], 0.5 * rg_wx[l])
        bg = 0.5 * jnp.stack([rg_ba[l], rg_bx[l]], axis=1)
        mix_p = (conv_w[l], conv_b[l], wg, bg, rg_lam[l], log_g[l], rg_w)

        pc = _in_proj(ctx, g_m, mc[0], mc[1], w_in_l)
        rg_c, ret_c, ctx_states = _mixer(pc, rope_ctx, zero_states, *mix_p)
        px = _in_proj(x, g_m, ml[0], ml[1], w_in_l)
        rg_x, ret_x, _ = _mixer(px, rope_lat, ctx_states, *mix_p)

        if l % 2 == 0:
            i = l // 2
            w1, w3, w2 = ffn_w1[i].astype(BF16), ffn_w3[i].astype(BF16), ffn_w2[i].astype(BF16)
            nxt = moe_w1[i], moe_w3[i], moe_w2[i]
            x, cast = _mix_ffn(x, rg_x, ret_x, w_out_l, ml[2], g_f, ml[3], ml[4], ml[5],
                               w1, w3, w2, cast=[w.reshape(-1, w.shape[-1]) for w in nxt])
            moe_w = [c.reshape(w.shape) for c, w in zip(cast, nxt)]
            if not last:
                ctx, _ = _mix_ffn(ctx, rg_c, ret_c, w_out_l, mc[2], g_f, mc[3], mc[4], mc[5],
                                  w1, w3, w2)
        else:
            i = l // 2
            assert ne <= SUBLANES
            wr = jnp.zeros((d, LANES), F32).at[:, :ne].set(moe_router[i])
            wr_hi = wr.astype(BF16)
            wr_lo = (wr - wr_hi.astype(F32)).astype(BF16)
            wr2 = jnp.concatenate([wr_hi, wr_lo], axis=1)
            br = jnp.full((1, LANES), NEG_BIG, F32).at[0, :ne].set(moe_router_b[i])
            x, hx, route, route_t, counts = _out_proj_route(
                x, rg_x, ret_x, w_out_l, ml[2], g_f, ml[3], ml[4], wr2, br)
            x = _moe(hx, x, ml[5], route, route_t, counts, g_final.reshape(1, d), *moe_w)
    return x
```

```python
import functools

import jax
import jax.numpy as jnp
from jax import lax
from jax.experimental import pallas as pl
from jax.experimental.pallas import tpu as pltpu

F32 = jnp.float32
BF16 = jnp.bfloat16

EPS = 1e-6
RG_C = 8.0
CONV_W = 4
CONV_LEFT = 2
RET_CHUNK = 256
ROPE_BASE = 10000.0
GRID_W = 64
N_MOD = 6

ROW_TILE = 512

LANES = 128
SUBLANES = 8
VMEM_LIMIT = 56 * 1024 * 1024
NEG_BIG = -1e30
LOG2_E = 1.4426950408889634


def _cparams(sem):
    return pltpu.CompilerParams(dimension_semantics=sem, vmem_limit_bytes=VMEM_LIMIT)


def _modulation_kernel(c_ref, w_ref, b_ref, o_ref):
    c = c_ref[...]
    s = c * jax.nn.sigmoid(c)
    o_ref[...] = jnp.dot(s, w_ref[...], precision=lax.Precision.HIGHEST,
                         preferred_element_type=F32) + b_ref[...]


def _modulation(c_all, w_mod, b_mod):
    depth, d, six_d = w_mod.shape
    rows = c_all.shape[0]
    tn = six_d // 2
    return pl.pallas_call(
        _modulation_kernel,
        grid=(depth, six_d // tn),
        in_specs=[
            pl.BlockSpec((rows, d), lambda l, j: (0, 0)),
            pl.BlockSpec((None, d, tn), lambda l, j: (l, 0, j)),
            pl.BlockSpec((None, 1, tn), lambda l, j: (l, 0, j)),
        ],
        out_specs=pl.BlockSpec((None, rows, tn), lambda l, j: (l, 0, j)),
        out_shape=jax.ShapeDtypeStruct((depth, rows, six_d), F32),
        compiler_params=_cparams(("arbitrary", "arbitrary")),
        name="modulation",
    )(c_all, w_mod, b_mod.reshape(depth, 1, six_d))


def _norm_mod(x, g, shift, scale):
    ms = jnp.mean(x * x, axis=-1, keepdims=True)
    y = x * lax.rsqrt(ms + EPS) * g
    return y * (1.0 + scale) + shift


def _in_proj_kernel(x_ref, g_ref, shift_ref, scale_ref, w_ref, o_ref):
    h = _norm_mod(x_ref[...], g_ref[...], shift_ref[...], scale_ref[...])
    o_ref[...] = jnp.dot(h.astype(BF16), w_ref[...],
                         preferred_element_type=F32).astype(o_ref.dtype)


def _in_proj(x, g, shift, scale, w):
    bsz, n, d = x.shape
    f = w.shape[1]
    tm = min(2 * ROW_TILE, n)
    vec = pl.BlockSpec((None, 1, d), lambda b, i: (b, 0, 0))
    return pl.pallas_call(
        _in_proj_kernel,
        grid=(bsz, n // tm),
        in_specs=[
            pl.BlockSpec((None, tm, d), lambda b, i: (b, i, 0)),
            pl.BlockSpec((1, d), lambda b, i: (0, 0)),
            vec, vec,
            pl.BlockSpec((d, f), lambda b, i: (0, 0)),
        ],
        out_specs=pl.BlockSpec((None, tm, f), lambda b, i: (b, i, 0)),
        out_shape=jax.ShapeDtypeStruct((bsz, n, f), BF16),
        compiler_params=_cparams(("parallel", "parallel")),
        name="in_proj",
    )(x, g, shift, scale, w)


RG_TILE = 256
RG_CG = 256
RG_UNROLL = 4
RG_SEG = RG_TILE // SUBLANES
RG_PITCH = RG_SEG + SUBLANES
assert RG_PITCH % SUBLANES == 0 and RG_PITCH % (2 * SUBLANES) != 0


def _rglru_kernel(u_ref, yg_ref, cw_ref, cb_ref, wg_ref, bg_ref, lam_ref, h0_ref,
                  o_ref, hl_ref, u_scr, hf_scr, ab_scr, nat_scr):
    n = u_ref.shape[0]
    tt = RG_TILE
    nt = n // tt
    nseg = n // RG_SEG
    cg = u_ref.shape[1]
    lane_cols = [slice(g * LANES, (g + 1) * LANES) for g in range(cg // LANES)]

    zeros = jnp.zeros((RG_PITCH, LANES), F32)
    for g, cols in enumerate(lane_cols):
        u_scr[g, pl.ds(0, RG_PITCH), :] = zeros
        u_scr[g, pl.ds(nseg * RG_PITCH + RG_SEG, SUBLANES), :] = zeros[0:SUBLANES]

    def fill(s, carry):
        src = pl.multiple_of(s * RG_SEG, RG_SEG)
        dst = pl.multiple_of((s + 1) * RG_PITCH, SUBLANES)
        for g, cols in enumerate(lane_cols):
            rows = u_ref[pl.ds(src, RG_SEG), cols].astype(F32)
            u_scr[g, pl.ds(dst, RG_SEG), :] = rows
            u_scr[g, pl.ds(dst - SUBLANES, SUBLANES), :] = rows[0:SUBLANES]
        return carry

    lax.fori_loop(0, nseg, fill, 0, unroll=SUBLANES)

    seg = RG_SEG
    cw_rows = [[jnp.broadcast_to(cw_ref[k:k + 1, c], (SUBLANES, LANES)) for c in lane_cols]
               for k in range(CONV_W)]
    cb_rows = [jnp.broadcast_to(cb_ref[:, c], (SUBLANES, LANES)) for c in lane_cols]

    def conv_tile(i):
        base = pl.multiple_of((i * SUBLANES + 1) * RG_PITCH, SUBLANES)
        blocks = []
        for j in range(seg):
            lane_groups = []
            for g in range(cg // LANES):
                acc = cb_rows[g]
                for k in range(CONV_W):
                    q = j + k - CONV_LEFT
                    start = base + q if q >= 0 else base - RG_PITCH + seg + q
                    acc = acc + cw_rows[k][g] * u_scr[g, pl.ds(start, SUBLANES, stride=RG_PITCH), :]
                lane_groups.append(acc)
            blocks.append(jnp.concatenate(lane_groups, axis=1))
        return jnp.concatenate(blocks, axis=0)

    def gates(uc, d):
        ub = uc.astype(BF16)
        ta = jnp.tanh(jnp.dot(ub, wg_ref[d, 0], preferred_element_type=F32) + bg_ref[d, 0:1, :])
        ti = jnp.tanh(jnp.dot(ub, wg_ref[d, 1], preferred_element_type=F32) + bg_ref[d, 1:2, :])
        z = -lam_ref[d:d + 1, :]
        sp = jnp.maximum(z, 0.0) + jnp.log1p(jnp.exp(-jnp.abs(z)))
        c2 = (-0.5 * RG_C * LOG2_E) * sp
        a = jnp.exp2(c2 + c2 * ta)
        b = jnp.exp2((0.5 * LOG2_E) * jnp.log(1.0 - a * a)) * ((0.5 + 0.5 * ti) * uc)
        return a, b

    def scan_tile(a, b, carry, reverse):
        steps = range(seg - 1, -1, -1) if reverse else range(seg)
        h = acum = None
        h_loc = [None] * seg
        a_cum = [None] * seg
        for j in steps:
            rows = slice(j * SUBLANES, (j + 1) * SUBLANES)
            h = b[rows] if h is None else a[rows] * h + b[rows]
            acum = a[rows] if acum is None else a[rows] * acum
            h_loc[j], a_cum[j] = h, acum
        order = range(SUBLANES - 1, -1, -1) if reverse else range(SUBLANES)
        c = carry
        c_in = [None] * SUBLANES
        for s in order:
            c_in[s] = c
            c = h[s:s + 1, :] + acum[s:s + 1, :] * c
        c_in = jnp.concatenate(c_in, axis=0)
        out = jnp.concatenate([h_loc[j] + a_cum[j] * c_in for j in range(seg)], axis=0)
        return out, c

    def fwd_body(i, carry):
        t0 = pl.multiple_of(i * tt, tt)
        uc = conv_tile(i)
        a, b = gates(uc, 0)
        h, carry = scan_tile(a, b, carry, False)
        hf_scr[pl.ds(t0, tt), :] = h
        a, b = gates(uc, 1)
        ab_scr[0, pl.ds(t0, tt), :] = a
        ab_scr[1, pl.ds(t0, tt), :] = b
        return carry

    hf_last = lax.fori_loop(0, nt, fwd_body, h0_ref[0:1, :], unroll=min(RG_UNROLL, nt))

    def bwd_body(i, carry):
        tile = nt - 1 - i
        t0 = pl.multiple_of(tile * tt, tt)
        h, carry = scan_tile(ab_scr[0, pl.ds(t0, tt), :], ab_scr[1, pl.ds(t0, tt), :], carry, True)
        hsum = hf_scr[pl.ds(t0, tt), :] + h
        for j in range(seg):
            for g, cols in enumerate(lane_cols):
                nat_scr[g, pl.ds(j, SUBLANES, stride=RG_PITCH), :] = (
                    hsum[j * SUBLANES:(j + 1) * SUBLANES, cols])
        hnat = jnp.concatenate(
            [jnp.concatenate([nat_scr[g, pl.ds(s * RG_PITCH, seg), :] for s in range(SUBLANES)],
                             axis=0) for g in range(len(lane_cols))], axis=1)
        yg = yg_ref[pl.ds(t0, tt), :].astype(F32)
        o_ref[pl.ds(t0, tt), :] = (jax.nn.gelu(yg) * hnat).astype(o_ref.dtype)
        return carry

    hb_last = lax.fori_loop(0, nt, bwd_body, h0_ref[1:2, :], unroll=min(RG_UNROLL, nt))
    hl_ref[0:1, :] = hf_last
    hl_ref[1:2, :] = hb_last


def _rglru(p, conv_w, conv_b, wg, bg, lam, h0):
    bsz, n, _ = p.shape
    rg_w = lam.shape[1]
    ncg = rg_w // RG_CG
    assert n % RG_TILE == 0 and rg_w % RG_CG == 0
    return pl.pallas_call(
        _rglru_kernel,
        grid=(bsz, ncg),
        in_specs=[
            pl.BlockSpec((None, n, RG_CG), lambda b, c: (b, 0, c)),
            pl.BlockSpec((None, n, RG_CG), lambda b, c: (b, 0, ncg + c)),
            pl.BlockSpec((CONV_W, RG_CG), lambda b, c: (0, c)),
            pl.BlockSpec((1, RG_CG), lambda b, c: (0, c)),
            pl.BlockSpec((2, 2, None, RG_CG, RG_CG), lambda b, c: (0, 0, c, 0, 0)),
            pl.BlockSpec((2, 2, RG_CG), lambda b, c: (0, 0, c)),
            pl.BlockSpec((2, RG_CG), lambda b, c: (0, c)),
            pl.BlockSpec((None, 2, RG_CG), lambda b, c: (b, 0, c)),
        ],
        out_specs=[
            pl.BlockSpec((None, n, RG_CG), lambda b, c: (b, 0, c)),
            pl.BlockSpec((None, 2, RG_CG), lambda b, c: (b, 0, c)),
        ],
        out_shape=[
            jax.ShapeDtypeStruct((bsz, n, rg_w), BF16),
            jax.ShapeDtypeStruct((bsz, 2, rg_w), F32),
        ],
        scratch_shapes=[
            pltpu.VMEM((RG_CG // LANES, (n // RG_SEG + 1) * RG_PITCH, LANES), F32),
            pltpu.VMEM((n, RG_CG), F32),
            pltpu.VMEM((2, n, RG_CG), F32),
            pltpu.VMEM((RG_CG // LANES, SUBLANES * RG_PITCH, LANES), F32),
        ],
        compiler_params=_cparams(("parallel", "parallel")),
        name="rglru",
    )(p, p, conv_w, conv_b.reshape(1, rg_w), wg, bg, lam, h0)


RET_UNROLL = 16


def _retention_kernel(lg_ref, q_ref, k_ref, v_ref, g_ref, cs_ref, sn_ref, s0_ref,
                      o_ref, so_ref, kr_scr, st_scr):
    n, hd = q_ref.shape
    L = RET_CHUNK
    nc = n // L
    head = pl.program_id(1)
    lgf = lg_ref[0, head]
    lgb = lg_ref[1, head]

    def rope(t, rows):
        return t * cs_ref[rows, :] + pltpu.roll(t, hd // 2, axis=1) * sn_ref[rows, :]

    kr_scr[...] = (rope(k_ref[...].astype(F32), slice(None)) * (hd ** -0.5)).astype(BF16)

    jc = lax.broadcasted_iota(jnp.int32, (L, 1), 0).astype(F32)
    q_dec_f = jnp.exp((jc + 1.0) * lgf)
    q_dec_b = jnp.exp((L - jc) * lgb)
    k_dec_f = jnp.exp((L - 1.0 - jc) * lgf)
    k_dec_b = jnp.exp(jc * lgb)
    ones = jnp.ones((1, hd), F32)
    chunk_f = jnp.exp(ones * (L * lgf))
    chunk_b = jnp.exp(ones * (L * lgb))
    ii = lax.broadcasted_iota(jnp.int32, (L, L), 0)
    jj = lax.broadcasted_iota(jnp.int32, (L, L), 1)
    diff = (ii - jj).astype(F32)
    dmat = jnp.where(diff >= 0.0, jnp.exp(jnp.maximum(diff, 0.0) * lgf),
                     jnp.exp(jnp.maximum(-diff, 0.0) * lgb))

    tn_dims = (((0,), (0,)), ((), ()))
    nt_dims = (((1,), (1,)), ((), ()))

    def state_update(c, state, k_dec, chunk_dec):
        t0 = pl.multiple_of(c * L, L)
        kd = (kr_scr[pl.ds(t0, L), :].astype(F32) * k_dec).astype(BF16)
        upd = lax.dot_general(kd, v_ref[pl.ds(t0, L), :], tn_dims, preferred_element_type=F32)
        return chunk_dec * state + upd

    def state_body(i, carry):
        f_state, r_state = carry
        cf = i
        cb = nc - 1 - i
        st_scr[cf, :, 0:hd] = f_state.astype(BF16)
        st_scr[cb, :, hd:2 * hd] = r_state.astype(BF16)
        return (state_update(cf, f_state, k_dec_f, chunk_f),
                state_update(cb, r_state, k_dec_b, chunk_b))

    f_fin, r_fin = lax.fori_loop(0, nc, state_body, (s0_ref[0], s0_ref[1]),
                                 unroll=min(RET_UNROLL, nc))
    so_ref[0] = f_fin
    so_ref[1] = r_fin

    def out_body(c, carry):
        t0 = pl.multiple_of(c * L, L)
        qc = rope(q_ref[pl.ds(t0, L), :].astype(F32), pl.ds(t0, L)).astype(BF16)
        kc = kr_scr[pl.ds(t0, L), :]
        vc = v_ref[pl.ds(t0, L), :]
        s = lax.dot_general(qc, kc, nt_dims, preferred_element_type=F32)
        o = jnp.dot((s * dmat).astype(BF16), vc, preferred_element_type=F32)
        cross = jnp.dot(qc, st_scr[c], preferred_element_type=F32)
        o = o + cross[:, 0:hd] * q_dec_f + cross[:, hd:2 * hd] * q_dec_b
        mu = jnp.mean(o, axis=-1, keepdims=True)
        oc = o - mu
        var = jnp.mean(oc * oc, axis=-1, keepdims=True)
        gate = g_ref[pl.ds(t0, L), :].astype(F32)
        o_ref[pl.ds(t0, L), :] = (gate * jax.nn.sigmoid(gate) * (oc * lax.rsqrt(var + EPS))).astype(o_ref.dtype)
        return carry

    lax.fori_loop(0, nc, out_body, 0, unroll=min(RET_UNROLL, nc))


def _retention(p, log_g, cs, sn, s0, col0):
    bsz, n, _ = p.shape
    nh = log_g.shape[1]
    hd = cs.shape[1]
    cb0 = col0 // hd

    def col(which):
        return pl.BlockSpec((None, n, hd), lambda b, h, lg: (b, 0, cb0 + which * nh + h))

    state = pl.BlockSpec((None, 2, None, hd, hd), lambda b, h, lg: (b, 0, h, 0, 0))
    table = pl.BlockSpec((n, hd), lambda b, h, lg: (0, 0))
    grid_spec = pltpu.PrefetchScalarGridSpec(
        num_scalar_prefetch=1,
        grid=(bsz, nh),
        in_specs=[col(0), col(1), col(2), col(3), table, table, state],
        out_specs=[pl.BlockSpec((None, n, hd), lambda b, h, lg: (b, 0, h)), state],
        scratch_shapes=[
            pltpu.VMEM((n, hd), BF16),
            pltpu.VMEM((n // RET_CHUNK, hd, 2 * hd), BF16),
        ],
    )
    return pl.pallas_call(
        _retention_kernel,
        grid_spec=grid_spec,
        out_shape=[
            jax.ShapeDtypeStruct((bsz, n, nh * hd), BF16),
            jax.ShapeDtypeStruct((bsz, 2, nh, hd, hd), F32),
        ],
        compiler_params=_cparams(("parallel", "parallel")),
        name="retention",
    )(log_g, p, p, p, p, cs, sn, s0)


R_RANK0, R_RANK1, R_E0, R_E1, R_W0, R_W1 = range(6)


def _mix_residual(x_ref, rg_ref, ret_ref, wo_ref, gate_ref, rows=slice(None)):
    rg_w = rg_ref.shape[1]
    y = jnp.dot(rg_ref[rows, :], wo_ref[0:rg_w, :], preferred_element_type=F32)
    y = y + jnp.dot(ret_ref[rows, :], wo_ref[rg_w:, :], preferred_element_type=F32)
    return x_ref[rows, :] + gate_ref[...] * y


ROUTE_ROWS = 512


def _out_proj_kernel(x_ref, rg_ref, ret_ref, wo_ref, gate_ref, g_ref, shift_ref, scale_ref,
                     wr2_ref, br_ref, xo_ref, h_ref, rt_ref, rtt_ref, cnt_ref, carry_scr):
    tm, d = x_ref.shape
    ns = d // LANES
    nr = min(ROUTE_ROWS, tm)

    @pl.when((pl.program_id(0) == 0) & (pl.program_id(1) == 0))
    def _():
        carry_scr[...] = jnp.zeros_like(carry_scr)

    erow = lax.broadcasted_iota(jnp.int32, (SUBLANES, nr), 0).astype(F32)
    ri = lax.broadcasted_iota(jnp.int32, (nr, nr), 0)
    ci = lax.broadcasted_iota(jnp.int32, (nr, nr), 1)
    earlier = jnp.where(ri < ci, 1.0, 0.0).astype(BF16)
    count = carry_scr[:, 0:1]

    for r0 in range(0, tm, nr):
        rows = pl.ds(r0, nr)
        xn = _mix_residual(x_ref, rg_ref, ret_ref, wo_ref, gate_ref, rows)
        xo_ref[rows, :] = xn
        hx = _norm_mod(xn, g_ref[...], shift_ref[...], scale_ref[...])

        for s in range(ns):
            h_ref[pl.ds(r0 * ns + s, nr, stride=ns), :] = hx[:, s * LANES:(s + 1) * LANES]

        h_hi = hx.astype(BF16)
        h_lo = (hx - h_hi.astype(F32)).astype(BF16)
        part = jnp.dot(h_hi, wr2_ref[...], preferred_element_type=F32)
        logits = (part[:, :LANES] + part[:, LANES:]
                  + jnp.dot(h_lo, wr2_ref[:, :LANES], preferred_element_type=F32) + br_ref[...])
        lt = jnp.transpose(logits)[0:SUBLANES, :]
        m0 = jnp.max(lt, axis=0, keepdims=True)
        i0 = jnp.min(jnp.where(lt == m0, erow, float(SUBLANES)), axis=0, keepdims=True)
        rest_l = jnp.where(erow == i0, NEG_BIG, lt)
        m1 = jnp.max(rest_l, axis=0, keepdims=True)
        i1 = jnp.min(jnp.where(rest_l == m1, erow, float(SUBLANES)), axis=0, keepdims=True)
        e = jnp.exp(m1 - m0)
        w0 = 1.0 / (1.0 + e)
        w1 = e / (1.0 + e)

        sel0 = erow == i0
        sel1 = erow == i1
        mask = jnp.where(sel0 | sel1, 1.0, 0.0)
        before = jnp.dot(mask.astype(BF16), earlier, preferred_element_type=F32) + count
        rank0 = jnp.sum(jnp.where(sel0, before, 0.0), axis=0, keepdims=True)
        rank1 = jnp.sum(jnp.where(sel1, before, 0.0), axis=0, keepdims=True)
        count = count + jnp.sum(mask, axis=1, keepdims=True)
        rec = jnp.zeros((SUBLANES, nr), F32)
        for idx, val in enumerate((rank0, rank1, i0, i1, w0, w1)):
            rec = jnp.where(erow == idx, val, rec)
        rtt_ref[:, r0:r0 + nr] = rec
        rt_ref[rows, :] = jnp.transpose(
            jnp.concatenate([rec, jnp.zeros((LANES - SUBLANES, nr), F32)], axis=0))

    carry_scr[...] = jnp.broadcast_to(count, carry_scr.shape)
    on_diag = (lax.broadcasted_iota(jnp.int32, (SUBLANES, LANES), 0)
               == lax.broadcasted_iota(jnp.int32, (SUBLANES, LANES), 1))
    cnt_ref[...] = jnp.sum(jnp.where(on_diag, count, 0.0), axis=0, keepdims=True)


def _out_proj_route(x, rg, ret, w_out, gate, g, shift, scale, wr2, br):
    bsz, n, d = x.shape
    tm = min(2 * ROUTE_ROWS, n)
    vec = pl.BlockSpec((None, 1, d), lambda b, i: (b, 0, 0))
    tile = lambda w: pl.BlockSpec((None, tm, w), lambda b, i: (b, i, 0))
    const = lambda shape: pl.BlockSpec(shape, lambda b, i: (0, 0))
    return pl.pallas_call(
        _out_proj_kernel,
        grid=(bsz, n // tm),
        in_specs=[
            tile(d), tile(rg.shape[2]), tile(ret.shape[2]), const(w_out.shape),
            vec, const((1, d)), vec, vec, const((d, 2 * LANES)), const((1, LANES)),
        ],
        out_specs=[
            tile(d),
            pl.BlockSpec((None, tm * (d // LANES), LANES), lambda b, i: (b, i, 0)),
            tile(LANES),
            pl.BlockSpec((SUBLANES, tm), lambda b, i: (0, b * (n // tm) + i)),
            const((1, LANES)),
        ],
        out_shape=[
            jax.ShapeDtypeStruct((bsz, n, d), F32),
            jax.ShapeDtypeStruct((bsz, n * (d // LANES), LANES), F32),
            jax.ShapeDtypeStruct((bsz, n, LANES), F32),
            jax.ShapeDtypeStruct((SUBLANES, bsz * n), F32),
            jax.ShapeDtypeStruct((1, LANES), F32),
        ],
        scratch_shapes=[pltpu.VMEM((SUBLANES, LANES), F32)],
        compiler_params=_cparams(("arbitrary", "arbitrary")),
        name="out_proj_route",
    )(x, rg, ret, w_out, gate, g, shift, scale, wr2, br)


def _swiglu_acc(h, w1_ref, w3_ref, w2_ref):
    a = jnp.dot(h, w1_ref[...], preferred_element_type=F32)
    b = jnp.dot(h, w3_ref[...], preferred_element_type=F32)
    act = a * jax.nn.sigmoid(a) * b
    return jnp.dot(act.astype(BF16), w2_ref[...], preferred_element_type=F32)


def _mix_ffn_kernel(n_cast, x_ref, rg_ref, ret_ref, wo_ref, gmix_ref, g_ref, shift_ref, scale_ref,
                    gffn_ref, w1_ref, w3_ref, w2_ref, *rest):
    cast_in, o_ref, cast_out = rest[:n_cast], rest[n_cast], rest[n_cast + 1:]
    xn = _mix_residual(x_ref, rg_ref, ret_ref, wo_ref, gmix_ref)
    hx = _norm_mod(xn, g_ref[...], shift_ref[...], scale_ref[...]).astype(BF16)
    o_ref[...] = xn + gffn_ref[...] * _swiglu_acc(hx, w1_ref, w3_ref, w2_ref)
    for src, dst in zip(cast_in, cast_out):
        dst[...] = src[...].astype(dst.dtype)


def _mix_ffn(x, rg, ret, w_out, gate_mix, g, shift, scale, gate_ffn, w1, w3, w2, cast=()):
    bsz, n, d = x.shape
    f = w1.shape[1]
    tm = min(ROW_TILE, n)
    steps = bsz * (n // tm)
    vec = pl.BlockSpec((None, 1, d), lambda b, i: (b, 0, 0))
    tile = lambda w: pl.BlockSpec((None, tm, w), lambda b, i: (b, i, 0))
    resident = lambda shape: pl.BlockSpec(shape, lambda b, i: (0, 0), pipeline_mode=pl.Buffered(1))
    slabs = []
    for a in cast:
        rows, cols = a.shape
        assert rows % (steps * 2 * SUBLANES) == 0
        slabs.append(pl.BlockSpec((rows // steps, cols), lambda b, i: (b * (n // tm) + i, 0)))
    out = pl.pallas_call(
        functools.partial(_mix_ffn_kernel, len(cast)),
        grid=(bsz, n // tm),
        in_specs=[
            tile(d), tile(rg.shape[2]), tile(ret.shape[2]), resident(w_out.shape),
            vec, resident((1, d)), vec, vec, vec,
            resident((d, f)), resident((d, f)), resident((f, d)),
        ] + slabs,
        out_specs=[tile(d)] + slabs,
        out_shape=[jax.ShapeDtypeStruct((bsz, n, d), F32)]
        + [jax.ShapeDtypeStruct(a.shape, BF16) for a in cast],
        compiler_params=_cparams(("parallel", "parallel")),
        name="mix_ffn",
    )(x, rg, ret, w_out, gate_mix, g, shift, scale, gate_ffn, w1, w3, w2, *cast)
    return out[0], tuple(out[1:])


MOE_BLK = 256
MOE_TOKENS = 1024
COMBINE_ROWS = 256


def _dispatch_kernel(nsteps, d0_ref, d1_ref, zr_ref, hx_ref, hs_ref, zbuf, stage, sem, zsem):
    tm = hx_ref.shape[0]
    step = pl.program_id(0)
    base = step * tm
    slot = step % 2

    @pl.when(pl.program_id(0) == 0)
    def _():
        zbuf[...] = jnp.zeros_like(zbuf)
        for z in range(zr_ref.shape[0]):
            @pl.when(zr_ref[z] >= 0)
            def _():
                pltpu.make_async_copy(zbuf, hs_ref.at[pl.ds(zr_ref[z], MOE_BLK)], zsem).start()
        for z in range(zr_ref.shape[0]):
            @pl.when(zr_ref[z] >= 0)
            def _():
                pltpu.make_async_copy(zbuf, hs_ref.at[pl.ds(zr_ref[z], MOE_BLK)], zsem).wait()

    def drain(which):
        for _ in range(2):
            pltpu.make_async_copy(stage.at[which], hs_ref.at[pl.ds(0, tm)], sem.at[which]).wait()

    @pl.when(step >= 2)
    def _():
        drain(slot)

    stage[slot] = hx_ref[...]
    src = stage.at[slot]

    def issue(t, carry):
        pltpu.make_async_copy(src.at[t], hs_ref.at[d0_ref[base + t]], sem.at[slot]).start(priority=0)
        pltpu.make_async_copy(src.at[t], hs_ref.at[d1_ref[base + t]], sem.at[slot]).start(priority=1)
        return carry

    lax.fori_loop(0, tm, issue, 0, unroll=8)

    @pl.when(step == nsteps - 1)
    def _():
        drain(slot)
        if nsteps >= 2:
            drain(1 - slot)


def _dispatch(dest0, dest1, zero_rows, hx, n_rows):
    n_tok, s, lanes = hx.shape
    tm = min(MOE_TOKENS, n_tok)
    grid_spec = pltpu.PrefetchScalarGridSpec(
        num_scalar_prefetch=3,
        grid=(n_tok // tm,),
        in_specs=[pl.BlockSpec((tm, s, lanes), lambda i, d0, d1, zr: (i, 0, 0))],
        out_specs=pl.BlockSpec(memory_space=pl.ANY),
        scratch_shapes=[pltpu.VMEM((MOE_BLK, s, lanes), F32),
                        pltpu.VMEM((2, tm, s, lanes), F32),
                        pltpu.SemaphoreType.DMA((2,)), pltpu.SemaphoreType.DMA(())],
    )
    return pl.pallas_call(
        functools.partial(_dispatch_kernel, n_tok // tm),
        grid_spec=grid_spec,
        out_shape=jax.ShapeDtypeStruct((n_rows, s, lanes), F32),
        compiler_params=_cparams(("arbitrary",)),
        name="moe_dispatch",
    )(dest0, dest1, zero_rows, hx)


def _experts_kernel(be_ref, nu_ref, hs_ref, w1_ref, w3_ref, w2_ref, y_ref):
    j = pl.program_id(0)
    d = w1_ref.shape[0]
    ns = d // LANES
    blk = hs_ref.shape[0] // ns

    @pl.when(j < nu_ref[0])
    def _():
        h = jnp.concatenate(
            [hs_ref[pl.ds(s, blk, stride=ns), :].astype(BF16) for s in range(ns)], axis=1)
        out = _swiglu_acc(h, w1_ref, w3_ref, w2_ref)
        for s in range(ns):
            y_ref[pl.ds(s, blk, stride=ns), :] = out[:, s * LANES:(s + 1) * LANES]

    @pl.when(j >= nu_ref[0])
    def _():
        y_ref[...] = jnp.zeros_like(y_ref)


def _experts(block_expert, n_used, hs, w1, w3, w2):
    ne, d, f = w1.shape
    s = d // LANES
    n_rows = hs.shape[0] // s
    n_blocks = n_rows // MOE_BLK

    def expert(shape):
        return pl.BlockSpec((None,) + shape, lambda j, be, nu: (be[j], 0, 0))

    grid_spec = pltpu.PrefetchScalarGridSpec(
        num_scalar_prefetch=2,
        grid=(n_blocks,),
        in_specs=[
            pl.BlockSpec((MOE_BLK * s, LANES), lambda j, be, nu: (jnp.minimum(j, nu[0] - 1), 0)),
            expert((d, f)), expert((d, f)), expert((f, d)),
        ],
        out_specs=pl.BlockSpec((MOE_BLK * s, LANES), lambda j, be, nu: (j, 0)),
    )
    return pl.pallas_call(
        _experts_kernel,
        grid_spec=grid_spec,
        out_shape=jax.ShapeDtypeStruct((n_rows * s, LANES), F32),
        compiler_params=_cparams(("arbitrary",)),
        name="moe_experts",
    )(block_expert, n_used, hs, w1, w3, w2)


def _combine_kernel(d0_ref, d1_ref, x_ref, gate_ref, rt_ref, gfin_ref, y_ref, o_ref, ybuf, sem):
    tm, d = x_ref.shape
    ns = d // LANES
    step = pl.program_id(0) * pl.num_programs(1) + pl.program_id(1)
    nsteps = pl.num_programs(0) * pl.num_programs(1)
    slot = step % 2

    def row(ref, r):
        return ref.at[pl.ds(pl.multiple_of(r * ns, ns), ns)]

    def start_gather(block, into):
        base = block * tm

        def issue(t, carry):
            pltpu.make_async_copy(row(y_ref, d0_ref[base + t]), row(ybuf.at[into, 0], t),
                                  sem.at[into]).start(priority=0)
            pltpu.make_async_copy(row(y_ref, d1_ref[base + t]), row(ybuf.at[into, 1], t),
                                  sem.at[into]).start(priority=1)
            return carry

        lax.fori_loop(0, tm, issue, 0, unroll=8)

    @pl.when(step == 0)
    def _():
        start_gather(0, 0)

    @pl.when(step + 1 < nsteps)
    def _():
        start_gather(step + 1, 1 - slot)

    for k in range(2):
        pltpu.make_async_copy(y_ref.at[pl.ds(0, tm * ns)], ybuf.at[slot, k], sem.at[slot]).wait()

    chunk = min(COMBINE_ROWS, tm)
    for r0 in range(0, tm, chunk):
        rows = pl.ds(r0, chunk)
        rt = rt_ref[rows, :]
        w0 = jnp.broadcast_to(rt[:, R_W0:R_W0 + 1], (chunk, LANES))
        w1 = jnp.broadcast_to(rt[:, R_W1:R_W1 + 1], (chunk, LANES))
        sq = jnp.zeros((chunk, LANES), F32)
        for s in range(ns):
            cols = slice(s * LANES, (s + 1) * LANES)
            moe = (w0 * ybuf[slot, 0, pl.ds(r0 * ns + s, chunk, stride=ns), :]
                   + w1 * ybuf[slot, 1, pl.ds(r0 * ns + s, chunk, stride=ns), :])
            xn = x_ref[rows, cols] + gate_ref[:, cols] * moe
            sq = sq + xn * xn
            o_ref[rows, cols] = xn
        ss = jnp.sum(sq, axis=-1, keepdims=True)
        o_ref[rows, :] = o_ref[rows, :] * lax.rsqrt(ss / d + EPS) * gfin_ref[...]


def _combine(dest0, dest1, x, gate, route, g_final, y):
    bsz, n, d = x.shape
    s = d // LANES
    tm = min(MOE_TOKENS // 2, n)
    grid_spec = pltpu.PrefetchScalarGridSpec(
        num_scalar_prefetch=2,
        grid=(bsz, n // tm),
        in_specs=[
            pl.BlockSpec((None, tm, d), lambda b, i, d0, d1: (b, i, 0)),
            pl.BlockSpec((None, 1, d), lambda b, i, d0, d1: (b, 0, 0)),
            pl.BlockSpec((None, tm, LANES), lambda b, i, d0, d1: (b, i, 0)),
            pl.BlockSpec((1, d), lambda b, i, d0, d1: (0, 0)),
            pl.BlockSpec(memory_space=pl.ANY),
        ],
        out_specs=pl.BlockSpec((None, tm, d), lambda b, i, d0, d1: (b, i, 0)),
        scratch_shapes=[pltpu.VMEM((2, 2, tm * s, LANES), F32), pltpu.SemaphoreType.DMA((2,))],
    )
    return pl.pallas_call(
        _combine_kernel,
        grid_spec=grid_spec,
        out_shape=jax.ShapeDtypeStruct((bsz, n, d), F32),
        compiler_params=_cparams(("arbitrary", "arbitrary")),
        name="moe_combine",
    )(dest0, dest1, x, gate, route, g_final, y)


def _moe(hx, x, gate, route, route_t, counts, g_final, w1, w3, w2):
    bsz, n, d = x.shape
    ne = w1.shape[0]
    n_tok = bsz * n
    n_rows = -(-(2 * n_tok) // MOE_BLK) * MOE_BLK + ne * MOE_BLK
    n_blocks = n_rows // MOE_BLK
    cnt = counts[0, :ne].astype(jnp.int32)
    padded = (cnt + MOE_BLK - 1) // MOE_BLK * MOE_BLK
    pad_end = jnp.cumsum(padded)
    pad_start = pad_end - padded
    n_used = pad_end[-1] // MOE_BLK
    blk = jnp.minimum(jnp.arange(n_blocks, dtype=jnp.int32), n_used - 1)
    block_expert = jnp.minimum(
        jnp.sum((blk[:, None] * MOE_BLK >= pad_end[None, :]).astype(jnp.int32), axis=1), ne - 1)
    def dest(e_row, rank_row):
        e = route_t[e_row].astype(jnp.int32)
        start = sum(jnp.where(e == k, pad_start[k], 0) for k in range(ne))
        return start + route_t[rank_row].astype(jnp.int32)

    dest0 = dest(R_E0, R_RANK0)
    dest1 = dest(R_E1, R_RANK1)

    last_blk = jnp.where(padded > 0, pad_end - MOE_BLK, -1)
    trail = n_used + jnp.arange(ne, dtype=jnp.int32)
    trail = jnp.where(trail < n_blocks, trail * MOE_BLK, -1)
    zero_rows = jnp.concatenate([last_blk, trail]).astype(jnp.int32)

    ns = d // LANES
    hs = _dispatch(dest0, dest1, zero_rows, hx.reshape(n_tok, ns, LANES), n_rows)
    y = _experts(block_expert, n_used.reshape(1).astype(jnp.int32),
                 hs.reshape(n_rows * ns, LANES), w1, w3, w2)
    return _combine(dest0, dest1, x, gate, route, g_final, y)


def _rope_tables(n, hd):
    rows = n // GRID_W
    n_freq = hd // 4
    inv = ROPE_BASE ** (-jnp.arange(n_freq, dtype=F32) / n_freq)
    row_ang = jnp.arange(rows, dtype=F32)[:, None] * inv
    col_ang = jnp.arange(GRID_W, dtype=F32)[:, None] * inv

    def table(of_row, of_col):
        a = jnp.broadcast_to(of_row[:, None, :], (rows, GRID_W, n_freq))
        b = jnp.broadcast_to(of_col[None, :, :], (rows, GRID_W, n_freq))
        return jnp.concatenate([a, b], axis=-1).reshape(n, 2 * n_freq)

    cos = table(jnp.cos(row_ang), jnp.cos(col_ang))
    sin = table(jnp.sin(row_ang), jnp.sin(col_ang))
    return jnp.concatenate([cos, cos], axis=-1), jnp.concatenate([-sin, sin], axis=-1)


def _block_diag_gates(rg_wa, rg_wx):
    w = jnp.stack([rg_wa, rg_wx], axis=1)
    nd, ng, nk, c, _ = w.shape
    per = RG_CG // c
    w = w.reshape(nd, ng, nk // per, per, c, c)
    eye = jnp.eye(per, dtype=w.dtype)
    full = jnp.einsum('dgmpij,pq->dgmpiqj', w, eye)
    return full.reshape(nd, ng, nk // per, RG_CG, RG_CG).astype(BF16)


def _mixer(p, rope, states, conv_w, conv_b, wg, bg, lam, log_g, rg_w):
    rg_out, h_last = _rglru(p, conv_w, conv_b, wg, bg, lam, states[0])
    ret_out, s_last = _retention(p, log_g, rope[0], rope[1], states[1], 2 * rg_w)
    return rg_out, ret_out, (h_last, s_last)


def kernel(x, c, ctx, c_ctx, w_mod, b_mod, g_mix, g_ffn, g_final, w_in, w_out, conv_w, conv_b,
           rg_wa, rg_ba, rg_wx, rg_bx, rg_lam, ret_decay, ffn_w1, ffn_w3, ffn_w2,
           moe_router, moe_router_b, moe_w1, moe_w3, moe_w2):
    bsz, n_lat, d = x.shape
    n_ctx = ctx.shape[1]
    depth = w_mod.shape[0]
    rg_w = rg_lam.shape[2]
    nh = ret_decay.shape[2]
    hd = (w_out.shape[1] - rg_w) // nh
    ne = moe_router.shape[2]
    assert depth == 2, "kernel is written for the two-layer block (dense FFN, then MoE)"

    rows = -(-(bsz + 1) // SUBLANES) * SUBLANES
    c_all = jnp.zeros((rows, d), F32).at[:bsz].set(c).at[bsz].set(c_ctx)
    mod = _modulation(c_all, w_mod, b_mod)
    mod = mod.reshape(depth, rows, N_MOD, d)
    mod_lat = mod[:, :bsz, None]
    mod_ctx = jnp.broadcast_to(mod[:, bsz:bsz + 1, None], mod_lat.shape)

    rope_lat = _rope_tables(n_lat, hd)
    rope_ctx = (jnp.ones((n_ctx, hd), F32), jnp.zeros((n_ctx, hd), F32))
    log_g = jax.nn.log_sigmoid(ret_decay.astype(F32))
    zero_states = (jnp.zeros((bsz, 2, rg_w), F32), jnp.zeros((bsz, 2, nh, hd, hd), F32))

    next_proj = None
    for l in range(depth):
        last = l == depth - 1
        ml = [mod_lat[l, :, :, j] for j in range(N_MOD)]
        mc = [mod_ctx[l, :, :, j] for j in range(N_MOD)]
        g_m = g_mix[l].reshape(1, d)
        g_f = g_ffn[l].reshape(1, d)
        if next_proj is None:
            w_in_l, w_out_l = w_in[l].astype(BF16), w_out[l].astype(BF16)
        else:
            (w_in_l, w_out_l), next_proj = next_proj, None
        wg = _block_diag_gates(0.5 * rg_wa[l], 0.5 * rg_wx[l])
        bg = 0.5 * jnp.stack([rg_ba[l], rg_bx[l]], axis=1)
        mix_p = (conv_w[l], conv_b[l], wg, bg, rg_lam[l], log_g[l], rg_w)

        pc = _in_proj(ctx, g_m, mc[0], mc[1], w_in_l)
        rg_c, ret_c, ctx_states = _mixer(pc, rope_ctx, zero_states, *mix_p)
        px = _in_proj(x, g_m, ml[0], ml[1], w_in_l)
        rg_x, ret_x, _ = _mixer(px, rope_lat, ctx_states, *mix_p)

        if l % 2 == 0:
            i = l // 2
            w1, w3, w2 = ffn_w1[i].astype(BF16), ffn_w3[i].astype(BF16), ffn_w2[i].astype(BF16)
            nxt = moe_w1[i], moe_w3[i], moe_w2[i], w_in[l + 1], w_out[l + 1]
            x, cast = _mix_ffn(x, rg_x, ret_x, w_out_l, ml[2], g_f, ml[3], ml[4], ml[5],
                               w1, w3, w2, cast=[w.reshape(-1, w.shape[-1]) for w in nxt])
            cast = [c.reshape(w.shape) for c, w in zip(cast, nxt)]
            moe_w, next_proj = cast[:3], cast[3:]
            if not last:
                ctx, _ = _mix_ffn(ctx, rg_c, ret_c, w_out_l, mc[2], g_f, mc[3], mc[4], mc[5],
                                  w1, w3, w2)
        else:
            i = l // 2
            assert ne <= SUBLANES
            wr = jnp.zeros((d, LANES), F32).at[:, :ne].set(moe_router[i])
            wr_hi = wr.astype(BF16)
            wr_lo = (wr - wr_hi.astype(F32)).astype(BF16)
            wr2 = jnp.concatenate([wr_hi, wr_lo], axis=1)
            br = jnp.full((1, LANES), NEG_BIG, F32).at[0, :ne].set(moe_router_b[i])
            x, hx, route, route_t, counts = _out_proj_route(
                x, rg_x, ret_x, w_out_l, ml[2], g_f, ml[3], ml[4], wr2, br)
            x = _moe(hx, x, ml[5], route, route_t, counts, g_final.reshape(1, d), *moe_w)
    return x
```

```python
import functools

import jax
import jax.numpy as jnp
from jax import lax
from jax.experimental import pallas as pl
from jax.experimental.pallas import tpu as pltpu

F32 = jnp.float32
BF16 = jnp.bfloat16

EPS = 1e-6
RG_C = 8.0
CONV_W = 4
CONV_LEFT = 2
RET_CHUNK = 256
ROPE_BASE = 10000.0
GRID_W = 64
N_MOD = 6

ROW_TILE = 512

LANES = 128
SUBLANES = 8
VMEM_LIMIT = 56 * 1024 * 1024
NEG_BIG = -1e30
LOG2_E = 1.4426950408889634


def _cparams(sem):
    return pltpu.CompilerParams(dimension_semantics=sem, vmem_limit_bytes=VMEM_LIMIT)


def _modulation_kernel(c_ref, w_ref, b_ref, o_ref):
    c = c_ref[...]
    s = c * jax.nn.sigmoid(c)
    o_ref[...] = jnp.dot(s, w_ref[...], precision=lax.Precision.HIGHEST,
                         preferred_element_type=F32) + b_ref[...]


def _modulation(c_all, w_mod, b_mod):
    depth, d, six_d = w_mod.shape
    rows = c_all.shape[0]
    tn = six_d // 2
    return pl.pallas_call(
        _modulation_kernel,
        grid=(depth, six_d // tn),
        in_specs=[
            pl.BlockSpec((rows, d), lambda l, j: (0, 0)),
            pl.BlockSpec((None, d, tn), lambda l, j: (l, 0, j)),
            pl.BlockSpec((None, 1, tn), lambda l, j: (l, 0, j)),
        ],
        out_specs=pl.BlockSpec((None, rows, tn), lambda l, j: (l, 0, j)),
        out_shape=jax.ShapeDtypeStruct((depth, rows, six_d), F32),
        compiler_params=_cparams(("arbitrary", "arbitrary")),
        name="modulation",
    )(c_all, w_mod, b_mod.reshape(depth, 1, six_d))


def _norm_mod(x, g, shift, scale):
    ms = jnp.mean(x * x, axis=-1, keepdims=True)
    y = x * lax.rsqrt(ms + EPS) * g
    return y * (1.0 + scale) + shift


def _in_proj_kernel(x_ref, g_ref, shift_ref, scale_ref, w_ref, o_ref):
    h = _norm_mod(x_ref[...], g_ref[...], shift_ref[...], scale_ref[...])
    o_ref[...] = jnp.dot(h.astype(BF16), w_ref[...],
                         preferred_element_type=F32).astype(o_ref.dtype)


def _in_proj(x, g, shift, scale, w):
    bsz, n, d = x.shape
    f = w.shape[1]
    tm = min(2 * ROW_TILE, n)
    vec = pl.BlockSpec((None, 1, d), lambda b, i: (b, 0, 0))
    return pl.pallas_call(
        _in_proj_kernel,
        grid=(bsz, n // tm),
        in_specs=[
            pl.BlockSpec((None, tm, d), lambda b, i: (b, i, 0)),
            pl.BlockSpec((1, d), lambda b, i: (0, 0)),
            vec, vec,
            pl.BlockSpec((d, f), lambda b, i: (0, 0)),
        ],
        out_specs=pl.BlockSpec((None, tm, f), lambda b, i: (b, i, 0)),
        out_shape=jax.ShapeDtypeStruct((bsz, n, f), BF16),
        compiler_params=_cparams(("parallel", "parallel")),
        name="in_proj",
    )(x, g, shift, scale, w)


RG_TILE = 256
RG_CG = 256
RG_UNROLL = 4
RG_SEG = RG_TILE // SUBLANES
RG_PITCH = RG_SEG + SUBLANES
assert RG_PITCH % SUBLANES == 0 and RG_PITCH % (2 * SUBLANES) != 0


def _rglru_kernel(u_ref, yg_ref, cw_ref, cb_ref, wg_ref, bg_ref, lam_ref, h0_ref,
                  o_ref, hl_ref, u_scr, hf_scr, ab_scr, nat_scr):
    n = u_ref.shape[0]
    tt = RG_TILE
    nt = n // tt
    nseg = n // RG_SEG
    cg = u_ref.shape[1]
    lane_cols = [slice(g * LANES, (g + 1) * LANES) for g in range(cg // LANES)]

    zeros = jnp.zeros((RG_PITCH, LANES), F32)
    for g, cols in enumerate(lane_cols):
        u_scr[g, pl.ds(0, RG_PITCH), :] = zeros
        u_scr[g, pl.ds(nseg * RG_PITCH + RG_SEG, SUBLANES), :] = zeros[0:SUBLANES]

    def fill(s, carry):
        src = pl.multiple_of(s * RG_SEG, RG_SEG)
        dst = pl.multiple_of((s + 1) * RG_PITCH, SUBLANES)
        for g, cols in enumerate(lane_cols):
            rows = u_ref[pl.ds(src, RG_SEG), cols].astype(F32)
            u_scr[g, pl.ds(dst, RG_SEG), :] = rows
            u_scr[g, pl.ds(dst - SUBLANES, SUBLANES), :] = rows[0:SUBLANES]
        return carry

    lax.fori_loop(0, nseg, fill, 0, unroll=SUBLANES)

    seg = RG_SEG
    cw_rows = [[jnp.broadcast_to(cw_ref[k:k + 1, c], (SUBLANES, LANES)) for c in lane_cols]
               for k in range(CONV_W)]
    cb_rows = [jnp.broadcast_to(cb_ref[:, c], (SUBLANES, LANES)) for c in lane_cols]

    def conv_tile(i):
        base = pl.multiple_of((i * SUBLANES + 1) * RG_PITCH, SUBLANES)
        blocks = []
        for j in range(seg):
            lane_groups = []
            for g in range(cg // LANES):
                acc = cb_rows[g]
                for k in range(CONV_W):
                    q = j + k - CONV_LEFT
                    start = base + q if q >= 0 else base - RG_PITCH + seg + q
                    acc = acc + cw_rows[k][g] * u_scr[g, pl.ds(start, SUBLANES, stride=RG_PITCH), :]
                lane_groups.append(acc)
            blocks.append(jnp.concatenate(lane_groups, axis=1))
        return jnp.concatenate(blocks, axis=0)

    def gates(uc, d):
        ub = uc.astype(BF16)
        ta = jnp.tanh(jnp.dot(ub, wg_ref[d, 0], preferred_element_type=F32) + bg_ref[d, 0:1, :])
        ti = jnp.tanh(jnp.dot(ub, wg_ref[d, 1], preferred_element_type=F32) + bg_ref[d, 1:2, :])
        z = -lam_ref[d:d + 1, :]
        sp = jnp.maximum(z, 0.0) + jnp.log1p(jnp.exp(-jnp.abs(z)))
        c2 = (-0.5 * RG_C * LOG2_E) * sp
        a = jnp.exp2(c2 + c2 * ta)
        b = jnp.exp2((0.5 * LOG2_E) * jnp.log(1.0 - a * a)) * ((0.5 + 0.5 * ti) * uc)
        return a, b

    def scan_tile(a, b, carry, reverse):
        steps = range(seg - 1, -1, -1) if reverse else range(seg)
        h = acum = None
        h_loc = [None] * seg
        a_cum = [None] * seg
        for j in steps:
            rows = slice(j * SUBLANES, (j + 1) * SUBLANES)
            h = b[rows] if h is None else a[rows] * h + b[rows]
            acum = a[rows] if acum is None else a[rows] * acum
            h_loc[j], a_cum[j] = h, acum
        order = range(SUBLANES - 1, -1, -1) if reverse else range(SUBLANES)
        c = carry
        c_in = [None] * SUBLANES
        for s in order:
            c_in[s] = c
            c = h[s:s + 1, :] + acum[s:s + 1, :] * c
        c_in = jnp.concatenate(c_in, axis=0)
        out = jnp.concatenate([h_loc[j] + a_cum[j] * c_in for j in range(seg)], axis=0)
        return out, c

    def fwd_body(i, carry):
        t0 = pl.multiple_of(i * tt, tt)
        uc = conv_tile(i)
        a, b = gates(uc, 0)
        h, carry = scan_tile(a, b, carry, False)
        hf_scr[pl.ds(t0, tt), :] = h
        a, b = gates(uc, 1)
        ab_scr[0, pl.ds(t0, tt), :] = a
        ab_scr[1, pl.ds(t0, tt), :] = b
        return carry

    hf_last = lax.fori_loop(0, nt, fwd_body, h0_ref[0:1, :], unroll=min(RG_UNROLL, nt))

    def bwd_body(i, carry):
        tile = nt - 1 - i
        t0 = pl.multiple_of(tile * tt, tt)
        h, carry = scan_tile(ab_scr[0, pl.ds(t0, tt), :], ab_scr[1, pl.ds(t0, tt), :], carry, True)
        hsum = hf_scr[pl.ds(t0, tt), :] + h
        for j in range(seg):
            for g, cols in enumerate(lane_cols):
                nat_scr[g, pl.ds(j, SUBLANES, stride=RG_PITCH), :] = (
                    hsum[j * SUBLANES:(j + 1) * SUBLANES, cols])
        hnat = jnp.concatenate(
            [jnp.concatenate([nat_scr[g, pl.ds(s * RG_PITCH, seg), :] for s in range(SUBLANES)],
                             axis=0) for g in range(len(lane_cols))], axis=1)
        yg = yg_ref[pl.ds(t0, tt), :].astype(F32)
        o_ref[pl.ds(t0, tt), :] = (jax.nn.gelu(yg) * hnat).astype(o_ref.dtype)
        return carry

    hb_last = lax.fori_loop(0, nt, bwd_body, h0_ref[1:2, :], unroll=min(RG_UNROLL, nt))
    hl_ref[0:1, :] = hf_last
    hl_ref[1:2, :] = hb_last


def _rglru(p, conv_w, conv_b, wg, bg, lam, h0):
    bsz, n, _ = p.shape
    rg_w = lam.shape[1]
    ncg = rg_w // RG_CG
    assert n % RG_TILE == 0 and rg_w % RG_CG == 0
    return pl.pallas_call(
        _rglru_kernel,
        grid=(bsz, ncg),
        in_specs=[
            pl.BlockSpec((None, n, RG_CG), lambda b, c: (b, 0, c)),
            pl.BlockSpec((None, n, RG_CG), lambda b, c: (b, 0, ncg + c)),
            pl.BlockSpec((CONV_W, RG_CG), lambda b, c: (0, c)),
            pl.BlockSpec((1, RG_CG), lambda b, c: (0, c)),
            pl.BlockSpec((2, 2, None, RG_CG, RG_CG), lambda b, c: (0, 0, c, 0, 0)),
            pl.BlockSpec((2, 2, RG_CG), lambda b, c: (0, 0, c)),
            pl.BlockSpec((2, RG_CG), lambda b, c: (0, c)),
            pl.BlockSpec((None, 2, RG_CG), lambda b, c: (b, 0, c)),
        ],
        out_specs=[
            pl.BlockSpec((None, n, RG_CG), lambda b, c: (b, 0, c)),
            pl.BlockSpec((None, 2, RG_CG), lambda b, c: (b, 0, c)),
        ],
        out_shape=[
            jax.ShapeDtypeStruct((bsz, n, rg_w), BF16),
            jax.ShapeDtypeStruct((bsz, 2, rg_w), F32),
        ],
        scratch_shapes=[
            pltpu.VMEM((RG_CG // LANES, (n // RG_SEG + 1) * RG_PITCH, LANES), F32),
            pltpu.VMEM((n, RG_CG), F32),
            pltpu.VMEM((2, n, RG_CG), F32),
            pltpu.VMEM((RG_CG // LANES, SUBLANES * RG_PITCH, LANES), F32),
        ],
        compiler_params=_cparams(("parallel", "parallel")),
        name="rglru",
    )(p, p, conv_w, conv_b.reshape(1, rg_w), wg, bg, lam, h0)


RET_UNROLL = 16


def _retention_kernel(lg_ref, q_ref, k_ref, v_ref, g_ref, cs_ref, sn_ref, s0_ref,
                      o_ref, so_ref, kr_scr, st_scr):
    n, hd = q_ref.shape
    L = RET_CHUNK
    nc = n // L
    head = pl.program_id(1)
    lgf = lg_ref[0, head]
    lgb = lg_ref[1, head]

    def rope(t, rows):
        return t * cs_ref[rows, :] + pltpu.roll(t, hd // 2, axis=1) * sn_ref[rows, :]

    kr_scr[...] = (rope(k_ref[...].astype(F32), slice(None)) * (hd ** -0.5)).astype(BF16)

    jc = lax.broadcasted_iota(jnp.int32, (L, 1), 0).astype(F32)
    q_dec_f = jnp.exp((jc + 1.0) * lgf)
    q_dec_b = jnp.exp((L - jc) * lgb)
    k_dec_f = jnp.exp((L - 1.0 - jc) * lgf)
    k_dec_b = jnp.exp(jc * lgb)
    ones = jnp.ones((1, hd), F32)
    chunk_f = jnp.exp(ones * (L * lgf))
    chunk_b = jnp.exp(ones * (L * lgb))
    ii = lax.broadcasted_iota(jnp.int32, (L, L), 0)
    jj = lax.broadcasted_iota(jnp.int32, (L, L), 1)
    diff = (ii - jj).astype(F32)
    dmat = jnp.where(diff >= 0.0, jnp.exp(jnp.maximum(diff, 0.0) * lgf),
                     jnp.exp(jnp.maximum(-diff, 0.0) * lgb))

    tn_dims = (((0,), (0,)), ((), ()))
    nt_dims = (((1,), (1,)), ((), ()))

    def state_update(c, state, k_dec, chunk_dec):
        t0 = pl.multiple_of(c * L, L)
        kd = (kr_scr[pl.ds(t0, L), :].astype(F32) * k_dec).astype(BF16)
        upd = lax.dot_general(kd, v_ref[pl.ds(t0, L), :], tn_dims, preferred_element_type=F32)
        return chunk_dec * state + upd

    def state_body(i, carry):
        f_state, r_state = carry
        cf = i
        cb = nc - 1 - i
        st_scr[cf, :, 0:hd] = f_state.astype(BF16)
        st_scr[cb, :, hd:2 * hd] = r_state.astype(BF16)
        return (state_update(cf, f_state, k_dec_f, chunk_f),
                state_update(cb, r_state, k_dec_b, chunk_b))

    f_fin, r_fin = lax.fori_loop(0, nc, state_body, (s0_ref[0], s0_ref[1]),
                                 unroll=min(RET_UNROLL, nc))
    so_ref[0] = f_fin
    so_ref[1] = r_fin

    def out_body(c, carry):
        t0 = pl.multiple_of(c * L, L)
        qc = rope(q_ref[pl.ds(t0, L), :].astype(F32), pl.ds(t0, L)).astype(BF16)
        kc = kr_scr[pl.ds(t0, L), :]
        vc = v_ref[pl.ds(t0, L), :]
        s = lax.dot_general(qc, kc, nt_dims, preferred_element_type=F32)
        o = jnp.dot((s * dmat).astype(BF16), vc, preferred_element_type=F32)
        cross = jnp.dot(qc, st_scr[c], preferred_element_type=F32)
        o = o + cross[:, 0:hd] * q_dec_f + cross[:, hd:2 * hd] * q_dec_b
        mu = jnp.mean(o, axis=-1, keepdims=True)
        oc = o - mu
        var = jnp.mean(oc * oc, axis=-1, keepdims=True)
        gate = g_ref[pl.ds(t0, L), :].astype(F32)
        o_ref[pl.ds(t0, L), :] = (gate * jax.nn.sigmoid(gate) * (oc * lax.rsqrt(var + EPS))).astype(o_ref.dtype)
        return carry

    lax.fori_loop(0, nc, out_body, 0, unroll=min(RET_UNROLL, nc))


def _retention(p, log_g, cs, sn, s0, col0):
    bsz, n, _ = p.shape
    nh = log_g.shape[1]
    hd = cs.shape[1]
    cb0 = col0 // hd

    def col(which):
        return pl.BlockSpec((None, n, hd), lambda b, h, lg: (b, 0, cb0 + which * nh + h))

    state = pl.BlockSpec((None, 2, None, hd, hd), lambda b, h, lg: (b, 0, h, 0, 0))
    table = pl.BlockSpec((n, hd), lambda b, h, lg: (0, 0))
    grid_spec = pltpu.PrefetchScalarGridSpec(
        num_scalar_prefetch=1,
        grid=(bsz, nh),
        in_specs=[col(0), col(1), col(2), col(3), table, table, state],
        out_specs=[pl.BlockSpec((None, n, hd), lambda b, h, lg: (b, 0, h)), state],
        scratch_shapes=[
            pltpu.VMEM((n, hd), BF16),
            pltpu.VMEM((n // RET_CHUNK, hd, 2 * hd), BF16),
        ],
    )
    return pl.pallas_call(
        _retention_kernel,
        grid_spec=grid_spec,
        out_shape=[
            jax.ShapeDtypeStruct((bsz, n, nh * hd), BF16),
            jax.ShapeDtypeStruct((bsz, 2, nh, hd, hd), F32),
        ],
        compiler_params=_cparams(("parallel", "parallel")),
        name="retention",
    )(log_g, p, p, p, p, cs, sn, s0)


R_RANK0, R_RANK1, R_E0, R_E1, R_W0, R_W1 = range(6)


def _mix_residual(x_ref, rg_ref, ret_ref, wo_ref, gate_ref, rows=slice(None)):
    rg_w = rg_ref.shape[1]
    y = jnp.dot(rg_ref[rows, :], wo_ref[0:rg_w, :], preferred_element_type=F32)
    y = y + jnp.dot(ret_ref[rows, :], wo_ref[rg_w:, :], preferred_element_type=F32)
    return x_ref[rows, :] + gate_ref[...] * y


ROUTE_ROWS = 512


def _out_proj_kernel(x_ref, rg_ref, ret_ref, wo_ref, gate_ref, g_ref, shift_ref, scale_ref,
                     wr2_ref, br_ref, xo_ref, h_ref, rt_ref, rtt_ref, cnt_ref, carry_scr):
    tm, d = x_ref.shape
    ns = d // LANES
    nr = min(ROUTE_ROWS, tm)

    @pl.when((pl.program_id(0) == 0) & (pl.program_id(1) == 0))
    def _():
        carry_scr[...] = jnp.zeros_like(carry_scr)

    erow = lax.broadcasted_iota(jnp.int32, (SUBLANES, nr), 0).astype(F32)
    ri = lax.broadcasted_iota(jnp.int32, (nr, nr), 0)
    ci = lax.broadcasted_iota(jnp.int32, (nr, nr), 1)
    earlier = jnp.where(ri < ci, 1.0, 0.0).astype(BF16)
    count = carry_scr[:, 0:1]

    for r0 in range(0, tm, nr):
        rows = pl.ds(r0, nr)
        xn = _mix_residual(x_ref, rg_ref, ret_ref, wo_ref, gate_ref, rows)
        xo_ref[rows, :] = xn
        hx = _norm_mod(xn, g_ref[...], shift_ref[...], scale_ref[...])

        for s in range(ns):
            h_ref[pl.ds(r0 * ns + s, nr, stride=ns), :] = hx[:, s * LANES:(s + 1) * LANES]

        h_hi = hx.astype(BF16)
        h_lo = (hx - h_hi.astype(F32)).astype(BF16)
        part = jnp.dot(h_hi, wr2_ref[...], preferred_element_type=F32)
        logits = (part[:, :LANES] + part[:, LANES:]
                  + jnp.dot(h_lo, wr2_ref[:, :LANES], preferred_element_type=F32) + br_ref[...])
        lt = jnp.transpose(logits)[0:SUBLANES, :]
        m0 = jnp.max(lt, axis=0, keepdims=True)
        i0 = jnp.min(jnp.where(lt == m0, erow, float(SUBLANES)), axis=0, keepdims=True)
        rest_l = jnp.where(erow == i0, NEG_BIG, lt)
        m1 = jnp.max(rest_l, axis=0, keepdims=True)
        i1 = jnp.min(jnp.where(rest_l == m1, erow, float(SUBLANES)), axis=0, keepdims=True)
        e = jnp.exp(m1 - m0)
        w0 = 1.0 / (1.0 + e)
        w1 = e / (1.0 + e)

        sel0 = erow == i0
        sel1 = erow == i1
        mask = jnp.where(sel0 | sel1, 1.0, 0.0)
        before = jnp.dot(mask.astype(BF16), earlier, preferred_element_type=F32) + count
        rank0 = jnp.sum(jnp.where(sel0, before, 0.0), axis=0, keepdims=True)
        rank1 = jnp.sum(jnp.where(sel1, before, 0.0), axis=0, keepdims=True)
        count = count + jnp.sum(mask, axis=1, keepdims=True)
        rec = jnp.zeros((SUBLANES, nr), F32)
        for idx, val in enumerate((rank0, rank1, i0, i1, w0, w1)):
            rec = jnp.where(erow == idx, val, rec)
        rtt_ref[:, r0:r0 + nr] = rec
        rt_ref[rows, :] = jnp.transpose(
            jnp.concatenate([rec, jnp.zeros((LANES - SUBLANES, nr), F32)], axis=0))

    carry_scr[...] = jnp.broadcast_to(count, carry_scr.shape)
    on_diag = (lax.broadcasted_iota(jnp.int32, (SUBLANES, LANES), 0)
               == lax.broadcasted_iota(jnp.int32, (SUBLANES, LANES), 1))
    cnt_ref[...] = jnp.sum(jnp.where(on_diag, count, 0.0), axis=0, keepdims=True)


def _out_proj_route(x, rg, ret, w_out, gate, g, shift, scale, wr2, br):
    bsz, n, d = x.shape
    tm = min(2 * ROUTE_ROWS, n)
    vec = pl.BlockSpec((None, 1, d), lambda b, i: (b, 0, 0))
    tile = lambda w: pl.BlockSpec((None, tm, w), lambda b, i: (b, i, 0))
    const = lambda shape: pl.BlockSpec(shape, lambda b, i: (0, 0))
    return pl.pallas_call(
        _out_proj_kernel,
        grid=(bsz, n // tm),
        in_specs=[
            tile(d), tile(rg.shape[2]), tile(ret.shape[2]), const(w_out.shape),
            vec, const((1, d)), vec, vec, const((d, 2 * LANES)), const((1, LANES)),
        ],
        out_specs=[
            tile(d),
            pl.BlockSpec((None, tm * (d // LANES), LANES), lambda b, i: (b, i, 0)),
            tile(LANES),
            pl.BlockSpec((SUBLANES, tm), lambda b, i: (0, b * (n // tm) + i)),
            const((1, LANES)),
        ],
        out_shape=[
            jax.ShapeDtypeStruct((bsz, n, d), F32),
            jax.ShapeDtypeStruct((bsz, n * (d // LANES), LANES), F32),
            jax.ShapeDtypeStruct((bsz, n, LANES), F32),
            jax.ShapeDtypeStruct((SUBLANES, bsz * n), F32),
            jax.ShapeDtypeStruct((1, LANES), F32),
        ],
        scratch_shapes=[pltpu.VMEM((SUBLANES, LANES), F32)],
        compiler_params=_cparams(("arbitrary", "arbitrary")),
        name="out_proj_route",
    )(x, rg, ret, w_out, gate, g, shift, scale, wr2, br)


def _swiglu_acc(h, w1_ref, w3_ref, w2_ref):
    a = jnp.dot(h, w1_ref[...], preferred_element_type=F32)
    b = jnp.dot(h, w3_ref[...], preferred_element_type=F32)
    act = a * jax.nn.sigmoid(a) * b
    return jnp.dot(act.astype(BF16), w2_ref[...], preferred_element_type=F32)


def _mix_ffn_kernel(n_cast, x_ref, rg_ref, ret_ref, wo_ref, gmix_ref, g_ref, shift_ref, scale_ref,
                    gffn_ref, w1_ref, w3_ref, w2_ref, *rest):
    cast_in, o_ref, cast_out = rest[:n_cast], rest[n_cast], rest[n_cast + 1:]
    xn = _mix_residual(x_ref, rg_ref, ret_ref, wo_ref, gmix_ref)
    hx = _norm_mod(xn, g_ref[...], shift_ref[...], scale_ref[...]).astype(BF16)
    o_ref[...] = xn + gffn_ref[...] * _swiglu_acc(hx, w1_ref, w3_ref, w2_ref)
    for src, dst in zip(cast_in, cast_out):
        dst[...] = src[...].astype(dst.dtype)


def _mix_ffn(x, rg, ret, w_out, gate_mix, g, shift, scale, gate_ffn, w1, w3, w2, cast=()):
    bsz, n, d = x.shape
    f = w1.shape[1]
    tm = min(ROW_TILE, n)
    steps = bsz * (n // tm)
    vec = pl.BlockSpec((None, 1, d), lambda b, i: (b, 0, 0))
    tile = lambda w: pl.BlockSpec((None, tm, w), lambda b, i: (b, i, 0))
    resident = lambda shape: pl.BlockSpec(shape, lambda b, i: (0, 0), pipeline_mode=pl.Buffered(1))
    slabs = []
    for a in cast:
        rows, cols = a.shape
        assert rows % (steps * 2 * SUBLANES) == 0
        slabs.append(pl.BlockSpec((rows // steps, cols), lambda b, i: (b * (n // tm) + i, 0)))
    out = pl.pallas_call(
        functools.partial(_mix_ffn_kernel, len(cast)),
        grid=(bsz, n // tm),
        in_specs=[
            tile(d), tile(rg.shape[2]), tile(ret.shape[2]), resident(w_out.shape),
            vec, resident((1, d)), vec, vec, vec,
            resident((d, f)), resident((d, f)), resident((f, d)),
        ] + slabs,
        out_specs=[tile(d)] + slabs,
        out_shape=[jax.ShapeDtypeStruct((bsz, n, d), F32)]
        + [jax.ShapeDtypeStruct(a.shape, BF16) for a in cast],
        compiler_params=_cparams(("parallel", "parallel")),
        name="mix_ffn",
    )(x, rg, ret, w_out, gate_mix, g, shift, scale, gate_ffn, w1, w3, w2, *cast)
    return out[0], tuple(out[1:])


MOE_BLK = 256
MOE_TOKENS = 2048
COMBINE_ROWS = 256


def _dispatch_kernel(nsteps, d0_ref, d1_ref, zr_ref, hx_ref, hs_ref, zbuf, stage, sem, zsem):
    tm = hx_ref.shape[0]
    step = pl.program_id(0)
    base = step * tm
    slot = step % 2

    @pl.when(pl.program_id(0) == 0)
    def _():
        zbuf[...] = jnp.zeros_like(zbuf)
        for z in range(zr_ref.shape[0]):
            @pl.when(zr_ref[z] >= 0)
            def _():
                pltpu.make_async_copy(zbuf, hs_ref.at[pl.ds(zr_ref[z], MOE_BLK)], zsem).start()
        for z in range(zr_ref.shape[0]):
            @pl.when(zr_ref[z] >= 0)
            def _():
                pltpu.make_async_copy(zbuf, hs_ref.at[pl.ds(zr_ref[z], MOE_BLK)], zsem).wait()

    def drain(which):
        for _ in range(2):
            pltpu.make_async_copy(stage.at[which], hs_ref.at[pl.ds(0, tm)], sem.at[which]).wait()

    @pl.when(step >= 2)
    def _():
        drain(slot)

    stage[slot] = hx_ref[...]
    src = stage.at[slot]

    def issue(t, carry):
        pltpu.make_async_copy(src.at[t], hs_ref.at[d0_ref[base + t]], sem.at[slot]).start(priority=0)
        pltpu.make_async_copy(src.at[t], hs_ref.at[d1_ref[base + t]], sem.at[slot]).start(priority=1)
        return carry

    lax.fori_loop(0, tm, issue, 0, unroll=8)

    @pl.when(step == nsteps - 1)
    def _():
        drain(slot)
        if nsteps >= 2:
            drain(1 - slot)


def _dispatch(dest0, dest1, zero_rows, hx, n_rows):
    n_tok, s, lanes = hx.shape
    tm = min(MOE_TOKENS, n_tok)
    grid_spec = pltpu.PrefetchScalarGridSpec(
        num_scalar_prefetch=3,
        grid=(n_tok // tm,),
        in_specs=[pl.BlockSpec((tm, s, lanes), lambda i, d0, d1, zr: (i, 0, 0))],
        out_specs=pl.BlockSpec(memory_space=pl.ANY),
        scratch_shapes=[pltpu.VMEM((MOE_BLK, s, lanes), F32),
                        pltpu.VMEM((2, tm, s, lanes), F32),
                        pltpu.SemaphoreType.DMA((2,)), pltpu.SemaphoreType.DMA(())],
    )
    return pl.pallas_call(
        functools.partial(_dispatch_kernel, n_tok // tm),
        grid_spec=grid_spec,
        out_shape=jax.ShapeDtypeStruct((n_rows, s, lanes), F32),
        compiler_params=_cparams(("arbitrary",)),
        name="moe_dispatch",
    )(dest0, dest1, zero_rows, hx)


def _experts_kernel(be_ref, nu_ref, hs_ref, w1_ref, w3_ref, w2_ref, y_ref):
    j = pl.program_id(0)
    d = w1_ref.shape[0]
    ns = d // LANES
    blk = hs_ref.shape[0] // ns

    @pl.when(j < nu_ref[0])
    def _():
        h = jnp.concatenate(
            [hs_ref[pl.ds(s, blk, stride=ns), :].astype(BF16) for s in range(ns)], axis=1)
        out = _swiglu_acc(h, w1_ref, w3_ref, w2_ref)
        for s in range(ns):
            y_ref[pl.ds(s, blk, stride=ns), :] = out[:, s * LANES:(s + 1) * LANES]

    @pl.when(j >= nu_ref[0])
    def _():
        y_ref[...] = jnp.zeros_like(y_ref)


def _experts(block_expert, n_used, hs, w1, w3, w2):
    ne, d, f = w1.shape
    s = d // LANES
    n_rows = hs.shape[0] // s
    n_blocks = n_rows // MOE_BLK

    def expert(shape):
        return pl.BlockSpec((None,) + shape, lambda j, be, nu: (be[j], 0, 0))

    grid_spec = pltpu.PrefetchScalarGridSpec(
        num_scalar_prefetch=2,
        grid=(n_blocks,),
        in_specs=[
            pl.BlockSpec((MOE_BLK * s, LANES), lambda j, be, nu: (jnp.minimum(j, nu[0] - 1), 0)),
            expert((d, f)), expert((d, f)), expert((f, d)),
        ],
        out_specs=pl.BlockSpec((MOE_BLK * s, LANES), lambda j, be, nu: (j, 0)),
    )
    return pl.pallas_call(
        _experts_kernel,
        grid_spec=grid_spec,
        out_shape=jax.ShapeDtypeStruct((n_rows * s, LANES), F32),
        compiler_params=_cparams(("arbitrary",)),
        name="moe_experts",
    )(block_expert, n_used, hs, w1, w3, w2)


def _combine_kernel(d0_ref, d1_ref, x_ref, gate_ref, rt_ref, gfin_ref, y_ref, o_ref, ybuf, sem):
    tm, d = x_ref.shape
    ns = d // LANES
    step = pl.program_id(0) * pl.num_programs(1) + pl.program_id(1)
    nsteps = pl.num_programs(0) * pl.num_programs(1)
    slot = step % 2

    def row(ref, r):
        return ref.at[pl.ds(pl.multiple_of(r * ns, ns), ns)]

    def start_gather(block, into):
        base = block * tm

        def issue(t, carry):
            pltpu.make_async_copy(row(y_ref, d0_ref[base + t]), row(ybuf.at[into, 0], t),
                                  sem.at[into]).start(priority=0)
            pltpu.make_async_copy(row(y_ref, d1_ref[base + t]), row(ybuf.at[into, 1], t),
                                  sem.at[into]).start(priority=1)
            return carry

        lax.fori_loop(0, tm, issue, 0, unroll=8)

    @pl.when(step == 0)
    def _():
        start_gather(0, 0)

    @pl.when(step + 1 < nsteps)
    def _():
        start_gather(step + 1, 1 - slot)

    for k in range(2):
        pltpu.make_async_copy(y_ref.at[pl.ds(0, tm * ns)], ybuf.at[slot, k], sem.at[slot]).wait()

    chunk = min(COMBINE_ROWS, tm)
    for r0 in range(0, tm, chunk):
        rows = pl.ds(r0, chunk)
        rt = rt_ref[rows, :]
        w0 = jnp.broadcast_to(rt[:, R_W0:R_W0 + 1], (chunk, LANES))
        w1 = jnp.broadcast_to(rt[:, R_W1:R_W1 + 1], (chunk, LANES))
        sq = jnp.zeros((chunk, LANES), F32)
        for s in range(ns):
            cols = slice(s * LANES, (s + 1) * LANES)
            moe = (w0 * ybuf[slot, 0, pl.ds(r0 * ns + s, chunk, stride=ns), :]
                   + w1 * ybuf[slot, 1, pl.ds(r0 * ns + s, chunk, stride=ns), :])
            xn = x_ref[rows, cols] + gate_ref[:, cols] * moe
            sq = sq + xn * xn
            o_ref[rows, cols] = xn
        ss = jnp.sum(sq, axis=-1, keepdims=True)
        o_ref[rows, :] = o_ref[rows, :] * lax.rsqrt(ss / d + EPS) * gfin_ref[...]


def _combine(dest0, dest1, x, gate, route, g_final, y):
    bsz, n, d = x.shape
    s = d // LANES
    tm = min(MOE_TOKENS // 2, n)
    grid_spec = pltpu.PrefetchScalarGridSpec(
        num_scalar_prefetch=2,
        grid=(bsz, n // tm),
        in_specs=[
            pl.BlockSpec((None, tm, d), lambda b, i, d0, d1: (b, i, 0)),
            pl.BlockSpec((None, 1, d), lambda b, i, d0, d1: (b, 0, 0)),
            pl.BlockSpec((None, tm, LANES), lambda b, i, d0, d1: (b, i, 0)),
            pl.BlockSpec((1, d), lambda b, i, d0, d1: (0, 0)),
            pl.BlockSpec(memory_space=pl.ANY),
        ],
        out_specs=pl.BlockSpec((None, tm, d), lambda b, i, d0, d1: (b, i, 0)),
        scratch_shapes=[pltpu.VMEM((2, 2, tm * s, LANES), F32), pltpu.SemaphoreType.DMA((2,))],
    )
    return pl.pallas_call(
        _combine_kernel,
        grid_spec=grid_spec,
        out_shape=jax.ShapeDtypeStruct((bsz, n, d), F32),
        compiler_params=_cparams(("arbitrary", "arbitrary")),
        name="moe_combine",
    )(dest0, dest1, x, gate, route, g_final, y)


def _moe(hx, x, gate, route, route_t, counts, g_final, w1, w3, w2):
    bsz, n, d = x.shape
    ne = w1.shape[0]
    n_tok = bsz * n
    n_rows = -(-(2 * n_tok) // MOE_BLK) * MOE_BLK + ne * MOE_BLK
    n_blocks = n_rows // MOE_BLK
    cnt = counts[0, :ne].astype(jnp.int32)
    padded = (cnt + MOE_BLK - 1) // MOE_BLK * MOE_BLK
    pad_end = jnp.cumsum(padded)
    pad_start = pad_end - padded
    n_used = pad_end[-1] // MOE_BLK
    blk = jnp.minimum(jnp.arange(n_blocks, dtype=jnp.int32), n_used - 1)
    block_expert = jnp.minimum(
        jnp.sum((blk[:, None] * MOE_BLK >= pad_end[None, :]).astype(jnp.int32), axis=1), ne - 1)
    def dest(e_row, rank_row):
        e = route_t[e_row].astype(jnp.int32)
        start = sum(jnp.where(e == k, pad_start[k], 0) for k in range(ne))
        return start + route_t[rank_row].astype(jnp.int32)

    dest0 = dest(R_E0, R_RANK0)
    dest1 = dest(R_E1, R_RANK1)

    last_blk = jnp.where(padded > 0, pad_end - MOE_BLK, -1)
    trail = n_used + jnp.arange(ne, dtype=jnp.int32)
    trail = jnp.where(trail < n_blocks, trail * MOE_BLK, -1)
    zero_rows = jnp.concatenate([last_blk, trail]).astype(jnp.int32)

    ns = d // LANES
    hs = _dispatch(dest0, dest1, zero_rows, hx.reshape(n_tok, ns, LANES), n_rows)
    y = _experts(block_expert, n_used.reshape(1).astype(jnp.int32),
                 hs.reshape(n_rows * ns, LANES), w1, w3, w2)
    return _combine(dest0, dest1, x, gate, route, g_final, y)


def _rope_tables(n, hd):
    rows = n // GRID_W
    n_freq = hd // 4
    inv = ROPE_BASE ** (-jnp.arange(n_freq, dtype=F32) / n_freq)
    row_ang = jnp.arange(rows, dtype=F32)[:, None] * inv
    col_ang = jnp.arange(GRID_W, dtype=F32)[:, None] * inv

    def table(of_row, of_col):
        a = jnp.broadcast_to(of_row[:, None, :], (rows, GRID_W, n_freq))
        b = jnp.broadcast_to(of_col[None, :, :], (rows, GRID_W, n_freq))
        return jnp.concatenate([a, b], axis=-1).reshape(n, 2 * n_freq)

    cos = table(jnp.cos(row_ang), jnp.cos(col_ang))
    sin = table(jnp.sin(row_ang), jnp.sin(col_ang))
    return jnp.concatenate([cos, cos], axis=-1), jnp.concatenate([-sin, sin], axis=-1)


def _block_diag_gates(rg_wa, rg_wx):
    w = jnp.stack([rg_wa, rg_wx], axis=1)
    nd, ng, nk, c, _ = w.shape
    per = RG_CG // c
    w = w.reshape(nd, ng, nk // per, per, c, c)
    eye = jnp.eye(per, dtype=w.dtype)
    full = jnp.einsum('dgmpij,pq->dgmpiqj', w, eye)
    return full.reshape(nd, ng, nk // per, RG_CG, RG_CG).astype(BF16)


def _mixer(p, rope, states, conv_w, conv_b, wg, bg, lam, log_g, rg_w):
    rg_out, h_last = _rglru(p, conv_w, conv_b, wg, bg, lam, states[0])
    ret_out, s_last = _retention(p, log_g, rope[0], rope[1], states[1], 2 * rg_w)
    return rg_out, ret_out, (h_last, s_last)


def kernel(x, c, ctx, c_ctx, w_mod, b_mod, g_mix, g_ffn, g_final, w_in, w_out, conv_w, conv_b,
           rg_wa, rg_ba, rg_wx, rg_bx, rg_lam, ret_decay, ffn_w1, ffn_w3, ffn_w2,
           moe_router, moe_router_b, moe_w1, moe_w3, moe_w2):
    bsz, n_lat, d = x.shape
    n_ctx = ctx.shape[1]
    depth = w_mod.shape[0]
    rg_w = rg_lam.shape[2]
    nh = ret_decay.shape[2]
    hd = (w_out.shape[1] - rg_w) // nh
    ne = moe_router.shape[2]
    assert depth == 2, "kernel is written for the two-layer block (dense FFN, then MoE)"

    rows = -(-(bsz + 1) // SUBLANES) * SUBLANES
    c_all = jnp.zeros((rows, d), F32).at[:bsz].set(c).at[bsz].set(c_ctx)
    mod = _modulation(c_all, w_mod, b_mod)
    mod = mod.reshape(depth, rows, N_MOD, d)
    mod_lat = mod[:, :bsz, None]
    mod_ctx = jnp.broadcast_to(mod[:, bsz:bsz + 1, None], mod_lat.shape)

    rope_lat = _rope_tables(n_lat, hd)
    rope_ctx = (jnp.ones((n_ctx, hd), F32), jnp.zeros((n_ctx, hd), F32))
    log_g = jax.nn.log_sigmoid(ret_decay.astype(F32))
    zero_states = (jnp.zeros((bsz, 2, rg_w), F32), jnp.zeros((bsz, 2, nh, hd, hd), F32))

    next_proj = None
    for l in range(depth):
        last = l == depth - 1
        ml = [mod_lat[l, :, :, j] for j in range(N_MOD)]
        mc = [mod_ctx[l, :, :, j] for j in range(N_MOD)]
        g_m = g_mix[l].reshape(1, d)
        g_f = g_ffn[l].reshape(1, d)
        if next_proj is None:
            w_in_l, w_out_l = w_in[l].astype(BF16), w_out[l].astype(BF16)
        else:
            (w_in_l, w_out_l), next_proj = next_proj, None
        wg = _block_diag_gates(0.5 * rg_wa[l], 0.5 * rg_wx[l])
        bg = 0.5 * jnp.stack([rg_ba[l], rg_bx[l]], axis=1)
        mix_p = (conv_w[l], conv_b[l], wg, bg, rg_lam[l], log_g[l], rg_w)

        pc = _in_proj(ctx, g_m, mc[0], mc[1], w_in_l)
        rg_c, ret_c, ctx_states = _mixer(pc, rope_ctx, zero_states, *mix_p)
        px = _in_proj(x, g_m, ml[0], ml[1], w_in_l)
        rg_x, ret_x, _ = _mixer(px, rope_lat, ctx_states, *mix_p)

        if l % 2 == 0:
            i = l // 2
            w1, w3, w2 = ffn_w1[i].astype(BF16), ffn_w3[i].astype(BF16), ffn_w2[i].astype(BF16)
            nxt = moe_w1[i], moe_w3[i], moe_w2[i], w_in[l + 1], w_out[l + 1]
            x, cast = _mix_ffn(x, rg_x, ret_x, w_out_l, ml[2], g_f, ml[3], ml[4], ml[5],
                               w1, w3, w2, cast=[w.reshape(-1, w.shape[-1]) for w in nxt])
            cast = [c.reshape(w.shape) for c, w in zip(cast, nxt)]
            moe_w, next_proj = cast[:3], cast[3:]
            if not last:
                ctx, _ = _mix_ffn(ctx, rg_c, ret_c, w_out_l, mc[2], g_f, mc[3], mc[4], mc[5],
                                  w1, w3, w2)
        else:
            i = l // 2
            assert ne <= SUBLANES
            wr = jnp.zeros((d, LANES), F32).at[:, :ne].set(moe_router[i])
            wr_hi = wr.astype(BF16)
            wr_lo = (wr - wr_hi.astype(F32)).astype(BF16)
            wr2 = jnp.concatenate([wr_hi, wr_lo], axis=1)
            br = jnp.full((1, LANES), NEG_BIG, F32).at[0, :ne].set(moe_router_b[i])
            x, hx, route, route_t, counts = _out_proj_route(
                x, rg_x, ret_x, w_out_l, ml[2], g_f, ml[3], ml[4], wr2, br)
            x = _moe(hx, x, ml[5], route, route_t, counts, g_final.reshape(1, d), *moe_w)
    return x
```

```python
import functools

import jax
import jax.numpy as jnp
from jax import lax
from jax.experimental import pallas as pl
from jax.experimental.pallas import tpu as pltpu

F32 = jnp.float32
BF16 = jnp.bfloat16

EPS = 1e-6
RG_C = 8.0
CONV_W = 4
CONV_LEFT = 2
RET_CHUNK = 256
ROPE_BASE = 10000.0
GRID_W = 64
N_MOD = 6

ROW_TILE = 512

LANES = 128
SUBLANES = 8
VMEM_LIMIT = 56 * 1024 * 1024
NEG_BIG = -1e30
LOG2_E = 1.4426950408889634


def _cparams(sem):
    return pltpu.CompilerParams(dimension_semantics=sem, vmem_limit_bytes=VMEM_LIMIT)


def _modulation_kernel(c_ref, w_ref, b_ref, o_ref):
    c = c_ref[...]
    s = c * jax.nn.sigmoid(c)
    o_ref[...] = jnp.dot(s, w_ref[...], precision=lax.Precision.HIGHEST,
                         preferred_element_type=F32) + b_ref[...]


def _modulation(c_all, w_mod, b_mod):
    depth, d, six_d = w_mod.shape
    rows = c_all.shape[0]
    tn = six_d // 2
    return pl.pallas_call(
        _modulation_kernel,
        grid=(depth, six_d // tn),
        in_specs=[
            pl.BlockSpec((rows, d), lambda l, j: (0, 0)),
            pl.BlockSpec((None, d, tn), lambda l, j: (l, 0, j)),
            pl.BlockSpec((None, 1, tn), lambda l, j: (l, 0, j)),
        ],
        out_specs=pl.BlockSpec((None, rows, tn), lambda l, j: (l, 0, j)),
        out_shape=jax.ShapeDtypeStruct((depth, rows, six_d), F32),
        compiler_params=_cparams(("arbitrary", "arbitrary")),
        name="modulation",
    )(c_all, w_mod, b_mod.reshape(depth, 1, six_d))


def _norm_mod(x, g, shift, scale):
    ms = jnp.mean(x * x, axis=-1, keepdims=True)
    y = x * lax.rsqrt(ms + EPS) * g
    return y * (1.0 + scale) + shift


def _in_proj_kernel(x_ref, g_ref, shift_ref, scale_ref, w_ref, o_ref):
    h = _norm_mod(x_ref[...], g_ref[...], shift_ref[...], scale_ref[...])
    o_ref[...] = jnp.dot(h.astype(BF16), w_ref[...],
                         preferred_element_type=F32).astype(o_ref.dtype)


def _in_proj(x, g, shift, scale, w):
    bsz, n, d = x.shape
    f = w.shape[1]
    tm = min(2 * ROW_TILE, n)
    vec = pl.BlockSpec((None, 1, d), lambda b, i: (b, 0, 0))
    return pl.pallas_call(
        _in_proj_kernel,
        grid=(bsz, n // tm),
        in_specs=[
            pl.BlockSpec((None, tm, d), lambda b, i: (b, i, 0)),
            pl.BlockSpec((1, d), lambda b, i: (0, 0)),
            vec, vec,
            pl.BlockSpec((d, f), lambda b, i: (0, 0)),
        ],
        out_specs=pl.BlockSpec((None, tm, f), lambda b, i: (b, i, 0)),
        out_shape=jax.ShapeDtypeStruct((bsz, n, f), BF16),
        compiler_params=_cparams(("parallel", "parallel")),
        name="in_proj",
    )(x, g, shift, scale, w)


RG_TILE = 256
RG_CG = 256
RG_UNROLL = 4
RG_SEG = RG_TILE // SUBLANES
RG_PITCH = RG_SEG + SUBLANES
assert RG_PITCH % SUBLANES == 0 and RG_PITCH % (2 * SUBLANES) != 0


def _rglru_kernel(u_ref, yg_ref, cw_ref, cb_ref, wg_ref, bg_ref, lam_ref, h0_ref,
                  o_ref, hl_ref, u_scr, hf_scr, ab_scr, nat_scr):
    n = u_ref.shape[0]
    tt = RG_TILE
    nt = n // tt
    nseg = n // RG_SEG
    cg = u_ref.shape[1]
    lane_cols = [slice(g * LANES, (g + 1) * LANES) for g in range(cg // LANES)]

    zeros = jnp.zeros((RG_PITCH, LANES), F32)
    for g, cols in enumerate(lane_cols):
        u_scr[g, pl.ds(0, RG_PITCH), :] = zeros
        u_scr[g, pl.ds(nseg * RG_PITCH + RG_SEG, SUBLANES), :] = zeros[0:SUBLANES]

    def fill(s, carry):
        src = pl.multiple_of(s * RG_SEG, RG_SEG)
        dst = pl.multiple_of((s + 1) * RG_PITCH, SUBLANES)
        for g, cols in enumerate(lane_cols):
            rows = u_ref[pl.ds(src, RG_SEG), cols].astype(F32)
            u_scr[g, pl.ds(dst, RG_SEG), :] = rows
            u_scr[g, pl.ds(dst - SUBLANES, SUBLANES), :] = rows[0:SUBLANES]
        return carry

    lax.fori_loop(0, nseg, fill, 0, unroll=SUBLANES)

    seg = RG_SEG
    cw_rows = [[jnp.broadcast_to(cw_ref[k:k + 1, c], (SUBLANES, LANES)) for c in lane_cols]
               for k in range(CONV_W)]
    cb_rows = [jnp.broadcast_to(cb_ref[:, c], (SUBLANES, LANES)) for c in lane_cols]

    def conv_tile(i):
        base = pl.multiple_of((i * SUBLANES + 1) * RG_PITCH, SUBLANES)
        blocks = []
        for j in range(seg):
            lane_groups = []
            for g in range(cg // LANES):
                acc = cb_rows[g]
                for k in range(CONV_W):
                    q = j + k - CONV_LEFT
                    start = base + q if q >= 0 else base - RG_PITCH + seg + q
                    acc = acc + cw_rows[k][g] * u_scr[g, pl.ds(start, SUBLANES, stride=RG_PITCH), :]
                lane_groups.append(acc)
            blocks.append(jnp.concatenate(lane_groups, axis=1))
        return jnp.concatenate(blocks, axis=0)

    def gates(uc, d):
        ub = uc.astype(BF16)
        ta = jnp.tanh(jnp.dot(ub, wg_ref[d, 0], preferred_element_type=F32) + bg_ref[d, 0:1, :])
        ti = jnp.tanh(jnp.dot(ub, wg_ref[d, 1], preferred_element_type=F32) + bg_ref[d, 1:2, :])
        z = -lam_ref[d:d + 1, :]
        sp = jnp.maximum(z, 0.0) + jnp.log1p(jnp.exp(-jnp.abs(z)))
        c2 = (-0.5 * RG_C * LOG2_E) * sp
        a = jnp.exp2(c2 + c2 * ta)
        b = jnp.exp2((0.5 * LOG2_E) * jnp.log(1.0 - a * a)) * ((0.5 + 0.5 * ti) * uc)
        return a, b

    def scan_tile(a, b, carry, reverse):
        steps = range(seg - 1, -1, -1) if reverse else range(seg)
        h = acum = None
        h_loc = [None] * seg
        a_cum = [None] * seg
        for j in steps:
            rows = slice(j * SUBLANES, (j + 1) * SUBLANES)
            h = b[rows] if h is None else a[rows] * h + b[rows]
            acum = a[rows] if acum is None else a[rows] * acum
            h_loc[j], a_cum[j] = h, acum
        order = range(SUBLANES - 1, -1, -1) if reverse else range(SUBLANES)
        c = carry
        c_in = [None] * SUBLANES
        for s in order:
            c_in[s] = c
            c = h[s:s + 1, :] + acum[s:s + 1, :] * c
        c_in = jnp.concatenate(c_in, axis=0)
        out = jnp.concatenate([h_loc[j] + a_cum[j] * c_in for j in range(seg)], axis=0)
        return out, c

    def fwd_body(i, carry):
        t0 = pl.multiple_of(i * tt, tt)
        uc = conv_tile(i)
        a, b = gates(uc, 0)
        h, carry = scan_tile(a, b, carry, False)
        hf_scr[pl.ds(t0, tt), :] = h
        a, b = gates(uc, 1)
        ab_scr[0, pl.ds(t0, tt), :] = a
        ab_scr[1, pl.ds(t0, tt), :] = b
        return carry

    hf_last = lax.fori_loop(0, nt, fwd_body, h0_ref[0:1, :], unroll=min(RG_UNROLL, nt))

    def bwd_body(i, carry):
        tile = nt - 1 - i
        t0 = pl.multiple_of(tile * tt, tt)
        h, carry = scan_tile(ab_scr[0, pl.ds(t0, tt), :], ab_scr[1, pl.ds(t0, tt), :], carry, True)
        hsum = hf_scr[pl.ds(t0, tt), :] + h
        for j in range(seg):
            for g, cols in enumerate(lane_cols):
                nat_scr[g, pl.ds(j, SUBLANES, stride=RG_PITCH), :] = (
                    hsum[j * SUBLANES:(j + 1) * SUBLANES, cols])
        hnat = jnp.concatenate(
            [jnp.concatenate([nat_scr[g, pl.ds(s * RG_PITCH, seg), :] for s in range(SUBLANES)],
                             axis=0) for g in range(len(lane_cols))], axis=1)
        yg = yg_ref[pl.ds(t0, tt), :].astype(F32)
        o_ref[pl.ds(t0, tt), :] = (jax.nn.gelu(yg) * hnat).astype(o_ref.dtype)
        return carry

    hb_last = lax.fori_loop(0, nt, bwd_body, h0_ref[1:2, :], unroll=min(RG_UNROLL, nt))
    hl_ref[0:1, :] = hf_last
    hl_ref[1:2, :] = hb_last


def _rglru(p, conv_w, conv_b, wg, bg, lam, h0):
    bsz, n, _ = p.shape
    rg_w = lam.shape[1]
    ncg = rg_w // RG_CG
    assert n % RG_TILE == 0 and rg_w % RG_CG == 0
    return pl.pallas_call(
        _rglru_kernel,
        grid=(bsz, ncg),
        in_specs=[
            pl.BlockSpec((None, n, RG_CG), lambda b, c: (b, 0, c)),
            pl.BlockSpec((None, n, RG_CG), lambda b, c: (b, 0, ncg + c)),
            pl.BlockSpec((CONV_W, RG_CG), lambda b, c: (0, c)),
            pl.BlockSpec((1, RG_CG), lambda b, c: (0, c)),
            pl.BlockSpec((2, 2, None, RG_CG, RG_CG), lambda b, c: (0, 0, c, 0, 0)),
            pl.BlockSpec((2, 2, RG_CG), lambda b, c: (0, 0, c)),
            pl.BlockSpec((2, RG_CG), lambda b, c: (0, c)),
            pl.BlockSpec((None, 2, RG_CG), lambda b, c: (b, 0, c)),
        ],
        out_specs=[
            pl.BlockSpec((None, n, RG_CG), lambda b, c: (b, 0, c)),
            pl.BlockSpec((None, 2, RG_CG), lambda b, c: (b, 0, c)),
        ],
        out_shape=[
            jax.ShapeDtypeStruct((bsz, n, rg_w), BF16),
            jax.ShapeDtypeStruct((bsz, 2, rg_w), F32),
        ],
        scratch_shapes=[
            pltpu.VMEM((RG_CG // LANES, (n // RG_SEG + 1) * RG_PITCH, LANES), F32),
            pltpu.VMEM((n, RG_CG), F32),
            pltpu.VMEM((2, n, RG_CG), F32),
            pltpu.VMEM((RG_CG // LANES, SUBLANES * RG_PITCH, LANES), F32),
        ],
        compiler_params=_cparams(("parallel", "parallel")),
        name="rglru",
    )(p, p, conv_w, conv_b.reshape(1, rg_w), wg, bg, lam, h0)


RET_UNROLL = 16


def _retention_kernel(lg_ref, q_ref, k_ref, v_ref, g_ref, cs_ref, sn_ref, s0_ref,
                      o_ref, so_ref, kr_scr, st_scr):
    n, hd = q_ref.shape
    L = RET_CHUNK
    nc = n // L
    head = pl.program_id(1)
    lgf = lg_ref[0, head]
    lgb = lg_ref[1, head]

    def rope(t, rows):
        return t * cs_ref[rows, :] + pltpu.roll(t, hd // 2, axis=1) * sn_ref[rows, :]

    kr_scr[...] = (rope(k_ref[...].astype(F32), slice(None)) * (hd ** -0.5)).astype(BF16)

    jc = lax.broadcasted_iota(jnp.int32, (L, 1), 0).astype(F32)
    q_dec_f = jnp.exp((jc + 1.0) * lgf)
    q_dec_b = jnp.exp((L - jc) * lgb)
    k_dec_f = jnp.exp((L - 1.0 - jc) * lgf)
    k_dec_b = jnp.exp(jc * lgb)
    ones = jnp.ones((1, hd), F32)
    chunk_f = jnp.exp(ones * (L * lgf))
    chunk_b = jnp.exp(ones * (L * lgb))
    ii = lax.broadcasted_iota(jnp.int32, (L, L), 0)
    jj = lax.broadcasted_iota(jnp.int32, (L, L), 1)
    diff = (ii - jj).astype(F32)
    dmat = jnp.where(diff >= 0.0, jnp.exp(jnp.maximum(diff, 0.0) * lgf),
                     jnp.exp(jnp.maximum(-diff, 0.0) * lgb))

    tn_dims = (((0,), (0,)), ((), ()))
    nt_dims = (((1,), (1,)), ((), ()))

    def state_update(c, state, k_dec, chunk_dec):
        t0 = pl.multiple_of(c * L, L)
        kd = (kr_scr[pl.ds(t0, L), :].astype(F32) * k_dec).astype(BF16)
        upd = lax.dot_general(kd, v_ref[pl.ds(t0, L), :], tn_dims, preferred_element_type=F32)
        return chunk_dec * state + upd

    def state_body(i, carry):
        f_state, r_state = carry
        cf = i
        cb = nc - 1 - i
        st_scr[cf, :, 0:hd] = f_state.astype(BF16)
        st_scr[cb, :, hd:2 * hd] = r_state.astype(BF16)
        return (state_update(cf, f_state, k_dec_f, chunk_f),
                state_update(cb, r_state, k_dec_b, chunk_b))

    f_fin, r_fin = lax.fori_loop(0, nc, state_body, (s0_ref[0], s0_ref[1]),
                                 unroll=min(RET_UNROLL, nc))
    so_ref[0] = f_fin
    so_ref[1] = r_fin

    def out_body(c, carry):
        t0 = pl.multiple_of(c * L, L)
        qc = rope(q_ref[pl.ds(t0, L), :].astype(F32), pl.ds(t0, L)).astype(BF16)
        kc = kr_scr[pl.ds(t0, L), :]
        vc = v_ref[pl.ds(t0, L), :]
        s = lax.dot_general(qc, kc, nt_dims, preferred_element_type=F32)
        o = jnp.dot((s * dmat).astype(BF16), vc, preferred_element_type=F32)
        cross = jnp.dot(qc, st_scr[c], preferred_element_type=F32)
        o = o + cross[:, 0:hd] * q_dec_f + cross[:, hd:2 * hd] * q_dec_b
        mu = jnp.mean(o, axis=-1, keepdims=True)
        oc = o - mu
        var = jnp.mean(oc * oc, axis=-1, keepdims=True)
        gate = g_ref[pl.ds(t0, L), :].astype(F32)
        o_ref[pl.ds(t0, L), :] = (gate * jax.nn.sigmoid(gate) * (oc * lax.rsqrt(var + EPS))).astype(o_ref.dtype)
        return carry

    lax.fori_loop(0, nc, out_body, 0, unroll=min(RET_UNROLL, nc))


def _retention(p, log_g, cs, sn, s0, col0):
    bsz, n, _ = p.shape
    nh = log_g.shape[1]
    hd = cs.shape[1]
    cb0 = col0 // hd

    def col(which):
        return pl.BlockSpec((None, n, hd), lambda b, h, lg: (b, 0, cb0 + which * nh + h))

    state = pl.BlockSpec((None, 2, None, hd, hd), lambda b, h, lg: (b, 0, h, 0, 0))
    table = pl.BlockSpec((n, hd), lambda b, h, lg: (0, 0))
    grid_spec = pltpu.PrefetchScalarGridSpec(
        num_scalar_prefetch=1,
        grid=(bsz, nh),
        in_specs=[col(0), col(1), col(2), col(3), table, table, state],
        out_specs=[pl.BlockSpec((None, n, hd), lambda b, h, lg: (b, 0, h)), state],
        scratch_shapes=[
            pltpu.VMEM((n, hd), BF16),
            pltpu.VMEM((n // RET_CHUNK, hd, 2 * hd), BF16),
        ],
    )
    return pl.pallas_call(
        _retention_kernel,
        grid_spec=grid_spec,
        out_shape=[
            jax.ShapeDtypeStruct((bsz, n, nh * hd), BF16),
            jax.ShapeDtypeStruct((bsz, 2, nh, hd, hd), F32),
        ],
        compiler_params=_cparams(("parallel", "parallel")),
        name="retention",
    )(log_g, p, p, p, p, cs, sn, s0)


R_RANK0, R_RANK1, R_E0, R_E1, R_W0, R_W1 = range(6)


def _mix_residual(x_ref, rg_ref, ret_ref, wo_ref, gate_ref, rows=slice(None)):
    rg_w = rg_ref.shape[1]
    y = jnp.dot(rg_ref[rows, :], wo_ref[0:rg_w, :], preferred_element_type=F32)
    y = y + jnp.dot(ret_ref[rows, :], wo_ref[rg_w:, :], preferred_element_type=F32)
    return x_ref[rows, :] + gate_ref[...] * y


ROUTE_ROWS = 512


def _out_proj_kernel(x_ref, rg_ref, ret_ref, wo_ref, gate_ref, g_ref, shift_ref, scale_ref,
                     wr2_ref, br_ref, xo_ref, h_ref, rt_ref, rtt_ref, cnt_ref, carry_scr):
    tm, d = x_ref.shape
    ns = d // LANES
    nr = min(ROUTE_ROWS, tm)

    @pl.when((pl.program_id(0) == 0) & (pl.program_id(1) == 0))
    def _():
        carry_scr[...] = jnp.zeros_like(carry_scr)

    erow = lax.broadcasted_iota(jnp.int32, (SUBLANES, nr), 0).astype(F32)
    ri = lax.broadcasted_iota(jnp.int32, (nr, nr), 0)
    ci = lax.broadcasted_iota(jnp.int32, (nr, nr), 1)
    earlier = jnp.where(ri < ci, 1.0, 0.0).astype(BF16)
    count = carry_scr[:, 0:1]

    for r0 in range(0, tm, nr):
        rows = pl.ds(r0, nr)
        xn = _mix_residual(x_ref, rg_ref, ret_ref, wo_ref, gate_ref, rows)
        xo_ref[rows, :] = xn
        hx = _norm_mod(xn, g_ref[...], shift_ref[...], scale_ref[...])

        for s in range(ns):
            h_ref[pl.ds(r0 * ns + s, nr, stride=ns), :] = hx[:, s * LANES:(s + 1) * LANES]

        h_hi = hx.astype(BF16)
        h_lo = (hx - h_hi.astype(F32)).astype(BF16)
        part = jnp.dot(h_hi, wr2_ref[...], preferred_element_type=F32)
        logits = (part[:, :LANES] + part[:, LANES:]
                  + jnp.dot(h_lo, wr2_ref[:, :LANES], preferred_element_type=F32) + br_ref[...])
        lt = jnp.transpose(logits)[0:SUBLANES, :]
        m0 = jnp.max(lt, axis=0, keepdims=True)
        i0 = jnp.min(jnp.where(lt == m0, erow, float(SUBLANES)), axis=0, keepdims=True)
        rest_l = jnp.where(erow == i0, NEG_BIG, lt)
        m1 = jnp.max(rest_l, axis=0, keepdims=True)
        i1 = jnp.min(jnp.where(rest_l == m1, erow, float(SUBLANES)), axis=0, keepdims=True)
        e = jnp.exp(m1 - m0)
        w0 = 1.0 / (1.0 + e)
        w1 = e / (1.0 + e)

        sel0 = erow == i0
        sel1 = erow == i1
        mask = jnp.where(sel0 | sel1, 1.0, 0.0)
        before = jnp.dot(mask.astype(BF16), earlier, preferred_element_type=F32) + count
        rank0 = jnp.sum(jnp.where(sel0, before, 0.0), axis=0, keepdims=True)
        rank1 = jnp.sum(jnp.where(sel1, before, 0.0), axis=0, keepdims=True)
        count = count + jnp.sum(mask, axis=1, keepdims=True)
        rec = jnp.zeros((SUBLANES, nr), F32)
        for idx, val in enumerate((rank0, rank1, i0, i1, w0, w1)):
            rec = jnp.where(erow == idx, val, rec)
        rtt_ref[:, r0:r0 + nr] = rec
        rt_ref[rows, :] = jnp.transpose(
            jnp.concatenate([rec, jnp.zeros((LANES - SUBLANES, nr), F32)], axis=0))

    carry_scr[...] = jnp.broadcast_to(count, carry_scr.shape)
    on_diag = (lax.broadcasted_iota(jnp.int32, (SUBLANES, LANES), 0)
               == lax.broadcasted_iota(jnp.int32, (SUBLANES, LANES), 1))
    cnt_ref[...] = jnp.sum(jnp.where(on_diag, count, 0.0), axis=0, keepdims=True)


def _out_proj_route(x, rg, ret, w_out, gate, g, shift, scale, wr2, br):
    bsz, n, d = x.shape
    tm = min(2 * ROUTE_ROWS, n)
    vec = pl.BlockSpec((None, 1, d), lambda b, i: (b, 0, 0))
    tile = lambda w: pl.BlockSpec((None, tm, w), lambda b, i: (b, i, 0))
    const = lambda shape: pl.BlockSpec(shape, lambda b, i: (0, 0))
    return pl.pallas_call(
        _out_proj_kernel,
        grid=(bsz, n // tm),
        in_specs=[
            tile(d), tile(rg.shape[2]), tile(ret.shape[2]), const(w_out.shape),
            vec, const((1, d)), vec, vec, const((d, 2 * LANES)), const((1, LANES)),
        ],
        out_specs=[
            tile(d),
            pl.BlockSpec((None, tm * (d // LANES), LANES), lambda b, i: (b, i, 0)),
            tile(LANES),
            pl.BlockSpec((SUBLANES, tm), lambda b, i: (0, b * (n // tm) + i)),
            const((1, LANES)),
        ],
        out_shape=[
            jax.ShapeDtypeStruct((bsz, n, d), F32),
            jax.ShapeDtypeStruct((bsz, n * (d // LANES), LANES), F32),
            jax.ShapeDtypeStruct((bsz, n, LANES), F32),
            jax.ShapeDtypeStruct((SUBLANES, bsz * n), F32),
            jax.ShapeDtypeStruct((1, LANES), F32),
        ],
        scratch_shapes=[pltpu.VMEM((SUBLANES, LANES), F32)],
        compiler_params=_cparams(("arbitrary", "arbitrary")),
        name="out_proj_route",
    )(x, rg, ret, w_out, gate, g, shift, scale, wr2, br)


def _swiglu_acc(h, w1_ref, w3_ref, w2_ref):
    a = jnp.dot(h, w1_ref[...], preferred_element_type=F32)
    b = jnp.dot(h, w3_ref[...], preferred_element_type=F32)
    act = a * jax.nn.sigmoid(a) * b
    return jnp.dot(act.astype(BF16), w2_ref[...], preferred_element_type=F32)


def _mix_ffn_kernel(n_cast, x_ref, rg_ref, ret_ref, wo_ref, gmix_ref, g_ref, shift_ref, scale_ref,
                    gffn_ref, w1_ref, w3_ref, w2_ref, *rest):
    cast_in, o_ref, cast_out = rest[:n_cast], rest[n_cast], rest[n_cast + 1:]
    xn = _mix_residual(x_ref, rg_ref, ret_ref, wo_ref, gmix_ref)
    hx = _norm_mod(xn, g_ref[...], shift_ref[...], scale_ref[...]).astype(BF16)
    o_ref[...] = xn + gffn_ref[...] * _swiglu_acc(hx, w1_ref, w3_ref, w2_ref)
    for src, dst in zip(cast_in, cast_out):
        dst[...] = src[...].astype(dst.dtype)


def _mix_ffn(x, rg, ret, w_out, gate_mix, g, shift, scale, gate_ffn, w1, w3, w2, cast=()):
    bsz, n, d = x.shape
    f = w1.shape[1]
    tm = min(ROW_TILE, n)
    steps = bsz * (n // tm)
    vec = pl.BlockSpec((None, 1, d), lambda b, i: (b, 0, 0))
    tile = lambda w: pl.BlockSpec((None, tm, w), lambda b, i: (b, i, 0))
    resident = lambda shape: pl.BlockSpec(shape, lambda b, i: (0, 0), pipeline_mode=pl.Buffered(1))
    slabs = []
    for a in cast:
        rows, cols = a.shape
        assert rows % (steps * 2 * SUBLANES) == 0
        slabs.append(pl.BlockSpec((rows // steps, cols), lambda b, i: (b * (n // tm) + i, 0)))
    out = pl.pallas_call(
        functools.partial(_mix_ffn_kernel, len(cast)),
        grid=(bsz, n // tm),
        in_specs=[
            tile(d), tile(rg.shape[2]), tile(ret.shape[2]), resident(w_out.shape),
            vec, resident((1, d)), vec, vec, vec,
            resident((d, f)), resident((d, f)), resident((f, d)),
        ] + slabs,
        out_specs=[tile(d)] + slabs,
        out_shape=[jax.ShapeDtypeStruct((bsz, n, d), F32)]
        + [jax.ShapeDtypeStruct(a.shape, BF16) for a in cast],
        compiler_params=_cparams(("parallel", "parallel")),
        name="mix_ffn",
    )(x, rg, ret, w_out, gate_mix, g, shift, scale, gate_ffn, w1, w3, w2, *cast)
    return out[0], tuple(out[1:])


MOE_BLK = 256
MOE_TOKENS = 1024
COMBINE_ROWS = 256


def _dispatch_kernel(nsteps, d0_ref, d1_ref, zr_ref, hx_ref, hs_ref, zbuf, stage, sem, zsem):
    tm = hx_ref.shape[0]
    step = pl.program_id(0)
    base = step * tm
    slot = step % 2

    @pl.when(pl.program_id(0) == 0)
    def _():
        zbuf[...] = jnp.zeros_like(zbuf)
        for z in range(zr_ref.shape[0]):
            @pl.when(zr_ref[z] >= 0)
            def _():
                pltpu.make_async_copy(zbuf, hs_ref.at[pl.ds(zr_ref[z], MOE_BLK)], zsem).start()
        for z in range(zr_ref.shape[0]):
            @pl.when(zr_ref[z] >= 0)
            def _():
                pltpu.make_async_copy(zbuf, hs_ref.at[pl.ds(zr_ref[z], MOE_BLK)], zsem).wait()

    def drain(which):
        for _ in range(2):
            pltpu.make_async_copy(stage.at[which], hs_ref.at[pl.ds(0, tm)], sem.at[which]).wait()

    @pl.when(step >= 2)
    def _():
        drain(slot)

    stage[slot] = hx_ref[...]
    src = stage.at[slot]

    def issue(t, carry):
        pltpu.make_async_copy(src.at[t], hs_ref.at[d0_ref[base + t]], sem.at[slot]).start(priority=0)
        pltpu.make_async_copy(src.at[t], hs_ref.at[d1_ref[base + t]], sem.at[slot]).start(priority=1)
        return carry

    lax.fori_loop(0, tm, issue, 0, unroll=8)

    @pl.when(step == nsteps - 1)
    def _():
        drain(slot)
        if nsteps >= 2:
            drain(1 - slot)


def _dispatch(dest0, dest1, zero_rows, hx, n_rows):
    n_tok, s, lanes = hx.shape
    tm = min(MOE_TOKENS, n_tok)
    grid_spec = pltpu.PrefetchScalarGridSpec(
        num_scalar_prefetch=3,
        grid=(n_tok // tm,),
        in_specs=[pl.BlockSpec((tm, s, lanes), lambda i, d0, d1, zr: (i, 0, 0))],
        out_specs=pl.BlockSpec(memory_space=pl.ANY),
        scratch_shapes=[pltpu.VMEM((MOE_BLK, s, lanes), F32),
                        pltpu.VMEM((2, tm, s, lanes), F32),
                        pltpu.SemaphoreType.DMA((2,)), pltpu.SemaphoreType.DMA(())],
    )
    return pl.pallas_call(
        functools.partial(_dispatch_kernel, n_tok // tm),
        grid_spec=grid_spec,
        out_shape=jax.ShapeDtypeStruct((n_rows, s, lanes), F32),
        compiler_params=_cparams(("arbitrary",)),
        name="moe_dispatch",
    )(dest0, dest1, zero_rows, hx)


def _experts_kernel(be_ref, nu_ref, hs_ref, w1_ref, w3_ref, w2_ref, y_ref):
    j = pl.program_id(0)
    d = w1_ref.shape[0]
    ns = d // LANES
    blk = hs_ref.shape[0] // ns

    @pl.when(j < nu_ref[0])
    def _():
        h = jnp.concatenate(
            [hs_ref[pl.ds(s, blk, stride=ns), :].astype(BF16) for s in range(ns)], axis=1)
        out = _swiglu_acc(h, w1_ref, w3_ref, w2_ref)
        for s in range(ns):
            y_ref[pl.ds(s, blk, stride=ns), :] = out[:, s * LANES:(s + 1) * LANES]

    @pl.when(j >= nu_ref[0])
    def _():
        y_ref[...] = jnp.zeros_like(y_ref)


def _experts(block_expert, n_used, hs, w1, w3, w2):
    ne, d, f = w1.shape
    s = d // LANES
    n_rows = hs.shape[0] // s
    n_blocks = n_rows // MOE_BLK

    def expert(shape):
        return pl.BlockSpec((None,) + shape, lambda j, be, nu: (be[j], 0, 0))

    grid_spec = pltpu.PrefetchScalarGridSpec(
        num_scalar_prefetch=2,
        grid=(n_blocks,),
        in_specs=[
            pl.BlockSpec((MOE_BLK * s, LANES), lambda j, be, nu: (jnp.minimum(j, nu[0] - 1), 0)),
            expert((d, f)), expert((d, f)), expert((f, d)),
        ],
        out_specs=pl.BlockSpec((MOE_BLK * s, LANES), lambda j, be, nu: (j, 0)),
    )
    return pl.pallas_call(
        _experts_kernel,
        grid_spec=grid_spec,
        out_shape=jax.ShapeDtypeStruct((n_rows * s, LANES), F32),
        compiler_params=_cparams(("arbitrary",)),
        name="moe_experts",
    )(block_expert, n_used, hs, w1, w3, w2)


def _combine_kernel(d0_ref, d1_ref, x_ref, gate_ref, rt_ref, gfin_ref, y_ref, o_ref, ybuf, sem):
    tm, d = x_ref.shape
    ns = d // LANES
    step = pl.program_id(0) * pl.num_programs(1) + pl.program_id(1)
    nsteps = pl.num_programs(0) * pl.num_programs(1)
    slot = step % 2

    def row(ref, r):
        return ref.at[pl.ds(pl.multiple_of(r * ns, ns), ns)]

    def start_gather(block, into):
        base = block * tm

        def issue(t, carry):
            pltpu.make_async_copy(row(y_ref, d0_ref[base + t]), row(ybuf.at[into, 0], t),
                                  sem.at[into]).start(priority=0)
            pltpu.make_async_copy(row(y_ref, d1_ref[base + t]), row(ybuf.at[into, 1], t),
                                  sem.at[into]).start(priority=1)
            return carry

        lax.fori_loop(0, tm, issue, 0, unroll=8)

    @pl.when(step == 0)
    def _():
        start_gather(0, 0)

    @pl.when(step + 1 < nsteps)
    def _():
        start_gather(step + 1, 1 - slot)

    for k in range(2):
        pltpu.make_async_copy(y_ref.at[pl.ds(0, tm * ns)], ybuf.at[slot, k], sem.at[slot]).wait()

    chunk = min(COMBINE_ROWS, tm)
    for r0 in range(0, tm, chunk):
        rows = pl.ds(r0, chunk)
        rt = rt_ref[rows, :]
        w0 = jnp.broadcast_to(rt[:, R_W0:R_W0 + 1], (chunk, LANES))
        w1 = jnp.broadcast_to(rt[:, R_W1:R_W1 + 1], (chunk, LANES))
        sq = jnp.zeros((chunk, LANES), F32)
        for s in range(ns):
            cols = slice(s * LANES, (s + 1) * LANES)
            moe = (w0 * ybuf[slot, 0, pl.ds(r0 * ns + s, chunk, stride=ns), :]
                   + w1 * ybuf[slot, 1, pl.ds(r0 * ns + s, chunk, stride=ns), :])
            xn = x_ref[rows, cols] + gate_ref[:, cols] * moe
            sq = sq + xn * xn
            o_ref[rows, cols] = xn
        ss = jnp.sum(sq, axis=-1, keepdims=True)
        o_ref[rows, :] = o_ref[rows, :] * lax.rsqrt(ss / d + EPS) * gfin_ref[...]


def _combine(dest0, dest1, x, gate, route, g_final, y):
    bsz, n, d = x.shape
    s = d // LANES
    tm = min(MOE_TOKENS // 4, n)
    grid_spec = pltpu.PrefetchScalarGridSpec(
        num_scalar_prefetch=2,
        grid=(bsz, n // tm),
        in_specs=[
            pl.BlockSpec((None, tm, d), lambda b, i, d0, d1: (b, i, 0)),
            pl.BlockSpec((None, 1, d), lambda b, i, d0, d1: (b, 0, 0)),
            pl.BlockSpec((None, tm, LANES), lambda b, i, d0, d1: (b, i, 0)),
            pl.BlockSpec((1, d), lambda b, i, d0, d1: (0, 0)),
            pl.BlockSpec(memory_space=pl.ANY),
        ],
        out_specs=pl.BlockSpec((None, tm, d), lambda b, i, d0, d1: (b, i, 0)),
        scratch_shapes=[pltpu.VMEM((2, 2, tm * s, LANES), F32), pltpu.SemaphoreType.DMA((2,))],
    )
    return pl.pallas_call(
        _combine_kernel,
        grid_spec=grid_spec,
        out_shape=jax.ShapeDtypeStruct((bsz, n, d), F32),
        compiler_params=_cparams(("arbitrary", "arbitrary")),
        name="moe_combine",
    )(dest0, dest1, x, gate, route, g_final, y)


def _moe(hx, x, gate, route, route_t, counts, g_final, w1, w3, w2):
    bsz, n, d = x.shape
    ne = w1.shape[0]
    n_tok = bsz * n
    n_rows = -(-(2 * n_tok) // MOE_BLK) * MOE_BLK + ne * MOE_BLK
    n_blocks = n_rows // MOE_BLK
    cnt = counts[0, :ne].astype(jnp.int32)
    padded = (cnt + MOE_BLK - 1) // MOE_BLK * MOE_BLK
    pad_end = jnp.cumsum(padded)
    pad_start = pad_end - padded
    n_used = pad_end[-1] // MOE_BLK
    blk = jnp.minimum(jnp.arange(n_blocks, dtype=jnp.int32), n_used - 1)
    block_expert = jnp.minimum(
        jnp.sum((blk[:, None] * MOE_BLK >= pad_end[None, :]).astype(jnp.int32), axis=1), ne - 1)
    def dest(e_row, rank_row):
        e = route_t[e_row].astype(jnp.int32)
        start = sum(jnp.where(e == k, pad_start[k], 0) for k in range(ne))
        return start + route_t[rank_row].astype(jnp.int32)

    dest0 = dest(R_E0, R_RANK0)
    dest1 = dest(R_E1, R_RANK1)

    last_blk = jnp.where(padded > 0, pad_end - MOE_BLK, -1)
    trail = n_used + jnp.arange(ne, dtype=jnp.int32)
    trail = jnp.where(trail < n_blocks, trail * MOE_BLK, -1)
    zero_rows = jnp.concatenate([last_blk, trail]).astype(jnp.int32)

    ns = d // LANES
    hs = _dispatch(dest0, dest1, zero_rows, hx.reshape(n_tok, ns, LANES), n_rows)
    y = _experts(block_expert, n_used.reshape(1).astype(jnp.int32),
                 hs.reshape(n_rows * ns, LANES), w1, w3, w2)
    return _combine(dest0, dest1, x, gate, route, g_final, y)


def _rope_tables(n, hd):
    rows = n // GRID_W
    n_freq = hd // 4
    inv = ROPE_BASE ** (-jnp.arange(n_freq, dtype=F32) / n_freq)
    row_ang = jnp.arange(rows, dtype=F32)[:, None] * inv
    col_ang = jnp.arange(GRID_W, dtype=F32)[:, None] * inv

    def table(of_row, of_col):
        a = jnp.broadcast_to(of_row[:, None, :], (rows, GRID_W, n_freq))
        b = jnp.broadcast_to(of_col[None, :, :], (rows, GRID_W, n_freq))
        return jnp.concatenate([a, b], axis=-1).reshape(n, 2 * n_freq)

    cos = table(jnp.cos(row_ang), jnp.cos(col_ang))
    sin = table(jnp.sin(row_ang), jnp.sin(col_ang))
    return jnp.concatenate([cos, cos], axis=-1), jnp.concatenate([-sin, sin], axis=-1)


def _block_diag_gates(rg_wa, rg_wx):
    w = jnp.stack([rg_wa, rg_wx], axis=1)
    nd, ng, nk, c, _ = w.shape
    per = RG_CG // c
    w = w.reshape(nd, ng, nk // per, per, c, c)
    eye = jnp.eye(per, dtype=w.dtype)
    full = jnp.einsum('dgmpij,pq->dgmpiqj', w, eye)
    return full.reshape(nd, ng, nk // per, RG_CG, RG_CG).astype(BF16)


def _mixer(p, rope, states, conv_w, conv_b, wg, bg, lam, log_g, rg_w):
    rg_out, h_last = _rglru(p, conv_w, conv_b, wg, bg, lam, states[0])
    ret_out, s_last = _retention(p, log_g, rope[0], rope[1], states[1], 2 * rg_w)
    return rg_out, ret_out, (h_last, s_last)


def kernel(x, c, ctx, c_ctx, w_mod, b_mod, g_mix, g_ffn, g_final, w_in, w_out, conv_w, conv_b,
           rg_wa, rg_ba, rg_wx, rg_bx, rg_lam, ret_decay, ffn_w1, ffn_w3, ffn_w2,
           moe_router, moe_router_b, moe_w1, moe_w3, moe_w2):
    bsz, n_lat, d = x.shape
    n_ctx = ctx.shape[1]
    depth = w_mod.shape[0]
    rg_w = rg_lam.shape[2]
    nh = ret_decay.shape[2]
    hd = (w_out.shape[1] - rg_w) // nh
    ne = moe_router.shape[2]
    assert depth == 2, "kernel is written for the two-layer block (dense FFN, then MoE)"

    rows = -(-(bsz + 1) // SUBLANES) * SUBLANES
    c_all = jnp.zeros((rows, d), F32).at[:bsz].set(c).at[bsz].set(c_ctx)
    mod = _modulation(c_all, w_mod, b_mod)
    mod = mod.reshape(depth, rows, N_MOD, d)
    mod_lat = mod[:, :bsz, None]
    mod_ctx = jnp.broadcast_to(mod[:, bsz:bsz + 1, None], mod_lat.shape)

    rope_lat = _rope_tables(n_lat, hd)
    rope_ctx = (jnp.ones((n_ctx, hd), F32), jnp.zeros((n_ctx, hd), F32))
    log_g = jax.nn.log_sigmoid(ret_decay.astype(F32))
    zero_states = (jnp.zeros((bsz, 2, rg_w), F32), jnp.zeros((bsz, 2, nh, hd, hd), F32))

    next_proj = None
    for l in range(depth):
        last = l == depth - 1
        ml = [mod_lat[l, :, :, j] for j in range(N_MOD)]
        mc = [mod_ctx[l, :, :, j] for j in range(N_MOD)]
        g_m = g_mix[l].reshape(1, d)
        g_f = g_ffn[l].reshape(1, d)
        if next_proj is None:
            w_in_l, w_out_l = w_in[l].astype(BF16), w_out[l].astype(BF16)
        else:
            (w_in_l, w_out_l), next_proj = next_proj, None
        wg = _block_diag_gates(0.5 * rg_wa[l], 0.5 * rg_wx[l])
        bg = 0.5 * jnp.stack([rg_ba[l], rg_bx[l]], axis=1)
        mix_p = (conv_w[l], conv_b[l], wg, bg, rg_lam[l], log_g[l], rg_w)

        pc = _in_proj(ctx, g_m, mc[0], mc[1], w_in_l)
        rg_c, ret_c, ctx_states = _mixer(pc, rope_ctx, zero_states, *mix_p)
        px = _in_proj(x, g_m, ml[0], ml[1], w_in_l)
        rg_x, ret_x, _ = _mixer(px, rope_lat, ctx_states, *mix_p)

        if l % 2 == 0:
            i = l // 2
            w1, w3, w2 = ffn_w1[i].astype(BF16), ffn_w3[i].astype(BF16), ffn_w2[i].astype(BF16)
            nxt = moe_w1[i], moe_w3[i], moe_w2[i], w_in[l + 1], w_out[l + 1]
            x, cast = _mix_ffn(x, rg_x, ret_x, w_out_l, ml[2], g_f, ml[3], ml[4], ml[5],
                               w1, w3, w2, cast=[w.reshape(-1, w.shape[-1]) for w in nxt])
            cast = [c.reshape(w.shape) for c, w in zip(cast, nxt)]
            moe_w, next_proj = cast[:3], cast[3:]
            if not last:
                ctx, _ = _mix_ffn(ctx, rg_c, ret_c, w_out_l, mc[2], g_f, mc[3], mc[4], mc[5],
                                  w1, w3, w2)
        else:
            i = l // 2
            assert ne <= SUBLANES
            wr = jnp.zeros((d, LANES), F32).at[:, :ne].set(moe_router[i])
            wr_hi = wr.astype(BF16)
            wr_lo = (wr - wr_hi.astype(F32)).astype(BF16)
            wr2 = jnp.concatenate([wr_hi, wr_lo], axis=1)
            br = jnp.full((1, LANES), NEG_BIG, F32).at[0, :ne].set(moe_router_b[i])
            x, hx, route, route_t, counts = _out_proj_route(
                x, rg_x, ret_x, w_out_l, ml[2], g_f, ml[3], ml[4], wr2, br)
            x = _moe(hx, x, ml[5], route, route_t, counts, g_final.reshape(1, d), *moe_w)
    return x
```

```python
import functools

import jax
import jax.numpy as jnp
from jax import lax
from jax.experimental import pallas as pl
from jax.experimental.pallas import tpu as pltpu

F32 = jnp.float32
BF16 = jnp.bfloat16

EPS = 1e-6
RG_C = 8.0
CONV_W = 4
CONV_LEFT = 2
RET_CHUNK = 256
ROPE_BASE = 10000.0
GRID_W = 64
N_MOD = 6

ROW_TILE = 512

LANES = 128
SUBLANES = 8
VMEM_LIMIT = 56 * 1024 * 1024
NEG_BIG = -1e30
LOG2_E = 1.4426950408889634


def _cparams(sem):
    return pltpu.CompilerParams(dimension_semantics=sem, vmem_limit_bytes=VMEM_LIMIT)


def _modulation_kernel(c_ref, w_ref, b_ref, o_ref):
    c = c_ref[...]
    s = c * jax.nn.sigmoid(c)
    o_ref[...] = jnp.dot(s, w_ref[...], precision=lax.Precision.HIGHEST,
                         preferred_element_type=F32) + b_ref[...]


def _modulation(c_all, w_mod, b_mod):
    depth, d, six_d = w_mod.shape
    rows = c_all.shape[0]
    tn = six_d // 2
    return pl.pallas_call(
        _modulation_kernel,
        grid=(depth, six_d // tn),
        in_specs=[
            pl.BlockSpec((rows, d), lambda l, j: (0, 0)),
            pl.BlockSpec((None, d, tn), lambda l, j: (l, 0, j)),
            pl.BlockSpec((None, 1, tn), lambda l, j: (l, 0, j)),
        ],
        out_specs=pl.BlockSpec((None, rows, tn), lambda l, j: (l, 0, j)),
        out_shape=jax.ShapeDtypeStruct((depth, rows, six_d), F32),
        compiler_params=_cparams(("arbitrary", "arbitrary")),
        name="modulation",
    )(c_all, w_mod, b_mod.reshape(depth, 1, six_d))


def _norm_mod(x, g, shift, scale):
    ms = jnp.mean(x * x, axis=-1, keepdims=True)
    y = x * lax.rsqrt(ms + EPS) * g
    return y * (1.0 + scale) + shift


def _in_proj_kernel(x_ref, g_ref, shift_ref, scale_ref, w_ref, o_ref):
    h = _norm_mod(x_ref[...], g_ref[...], shift_ref[...], scale_ref[...])
    o_ref[...] = jnp.dot(h.astype(BF16), w_ref[...],
                         preferred_element_type=F32).astype(o_ref.dtype)


def _in_proj(x, g, shift, scale, w):
    bsz, n, d = x.shape
    f = w.shape[1]
    tm = min(2 * ROW_TILE, n)
    vec = pl.BlockSpec((None, 1, d), lambda b, i: (b, 0, 0))
    return pl.pallas_call(
        _in_proj_kernel,
        grid=(bsz, n // tm),
        in_specs=[
            pl.BlockSpec((None, tm, d), lambda b, i: (b, i, 0)),
            pl.BlockSpec((1, d), lambda b, i: (0, 0)),
            vec, vec,
            pl.BlockSpec((d, f), lambda b, i: (0, 0)),
        ],
        out_specs=pl.BlockSpec((None, tm, f), lambda b, i: (b, i, 0)),
        out_shape=jax.ShapeDtypeStruct((bsz, n, f), BF16),
        compiler_params=_cparams(("parallel", "parallel")),
        name="in_proj",
    )(x, g, shift, scale, w)


RG_TILE = 256
RG_CG = 256
RG_UNROLL = 4
RG_SEG = RG_TILE // SUBLANES
RG_PITCH = RG_SEG + SUBLANES
assert RG_PITCH % SUBLANES == 0 and RG_PITCH % (2 * SUBLANES) != 0


def _rglru_kernel(u_ref, yg_ref, cw_ref, cb_ref, wg_ref, bg_ref, lam_ref, h0_ref,
                  o_ref, hl_ref, u_scr, hf_scr, ab_scr, nat_scr):
    n = u_ref.shape[0]
    tt = RG_TILE
    nt = n // tt
    nseg = n // RG_SEG
    cg = u_ref.shape[1]
    lane_cols = [slice(g * LANES, (g + 1) * LANES) for g in range(cg // LANES)]

    zeros = jnp.zeros((RG_PITCH, LANES), F32)
    for g, cols in enumerate(lane_cols):
        u_scr[g, pl.ds(0, RG_PITCH), :] = zeros
        u_scr[g, pl.ds(nseg * RG_PITCH + RG_SEG, SUBLANES), :] = zeros[0:SUBLANES]

    def fill(s, carry):
        src = pl.multiple_of(s * RG_SEG, RG_SEG)
        dst = pl.multiple_of((s + 1) * RG_PITCH, SUBLANES)
        for g, cols in enumerate(lane_cols):
            rows = u_ref[pl.ds(src, RG_SEG), cols].astype(F32)
            u_scr[g, pl.ds(dst, RG_SEG), :] = rows
            u_scr[g, pl.ds(dst - SUBLANES, SUBLANES), :] = rows[0:SUBLANES]
        return carry

    lax.fori_loop(0, nseg, fill, 0, unroll=SUBLANES)

    seg = RG_SEG
    cw_rows = [[jnp.broadcast_to(cw_ref[k:k + 1, c], (SUBLANES, LANES)) for c in lane_cols]
               for k in range(CONV_W)]
    cb_rows = [jnp.broadcast_to(cb_ref[:, c], (SUBLANES, LANES)) for c in lane_cols]

    def conv_tile(i):
        base = pl.multiple_of((i * SUBLANES + 1) * RG_PITCH, SUBLANES)
        blocks = []
        for j in range(seg):
            lane_groups = []
            for g in range(cg // LANES):
                acc = cb_rows[g]
                for k in range(CONV_W):
                    q = j + k - CONV_LEFT
                    start = base + q if q >= 0 else base - RG_PITCH + seg + q
                    acc = acc + cw_rows[k][g] * u_scr[g, pl.ds(start, SUBLANES, stride=RG_PITCH), :]
                lane_groups.append(acc)
            blocks.append(jnp.concatenate(lane_groups, axis=1))
        return jnp.concatenate(blocks, axis=0)

    def gates(uc, d):
        ub = uc.astype(BF16)
        ta = jnp.tanh(jnp.dot(ub, wg_ref[d, 0], preferred_element_type=F32) + bg_ref[d, 0:1, :])
        ti = jnp.tanh(jnp.dot(ub, wg_ref[d, 1], preferred_element_type=F32) + bg_ref[d, 1:2, :])
        z = -lam_ref[d:d + 1, :]
        sp = jnp.maximum(z, 0.0) + jnp.log1p(jnp.exp(-jnp.abs(z)))
        c2 = (-0.5 * RG_C * LOG2_E) * sp
        a = jnp.exp2(c2 + c2 * ta)
        b = jnp.exp2((0.5 * LOG2_E) * jnp.log(1.0 - a * a)) * ((0.5 + 0.5 * ti) * uc)
        return a, b

    def scan_tile(a, b, carry, reverse):
        steps = range(seg - 1, -1, -1) if reverse else range(seg)
        h = acum = None
        h_loc = [None] * seg
        a_cum = [None] * seg
        for j in steps:
            rows = slice(j * SUBLANES, (j + 1) * SUBLANES)
            h = b[rows] if h is None else a[rows] * h + b[rows]
            acum = a[rows] if acum is None else a[rows] * acum
            h_loc[j], a_cum[j] = h, acum
        order = range(SUBLANES - 1, -1, -1) if reverse else range(SUBLANES)
        c = carry
        c_in = [None] * SUBLANES
        for s in order:
            c_in[s] = c
            c = h[s:s + 1, :] + acum[s:s + 1, :] * c
        c_in = jnp.concatenate(c_in, axis=0)
        out = jnp.concatenate([h_loc[j] + a_cum[j] * c_in for j in range(seg)], axis=0)
        return out, c

    def fwd_body(i, carry):
        t0 = pl.multiple_of(i * tt, tt)
        uc = conv_tile(i)
        a, b = gates(uc, 0)
        h, carry = scan_tile(a, b, carry, False)
        hf_scr[pl.ds(t0, tt), :] = h
        a, b = gates(uc, 1)
        ab_scr[0, pl.ds(t0, tt), :] = a
        ab_scr[1, pl.ds(t0, tt), :] = b
        return carry

    hf_last = lax.fori_loop(0, nt, fwd_body, h0_ref[0:1, :], unroll=min(RG_UNROLL, nt))

    def bwd_body(i, carry):
        tile = nt - 1 - i
        t0 = pl.multiple_of(tile * tt, tt)
        h, carry = scan_tile(ab_scr[0, pl.ds(t0, tt), :], ab_scr[1, pl.ds(t0, tt), :], carry, True)
        hsum = hf_scr[pl.ds(t0, tt), :] + h
        for j in range(seg):
            for g, cols in enumerate(lane_cols):
                nat_scr[g, pl.ds(j, SUBLANES, stride=RG_PITCH), :] = (
                    hsum[j * SUBLANES:(j + 1) * SUBLANES, cols])
        hnat = jnp.concatenate(
            [jnp.concatenate([nat_scr[g, pl.ds(s * RG_PITCH, seg), :] for s in range(SUBLANES)],
                             axis=0) for g in range(len(lane_cols))], axis=1)
        yg = yg_ref[pl.ds(t0, tt), :].astype(F32)
        o_ref[pl.ds(t0, tt), :] = (jax.nn.gelu(yg) * hnat).astype(o_ref.dtype)
        return carry

    hb_last = lax.fori_loop(0, nt, bwd_body, h0_ref[1:2, :], unroll=min(RG_UNROLL, nt))
    hl_ref[0:1, :] = hf_last
    hl_ref[1:2, :] = hb_last


def _rglru(p, conv_w, conv_b, wg, bg, lam, h0):
    bsz, n, _ = p.shape
    rg_w = lam.shape[1]
    ncg = rg_w // RG_CG
    assert n % RG_TILE == 0 and rg_w % RG_CG == 0
    return pl.pallas_call(
        _rglru_kernel,
        grid=(bsz, ncg),
        in_specs=[
            pl.BlockSpec((None, n, RG_CG), lambda b, c: (b, 0, c)),
            pl.BlockSpec((None, n, RG_CG), lambda b, c: (b, 0, ncg + c)),
            pl.BlockSpec((CONV_W, RG_CG), lambda b, c: (0, c)),
            pl.BlockSpec((1, RG_CG), lambda b, c: (0, c)),
            pl.BlockSpec((2, 2, None, RG_CG, RG_CG), lambda b, c: (0, 0, c, 0, 0)),
            pl.BlockSpec((2, 2, RG_CG), lambda b, c: (0, 0, c)),
            pl.BlockSpec((2, RG_CG), lambda b, c: (0, c)),
            pl.BlockSpec((None, 2, RG_CG), lambda b, c: (b, 0, c)),
        ],
        out_specs=[
            pl.BlockSpec((None, n, RG_CG), lambda b, c: (b, 0, c)),
            pl.BlockSpec((None, 2, RG_CG), lambda b, c: (b, 0, c)),
        ],
        out_shape=[
            jax.ShapeDtypeStruct((bsz, n, rg_w), BF16),
            jax.ShapeDtypeStruct((bsz, 2, rg_w), F32),
        ],
        scratch_shapes=[
            pltpu.VMEM((RG_CG // LANES, (n // RG_SEG + 1) * RG_PITCH, LANES), F32),
            pltpu.VMEM((n, RG_CG), F32),
            pltpu.VMEM((2, n, RG_CG), F32),
            pltpu.VMEM((RG_CG // LANES, SUBLANES * RG_PITCH, LANES), F32),
        ],
        compiler_params=_cparams(("parallel", "parallel")),
        name="rglru",
    )(p, p, conv_w, conv_b.reshape(1, rg_w), wg, bg, lam, h0)


RET_UNROLL = 16


def _retention_kernel(lg_ref, q_ref, k_ref, v_ref, g_ref, cs_ref, sn_ref, s0_ref,
                      o_ref, so_ref, kr_scr, st_scr):
    n, hd = q_ref.shape
    L = RET_CHUNK
    nc = n // L
    head = pl.program_id(1)
    lgf = lg_ref[0, head]
    lgb = lg_ref[1, head]

    def rope(t, rows):
        return t * cs_ref[rows, :] + pltpu.roll(t, hd // 2, axis=1) * sn_ref[rows, :]

    kr_scr[...] = (rope(k_ref[...].astype(F32), slice(None)) * (hd ** -0.5)).astype(BF16)

    jc = lax.broadcasted_iota(jnp.int32, (L, 1), 0).astype(F32)
    q_dec_f = jnp.exp((jc + 1.0) * lgf)
    q_dec_b = jnp.exp((L - jc) * lgb)
    k_dec_f = jnp.exp((L - 1.0 - jc) * lgf)
    k_dec_b = jnp.exp(jc * lgb)
    ones = jnp.ones((1, hd), F32)
    chunk_f = jnp.exp(ones * (L * lgf))
    chunk_b = jnp.exp(ones * (L * lgb))
    ii = lax.broadcasted_iota(jnp.int32, (L, L), 0)
    jj = lax.broadcasted_iota(jnp.int32, (L, L), 1)
    diff = (ii - jj).astype(F32)
    dmat = jnp.where(diff >= 0.0, jnp.exp(jnp.maximum(diff, 0.0) * lgf),
                     jnp.exp(jnp.maximum(-diff, 0.0) * lgb))

    tn_dims = (((0,), (0,)), ((), ()))
    nt_dims = (((1,), (1,)), ((), ()))

    def state_update(c, state, k_dec, chunk_dec):
        t0 = pl.multiple_of(c * L, L)
        kd = (kr_scr[pl.ds(t0, L), :].astype(F32) * k_dec).astype(BF16)
        upd = lax.dot_general(kd, v_ref[pl.ds(t0, L), :], tn_dims, preferred_element_type=F32)
        return chunk_dec * state + upd

    def state_body(i, carry):
        f_state, r_state = carry
        cf = i
        cb = nc - 1 - i
        st_scr[cf, :, 0:hd] = f_state.astype(BF16)
        st_scr[cb, :, hd:2 * hd] = r_state.astype(BF16)
        return (state_update(cf, f_state, k_dec_f, chunk_f),
                state_update(cb, r_state, k_dec_b, chunk_b))

    f_fin, r_fin = lax.fori_loop(0, nc, state_body, (s0_ref[0], s0_ref[1]),
                                 unroll=min(RET_UNROLL, nc))
    so_ref[0] = f_fin
    so_ref[1] = r_fin

    def out_body(c, carry):
        t0 = pl.multiple_of(c * L, L)
        qc = rope(q_ref[pl.ds(t0, L), :].astype(F32), pl.ds(t0, L)).astype(BF16)
        kc = kr_scr[pl.ds(t0, L), :]
        vc = v_ref[pl.ds(t0, L), :]
        s = lax.dot_general(qc, kc, nt_dims, preferred_element_type=F32)
        o = jnp.dot((s * dmat).astype(BF16), vc, preferred_element_type=F32)
        cross = jnp.dot(qc, st_scr[c], preferred_element_type=F32)
        o = o + cross[:, 0:hd] * q_dec_f + cross[:, hd:2 * hd] * q_dec_b
        mu = jnp.mean(o, axis=-1, keepdims=True)
        oc = o - mu
        var = jnp.mean(oc * oc, axis=-1, keepdims=True)
        gate = g_ref[pl.ds(t0, L), :].astype(F32)
        o_ref[pl.ds(t0, L), :] = (gate * jax.nn.sigmoid(gate) * (oc * lax.rsqrt(var + EPS))).astype(o_ref.dtype)
        return carry

    lax.fori_loop(0, nc, out_body, 0, unroll=min(RET_UNROLL, nc))


def _retention(p, log_g, cs, sn, s0, col0):
    bsz, n, _ = p.shape
    nh = log_g.shape[1]
    hd = cs.shape[1]
    cb0 = col0 // hd

    def col(which):
        return pl.BlockSpec((None, n, hd), lambda b, h, lg: (b, 0, cb0 + which * nh + h))

    state = pl.BlockSpec((None, 2, None, hd, hd), lambda b, h, lg: (b, 0, h, 0, 0))
    table = pl.BlockSpec((n, hd), lambda b, h, lg: (0, 0))
    grid_spec = pltpu.PrefetchScalarGridSpec(
        num_scalar_prefetch=1,
        grid=(bsz, nh),
        in_specs=[col(0), col(1), col(2), col(3), table, table, state],
        out_specs=[pl.BlockSpec((None, n, hd), lambda b, h, lg: (b, 0, h)), state],
        scratch_shapes=[
            pltpu.VMEM((n, hd), BF16),
            pltpu.VMEM((n // RET_CHUNK, hd, 2 * hd), BF16),
        ],
    )
    return pl.pallas_call(
        _retention_kernel,
        grid_spec=grid_spec,
        out_shape=[
            jax.ShapeDtypeStruct((bsz, n, nh * hd), BF16),
            jax.ShapeDtypeStruct((bsz, 2, nh, hd, hd), F32),
        ],
        compiler_params=_cparams(("parallel", "parallel")),
        name="retention",
    )(log_g, p, p, p, p, cs, sn, s0)


R_RANK0, R_RANK1, R_E0, R_E1, R_W0, R_W1 = range(6)


def _mix_residual(x_ref, rg_ref, ret_ref, wo_ref, gate_ref, rows=slice(None)):
    rg_w = rg_ref.shape[1]
    y = jnp.dot(rg_ref[rows, :], wo_ref[0:rg_w, :], preferred_element_type=F32)
    y = y + jnp.dot(ret_ref[rows, :], wo_ref[rg_w:, :], preferred_element_type=F32)
    return x_ref[rows, :] + gate_ref[...] * y


ROUTE_ROWS = 512


def _out_proj_kernel(x_ref, rg_ref, ret_ref, wo_ref, gate_ref, g_ref, shift_ref, scale_ref,
                     wr2_ref, br_ref, xo_ref, h_ref, rtt_ref, cnt_ref, carry_scr):
    tm, d = x_ref.shape
    ns = d // LANES
    nr = min(ROUTE_ROWS, tm)

    @pl.when((pl.program_id(0) == 0) & (pl.program_id(1) == 0))
    def _():
        carry_scr[...] = jnp.zeros_like(carry_scr)

    erow = lax.broadcasted_iota(jnp.int32, (SUBLANES, nr), 0).astype(F32)
    ri = lax.broadcasted_iota(jnp.int32, (nr, nr), 0)
    ci = lax.broadcasted_iota(jnp.int32, (nr, nr), 1)
    earlier = jnp.where(ri < ci, 1.0, 0.0).astype(BF16)
    count = carry_scr[:, 0:1]

    for r0 in range(0, tm, nr):
        rows = pl.ds(r0, nr)
        xn = _mix_residual(x_ref, rg_ref, ret_ref, wo_ref, gate_ref, rows)
        xo_ref[rows, :] = xn
        hx = _norm_mod(xn, g_ref[...], shift_ref[...], scale_ref[...])

        for s in range(ns):
            h_ref[pl.ds(r0 * ns + s, nr, stride=ns), :] = hx[:, s * LANES:(s + 1) * LANES]

        h_hi = hx.astype(BF16)
        h_lo = (hx - h_hi.astype(F32)).astype(BF16)
        part = jnp.dot(h_hi, wr2_ref[...], preferred_element_type=F32)
        logits = (part[:, :LANES] + part[:, LANES:]
                  + jnp.dot(h_lo, wr2_ref[:, :LANES], preferred_element_type=F32) + br_ref[...])
        lt = jnp.transpose(logits)[0:SUBLANES, :]
        m0 = jnp.max(lt, axis=0, keepdims=True)
        i0 = jnp.min(jnp.where(lt == m0, erow, float(SUBLANES)), axis=0, keepdims=True)
        rest_l = jnp.where(erow == i0, NEG_BIG, lt)
        m1 = jnp.max(rest_l, axis=0, keepdims=True)
        i1 = jnp.min(jnp.where(rest_l == m1, erow, float(SUBLANES)), axis=0, keepdims=True)
        e = jnp.exp(m1 - m0)
        w0 = 1.0 / (1.0 + e)
        w1 = e / (1.0 + e)

        sel0 = erow == i0
        sel1 = erow == i1
        mask = jnp.where(sel0 | sel1, 1.0, 0.0)
        before = jnp.dot(mask.astype(BF16), earlier, preferred_element_type=F32) + count
        rank0 = jnp.sum(jnp.where(sel0, before, 0.0), axis=0, keepdims=True)
        rank1 = jnp.sum(jnp.where(sel1, before, 0.0), axis=0, keepdims=True)
        count = count + jnp.sum(mask, axis=1, keepdims=True)
        rec = jnp.zeros((SUBLANES, nr), F32)
        for idx, val in enumerate((rank0, rank1, i0, i1, w0, w1)):
            rec = jnp.where(erow == idx, val, rec)
        rtt_ref[:, r0:r0 + nr] = rec

    carry_scr[...] = jnp.broadcast_to(count, carry_scr.shape)
    on_diag = (lax.broadcasted_iota(jnp.int32, (SUBLANES, LANES), 0)
               == lax.broadcasted_iota(jnp.int32, (SUBLANES, LANES), 1))
    cnt_ref[...] = jnp.sum(jnp.where(on_diag, count, 0.0), axis=0, keepdims=True)


def _out_proj_route(x, rg, ret, w_out, gate, g, shift, scale, wr2, br):
    bsz, n, d = x.shape
    tm = min(2 * ROUTE_ROWS, n)
    vec = pl.BlockSpec((None, 1, d), lambda b, i: (b, 0, 0))
    tile = lambda w: pl.BlockSpec((None, tm, w), lambda b, i: (b, i, 0))
    const = lambda shape: pl.BlockSpec(shape, lambda b, i: (0, 0))
    return pl.pallas_call(
        _out_proj_kernel,
        grid=(bsz, n // tm),
        in_specs=[
            tile(d), tile(rg.shape[2]), tile(ret.shape[2]), const(w_out.shape),
            vec, const((1, d)), vec, vec, const((d, 2 * LANES)), const((1, LANES)),
        ],
        out_specs=[
            tile(d),
            pl.BlockSpec((None, tm * (d // LANES), LANES), lambda b, i: (b, i, 0)),
            pl.BlockSpec((SUBLANES, tm), lambda b, i: (0, b * (n // tm) + i)),
            const((1, LANES)),
        ],
        out_shape=[
            jax.ShapeDtypeStruct((bsz, n, d), F32),
            jax.ShapeDtypeStruct((bsz, n * (d // LANES), LANES), F32),
            jax.ShapeDtypeStruct((SUBLANES, bsz * n), F32),
            jax.ShapeDtypeStruct((1, LANES), F32),
        ],
        scratch_shapes=[pltpu.VMEM((SUBLANES, LANES), F32)],
        compiler_params=_cparams(("arbitrary", "arbitrary")),
        name="out_proj_route",
    )(x, rg, ret, w_out, gate, g, shift, scale, wr2, br)


def _swiglu_acc(h, w1_ref, w3_ref, w2_ref):
    a = jnp.dot(h, w1_ref[...], preferred_element_type=F32)
    b = jnp.dot(h, w3_ref[...], preferred_element_type=F32)
    act = a * jax.nn.sigmoid(a) * b
    return jnp.dot(act.astype(BF16), w2_ref[...], preferred_element_type=F32)


def _mix_ffn_kernel(n_cast, x_ref, rg_ref, ret_ref, wo_ref, gmix_ref, g_ref, shift_ref, scale_ref,
                    gffn_ref, w1_ref, w3_ref, w2_ref, *rest):
    cast_in, o_ref, cast_out = rest[:n_cast], rest[n_cast], rest[n_cast + 1:]
    xn = _mix_residual(x_ref, rg_ref, ret_ref, wo_ref, gmix_ref)
    hx = _norm_mod(xn, g_ref[...], shift_ref[...], scale_ref[...]).astype(BF16)
    o_ref[...] = xn + gffn_ref[...] * _swiglu_acc(hx, w1_ref, w3_ref, w2_ref)
    for src, dst in zip(cast_in, cast_out):
        dst[...] = src[...].astype(dst.dtype)


def _mix_ffn(x, rg, ret, w_out, gate_mix, g, shift, scale, gate_ffn, w1, w3, w2, cast=()):
    bsz, n, d = x.shape
    f = w1.shape[1]
    tm = min(ROW_TILE, n)
    steps = bsz * (n // tm)
    vec = pl.BlockSpec((None, 1, d), lambda b, i: (b, 0, 0))
    tile = lambda w: pl.BlockSpec((None, tm, w), lambda b, i: (b, i, 0))
    resident = lambda shape: pl.BlockSpec(shape, lambda b, i: (0, 0), pipeline_mode=pl.Buffered(1))
    slabs = []
    for a in cast:
        rows, cols = a.shape
        assert rows % (steps * 2 * SUBLANES) == 0
        slabs.append(pl.BlockSpec((rows // steps, cols), lambda b, i: (b * (n // tm) + i, 0)))
    out = pl.pallas_call(
        functools.partial(_mix_ffn_kernel, len(cast)),
        grid=(bsz, n // tm),
        in_specs=[
            tile(d), tile(rg.shape[2]), tile(ret.shape[2]), resident(w_out.shape),
            vec, resident((1, d)), vec, vec, vec,
            resident((d, f)), resident((d, f)), resident((f, d)),
        ] + slabs,
        out_specs=[tile(d)] + slabs,
        out_shape=[jax.ShapeDtypeStruct((bsz, n, d), F32)]
        + [jax.ShapeDtypeStruct(a.shape, BF16) for a in cast],
        compiler_params=_cparams(("parallel", "parallel")),
        name="mix_ffn",
    )(x, rg, ret, w_out, gate_mix, g, shift, scale, gate_ffn, w1, w3, w2, *cast)
    return out[0], tuple(out[1:])


MOE_BLK = 256
MOE_TOKENS = 1024
COMBINE_ROWS = 256


def _dispatch_kernel(nsteps, d0_ref, d1_ref, zr_ref, hx_ref, hs_ref, zbuf, stage, sem, zsem):
    tm = hx_ref.shape[0]
    step = pl.program_id(0)
    base = step * tm
    slot = step % 2

    @pl.when(pl.program_id(0) == 0)
    def _():
        zbuf[...] = jnp.zeros_like(zbuf)
        for z in range(zr_ref.shape[0]):
            @pl.when(zr_ref[z] >= 0)
            def _():
                pltpu.make_async_copy(zbuf, hs_ref.at[pl.ds(zr_ref[z], MOE_BLK)], zsem).start()
        for z in range(zr_ref.shape[0]):
            @pl.when(zr_ref[z] >= 0)
            def _():
                pltpu.make_async_copy(zbuf, hs_ref.at[pl.ds(zr_ref[z], MOE_BLK)], zsem).wait()

    def drain(which):
        for _ in range(2):
            pltpu.make_async_copy(stage.at[which], hs_ref.at[pl.ds(0, tm)], sem.at[which]).wait()

    @pl.when(step >= 2)
    def _():
        drain(slot)

    stage[slot] = hx_ref[...]
    src = stage.at[slot]

    def issue(t, carry):
        pltpu.make_async_copy(src.at[t], hs_ref.at[d0_ref[base + t]], sem.at[slot]).start(priority=0)
        pltpu.make_async_copy(src.at[t], hs_ref.at[d1_ref[base + t]], sem.at[slot]).start(priority=1)
        return carry

    lax.fori_loop(0, tm, issue, 0, unroll=8)

    @pl.when(step == nsteps - 1)
    def _():
        drain(slot)
        if nsteps >= 2:
            drain(1 - slot)


def _dispatch(dest0, dest1, zero_rows, hx, n_rows):
    n_tok, s, lanes = hx.shape
    tm = min(MOE_TOKENS, n_tok)
    grid_spec = pltpu.PrefetchScalarGridSpec(
        num_scalar_prefetch=3,
        grid=(n_tok // tm,),
        in_specs=[pl.BlockSpec((tm, s, lanes), lambda i, d0, d1, zr: (i, 0, 0))],
        out_specs=pl.BlockSpec(memory_space=pl.ANY),
        scratch_shapes=[pltpu.VMEM((MOE_BLK, s, lanes), F32),
                        pltpu.VMEM((2, tm, s, lanes), F32),
                        pltpu.SemaphoreType.DMA((2,)), pltpu.SemaphoreType.DMA(())],
    )
    return pl.pallas_call(
        functools.partial(_dispatch_kernel, n_tok // tm),
        grid_spec=grid_spec,
        out_shape=jax.ShapeDtypeStruct((n_rows, s, lanes), F32),
        compiler_params=_cparams(("arbitrary",)),
        name="moe_dispatch",
    )(dest0, dest1, zero_rows, hx)


def _experts_kernel(be_ref, nu_ref, hs_ref, w1_ref, w3_ref, w2_ref, y_ref):
    j = pl.program_id(0)
    d = w1_ref.shape[0]
    ns = d // LANES
    blk = hs_ref.shape[0] // ns

    @pl.when(j < nu_ref[0])
    def _():
        h = jnp.concatenate(
            [hs_ref[pl.ds(s, blk, stride=ns), :].astype(BF16) for s in range(ns)], axis=1)
        out = _swiglu_acc(h, w1_ref, w3_ref, w2_ref)
        for s in range(ns):
            y_ref[pl.ds(s, blk, stride=ns), :] = out[:, s * LANES:(s + 1) * LANES]

    @pl.when(j >= nu_ref[0])
    def _():
        y_ref[...] = jnp.zeros_like(y_ref)


def _experts(block_expert, n_used, hs, w1, w3, w2):
    ne, d, f = w1.shape
    s = d // LANES
    n_rows = hs.shape[0] // s
    n_blocks = n_rows // MOE_BLK

    def expert(shape):
        return pl.BlockSpec((None,) + shape, lambda j, be, nu: (be[j], 0, 0))

    grid_spec = pltpu.PrefetchScalarGridSpec(
        num_scalar_prefetch=2,
        grid=(n_blocks,),
        in_specs=[
            pl.BlockSpec((MOE_BLK * s, LANES), lambda j, be, nu: (jnp.minimum(j, nu[0] - 1), 0)),
            expert((d, f)), expert((d, f)), expert((f, d)),
        ],
        out_specs=pl.BlockSpec((MOE_BLK * s, LANES), lambda j, be, nu: (j, 0)),
    )
    return pl.pallas_call(
        _experts_kernel,
        grid_spec=grid_spec,
        out_shape=jax.ShapeDtypeStruct((n_rows * s, LANES), F32),
        compiler_params=_cparams(("arbitrary",)),
        name="moe_experts",
    )(block_expert, n_used, hs, w1, w3, w2)


def _combine_kernel(d0_ref, d1_ref, x_ref, gate_ref, rt_ref, gfin_ref, y_ref, o_ref, ybuf, sem):
    tm, d = x_ref.shape
    ns = d // LANES
    step = pl.program_id(0) * pl.num_programs(1) + pl.program_id(1)
    nsteps = pl.num_programs(0) * pl.num_programs(1)
    slot = step % 2

    def row(ref, r):
        return ref.at[pl.ds(pl.multiple_of(r * ns, ns), ns)]

    def start_gather(block, into):
        base = block * tm

        def issue(t, carry):
            pltpu.make_async_copy(row(y_ref, d0_ref[base + t]), row(ybuf.at[into, 0], t),
                                  sem.at[into]).start(priority=0)
            pltpu.make_async_copy(row(y_ref, d1_ref[base + t]), row(ybuf.at[into, 1], t),
                                  sem.at[into]).start(priority=1)
            return carry

        lax.fori_loop(0, tm, issue, 0, unroll=8)

    @pl.when(step == 0)
    def _():
        start_gather(0, 0)

    @pl.when(step + 1 < nsteps)
    def _():
        start_gather(step + 1, 1 - slot)

    for k in range(2):
        pltpu.make_async_copy(y_ref.at[pl.ds(0, tm * ns)], ybuf.at[slot, k], sem.at[slot]).wait()

    chunk = min(COMBINE_ROWS, tm)
    for r0 in range(0, tm, chunk):
        rows = pl.ds(r0, chunk)
        rt = jnp.transpose(jnp.concatenate(
            [rt_ref[:, r0:r0 + chunk], jnp.zeros((LANES - SUBLANES, chunk), F32)], axis=0))
        w0 = jnp.broadcast_to(rt[:, R_W0:R_W0 + 1], (chunk, LANES))
        w1 = jnp.broadcast_to(rt[:, R_W1:R_W1 + 1], (chunk, LANES))
        sq = jnp.zeros((chunk, LANES), F32)
        for s in range(ns):
            cols = slice(s * LANES, (s + 1) * LANES)
            moe = (w0 * ybuf[slot, 0, pl.ds(r0 * ns + s, chunk, stride=ns), :]
                   + w1 * ybuf[slot, 1, pl.ds(r0 * ns + s, chunk, stride=ns), :])
            xn = x_ref[rows, cols] + gate_ref[:, cols] * moe
            sq = sq + xn * xn
            o_ref[rows, cols] = xn
        ss = jnp.sum(sq, axis=-1, keepdims=True)
        o_ref[rows, :] = o_ref[rows, :] * lax.rsqrt(ss / d + EPS) * gfin_ref[...]


def _combine(dest0, dest1, x, gate, route_t, g_final, y):
    bsz, n, d = x.shape
    s = d // LANES
    tm = min(MOE_TOKENS // 2, n)
    grid_spec = pltpu.PrefetchScalarGridSpec(
        num_scalar_prefetch=2,
        grid=(bsz, n // tm),
        in_specs=[
            pl.BlockSpec((None, tm, d), lambda b, i, d0, d1: (b, i, 0)),
            pl.BlockSpec((None, 1, d), lambda b, i, d0, d1: (b, 0, 0)),
            pl.BlockSpec((SUBLANES, tm), lambda b, i, d0, d1: (0, b * (n // tm) + i)),
            pl.BlockSpec((1, d), lambda b, i, d0, d1: (0, 0)),
            pl.BlockSpec(memory_space=pl.ANY),
        ],
        out_specs=pl.BlockSpec((None, tm, d), lambda b, i, d0, d1: (b, i, 0)),
        scratch_shapes=[pltpu.VMEM((2, 2, tm * s, LANES), F32), pltpu.SemaphoreType.DMA((2,))],
    )
    return pl.pallas_call(
        _combine_kernel,
        grid_spec=grid_spec,
        out_shape=jax.ShapeDtypeStruct((bsz, n, d), F32),
        compiler_params=_cparams(("arbitrary", "arbitrary")),
        name="moe_combine",
    )(dest0, dest1, x, gate, route_t, g_final, y)


def _moe(hx, x, gate, route_t, counts, g_final, w1, w3, w2):
    bsz, n, d = x.shape
    ne = w1.shape[0]
    n_tok = bsz * n
    n_rows = -(-(2 * n_tok) // MOE_BLK) * MOE_BLK + ne * MOE_BLK
    n_blocks = n_rows // MOE_BLK
    cnt = counts[0, :ne].astype(jnp.int32)
    padded = (cnt + MOE_BLK - 1) // MOE_BLK * MOE_BLK
    pad_end = jnp.cumsum(padded)
    pad_start = pad_end - padded
    n_used = pad_end[-1] // MOE_BLK
    blk = jnp.minimum(jnp.arange(n_blocks, dtype=jnp.int32), n_used - 1)
    block_expert = jnp.minimum(
        jnp.sum((blk[:, None] * MOE_BLK >= pad_end[None, :]).astype(jnp.int32), axis=1), ne - 1)
    def dest(e_row, rank_row):
        e = route_t[e_row].astype(jnp.int32)
        start = sum(jnp.where(e == k, pad_start[k], 0) for k in range(ne))
        return start + route_t[rank_row].astype(jnp.int32)

    dest0 = dest(R_E0, R_RANK0)
    dest1 = dest(R_E1, R_RANK1)

    last_blk = jnp.where(padded > 0, pad_end - MOE_BLK, -1)
    trail = n_used + jnp.arange(ne, dtype=jnp.int32)
    trail = jnp.where(trail < n_blocks, trail * MOE_BLK, -1)
    zero_rows = jnp.concatenate([last_blk, trail]).astype(jnp.int32)

    ns = d // LANES
    hs = _dispatch(dest0, dest1, zero_rows, hx.reshape(n_tok, ns, LANES), n_rows)
    y = _experts(block_expert, n_used.reshape(1).astype(jnp.int32),
                 hs.reshape(n_rows * ns, LANES), w1, w3, w2)
    return _combine(dest0, dest1, x, gate, route_t, g_final, y)


def _rope_tables(n, hd):
    rows = n // GRID_W
    n_freq = hd // 4
    inv = ROPE_BASE ** (-jnp.arange(n_freq, dtype=F32) / n_freq)
    row_ang = jnp.arange(rows, dtype=F32)[:, None] * inv
    col_ang = jnp.arange(GRID_W, dtype=F32)[:, None] * inv

    def table(of_row, of_col):
        a = jnp.broadcast_to(of_row[:, None, :], (rows, GRID_W, n_freq))
        b = jnp.broadcast_to(of_col[None, :, :], (rows, GRID_W, n_freq))
        return jnp.concatenate([a, b], axis=-1).reshape(n, 2 * n_freq)

    cos = table(jnp.cos(row_ang), jnp.cos(col_ang))
    sin = table(jnp.sin(row_ang), jnp.sin(col_ang))
    return jnp.concatenate([cos, cos], axis=-1), jnp.concatenate([-sin, sin], axis=-1)


def _block_diag_gates(rg_wa, rg_wx):
    w = jnp.stack([rg_wa, rg_wx], axis=1)
    nd, ng, nk, c, _ = w.shape
    per = RG_CG // c
    w = w.reshape(nd, ng, nk // per, per, c, c)
    eye = jnp.eye(per, dtype=w.dtype)
    full = jnp.einsum('dgmpij,pq->dgmpiqj', w, eye)
    return full.reshape(nd, ng, nk // per, RG_CG, RG_CG).astype(BF16)


def _mixer(p, rope, states, conv_w, conv_b, wg, bg, lam, log_g, rg_w):
    rg_out, h_last = _rglru(p, conv_w, conv_b, wg, bg, lam, states[0])
    ret_out, s_last = _retention(p, log_g, rope[0], rope[1], states[1], 2 * rg_w)
    return rg_out, ret_out, (h_last, s_last)


def kernel(x, c, ctx, c_ctx, w_mod, b_mod, g_mix, g_ffn, g_final, w_in, w_out, conv_w, conv_b,
           rg_wa, rg_ba, rg_wx, rg_bx, rg_lam, ret_decay, ffn_w1, ffn_w3, ffn_w2,
           moe_router, moe_router_b, moe_w1, moe_w3, moe_w2):
    bsz, n_lat, d = x.shape
    n_ctx = ctx.shape[1]
    depth = w_mod.shape[0]
    rg_w = rg_lam.shape[2]
    nh = ret_decay.shape[2]
    hd = (w_out.shape[1] - rg_w) // nh
    ne = moe_router.shape[2]
    assert depth == 2, "kernel is written for the two-layer block (dense FFN, then MoE)"

    rows = -(-(bsz + 1) // SUBLANES) * SUBLANES
    c_all = jnp.zeros((rows, d), F32).at[:bsz].set(c).at[bsz].set(c_ctx)
    mod = _modulation(c_all, w_mod, b_mod)
    mod = mod.reshape(depth, rows, N_MOD, d)
    mod_lat = mod[:, :bsz, None]
    mod_ctx = jnp.broadcast_to(mod[:, bsz:bsz + 1, None], mod_lat.shape)

    rope_lat = _rope_tables(n_lat, hd)
    rope_ctx = (jnp.ones((n_ctx, hd), F32), jnp.zeros((n_ctx, hd), F32))
    log_g = jax.nn.log_sigmoid(ret_decay.astype(F32))
    zero_states = (jnp.zeros((bsz, 2, rg_w), F32), jnp.zeros((bsz, 2, nh, hd, hd), F32))

    next_proj = None
    for l in range(depth):
        last = l == depth - 1
        ml = [mod_lat[l, :, :, j] for j in range(N_MOD)]
        mc = [mod_ctx[l, :, :, j] for j in range(N_MOD)]
        g_m = g_mix[l].reshape(1, d)
        g_f = g_ffn[l].reshape(1, d)
        if next_proj is None:
            w_in_l, w_out_l = w_in[l].astype(BF16), w_out[l].astype(BF16)
        else:
            (w_in_l, w_out_l), next_proj = next_proj, None
        wg = _block_diag_gates(0.5 * rg_wa[l], 0.5 * rg_wx[l])
        bg = 0.5 * jnp.stack([rg_ba[l], rg_bx[l]], axis=1)
        mix_p = (conv_w[l], conv_b[l], wg, bg, rg_lam[l], log_g[l], rg_w)

        pc = _in_proj(ctx, g_m, mc[0], mc[1], w_in_l)
        rg_c, ret_c, ctx_states = _mixer(pc, rope_ctx, zero_states, *mix_p)
        px = _in_proj(x, g_m, ml[0], ml[1], w_in_l)
        rg_x, ret_x, _ = _mixer(px, rope_lat, ctx_states, *mix_p)

        if l % 2 == 0:
            i = l // 2
            w1, w3, w2 = ffn_w1[i].astype(BF16), ffn_w3[i].astype(BF16), ffn_w2[i].astype(BF16)
            nxt = moe_w1[i], moe_w3[i], moe_w2[i], w_in[l + 1], w_out[l + 1]
            x, cast = _mix_ffn(x, rg_x, ret_x, w_out_l, ml[2], g_f, ml[3], ml[4], ml[5],
                               w1, w3, w2, cast=[w.reshape(-1, w.shape[-1]) for w in nxt])
            cast = [c.reshape(w.shape) for c, w in zip(cast, nxt)]
            moe_w, next_proj = cast[:3], cast[3:]
            if not last:
                ctx, _ = _mix_ffn(ctx, rg_c, ret_c, w_out_l, mc[2], g_f, mc[3], mc[4], mc[5],
                                  w1, w3, w2)
        else:
            i = l // 2
            assert ne <= SUBLANES
            wr = jnp.zeros((d, LANES), F32).at[:, :ne].set(moe_router[i])
            wr_hi = wr.astype(BF16)
            wr_lo = (wr - wr_hi.astype(F32)).astype(BF16)
            wr2 = jnp.concatenate([wr_hi, wr_lo], axis=1)
            br = jnp.full((1, LANES), NEG_BIG, F32).at[0, :ne].set(moe_router_b[i])
            x, hx, route_t, counts = _out_proj_route(
                x, rg_x, ret_x, w_out_l, ml[2], g_f, ml[3], ml[4], wr2, br)
            x = _moe(hx, x, ml[5], route_t, counts, g_final.reshape(1, d), *moe_w)
    return x
```
